```python
import math
import jax, jax.numpy as jnp
from jax import lax
import numpy as np

D_MODEL = 1024
BATCH = 8
SEQ = 2048
DEPTH = 1

HEAD_DIM = 64
SCALE = HEAD_DIM ** -0.5
Q_BLOCK = 128
NSA_HEADS = 8
NSA_KV_GROUPS = 2
NSA_HPG = NSA_HEADS // NSA_KV_GROUPS
CMP_BLOCK = 32
CMP_STRIDE = 16
CMP_HIDDEN = 256
SLC_BLOCK = 64
SLC_TOP_N = 16
SLC_Q_CHUNK = 32
WINDOW = 512
FORCE_SCORE = 1e4
DIFF_HEADS = 4
DIFF_V_DIM = 2 * HEAD_DIM
REL_BUCKETS = 32
REL_MAX_DIST = 128
N_ATTN_HEADS = NSA_HEADS + DIFF_HEADS
N_GROUPS = 4
EXPERTS_PER_GROUP = 4
N_EXPERTS = N_GROUPS * EXPERTS_PER_GROUP
TOP_K_IN_GROUP = 2
EXPERT_HIDDEN = 512
COL_NSA_Q = NSA_HEADS * HEAD_DIM
COL_NSA_KV = 3 * 2 * NSA_KV_GROUPS * HEAD_DIM
COL_NSA_GATE = 3 * NSA_HEADS
COL_DIFF_Q = DIFF_HEADS * 2 * HEAD_DIM
COL_DIFF_K = DIFF_HEADS * 2 * HEAD_DIM
COL_DIFF_V = DIFF_HEADS * DIFF_V_DIM
COL_MERGE = 2 * D_MODEL
IN_COLS = COL_NSA_Q + COL_NSA_KV + COL_NSA_GATE + COL_DIFF_Q + COL_DIFF_K + COL_DIFF_V + COL_MERGE
RMS_EPS = 1e-6
NEG = -1e30

kernel_name = "hybrid_nsa_diffattn_hmoe_block"


def rms_norm(x, g):
    xf = x.astype(jnp.float32)
    y = xf * lax.rsqrt(jnp.mean(xf * xf, axis=-1, keepdims=True) + RMS_EPS)
    return (y * g.astype(jnp.float32)).astype(x.dtype)


def masked_softmax(logits, mask):
    logits = jnp.where(mask, logits.astype(jnp.float32), NEG)
    p = jax.nn.softmax(logits, axis=-1)
    return p * jnp.any(mask, axis=-1, keepdims=True)


def t5_bucket(dist):
    n = jnp.maximum(dist, 0)
    max_exact = REL_BUCKETS // 2
    nf = jnp.maximum(n, 1).astype(jnp.float32)
    large = max_exact + (jnp.log(nf / max_exact) / math.log(REL_MAX_DIST / max_exact)
                         * (REL_BUCKETS - max_exact)).astype(jnp.int32)
    large = jnp.minimum(large, REL_BUCKETS - 1)
    return jnp.where(n < max_exact, n, large)


def nsa_compress(tok, pos_emb, w1, w2):
    B, S, G, dh = tok.shape
    n_cmp = (S - CMP_BLOCK) // CMP_STRIDE + 1
    idx = np.arange(n_cmp)[:, None] * CMP_STRIDE + np.arange(CMP_BLOCK)[None, :]
    blocks = tok[:, idx] + pos_emb[:, None, :]
    blocks = blocks.transpose(0, 1, 3, 2, 4).reshape(B, n_cmp, G, CMP_BLOCK * dh)
    return jax.nn.gelu(blocks @ w1) @ w2


def nsa_compressed_attention(q, k_cmp, v_cmp, tbl):
    S = q.shape[1]
    n_cmp = k_cmp.shape[1]
    block_end = jnp.arange(n_cmp) * CMP_STRIDE + CMP_BLOCK - 1
    dist = jnp.arange(S)[:, None] - block_end[None, :]
    mask = dist >= 0
    bias = tbl[:, :, t5_bucket(dist)]
    logits = jnp.einsum('bsghd,bngd->bghsn', q, k_cmp) * SCALE + bias
    p = masked_softmax(logits, mask)
    o = jnp.einsum('bghsn,bngd->bsghd', p.astype(v_cmp.dtype), v_cmp)
    return o, p


def nsa_select_blocks(p_cmp):
    S, n_cmp = p_cmp.shape[3], p_cmp.shape[4]
    n_slc = S // SLC_BLOCK
    c_start = np.arange(n_cmp) * CMP_STRIDE
    s_start = np.arange(n_slc) * SLC_BLOCK
    lo = np.maximum(s_start[:, None], c_start[None, :])
    hi = np.minimum(s_start[:, None] + SLC_BLOCK, c_start[None, :] + CMP_BLOCK)
    overlap = (np.maximum(hi - lo, 0) / CMP_BLOCK).astype(np.float32)
    imp = jnp.einsum('bghsn,jn->bgsj', p_cmp, jnp.asarray(overlap))
    blk = jnp.arange(n_slc)[None, :]
    cur = (jnp.arange(S) // SLC_BLOCK)[:, None]
    valid = blk <= cur
    forced = (blk == 0) | (blk == cur) | (blk == cur - 1)
    score = jnp.where(valid, imp + jnp.where(forced, FORCE_SCORE, 0.0), NEG)
    _, idx = lax.top_k(score, min(SLC_TOP_N, n_slc))
    return idx


def nsa_selected_attention(q, k_slc, v_slc, sel_idx, tbl):
    B, S, G, Hg, dh = q.shape
    n_slc = S // SLC_BLOCK
    k_blocks = k_slc.reshape(B, n_slc, SLC_BLOCK, G, dh).transpose(0, 3, 1, 2, 4)
    v_blocks = v_slc.reshape(B, n_slc, SLC_BLOCK, G, dh).transpose(0, 3, 1, 2, 4)
    q_g = q.transpose(0, 2, 3, 1, 4)
    b_ix = jnp.arange(B)[:, None, None, None]
    g_ix = jnp.arange(G)[None, :, None, None]
    tg = jnp.arange(G)[None, :, None, None, None, None]
    th = jnp.arange(Hg)[None, None, :, None, None, None]
    in_block = jnp.arange(SLC_BLOCK)
    n_keys = sel_idx.shape[-1] * SLC_BLOCK

    def chunk(c):
        s0 = c * SLC_Q_CHUNK
        qc = lax.dynamic_slice_in_dim(q_g, s0, SLC_Q_CHUNK, axis=3)
        ic = lax.dynamic_slice_in_dim(sel_idx, s0, SLC_Q_CHUNK, axis=2)
        kc = k_blocks[b_ix, g_ix, ic]
        vc = v_blocks[b_ix, g_ix, ic]
        tq = s0 + jnp.arange(SLC_Q_CHUNK)
        dist = tq[None, None, :, None, None] - (ic[..., None] * SLC_BLOCK + in_block)
        bias = tbl[tg, th, t5_bucket(dist)[:, :, None]]
        logits = jnp.einsum('bghqd,bgqnld->bghqnl', qc, kc) * SCALE + bias
        p = masked_softmax(logits.reshape(B, G, Hg, SLC_Q_CHUNK, n_keys),
                           (dist >= 0).reshape(B, G, 1, SLC_Q_CHUNK, n_keys))
        return jnp.einsum('bghqm,bgqmd->bghqd', p.astype(vc.dtype),
                          vc.reshape(B, G, SLC_Q_CHUNK, n_keys, dh))

    out = lax.map(chunk, jnp.arange(S // SLC_Q_CHUNK))
    return out.transpose(1, 0, 4, 2, 3, 5).reshape(B, S, G, Hg, dh)


def nsa_window_attention(q, k_win, v_win, tbl):
    B, S, G, Hg, dh = q.shape
    q_g = q.transpose(0, 2, 3, 1, 4)
    pad = ((0, 0), (WINDOW, 0), (0, 0), (0, 0))
    kp = jnp.pad(k_win, pad)
    vp = jnp.pad(v_win, pad)
    span = WINDOW + Q_BLOCK
    r = jnp.arange(span)
    dist = jnp.arange(Q_BLOCK)[:, None] - (r[None, :] - WINDOW)
    in_band = (dist >= 0) & (dist < WINDOW)
    bias = tbl[:, :, t5_bucket(dist)]

    def block(i):
        s0 = i * Q_BLOCK
        qb = lax.dynamic_slice_in_dim(q_g, s0, Q_BLOCK, axis=3)
        kb = lax.dynamic_slice_in_dim(kp, s0, span, axis=1)
        vb = lax.dynamic_slice_in_dim(vp, s0, span, axis=1)
        mask = in_band & (s0 - WINDOW + r >= 0)[None, :]
        logits = jnp.einsum('bghqd,bkgd->bghqk', qb, kb) * SCALE + bias
        p = masked_softmax(logits, mask)
        return jnp.einsum('bghqk,bkgd->bghqd', p.astype(vb.dtype), vb)

    out = lax.map(block, jnp.arange(S // Q_BLOCK))
    return out.transpose(1, 0, 4, 2, 3, 5).reshape(B, S, G, Hg, dh)


def diff_attention(q1, q2, k1, k2, v, lam, tbl):
    B, S, H, _ = q1.shape
    kpos = jnp.arange(S)

    def block(i):
        s0 = i * Q_BLOCK
        dist = (s0 + jnp.arange(Q_BLOCK))[:, None] - kpos[None, :]
        mask = dist >= 0
        bias = tbl[:, t5_bucket(dist)]
        q1b = lax.dynamic_slice_in_dim(q1, s0, Q_BLOCK, axis=1)
        q2b = lax.dynamic_slice_in_dim(q2, s0, Q_BLOCK, axis=1)
        a1 = masked_softmax(jnp.einsum('bqhd,bkhd->bhqk', q1b, k1) * SCALE + bias, mask)
        a2 = masked_softmax(jnp.einsum('bqhd,bkhd->bhqk', q2b, k2) * SCALE + bias, mask)
        attn = a1 - lam * a2
        return jnp.einsum('bhqk,bkhe->bqhe', attn.astype(v.dtype), v)

    out = lax.map(block, jnp.arange(S // Q_BLOCK))
    return out.transpose(1, 0, 2, 3, 4).reshape(B, S, H, v.shape[-1])


def mixer_layer(h, w_in, nsa_q_norm, nsa_k_norm, cmp_pos, cmp_w1, cmp_w2,
                diff_q_norm, diff_k_norm, diff_lambda, diff_out_norm, rel_bias,
                w_branch_nsa, w_branch_diff, w_out, lambda_init):
    B, S, _ = h.shape
    G, Hg, dh = NSA_KV_GROUPS, NSA_HPG, HEAD_DIM
    proj = h @ w_in
    sizes = (COL_NSA_Q, COL_NSA_KV, COL_NSA_GATE, COL_DIFF_Q, COL_DIFF_K, COL_DIFF_V)
    points, acc = [], 0
    for s in sizes:
        acc += s
        points.append(acc)
    q_nsa, kv_nsa, g_nsa, q_diff, k_diff, v_diff, g_merge = jnp.split(proj, points, axis=-1)

    tbl_nsa = rel_bias[:, :NSA_HEADS].T.reshape(G, Hg, REL_BUCKETS)
    tbl_diff = rel_bias[:, NSA_HEADS:].T

    q = rms_norm(q_nsa.reshape(B, S, G, Hg, dh), nsa_q_norm)
    kv = kv_nsa.reshape(B, S, 3, 2, G, dh)
    k_cmp = rms_norm(nsa_compress(kv[:, :, 0, 0], cmp_pos[0], cmp_w1[0], cmp_w2[0]), nsa_k_norm[0])
    v_cmp = nsa_compress(kv[:, :, 0, 1], cmp_pos[1], cmp_w1[1], cmp_w2[1])
    k_slc = rms_norm(kv[:, :, 1, 0], nsa_k_norm[1])
    v_slc = kv[:, :, 1, 1]
    k_win = rms_norm(kv[:, :, 2, 0], nsa_k_norm[2])
    v_win = kv[:, :, 2, 1]
    o_cmp, p_cmp = nsa_compressed_attention(q, k_cmp, v_cmp, tbl_nsa)
    sel_idx = nsa_select_blocks(p_cmp)
    o_slc = nsa_selected_attention(q, k_slc, v_slc, sel_idx, tbl_nsa)
    o_win = nsa_window_attention(q, k_win, v_win, tbl_nsa)
    gates = jax.nn.sigmoid(g_nsa.reshape(B, S, 3, G, Hg, 1))
    o_nsa = gates[:, :, 0] * o_cmp + gates[:, :, 1] * o_slc + gates[:, :, 2] * o_win
    o_nsa = o_nsa.reshape(B, S, NSA_HEADS * dh)

    qd = rms_norm(q_diff.reshape(B, S, DIFF_HEADS, 2, dh), diff_q_norm)
    kd = rms_norm(k_diff.reshape(B, S, DIFF_HEADS, 2, dh), diff_k_norm)
    vd = v_diff.reshape(B, S, DIFF_HEADS, DIFF_V_DIM)
    lam_f = diff_lambda.astype(jnp.float32)
    lam = (jnp.exp(jnp.sum(lam_f[0] * lam_f[1])) - jnp.exp(jnp.sum(lam_f[2] * lam_f[3]))
           + lambda_init)
    o_diff = diff_attention(qd[..., 0, :], qd[..., 1, :], kd[..., 0, :], kd[..., 1, :], vd, lam, tbl_diff)
    o_diff = rms_norm(o_diff, diff_out_norm) * (1.0 - lambda_init)
    o_diff = o_diff.reshape(B, S, DIFF_HEADS * DIFF_V_DIM)

    gm = jax.nn.sigmoid(g_merge.reshape(B, S, 2, D_MODEL))
    y = gm[:, :, 0] * (o_nsa @ w_branch_nsa) + gm[:, :, 1] * (o_diff @ w_branch_diff)
    return y @ w_out


def hier_moe(h, rg_w, rg_b, re_w, re_b, w_gate, w_up, w_down):
    B, S, D = h.shape
    hf = h.reshape(B * S, D)
    n_tok = hf.shape[0]
    g_prob = jax.nn.softmax((hf @ rg_w + rg_b).astype(jnp.float32), axis=-1)
    g_top_p, g_top = lax.top_k(g_prob, 1)
    e_logits = (hf @ re_w + re_b).astype(jnp.float32).reshape(n_tok, N_GROUPS, EXPERTS_PER_GROUP)
    e_in = jnp.take_along_axis(e_logits, g_top[:, :, None], axis=1)[:, 0]
    e_prob = jax.nn.softmax(e_in, axis=-1)
    e_top_p, e_top = lax.top_k(e_prob, TOP_K_IN_GROUP)
    e_top_p = e_top_p / jnp.sum(e_top_p, axis=-1, keepdims=True)
    weights = g_top_p * e_top_p
    expert_id = g_top * EXPERTS_PER_GROUP + e_top
    combine = jnp.sum(jax.nn.one_hot(expert_id, N_EXPERTS, dtype=jnp.float32) * weights[..., None], axis=1)
    combine = combine.astype(hf.dtype)
    hid = jax.nn.silu(jnp.einsum('nd,edf->nef', hf, w_gate)) * jnp.einsum('nd,edf->nef', hf, w_up)
    out = jnp.einsum('nef,efd->nd', hid * combine[:, :, None], w_down)
    return out.reshape(B, S, D)


def setup_inputs(seed: int = 0) -> dict:
    key = jax.random.key(seed)
    ks = jax.random.split(key, 24)
    L, dh = DEPTH, HEAD_DIM

    def nrm(k, shape, scale):
        return jax.random.normal(k, shape, jnp.float32) * scale

    def gain(k, shape):
        return 1.0 + 0.1 * jax.random.normal(k, shape, jnp.float32)

    return {
        "x": nrm(ks[0], (BATCH, SEQ, D_MODEL), 1.0),
        "norm1_g": gain(ks[1], (L, D_MODEL)),
        "w_in": nrm(ks[2], (L, D_MODEL, IN_COLS), D_MODEL ** -0.5),
        "nsa_q_norm": gain(ks[3], (L, dh)),
        "nsa_k_norm": gain(ks[4], (L, 3, dh)),
        "cmp_pos": nrm(ks[5], (L, 2, CMP_BLOCK, dh), 0.1),
        "cmp_w1": nrm(ks[6], (L, 2, CMP_BLOCK * dh, CMP_HIDDEN), (CMP_BLOCK * dh) ** -0.5),
        "cmp_w2": nrm(ks[7], (L, 2, CMP_HIDDEN, dh), CMP_HIDDEN ** -0.5),
        "diff_q_norm": gain(ks[8], (L, 2, dh)),
        "diff_k_norm": gain(ks[9], (L, 2, dh)),
        "diff_lambda": nrm(ks[10], (L, 4, dh), 0.1),
        "diff_out_norm": gain(ks[11], (L, DIFF_V_DIM)),
        "rel_bias": nrm(ks[12], (REL_BUCKETS, N_ATTN_HEADS), 0.2),
        "w_branch_nsa": nrm(ks[13], (L, NSA_HEADS * dh, D_MODEL), (NSA_HEADS * dh) ** -0.5),
        "w_branch_diff": nrm(ks[14], (L, DIFF_HEADS * DIFF_V_DIM, D_MODEL), (DIFF_HEADS * DIFF_V_DIM) ** -0.5),
        "w_out": nrm(ks[15], (L, D_MODEL, D_MODEL), D_MODEL ** -0.5),
        "norm2_g": gain(ks[16], (L, D_MODEL)),
        "router_group_w": nrm(ks[17], (L, D_MODEL, N_GROUPS), D_MODEL ** -0.5),
        "router_group_b": nrm(ks[18], (L, N_GROUPS), 0.01),
        "router_expert_w": nrm(ks[19], (L, D_MODEL, N_EXPERTS), D_MODEL ** -0.5),
        "router_expert_b": nrm(ks[20], (L, N_EXPERTS), 0.01),
        "expert_w_gate": nrm(ks[21], (L, N_EXPERTS, D_MODEL, EXPERT_HIDDEN), D_MODEL ** -0.5),
        "expert_w_up": nrm(ks[22], (L, N_EXPERTS, D_MODEL, EXPERT_HIDDEN), D_MODEL ** -0.5),
        "expert_w_down": nrm(ks[23], (L, N_EXPERTS, EXPERT_HIDDEN, D_MODEL), EXPERT_HIDDEN ** -0.5),
    }


def reference(x, norm1_g, w_in, nsa_q_norm, nsa_k_norm, cmp_pos, cmp_w1, cmp_w2,
              diff_q_norm, diff_k_norm, diff_lambda, diff_out_norm, rel_bias,
              w_branch_nsa, w_branch_diff, w_out, norm2_g, router_group_w,
              router_group_b, router_expert_w, router_expert_b, expert_w_gate,
              expert_w_up, expert_w_down):
    for layer in range(DEPTH):
        lambda_init = 0.8 - 0.6 * math.exp(-0.3 * layer)
        h = rms_norm(x, norm1_g[layer])
        x = x + mixer_layer(h, w_in[layer], nsa_q_norm[layer], nsa_k_norm[layer],
                            cmp_pos[layer], cmp_w1[layer], cmp_w2[layer],
                            diff_q_norm[layer], diff_k_norm[layer], diff_lambda[layer],
                            diff_out_norm[layer], rel_bias, w_branch_nsa[layer],
                            w_branch_diff[layer], w_out[layer], lambda_init)
        h = rms_norm(x, norm2_g[layer])
        x = x + hier_moe(h, router_group_w[layer], router_group_b[layer],
                         router_expert_w[layer], router_expert_b[layer],
                         expert_w_gate[layer], expert_w_up[layer], expert_w_down[layer])
    return x
```

```python
import functools
import math

import numpy as np
import jax
import jax.numpy as jnp
from jax import lax
from jax.experimental import pallas as pl
from jax.experimental.pallas import tpu as pltpu

F32 = jnp.float32
BF16 = jnp.bfloat16

HEAD_DIM = 64
SCALE = HEAD_DIM ** -0.5
NSA_HEADS = 8
NSA_KV_GROUPS = 2
NSA_HPG = NSA_HEADS // NSA_KV_GROUPS
CMP_BLOCK = 32
CMP_STRIDE = 16
CMP_HIDDEN = 256
SLC_BLOCK = 64
SLC_TOP_N = 16
WINDOW = 512
FORCE_SCORE = 1e4
DIFF_HEADS = 4
DIFF_V_DIM = 2 * HEAD_DIM
REL_BUCKETS = 32
REL_MAX_DIST = 128
N_GROUPS = 4
EXPERTS_PER_GROUP = 4
N_EXPERTS = N_GROUPS * EXPERTS_PER_GROUP
EXPERT_HIDDEN = 512
RMS_EPS = 1e-6
NEG = -1e30

LANE = 128
TILE = 128
N_PAIRS = 6
N_BUCKETS = N_GROUPS * N_PAIRS

C_GM = 0
C_QN = 2048
C_KV = 2560
C_CMP = 3072
C_GATE = 3328
C_QD = 3584
C_KD = 4096
C_VD = 4608
C_TOTAL = 5120

VMEM_LIMIT = 56 * 1024 * 1024


def _cparams(sem):
    return pltpu.CompilerParams(dimension_semantics=sem, vmem_limit_bytes=VMEM_LIMIT)


def _rms(x, gain):
    return x * lax.rsqrt(jnp.mean(x * x, axis=-1, keepdims=True) + RMS_EPS) * gain


def _in_proj_columns(d_model):
    dh, G, Hg = HEAD_DIM, NSA_KV_GROUPS, NSA_HPG
    o_q = 0
    o_kv = o_q + NSA_HEADS * dh
    o_g = o_kv + 3 * 2 * G * dh
    o_qd = o_g + 3 * NSA_HEADS
    o_kd = o_qd + DIFF_HEADS * 2 * dh
    o_vd = o_kd + DIFF_HEADS * 2 * dh
    o_gm = o_vd + DIFF_HEADS * DIFF_V_DIM
    cols = np.full((C_TOTAL,), -1, np.int64)
    cols[C_GM:C_GM + 2 * d_model] = o_gm + np.arange(2 * d_model)
    cols[C_QN:C_QN + NSA_HEADS * dh] = o_q + np.arange(NSA_HEADS * dh)

    def kv_col(kind, kv, g):
        return o_kv + ((kind * 2 + kv) * G + g) * dh + np.arange(dh)

    for g in range(G):
        base = C_KV + g * 4 * dh
        cols[base:base + dh] = kv_col(1, 0, g)
        cols[base + dh:base + 2 * dh] = kv_col(1, 1, g)
        cols[base + 2 * dh:base + 3 * dh] = kv_col(2, 0, g)
        cols[base + 3 * dh:base + 4 * dh] = kv_col(2, 1, g)
        base = C_CMP + g * 2 * dh
        cols[base:base + dh] = kv_col(0, 0, g)
        cols[base + dh:base + 2 * dh] = kv_col(0, 1, g)
        for kind in range(3):
            for hg in range(Hg):
                cols[C_GATE + g * LANE + kind * Hg + hg] = o_g + (kind * G + g) * Hg + hg
    n = DIFF_HEADS * 2 * dh
    cols[C_QD:C_QD + n] = o_qd + np.arange(n)
    cols[C_KD:C_KD + n] = o_kd + np.arange(n)
    cols[C_VD:C_VD + n] = o_vd + np.arange(n)
    return cols


def _t5_bucket(dist):
    n = jnp.maximum(dist, 0)
    max_exact = REL_BUCKETS // 2
    nf = jnp.maximum(n, 1).astype(F32)
    large = max_exact + (jnp.log(nf / max_exact) / math.log(REL_MAX_DIST / max_exact)
                         * (REL_BUCKETS - max_exact)).astype(jnp.int32)
    large = jnp.minimum(large, REL_BUCKETS - 1)
    return jnp.where(n < max_exact, n, large)


def _bias_tables(rel_bias, S):
    bd = rel_bias[_t5_bucket(jnp.arange(S))].T.astype(F32)
    i = np.arange(TILE)[:, None]
    j = np.arange(TILE)[None, :]
    tiles = []
    for r in range(WINDOW // TILE + 1):
        dist = r * TILE + i - j
        ok = (dist >= 0) & (dist < WINDOW)
        tiles.append(jnp.where(jnp.asarray(ok)[None], bd[:, np.clip(dist, 0, S - 1)], NEG))
    tiles = jnp.stack(tiles, axis=0)
    n_cmp = (S - CMP_BLOCK) // CMP_STRIDE + 1
    t = np.arange(S)[:, None]
    n = np.arange(LANE)[None, :]
    dist = t - (n * CMP_STRIDE + CMP_BLOCK - 1)
    ok = (dist >= 0) & (n < n_cmp)
    bias_cmp = jnp.where(jnp.asarray(ok)[None], bd[:NSA_HEADS][:, np.clip(dist, 0, S - 1)], NEG)
    return tiles, bias_cmp


def _overlap_matrix(S):
    n_cmp = (S - CMP_BLOCK) // CMP_STRIDE + 1
    n_slc = S // SLC_BLOCK
    c_start = np.arange(n_cmp) * CMP_STRIDE
    s_start = np.arange(n_slc) * SLC_BLOCK
    lo = np.maximum(s_start[:, None], c_start[None, :])
    hi = np.minimum(s_start[:, None] + SLC_BLOCK, c_start[None, :] + CMP_BLOCK)
    overlap = (np.maximum(hi - lo, 0) / CMP_BLOCK).astype(np.float32)
    out = np.zeros((LANE, n_slc), np.float32)
    out[:n_cmp] = overlap.T
    return out


def _in_proj_kernel(x_ref, g_ref, w_ref, o_ref, *, chunk):
    h = _rms(x_ref[...], g_ref[...]).astype(BF16)
    for c in range(0, C_TOTAL, chunk):
        o_ref[:, c:c + chunk] = jnp.dot(h, w_ref[:, c:c + chunk],
                                        preferred_element_type=F32).astype(BF16)


def _in_proj(x2d, gain, w):
    n, d = x2d.shape
    tm = 256
    return pl.pallas_call(
        functools.partial(_in_proj_kernel, chunk=512),
        grid=(n // tm,),
        in_specs=[pl.BlockSpec((tm, d), lambda i: (i, 0)),
                  pl.BlockSpec((1, d), lambda i: (0, 0)),
                  pl.BlockSpec((d, C_TOTAL), lambda i: (0, 0))],
        out_specs=pl.BlockSpec((tm, C_TOTAL), lambda i: (i, 0)),
        out_shape=jax.ShapeDtypeStruct((n, C_TOTAL), BF16),
        compiler_params=_cparams(("arbitrary",)),
        name="in_proj",
    )(x2d, gain, w)


def _compress_kernel(t_ref, pos_ref, w1_ref, w2_ref, kg_ref, k_ref, v_ref):
    half = CMP_STRIDE * HEAD_DIM

    def mlp(idx):
        t = t_ref[idx].astype(F32)
        ta = (t + pos_ref[idx, 0:1, :]).astype(BF16)
        tb = (t + pos_ref[idx, 1:2, :]).astype(BF16)
        a = jnp.dot(ta, w1_ref[idx, :half, :], preferred_element_type=F32)
        b = jnp.dot(tb, w1_ref[idx, half:, :], preferred_element_type=F32)
        pre = a + jnp.concatenate([b[1:], b[:1]], axis=0)
        hid = jax.nn.gelu(pre).astype(BF16)
        return jnp.dot(hid, w2_ref[idx], preferred_element_type=F32)

    k_ref[...] = _rms(mlp(0), kg_ref[...]).astype(BF16)
    v_ref[...] = mlp(1).astype(BF16)


def _compress(tok, pos, w1, w2, k_gain):
    B, G, _, m, width = tok.shape
    out = jax.ShapeDtypeStruct((B, G, m, HEAD_DIM), BF16)
    ospec = pl.BlockSpec((None, None, m, HEAD_DIM), lambda b, g: (b, g, 0, 0))
    return pl.pallas_call(
        _compress_kernel,
        grid=(B, G),
        in_specs=[pl.BlockSpec((None, None, 2, m, width), lambda b, g: (b, g, 0, 0, 0)),
                  pl.BlockSpec(pos.shape, lambda b, g: (0, 0, 0)),
                  pl.BlockSpec(w1.shape, lambda b, g: (0, 0, 0)),
                  pl.BlockSpec(w2.shape, lambda b, g: (0, 0, 0)),
                  pl.BlockSpec((1, HEAD_DIM), lambda b, g: (0, 0))],
        out_specs=[ospec, ospec],
        out_shape=[out, out],
        compiler_params=_cparams(("arbitrary", "arbitrary")),
        name="nsa_compress",
    )(tok, pos, w1, w2, k_gain)


def _online_softmax_step(s, v, m, l, acc):
    m_new = jnp.maximum(m, jnp.max(s, axis=-1, keepdims=True))
    alpha = jnp.exp(m - m_new)
    p = jnp.exp(s - m_new)
    l = alpha * l + jnp.sum(p, axis=-1, keepdims=True)
    acc = alpha * acc + jnp.dot(p.astype(BF16), v, preferred_element_type=F32)
    return m_new, l, acc


def _dot_nt(a, b):
    return lax.dot_general(a, b, (((1,), (1,)), ((), ())), preferred_element_type=F32)


def _nsa_kernel(q_ref, kv_ref, kc_ref, vc_ref, gate_ref, qg_ref, kg_ref, bc_ref, bt_ref,
                ot_ref, ex_ref, o_ref, ks_scr, kw_scr, mb_scr, *, n_tiles):
    T, dh, Hg = TILE, HEAD_DIM, NSA_HPG
    qi = pl.program_id(2)

    @pl.when(qi == 0)
    def _():
        ks_scr[...] = _rms(kv_ref[:, 0:dh].astype(F32), kg_ref[1:2, :]).astype(BF16)
        kw_scr[...] = _rms(kv_ref[:, 2 * dh:3 * dh].astype(F32), kg_ref[2:3, :]).astype(BF16)

    q = jnp.concatenate(
        [(_rms(q_ref[:, h * dh:(h + 1) * dh].astype(F32), qg_ref[...]) * SCALE).astype(BF16)
         for h in range(Hg)], axis=0)

    lc = _dot_nt(q, kc_ref[...]) + bc_ref[...].reshape(Hg * T, LANE)
    mc = jnp.max(lc, axis=-1, keepdims=True)
    pc = jnp.exp(lc - mc)
    row = lax.broadcasted_iota(jnp.int32, (Hg, T, 1), 1).reshape(Hg * T, 1)
    visible = (qi * T + row >= CMP_BLOCK - 1).astype(F32)
    pc = pc / jnp.sum(pc, axis=-1, keepdims=True) * visible
    o_cmp = jnp.dot(pc.astype(BF16), vc_ref[...], preferred_element_type=F32)
    p_sum = pc[0:T] + pc[T:2 * T] + pc[2 * T:3 * T] + pc[3 * T:4 * T]
    imp = jnp.dot(p_sum, ot_ref[...], preferred_element_type=F32,
                  precision=lax.Precision.HIGHEST)

    n_slc = imp.shape[-1]
    blk = lax.broadcasted_iota(jnp.int32, (T, n_slc), 1)
    cur = (qi * T + lax.broadcasted_iota(jnp.int32, (T, n_slc), 0)) // SLC_BLOCK
    forced = (blk == 0) | (blk == cur) | (blk == cur - 1)
    score = jnp.where(blk <= cur, imp + jnp.where(forced, FORCE_SCORE, 0.0), NEG)
    rank = jnp.zeros((T, n_slc), jnp.int32)
    for i in range(n_slc):
        col = score[:, i:i + 1]
        ahead = (col > score) | ((col == score) & (blk > i))
        rank = rank + ahead.astype(jnp.int32)
    sel_bias = jnp.where(rank < SLC_TOP_N, 0.0, NEG).astype(BF16)
    for kt in range(n_tiles):
        mb_scr[kt] = jnp.dot(sel_bias, ex_ref[:, kt * T:(kt + 1) * T], preferred_element_type=F32)

    init = (jnp.full((Hg * T, 1), NEG, F32), jnp.zeros((Hg * T, 1), F32), jnp.zeros((Hg * T, dh), F32))

    def sel_body(kt, carry):
        off = pl.multiple_of(kt * T, T)
        s = _dot_nt(q, ks_scr[pl.ds(off, T), :]).reshape(Hg, T, T)
        s = s + bt_ref[jnp.minimum(qi - kt, 2)] + mb_scr[kt][None]
        return _online_softmax_step(s.reshape(Hg * T, T), kv_ref[pl.ds(off, T), dh:2 * dh], *carry)

    _, l_s, acc_s = lax.fori_loop(0, qi + 1, sel_body, init)
    o_slc = acc_s / l_s

    def win_body(kt, carry):
        off = pl.multiple_of(kt * T, T)
        s = _dot_nt(q, kw_scr[pl.ds(off, T), :]).reshape(Hg, T, T) + bt_ref[qi - kt]
        return _online_softmax_step(s.reshape(Hg * T, T), kv_ref[pl.ds(off, T), 3 * dh:4 * dh], *carry)

    _, l_w, acc_w = lax.fori_loop(jnp.maximum(qi - WINDOW // T, 0), qi + 1, win_body, init)
    o_win = acc_w / l_w

    gates = jax.nn.sigmoid(gate_ref[...].astype(F32))
    outs = []
    for h in range(Hg):
        rows = slice(h * T, (h + 1) * T)
        outs.append(gates[:, h:h + 1] * o_cmp[rows]
                    + gates[:, Hg + h:Hg + h + 1] * o_slc[rows]
                    + gates[:, 2 * Hg + h:2 * Hg + h + 1] * o_win[rows])
    o_ref[...] = jnp.concatenate(outs, axis=1).astype(BF16)


def _nsa_attention(proj, k_cmp, v_cmp, q_gain, k_gain, bias_cmp, bias_tiles, overlap_t, expand):
    B, S, _ = proj.shape
    G, Hg, dh, T = NSA_KV_GROUPS, NSA_HPG, HEAD_DIM, TILE
    nq = S // T
    n_slc = S // SLC_BLOCK
    return pl.pallas_call(
        functools.partial(_nsa_kernel, n_tiles=nq),
        grid=(B, G, nq),
        in_specs=[
            pl.BlockSpec((None, T, Hg * dh), lambda b, g, i: (b, i, C_QN // (Hg * dh) + g)),
            pl.BlockSpec((None, S, 4 * dh), lambda b, g, i: (b, 0, C_KV // (4 * dh) + g)),
            pl.BlockSpec((None, None, LANE, dh), lambda b, g, i: (b, g, 0, 0)),
            pl.BlockSpec((None, None, LANE, dh), lambda b, g, i: (b, g, 0, 0)),
            pl.BlockSpec((None, T, LANE), lambda b, g, i: (b, i, C_GATE // LANE + g)),
            pl.BlockSpec((1, dh), lambda b, g, i: (0, 0)),
            pl.BlockSpec((3, dh), lambda b, g, i: (0, 0)),
            pl.BlockSpec((None, Hg, T, LANE), lambda b, g, i: (g, 0, i, 0)),
            pl.BlockSpec((None, WINDOW // T + 1, Hg, T, T), lambda b, g, i: (g, 0, 0, 0, 0)),
            pl.BlockSpec((LANE, n_slc), lambda b, g, i: (0, 0)),
            pl.BlockSpec((n_slc, S), lambda b, g, i: (0, 0)),
        ],
        out_specs=pl.BlockSpec((None, T, Hg * dh), lambda b, g, i: (b, i, g)),
        out_shape=jax.ShapeDtypeStruct((B, S, NSA_HEADS * dh), BF16),
        scratch_shapes=[pltpu.VMEM((S, dh), BF16), pltpu.VMEM((S, dh), BF16),
                        pltpu.VMEM((nq, T, T), F32)],
        compiler_params=_cparams(("arbitrary", "arbitrary", "arbitrary")),
        name="nsa_attention",
    )(proj, proj, k_cmp, v_cmp, proj, q_gain, k_gain, bias_cmp, bias_tiles, overlap_t, expand)


def _diff_kernel(q_ref, k_ref, v_ref, qg_ref, kg_ref, lam_ref, og_ref, bt_ref, o_ref,
                 k1_scr, k2_scr, *, lambda_init):
    T, dh = TILE, HEAD_DIM
    qi = pl.program_id(2)

    @pl.when(qi == 0)
    def _():
        k1_scr[...] = _rms(k_ref[:, 0:dh].astype(F32), kg_ref[0:1, :]).astype(BF16)
        k2_scr[...] = _rms(k_ref[:, dh:2 * dh].astype(F32), kg_ref[1:2, :]).astype(BF16)

    q1 = (_rms(q_ref[:, 0:dh].astype(F32), qg_ref[0:1, :]) * SCALE).astype(BF16)
    q2 = (_rms(q_ref[:, dh:2 * dh].astype(F32), qg_ref[1:2, :]) * SCALE).astype(BF16)
    init = (jnp.full((T, 1), NEG, F32), jnp.zeros((T, 1), F32), jnp.zeros((T, DIFF_V_DIM), F32))

    def body(kt, carry):
        c1, c2 = carry
        off = pl.multiple_of(kt * T, T)
        bias = bt_ref[jnp.minimum(qi - kt, 2)]
        v = v_ref[pl.ds(off, T), :]
        c1 = _online_softmax_step(_dot_nt(q1, k1_scr[pl.ds(off, T), :]) + bias, v, *c1)
        c2 = _online_softmax_step(_dot_nt(q2, k2_scr[pl.ds(off, T), :]) + bias, v, *c2)
        return c1, c2

    (_, l1, a1), (_, l2, a2) = lax.fori_loop(0, qi + 1, body, (init, init))
    lam_p = lam_ref[...]
    lam = (jnp.exp(jnp.sum(lam_p[0:1] * lam_p[1:2], axis=-1, keepdims=True))
           - jnp.exp(jnp.sum(lam_p[2:3] * lam_p[3:4], axis=-1, keepdims=True)) + lambda_init)
    o = a1 / l1 - lam * (a2 / l2)
    o_ref[...] = (_rms(o, og_ref[...]) * (1.0 - lambda_init)).astype(BF16)


def _diff_attention(proj, q_gain, k_gain, lam_p, out_gain, bias_tiles, lambda_init):
    B, S, _ = proj.shape
    T, dh, H = TILE, HEAD_DIM, DIFF_HEADS
    nq = S // T
    return pl.pallas_call(
        functools.partial(_diff_kernel, lambda_init=lambda_init),
        grid=(B, H, nq),
        in_specs=[
            pl.BlockSpec((None, T, 2 * dh), lambda b, h, i: (b, i, C_QD // (2 * dh) + h)),
            pl.BlockSpec((None, S, 2 * dh), lambda b, h, i: (b, 0, C_KD // (2 * dh) + h)),
            pl.BlockSpec((None, S, DIFF_V_DIM), lambda b, h, i: (b, 0, C_VD // DIFF_V_DIM + h)),
            pl.BlockSpec((2, dh), lambda b, h, i: (0, 0)),
            pl.BlockSpec((2, dh), lambda b, h, i: (0, 0)),
            pl.BlockSpec((4, dh), lambda b, h, i: (0, 0)),
            pl.BlockSpec((1, DIFF_V_DIM), lambda b, h, i: (0, 0)),
            pl.BlockSpec((None, 3, T, T), lambda b, h, i: (h, 0, 0, 0)),
        ],
        out_specs=pl.BlockSpec((None, T, DIFF_V_DIM), lambda b, h, i: (b, i, h)),
        out_shape=jax.ShapeDtypeStruct((B, S, H * DIFF_V_DIM), BF16),
        scratch_shapes=[pltpu.VMEM((S, dh), BF16), pltpu.VMEM((S, dh), BF16)],
        compiler_params=_cparams(("arbitrary", "arbitrary", "arbitrary")),
        name="diff_attention",
    )(proj, proj, proj, q_gain, k_gain, lam_p, out_gain, bias_tiles)


def _merge_kernel(x_ref, on_ref, od_ref, gm_ref, wn_ref, wd_ref, wo_ref, g2_ref, rw_ref, rb_ref,
                  x2_ref, route_ref):
    d = x_ref.shape[-1]
    a = jnp.dot(on_ref[...], wn_ref[...], preferred_element_type=F32)
    b = jnp.dot(od_ref[...], wd_ref[...], preferred_element_type=F32)
    y = (jax.nn.sigmoid(gm_ref[:, :d].astype(F32)) * a
         + jax.nn.sigmoid(gm_ref[:, d:].astype(F32)) * b)
    x2 = x_ref[...] + jnp.dot(y.astype(BF16), wo_ref[...], preferred_element_type=F32)
    x2_ref[...] = x2

    logits = jnp.dot(_rms(x2, g2_ref[...]), rw_ref[...], preferred_element_type=F32,
                     precision=lax.Precision.HIGHEST) + rb_ref[...]
    lane = lax.broadcasted_iota(jnp.int32, logits.shape, 1)
    big = jnp.int32(LANE)

    def first_argmax(vals, mask):
        vals = jnp.where(mask, vals, -jnp.inf)
        top = jnp.max(vals, axis=-1, keepdims=True)
        idx = jnp.min(jnp.where(vals == top, lane, big), axis=-1, keepdims=True)
        return top, idx

    is_group = lane < N_GROUPS
    g_max, g_top = first_argmax(logits, is_group)
    g_top_p = 1.0 / jnp.sum(jnp.where(is_group, jnp.exp(logits - g_max), 0.0), axis=-1, keepdims=True)
    lo = N_GROUPS + g_top * EXPERTS_PER_GROUP
    in_group = (lane >= lo) & (lane < lo + EXPERTS_PER_GROUP)
    m1, i1 = first_argmax(logits, in_group)
    m2, i2 = first_argmax(logits, in_group & (lane != i1))
    e2 = jnp.exp(m2 - m1)
    w1 = g_top_p / (1.0 + e2)
    w2 = g_top_p * e2 / (1.0 + e2)
    la = jnp.minimum(i1, i2) - lo
    lb = jnp.maximum(i1, i2) - lo
    pair = (la * (7 - la)) // 2 + (lb - la - 1)
    bucket = (g_top * N_PAIRS + pair).astype(F32)
    exp_lane = lane + N_GROUPS
    route = (jnp.where(exp_lane == i1, w1, 0.0) + jnp.where(exp_lane == i2, w2, 0.0)
             + jnp.where(lane == N_EXPERTS, bucket, 0.0))
    route_ref[...] = route


def _merge(x2d, o_nsa, o_diff, proj2d, wn, wd, wo, g2, rw, rb):
    n, d = x2d.shape
    tm = 256
    full = lambda a: pl.BlockSpec(a.shape, lambda i: (0,) * a.ndim)
    return pl.pallas_call(
        _merge_kernel,
        grid=(n // tm,),
        in_specs=[pl.BlockSpec((tm, d), lambda i: (i, 0)),
                  pl.BlockSpec((tm, o_nsa.shape[1]), lambda i: (i, 0)),
                  pl.BlockSpec((tm, o_diff.shape[1]), lambda i: (i, 0)),
                  pl.BlockSpec((tm, 2 * d), lambda i: (i, C_GM // (2 * d))),
                  full(wn), full(wd), full(wo), full(g2), full(rw), full(rb)],
        out_specs=[pl.BlockSpec((tm, d), lambda i: (i, 0)),
                   pl.BlockSpec((tm, LANE), lambda i: (i, 0))],
        out_shape=[jax.ShapeDtypeStruct((n, d), F32), jax.ShapeDtypeStruct((n, LANE), F32)],
        compiler_params=_cparams(("arbitrary",)),
        name="merge_router",
    )(x2d, o_nsa, o_diff, proj2d, wn, wd, wo, g2, rw, rb)


def _gather_kernel(idx_ref, *refs, rows, n_arrays):
    srcs = refs[:n_arrays]
    outs = refs[n_arrays:2 * n_arrays]
    sem = refs[2 * n_arrays]
    base = pl.program_id(0) * rows

    def copy(a, r):
        return pltpu.make_async_copy(srcs[a].at[pl.ds(idx_ref[base + r], 1), :],
                                     outs[a].at[pl.ds(r, 1), :], sem.at[a])

    def start(r, c):
        for a in range(n_arrays):
            copy(a, r).start()
        return c

    def wait(r, c):
        for a in range(n_arrays):
            copy(a, r).wait()
        return c

    lax.fori_loop(0, rows, start, 0)
    lax.fori_loop(0, rows, wait, 0)


def _gather_rows(idx, arrays):
    n = idx.shape[0]
    rows = 256
    k = len(arrays)
    return pl.pallas_call(
        functools.partial(_gather_kernel, rows=rows, n_arrays=k),
        grid_spec=pltpu.PrefetchScalarGridSpec(
            num_scalar_prefetch=1,
            grid=(n // rows,),
            in_specs=[pl.BlockSpec(memory_space=pl.ANY)] * k,
            out_specs=[pl.BlockSpec((rows, a.shape[1]), lambda i, idx_ref: (i, 0)) for a in arrays],
            scratch_shapes=[pltpu.SemaphoreType.DMA((k,))]),
        out_shape=[jax.ShapeDtypeStruct((n, a.shape[1]), a.dtype) for a in arrays],
        compiler_params=_cparams(("arbitrary",)),
        name="row_gather",
    )(idx, *arrays)


def _moe_kernel(tile_ref, ea_ref, eb_ref, lo_ref, hi_ref, first_ref,
                xs_ref, cw_ref, g2_ref, wga_ref, wua_ref, wda_ref, wgb_ref, wub_ref, wdb_ref,
                o_ref, h_scr):
    v = pl.program_id(0)

    @pl.when(first_ref[v] == 1)
    def _():
        xs = xs_ref[...]
        h_scr[...] = _rms(xs, g2_ref[...]).astype(BF16)
        o_ref[...] = xs

    h = h_scr[...]
    cw = cw_ref[...]
    lane = lax.broadcasted_iota(jnp.int32, cw.shape, 1)
    row = lax.broadcasted_iota(jnp.int32, (cw.shape[0], 1), 0)
    in_bucket = (row >= lo_ref[v]) & (row < hi_ref[v])

    def expert(e, wg_ref, wu_ref, wd_ref):
        w = jnp.sum(jnp.where(lane == e, cw, 0.0), axis=-1, keepdims=True)
        w = jnp.where(in_bucket, w, 0.0)
        gate = jnp.dot(h, wg_ref[...], preferred_element_type=F32)
        up = jnp.dot(h, wu_ref[...], preferred_element_type=F32)
        hid = jax.nn.silu(gate) * up * w
        return jnp.dot(hid.astype(BF16), wd_ref[...], preferred_element_type=F32)

    o_ref[...] += (expert(ea_ref[v], wga_ref, wua_ref, wda_ref)
                   + expert(eb_ref[v], wgb_ref, wub_ref, wdb_ref))


def _moe(sched, xs, cw, g2, wg, wu, wd, tm):
    n, d = xs.shape
    f = wg.shape[-1]
    n_visits = sched[0].shape[0]
    w_in_a = pl.BlockSpec((None, d, f), lambda v, t, ea, eb, lo, hi, fi: (ea[v], 0, 0))
    w_in_b = pl.BlockSpec((None, d, f), lambda v, t, ea, eb, lo, hi, fi: (eb[v], 0, 0))
    w_out_a = pl.BlockSpec((None, f, d), lambda v, t, ea, eb, lo, hi, fi: (ea[v], 0, 0))
    w_out_b = pl.BlockSpec((None, f, d), lambda v, t, ea, eb, lo, hi, fi: (eb[v], 0, 0))
    tile = lambda width: pl.BlockSpec((tm, width), lambda v, t, ea, eb, lo, hi, fi: (t[v], 0))
    return pl.pallas_call(
        _moe_kernel,
        grid_spec=pltpu.PrefetchScalarGridSpec(
            num_scalar_prefetch=6,
            grid=(n_visits,),
            in_specs=[tile(d), tile(LANE),
                      pl.BlockSpec((1, d), lambda v, *_: (0, 0)),
                      w_in_a, w_in_a, w_out_a, w_in_b, w_in_b, w_out_b],
            out_specs=tile(d),
            scratch_shapes=[pltpu.VMEM((tm, d), BF16)]),
        out_shape=jax.ShapeDtypeStruct((n, d), F32),
        compiler_params=_cparams(("arbitrary",)),
        name="moe_experts",
    )(*sched, xs, cw, g2, wg, wu, wd, wg, wu, wd)


def _moe_schedule(bucket, tm):
    n = bucket.shape[0]
    n_tiles = n // tm
    n_visits = n_tiles + N_BUCKETS - 1
    counts = jnp.zeros((N_BUCKETS,), jnp.int32).at[bucket].add(1)
    ends = jnp.cumsum(counts)
    starts = ends - counts
    first_tile = starts // tm
    last_tile = jnp.maximum(ends - 1, 0) // tm
    n_vis = jnp.where(counts > 0, last_tile - first_tile + 1, 0)
    vis_end = jnp.cumsum(n_vis)
    vis_start = vis_end - n_vis
    v = jnp.arange(n_visits, dtype=jnp.int32)
    b = jnp.minimum(jnp.searchsorted(vis_end, v, side="right"), N_BUCKETS - 1).astype(jnp.int32)
    live = v < vis_end[-1]
    tile = jnp.where(live, first_tile[b] + v - vis_start[b], n_tiles - 1).astype(jnp.int32)
    lo = jnp.where(live, jnp.clip(starts[b] - tile * tm, 0, tm), 0).astype(jnp.int32)
    hi = jnp.where(live, jnp.clip(ends[b] - tile * tm, 0, tm), 0).astype(jnp.int32)
    b_last = b[jnp.maximum(vis_end[-1] - 1, 0)]
    b = jnp.where(live, b, b_last)
    pair_a = np.array([0, 0, 0, 1, 1, 2], np.int32)
    pair_b = np.array([1, 2, 3, 2, 3, 3], np.int32)
    grp = b // N_PAIRS
    ea = (grp * EXPERTS_PER_GROUP + jnp.asarray(pair_a)[b % N_PAIRS]).astype(jnp.int32)
    eb = (grp * EXPERTS_PER_GROUP + jnp.asarray(pair_b)[b % N_PAIRS]).astype(jnp.int32)
    first = jnp.concatenate([jnp.ones((1,), jnp.int32), (tile[1:] != tile[:-1]).astype(jnp.int32)])
    return tile, ea, eb, lo, hi, first


def _layer(x, layer, p):
    B, S, D = x.shape
    G, dh = NSA_KV_GROUPS, HEAD_DIM
    lambda_init = 0.8 - 0.6 * math.exp(-0.3 * layer)
    x2d = x.reshape(B * S, D)

    cols = _in_proj_columns(D)
    w_in = jnp.where(jnp.asarray(cols >= 0)[None, :], p["w_in"][:, np.maximum(cols, 0)], 0.0).astype(BF16)
    proj2d = _in_proj(x2d, p["norm1_g"][None, :], w_in)
    proj = proj2d.reshape(B, S, C_TOTAL)

    m = S // CMP_STRIDE
    tok = proj[:, :, C_CMP:C_CMP + 2 * G * dh].reshape(B, m, CMP_STRIDE, G, 2, dh)
    tok = tok.transpose(0, 3, 4, 1, 2, 5).reshape(B, G, 2, m, CMP_STRIDE * dh)
    pos = p["cmp_pos"].reshape(2, 2, CMP_STRIDE * dh)
    k_cmp, v_cmp = _compress(tok, pos, p["cmp_w1"].astype(BF16), p["cmp_w2"].astype(BF16),
                             p["nsa_k_norm"][0:1])

    tiles, bias_cmp = _bias_tables(p["rel_bias"], S)
    nsa_tiles = tiles[:, :NSA_HEADS].reshape(tiles.shape[0], G, NSA_HPG, TILE, TILE).transpose(1, 0, 2, 3, 4)
    diff_tiles = tiles[:3, NSA_HEADS:].transpose(1, 0, 2, 3)
    bias_cmp = bias_cmp.reshape(G, NSA_HPG, S, LANE)
    expand = jnp.asarray(np.repeat(np.eye(S // SLC_BLOCK, dtype=np.float32), SLC_BLOCK, axis=1), BF16)
    o_nsa = _nsa_attention(proj, k_cmp, v_cmp, p["nsa_q_norm"][None, :], p["nsa_k_norm"],
                           bias_cmp, nsa_tiles, jnp.asarray(_overlap_matrix(S)), expand)
    o_diff = _diff_attention(proj, p["diff_q_norm"], p["diff_k_norm"], p["diff_lambda"],
                             p["diff_out_norm"][None, :], diff_tiles, lambda_init)

    rw = jnp.zeros((D, LANE), F32)
    rw = rw.at[:, :N_GROUPS].set(p["router_group_w"]).at[:, N_GROUPS:N_GROUPS + N_EXPERTS].set(p["router_expert_w"])
    rb = jnp.zeros((1, LANE), F32)
    rb = rb.at[0, :N_GROUPS].set(p["router_group_b"]).at[0, N_GROUPS:N_GROUPS + N_EXPERTS].set(p["router_expert_b"])
    x2, route = _merge(x2d, o_nsa.reshape(B * S, -1), o_diff.reshape(B * S, -1), proj2d,
                       p["w_branch_nsa"].astype(BF16), p["w_branch_diff"].astype(BF16),
                       p["w_out"].astype(BF16), p["norm2_g"][None, :], rw, rb)

    tm = 256
    bucket = route[:, N_EXPERTS].astype(jnp.int32)
    perm = jnp.argsort(bucket).astype(jnp.int32)
    inv_perm = jnp.zeros_like(perm).at[perm].set(jnp.arange(perm.shape[0], dtype=jnp.int32))
    sched = _moe_schedule(bucket, tm)
    xs, cw = _gather_rows(perm, [x2, route])
    ys = _moe(sched, xs, cw, p["norm2_g"][None, :], p["expert_w_gate"].astype(BF16),
              p["expert_w_up"].astype(BF16), p["expert_w_down"].astype(BF16), tm)
    (out,) = _gather_rows(inv_perm, [ys])
    return out.reshape(B, S, D)


def kernel(x, norm1_g, w_in, nsa_q_norm, nsa_k_norm, cmp_pos, cmp_w1, cmp_w2, diff_q_norm, diff_k_norm,
           diff_lambda, diff_out_norm, rel_bias, w_branch_nsa, w_branch_diff, w_out, norm2_g,
           router_group_w, router_group_b, router_expert_w, router_expert_b, expert_w_gate,
           expert_w_up, expert_w_down):
    per_layer = dict(norm1_g=norm1_g, w_in=w_in, nsa_q_norm=nsa_q_norm, nsa_k_norm=nsa_k_norm,
                     cmp_pos=cmp_pos, cmp_w1=cmp_w1, cmp_w2=cmp_w2, diff_q_norm=diff_q_norm,
                     diff_k_norm=diff_k_norm, diff_lambda=diff_lambda, diff_out_norm=diff_out_norm,
                     w_branch_nsa=w_branch_nsa, w_branch_diff=w_branch_diff, w_out=w_out,
                     norm2_g=norm2_g, router_group_w=router_group_w, router_group_b=router_group_b,
                     router_expert_w=router_expert_w, router_expert_b=router_expert_b,
                     expert_w_gate=expert_w_gate, expert_w_up=expert_w_up, expert_w_down=expert_w_down)
    for layer in range(norm1_g.shape[0]):
        p = {k: v[layer] for k, v in per_layer.items()}
        p["rel_bias"] = rel_bias
        x = _layer(x, layer, p)
    return x
```

```python
import functools
import math

import numpy as np
import jax
import jax.numpy as jnp
from jax import lax
from jax.experimental import pallas as pl
from jax.experimental.pallas import tpu as pltpu

F32 = jnp.float32
BF16 = jnp.bfloat16

HEAD_DIM = 64
SCALE = HEAD_DIM ** -0.5
NSA_HEADS = 8
NSA_KV_GROUPS = 2
NSA_HPG = NSA_HEADS // NSA_KV_GROUPS
CMP_BLOCK = 32
CMP_STRIDE = 16
CMP_HIDDEN = 256
SLC_BLOCK = 64
SLC_TOP_N = 16
WINDOW = 512
FORCE_SCORE = 1e4
DIFF_HEADS = 4
DIFF_V_DIM = 2 * HEAD_DIM
REL_BUCKETS = 32
REL_MAX_DIST = 128
N_GROUPS = 4
EXPERTS_PER_GROUP = 4
N_EXPERTS = N_GROUPS * EXPERTS_PER_GROUP
EXPERT_HIDDEN = 512
RMS_EPS = 1e-6
NEG = -1e30
LOG2E = math.log2(math.e)

LANE = 128
TQ = 256
N_PAIRS = 6
N_BUCKETS = N_GROUPS * N_PAIRS

C_GM = 0
C_QN = 2048
C_KV = 2560
C_CMP = 3072
C_GATE = 3328
C_QD = 3584
C_KD = 4096
C_VD = 4608
C_TOTAL = 5120

VMEM_LIMIT = 56 * 1024 * 1024


def _cparams(sem):
    return pltpu.CompilerParams(dimension_semantics=sem, vmem_limit_bytes=VMEM_LIMIT)


def _rms(x, gain):
    return x * lax.rsqrt(jnp.mean(x * x, axis=-1, keepdims=True) + RMS_EPS) * gain


def _reorder_w_in(w_in):
    d_model = w_in.shape[0]
    dh, G, Hg = HEAD_DIM, NSA_KV_GROUPS, NSA_HPG
    o_q = 0
    o_kv = o_q + NSA_HEADS * dh
    o_g = o_kv + 3 * 2 * G * dh
    o_qd = o_g + 3 * NSA_HEADS
    o_gm = o_qd + 3 * DIFF_HEADS * 2 * dh

    def cols(start, n):
        return w_in[:, start:start + n]

    def kv(kind, which, g):
        return cols(o_kv + ((kind * 2 + which) * G + g) * dh, dh)

    parts = [cols(o_gm, 2 * d_model), cols(o_q, NSA_HEADS * dh)]
    parts += [kv(kind, which, g) for g in range(G) for kind in (1, 2) for which in (0, 1)]
    parts += [kv(0, which, g) for g in range(G) for which in (0, 1)]
    for g in range(G):
        parts += [cols(o_g + (kind * G + g) * Hg, Hg) for kind in range(3)]
        parts.append(jnp.zeros((d_model, LANE - 3 * Hg), w_in.dtype))
    parts.append(cols(o_qd, 3 * DIFF_HEADS * 2 * dh))
    w = jnp.concatenate(parts, axis=1)
    assert w.shape[1] == C_TOTAL
    return w.astype(BF16)


def _t5_bucket(dist):
    n = jnp.maximum(dist, 0)
    max_exact = REL_BUCKETS // 2
    nf = jnp.maximum(n, 1).astype(F32)
    large = max_exact + (jnp.log(nf / max_exact) / math.log(REL_MAX_DIST / max_exact)
                         * (REL_BUCKETS - max_exact)).astype(jnp.int32)
    large = jnp.minimum(large, REL_BUCKETS - 1)
    return jnp.where(n < max_exact, n, large)


def _skew(v_ext, rows, step=1):
    length = v_ext.shape[-1]
    lead = v_ext.shape[:-1]
    flat = jnp.tile(v_ext, (1,) * len(lead) + (rows,))[..., :rows * (length - step)]
    return flat.reshape(lead + (rows, length - step))


def _bias_tables(rel_bias, S):
    heads = rel_bias.shape[1]
    bd = rel_bias[_t5_bucket(jnp.arange(S))].T.astype(F32) * LOG2E
    rel = bd - bd[:, REL_MAX_DIST:REL_MAX_DIST + 1]
    neg = lambda n: jnp.full((heads, n), NEG, F32)
    zero = lambda n: jnp.zeros((heads, n), F32)
    rev = lambda a: a[:, ::-1]
    T = TQ
    v1 = jnp.concatenate([rev(rel[:, 0:T + 1]), neg(T), zero(T - 1)], axis=1)
    v0 = jnp.concatenate([rel[:, 0:1], neg(2 * T), rev(rel[:, 1:T])], axis=1)
    near = jnp.stack([_skew(v0, T)[..., :2 * T], _skew(v1, T)[..., :2 * T]], axis=1)
    n_cmp = (S - CMP_BLOCK) // CMP_STRIDE + 1
    first = CMP_BLOCK - 1
    length = 2 * S + CMP_STRIDE
    hb = bd[:NSA_HEADS]
    v = jnp.concatenate([jnp.full((NSA_HEADS, first), NEG, F32), hb[:, :S - first],
                         jnp.full((NSA_HEADS, length - S), NEG, F32)], axis=1)
    bt = _skew(v, LANE, CMP_STRIDE)[..., :S]
    bt = jnp.where((np.arange(LANE) < n_cmp)[None, :, None], bt, NEG)
    return near, bt.transpose(0, 2, 1)


def _overlap_matrix(S):
    n_cmp = (S - CMP_BLOCK) // CMP_STRIDE + 1
    n_slc = S // SLC_BLOCK
    c_start = np.arange(n_cmp) * CMP_STRIDE
    s_start = np.arange(n_slc) * SLC_BLOCK
    lo = np.maximum(s_start[:, None], c_start[None, :])
    hi = np.minimum(s_start[:, None] + SLC_BLOCK, c_start[None, :] + CMP_BLOCK)
    overlap = (np.maximum(hi - lo, 0) / CMP_BLOCK).astype(np.float32)
    out = np.zeros((n_slc, LANE), np.float32)
    out[:, :n_cmp] = overlap
    return out


def _in_proj_kernel(x_ref, g_ref, w_ref, o_ref, *, chunk):
    h = _rms(x_ref[...], g_ref[...]).astype(BF16)
    for c in range(0, C_TOTAL, chunk):
        o_ref[:, c:c + chunk] = jnp.dot(h, w_ref[:, c:c + chunk],
                                        preferred_element_type=F32).astype(BF16)


def _in_proj(x2d, gain, w):
    n, d = x2d.shape
    tm = 256
    return pl.pallas_call(
        functools.partial(_in_proj_kernel, chunk=512),
        grid=(n // tm,),
        in_specs=[pl.BlockSpec((tm, d), lambda i: (i, 0)),
                  pl.BlockSpec((1, d), lambda i: (0, 0)),
                  pl.BlockSpec((d, C_TOTAL), lambda i: (0, 0))],
        out_specs=pl.BlockSpec((tm, C_TOTAL), lambda i: (i, 0)),
        out_shape=jax.ShapeDtypeStruct((n, C_TOTAL), BF16),
        compiler_params=_cparams(("arbitrary",)),
        name="in_proj",
    )(x2d, gain, w)


def _compress_kernel(t_ref, pos_ref, w1_ref, w2_ref, kg_ref, k_ref, v_ref):
    half = CMP_STRIDE * HEAD_DIM

    def mlp(idx):
        t = t_ref[idx].astype(F32)
        ta = (t + pos_ref[idx, 0:1, :]).astype(BF16)
        tb = (t + pos_ref[idx, 1:2, :]).astype(BF16)
        a = jnp.dot(ta, w1_ref[idx, :half, :], preferred_element_type=F32)
        b = jnp.dot(tb, w1_ref[idx, half:, :], preferred_element_type=F32)
        pre = a + jnp.concatenate([b[1:], b[:1]], axis=0)
        hid = jax.nn.gelu(pre).astype(BF16)
        return jnp.dot(hid, w2_ref[idx], preferred_element_type=F32)

    k_ref[...] = _rms(mlp(0), kg_ref[...]).astype(BF16)
    v_ref[...] = mlp(1).astype(BF16)


def _compress(tok, pos, w1, w2, k_gain):
    B, G, _, m, width = tok.shape
    out = jax.ShapeDtypeStruct((B, G, m, HEAD_DIM), BF16)
    ospec = pl.BlockSpec((None, None, m, HEAD_DIM), lambda b, g: (b, g, 0, 0))
    return pl.pallas_call(
        _compress_kernel,
        grid=(B, G),
        in_specs=[pl.BlockSpec((None, None, 2, m, width), lambda b, g: (b, g, 0, 0, 0)),
                  pl.BlockSpec(pos.shape, lambda b, g: (0, 0, 0)),
                  pl.BlockSpec(w1.shape, lambda b, g: (0, 0, 0)),
                  pl.BlockSpec(w2.shape, lambda b, g: (0, 0, 0)),
                  pl.BlockSpec((1, HEAD_DIM), lambda b, g: (0, 0))],
        out_specs=[ospec, ospec],
        out_shape=[out, out],
        compiler_params=_cparams(("arbitrary", "arbitrary")),
        name="nsa_compress",
    )(tok, pos, w1, w2, k_gain)


def _dot_nt(a, b):
    return lax.dot_general(a, b, (((1,), (1,)), ((), ())), preferred_element_type=F32)


def _softmax_step(s, v, carry):
    m, l, acc = carry
    m_new = jnp.maximum(m, jnp.max(s, axis=-1, keepdims=True))
    alpha = jnp.exp2(m - m_new)
    p = jnp.exp2(s - m_new)
    l = alpha * l + jnp.sum(p, axis=-1, keepdims=True)
    acc = alpha * acc + jnp.dot(p.astype(BF16), v, preferred_element_type=F32)
    return m_new, l, acc


def _softmax_init(rows, dv):
    return (jnp.full((rows, 1), NEG, F32), jnp.zeros((rows, 1), F32), jnp.zeros((rows, dv), F32))


def _far_schedule(qi):
    n_far = jnp.maximum(qi - 1, 0)
    return n_far % 2, n_far // 2


def _q_prep(x, gain):
    return (_rms(x.astype(F32), gain) * (SCALE * LOG2E)).astype(BF16)


def _nsa_kernel(q_ref, kv_ref, kc_ref, vc_ref, gate_ref, qg_ref, kg_ref, bc_ref, nb_ref, wm_ref,
                ov_ref, ex_ref, o_ref, ks_scr, kw_scr, mb_scr, *, seq_len):
    T, dh, Hg = TQ, HEAD_DIM, NSA_HPG
    R = Hg * T
    qi = pl.program_id(2)
    n_slc = seq_len // SLC_BLOCK

    @pl.when(qi == 0)
    def _():
        ks_scr[...] = _rms(kv_ref[:, 0:dh].astype(F32), kg_ref[1:2, :]).astype(BF16)
        kw_scr[...] = _rms(kv_ref[:, 2 * dh:3 * dh].astype(F32), kg_ref[2:3, :]).astype(BF16)

    q = jnp.concatenate([_q_prep(q_ref[:, h * dh:(h + 1) * dh], qg_ref[...]) for h in range(Hg)],
                        axis=0)

    lc = _dot_nt(q, kc_ref[...]) + bc_ref[...].reshape(R, LANE)
    pc = jnp.exp2(lc - jnp.max(lc, axis=-1, keepdims=True))
    row = lax.broadcasted_iota(jnp.int32, (Hg, T, 1), 1).reshape(R, 1)
    visible = (qi * T + row >= CMP_BLOCK - 1).astype(F32)
    pc = pc * (visible / jnp.sum(pc, axis=-1, keepdims=True))
    o_cmp = jnp.dot(pc.astype(BF16), vc_ref[...], preferred_element_type=F32)

    @pl.when((qi + 1) * T <= SLC_TOP_N * SLC_BLOCK)
    def _():
        mb_scr[...] = jnp.zeros_like(mb_scr)

    @pl.when((qi + 1) * T > SLC_TOP_N * SLC_BLOCK)
    def _():
        p_sum = pc[0:T] + pc[T:2 * T] + pc[2 * T:3 * T] + pc[3 * T:4 * T]
        imp = lax.dot_general(ov_ref[...], p_sum, (((1,), (1,)), ((), ())), preferred_element_type=F32,
                              precision=lax.Precision.HIGHEST)
        blk = lax.broadcasted_iota(jnp.int32, (n_slc, T), 0)
        cur = (qi * T + lax.broadcasted_iota(jnp.int32, (n_slc, T), 1)) // SLC_BLOCK
        forced = (blk == 0) | (blk == cur) | (blk == cur - 1)
        score = jnp.where(blk <= cur, imp + jnp.where(forced, FORCE_SCORE, 0.0), NEG)
        rank = jnp.zeros((n_slc, T), jnp.int32)
        for i in range(n_slc):
            other = score[i:i + 1, :]
            ahead = (other > score) | ((other == score) & (blk > i))
            rank = rank + ahead.astype(jnp.int32)
        sel_bias = jnp.where(rank < SLC_TOP_N, 0.0, NEG).T.astype(BF16)
        mb_scr[...] = jnp.dot(sel_bias, ex_ref[...], preferred_element_type=F32)

    near_start = pl.multiple_of(jnp.maximum(qi - 1, 0) * T, T)
    near_bias = nb_ref[jnp.minimum(qi, 1)]
    n_single, n_pair = _far_schedule(qi)

    def sel_step(start, width, bias, carry):
        s = _dot_nt(q, ks_scr[pl.ds(start, width), :]).reshape(Hg, T, width)
        mask = mb_scr[:, pl.ds(start, width)][None]
        s = s + (mask if bias is None else bias + mask)
        return _softmax_step(s.reshape(R, width), kv_ref[pl.ds(start, width), dh:2 * dh], carry)

    carry = sel_step(near_start, 2 * T, near_bias, _softmax_init(R, dh))
    carry = lax.fori_loop(0, n_single, lambda j, c: sel_step(0, T, None, c), carry)
    carry = lax.fori_loop(
        0, n_pair,
        lambda j, c: sel_step(pl.multiple_of((n_single + 2 * j) * T, T), 2 * T, None, c), carry)
    o_slc = carry[2] / carry[1]

    def win_step(start, width, bias, carry):
        s = _dot_nt(q, kw_scr[pl.ds(start, width), :]).reshape(Hg, T, width) + bias
        return _softmax_step(s.reshape(R, width), kv_ref[pl.ds(start, width), 3 * dh:4 * dh], carry)

    carry = win_step(near_start, 2 * T, near_bias, _softmax_init(R, dh))
    carry = lax.fori_loop(
        0, (qi >= 2).astype(jnp.int32),
        lambda j, c: win_step(pl.multiple_of((qi - 2) * T, T), T, wm_ref[...][None], c), carry)
    o_win = carry[2] / carry[1]

    gates = jax.nn.sigmoid(gate_ref[...].astype(F32))
    outs = []
    for h in range(Hg):
        rows = slice(h * T, (h + 1) * T)
        outs.append(gates[:, h:h + 1] * o_cmp[rows]
                    + gates[:, Hg + h:Hg + h + 1] * o_slc[rows]
                    + gates[:, 2 * Hg + h:2 * Hg + h + 1] * o_win[rows])
    o_ref[...] = jnp.concatenate(outs, axis=1).astype(BF16)


def _nsa_attention(proj, k_cmp, v_cmp, q_gain, k_gain, bias_cmp, near_bias, win_mask, overlap, expand):
    B, S, _ = proj.shape
    G, Hg, dh, T = NSA_KV_GROUPS, NSA_HPG, HEAD_DIM, TQ
    assert WINDOW == 2 * T and S % (2 * T) == 0
    n_slc = S // SLC_BLOCK
    return pl.pallas_call(
        functools.partial(_nsa_kernel, seq_len=S),
        grid=(B, G, S // T),
        in_specs=[
            pl.BlockSpec((None, T, Hg * dh), lambda b, g, i: (b, i, C_QN // (Hg * dh) + g)),
            pl.BlockSpec((None, S, 4 * dh), lambda b, g, i: (b, 0, C_KV // (4 * dh) + g)),
            pl.BlockSpec((None, None, LANE, dh), lambda b, g, i: (b, g, 0, 0)),
            pl.BlockSpec((None, None, LANE, dh), lambda b, g, i: (b, g, 0, 0)),
            pl.BlockSpec((None, T, LANE), lambda b, g, i: (b, i, C_GATE // LANE + g)),
            pl.BlockSpec((1, dh), lambda b, g, i: (0, 0)),
            pl.BlockSpec((3, dh), lambda b, g, i: (0, 0)),
            pl.BlockSpec((None, Hg, T, LANE), lambda b, g, i: (g, 0, i, 0)),
            pl.BlockSpec((None, 2, Hg, T, 2 * T), lambda b, g, i: (g, 0, 0, 0, 0)),
            pl.BlockSpec((T, T), lambda b, g, i: (0, 0)),
            pl.BlockSpec((n_slc, LANE), lambda b, g, i: (0, 0)),
            pl.BlockSpec((n_slc, S), lambda b, g, i: (0, 0)),
        ],
        out_specs=pl.BlockSpec((None, T, Hg * dh), lambda b, g, i: (b, i, g)),
        out_shape=jax.ShapeDtypeStruct((B, S, NSA_HEADS * dh), BF16),
        scratch_shapes=[pltpu.VMEM((S, dh), BF16), pltpu.VMEM((S, dh), BF16),
                        pltpu.VMEM((T, S), F32)],
        compiler_params=_cparams(("arbitrary", "arbitrary", "arbitrary")),
        name="nsa_attention",
    )(proj, proj, k_cmp, v_cmp, proj, q_gain, k_gain, bias_cmp, near_bias, win_mask, overlap, expand)


def _diff_kernel(q_ref, k_ref, v_ref, qg_ref, kg_ref, lam_ref, og_ref, nb_ref, o_ref,
                 k1_scr, k2_scr, *, lambda_init):
    T, dh = TQ, HEAD_DIM
    qi = pl.program_id(2)

    @pl.when(qi == 0)
    def _():
        k1_scr[...] = _rms(k_ref[:, 0:dh].astype(F32), kg_ref[0:1, :]).astype(BF16)
        k2_scr[...] = _rms(k_ref[:, dh:2 * dh].astype(F32), kg_ref[1:2, :]).astype(BF16)

    q1 = _q_prep(q_ref[:, 0:dh], qg_ref[0:1, :])
    q2 = _q_prep(q_ref[:, dh:2 * dh], qg_ref[1:2, :])

    def step(start, width, bias, carry):
        c1, c2 = carry
        v = v_ref[pl.ds(start, width), :]
        s1 = _dot_nt(q1, k1_scr[pl.ds(start, width), :])
        s2 = _dot_nt(q2, k2_scr[pl.ds(start, width), :])
        if bias is not None:
            s1, s2 = s1 + bias, s2 + bias
        return _softmax_step(s1, v, c1), _softmax_step(s2, v, c2)

    init = _softmax_init(T, DIFF_V_DIM)
    n_single, n_pair = _far_schedule(qi)
    carry = step(pl.multiple_of(jnp.maximum(qi - 1, 0) * T, T), 2 * T, nb_ref[jnp.minimum(qi, 1)],
                 (init, init))
    carry = lax.fori_loop(0, n_single, lambda j, c: step(0, T, None, c), carry)
    carry = lax.fori_loop(
        0, n_pair, lambda j, c: step(pl.multiple_of((n_single + 2 * j) * T, T), 2 * T, None, c), carry)
    (_, l1, a1), (_, l2, a2) = carry

    lam_p = lam_ref[...]
    lam = (jnp.exp(jnp.sum(lam_p[0:1] * lam_p[1:2], axis=-1, keepdims=True))
           - jnp.exp(jnp.sum(lam_p[2:3] * lam_p[3:4], axis=-1, keepdims=True)) + lambda_init)
    o = a1 / l1 - lam * (a2 / l2)
    o_ref[...] = (_rms(o, og_ref[...]) * (1.0 - lambda_init)).astype(BF16)


def _diff_attention(proj, q_gain, k_gain, lam_p, out_gain, near_bias, lambda_init):
    B, S, _ = proj.shape
    T, dh, H = TQ, HEAD_DIM, DIFF_HEADS
    return pl.pallas_call(
        functools.partial(_diff_kernel, lambda_init=lambda_init),
        grid=(B, H, S // T),
        in_specs=[
            pl.BlockSpec((None, T, 2 * dh), lambda b, h, i: (b, i, C_QD // (2 * dh) + h)),
            pl.BlockSpec((None, S, 2 * dh), lambda b, h, i: (b, 0, C_KD // (2 * dh) + h)),
            pl.BlockSpec((None, S, DIFF_V_DIM), lambda b, h, i: (b, 0, C_VD // DIFF_V_DIM + h)),
            pl.BlockSpec((2, dh), lambda b, h, i: (0, 0)),
            pl.BlockSpec((2, dh), lambda b, h, i: (0, 0)),
            pl.BlockSpec((4, dh), lambda b, h, i: (0, 0)),
            pl.BlockSpec((1, DIFF_V_DIM), lambda b, h, i: (0, 0)),
            pl.BlockSpec((None, 2, T, 2 * T), lambda b, h, i: (h, 0, 0, 0)),
        ],
        out_specs=pl.BlockSpec((None, T, DIFF_V_DIM), lambda b, h, i: (b, i, h)),
        out_shape=jax.ShapeDtypeStruct((B, S, H * DIFF_V_DIM), BF16),
        scratch_shapes=[pltpu.VMEM((S, dh), BF16), pltpu.VMEM((S, dh), BF16)],
        compiler_params=_cparams(("arbitrary", "arbitrary", "arbitrary")),
        name="diff_attention",
    )(proj, proj, proj, q_gain, k_gain, lam_p, out_gain, near_bias)


def _merge_kernel(x_ref, on_ref, od_ref, gm_ref, wn_ref, wd_ref, wo_ref, g2_ref, rw_ref, rb_ref,
                  x2_ref, route_ref):
    d = x_ref.shape[-1]
    a = jnp.dot(on_ref[...], wn_ref[...], preferred_element_type=F32)
    b = jnp.dot(od_ref[...], wd_ref[...], preferred_element_type=F32)
    y = (jax.nn.sigmoid(gm_ref[:, :d].astype(F32)) * a
         + jax.nn.sigmoid(gm_ref[:, d:].astype(F32)) * b)
    x2 = x_ref[...] + jnp.dot(y.astype(BF16), wo_ref[...], preferred_element_type=F32)
    x2_ref[...] = x2

    logits = jnp.dot(_rms(x2, g2_ref[...]), rw_ref[...], preferred_element_type=F32,
                     precision=lax.Precision.HIGHEST) + rb_ref[...]
    lane = lax.broadcasted_iota(jnp.int32, logits.shape, 1)
    big = jnp.int32(LANE)

    def first_argmax(vals, mask):
        vals = jnp.where(mask, vals, -jnp.inf)
        top = jnp.max(vals, axis=-1, keepdims=True)
        idx = jnp.min(jnp.where(vals == top, lane, big), axis=-1, keepdims=True)
        return top, idx

    is_group = lane < N_GROUPS
    g_max, g_top = first_argmax(logits, is_group)
    g_top_p = 1.0 / jnp.sum(jnp.where(is_group, jnp.exp(logits - g_max), 0.0), axis=-1, keepdims=True)
    lo = N_GROUPS + g_top * EXPERTS_PER_GROUP
    in_group = (lane >= lo) & (lane < lo + EXPERTS_PER_GROUP)
    m1, i1 = first_argmax(logits, in_group)
    m2, i2 = first_argmax(logits, in_group & (lane != i1))
    e2 = jnp.exp(m2 - m1)
    w1 = g_top_p / (1.0 + e2)
    w2 = g_top_p * e2 / (1.0 + e2)
    la = jnp.minimum(i1, i2) - lo
    lb = jnp.maximum(i1, i2) - lo
    pair = jnp.right_shift(la * (7 - la), 1) + (lb - la - 1)
    bucket = (g_top * N_PAIRS + pair).astype(F32)
    exp_lane = lane + N_GROUPS
    route = (jnp.where(exp_lane == i1, w1, 0.0) + jnp.where(exp_lane == i2, w2, 0.0)
             + jnp.where(lane == N_EXPERTS, bucket, 0.0))
    route_ref[...] = route


def _merge(x2d, o_nsa, o_diff, proj2d, wn, wd, wo, g2, rw, rb):
    n, d = x2d.shape
    tm = 256
    full = lambda a: pl.BlockSpec(a.shape, lambda i: (0,) * a.ndim)
    return pl.pallas_call(
        _merge_kernel,
        grid=(n // tm,),
        in_specs=[pl.BlockSpec((tm, d), lambda i: (i, 0)),
                  pl.BlockSpec((tm, o_nsa.shape[1]), lambda i: (i, 0)),
                  pl.BlockSpec((tm, o_diff.shape[1]), lambda i: (i, 0)),
                  pl.BlockSpec((tm, 2 * d), lambda i: (i, C_GM // (2 * d))),
                  full(wn), full(wd), full(wo), full(g2), full(rw), full(rb)],
        out_specs=[pl.BlockSpec((tm, d), lambda i: (i, 0)),
                   pl.BlockSpec((tm, LANE), lambda i: (i, 0))],
        out_shape=[jax.ShapeDtypeStruct((n, d), F32), jax.ShapeDtypeStruct((n, LANE), F32)],
        compiler_params=_cparams(("arbitrary",)),
        name="merge_router",
    )(x2d, o_nsa, o_diff, proj2d, wn, wd, wo, g2, rw, rb)


def _gather_kernel(idx_ref, *refs, rows, n_arrays):
    srcs = refs[:n_arrays]
    outs = refs[n_arrays:2 * n_arrays]
    sem = refs[2 * n_arrays]
    base = pl.program_id(0) * rows

    def copy(a, r):
        return pltpu.make_async_copy(srcs[a].at[pl.ds(idx_ref[base + r], 1), :],
                                     outs[a].at[pl.ds(r, 1), :], sem.at[a])

    def start(r, c):
        for a in range(n_arrays):
            copy(a, r).start()
        return c

    def wait(r, c):
        for a in range(n_arrays):
            copy(a, r).wait()
        return c

    lax.fori_loop(0, rows, start, 0)
    lax.fori_loop(0, rows, wait, 0)


def _gather_rows(idx, arrays):
    n = idx.shape[0]
    rows = 256
    k = len(arrays)
    return pl.pallas_call(
        functools.partial(_gather_kernel, rows=rows, n_arrays=k),
        grid_spec=pltpu.PrefetchScalarGridSpec(
            num_scalar_prefetch=1,
            grid=(n // rows,),
            in_specs=[pl.BlockSpec(memory_space=pl.ANY)] * k,
            out_specs=[pl.BlockSpec((rows, a.shape[1]), lambda i, idx_ref: (i, 0)) for a in arrays],
            scratch_shapes=[pltpu.SemaphoreType.DMA((k,))]),
        out_shape=[jax.ShapeDtypeStruct((n, a.shape[1]), a.dtype) for a in arrays],
        compiler_params=_cparams(("arbitrary",)),
        name="row_gather",
    )(idx, *arrays)


def _moe_kernel(tile_ref, ea_ref, eb_ref, lo_ref, hi_ref, first_ref,
                xs_ref, cw_ref, g2_ref, wga_ref, wua_ref, wda_ref, wgb_ref, wub_ref, wdb_ref,
                o_ref, h_scr):
    v = pl.program_id(0)

    @pl.when(first_ref[v] == 1)
    def _():
        xs = xs_ref[...]
        h_scr[...] = _rms(xs, g2_ref[...]).astype(BF16)
        o_ref[...] = xs

    h = h_scr[...]
    cw = cw_ref[...]
    lane = lax.broadcasted_iota(jnp.int32, cw.shape, 1)
    row = lax.broadcasted_iota(jnp.int32, (cw.shape[0], 1), 0)
    in_bucket = (row >= lo_ref[v]) & (row < hi_ref[v])

    def expert(e, wg_ref, wu_ref, wd_ref):
        w = jnp.sum(jnp.where(lane == e, cw, 0.0), axis=-1, keepdims=True)
        w = jnp.where(in_bucket, w, 0.0)
        gate = jnp.dot(h, wg_ref[...], preferred_element_type=F32)
        up = jnp.dot(h, wu_ref[...], preferred_element_type=F32)
        hid = jax.nn.silu(gate) * up * w
        return jnp.dot(hid.astype(BF16), wd_ref[...], preferred_element_type=F32)

    o_ref[...] += (expert(ea_ref[v], wga_ref, wua_ref, wda_ref)
                   + expert(eb_ref[v], wgb_ref, wub_ref, wdb_ref))


def _moe(sched, xs, cw, g2, wg, wu, wd, tm):
    n, d = xs.shape
    f = wg.shape[-1]
    n_visits = sched[0].shape[0]
    w_in_a = pl.BlockSpec((None, d, f), lambda v, t, ea, eb, lo, hi, fi: (ea[v], 0, 0))
    w_in_b = pl.BlockSpec((None, d, f), lambda v, t, ea, eb, lo, hi, fi: (eb[v], 0, 0))
    w_out_a = pl.BlockSpec((None, f, d), lambda v, t, ea, eb, lo, hi, fi: (ea[v], 0, 0))
    w_out_b = pl.BlockSpec((None, f, d), lambda v, t, ea, eb, lo, hi, fi: (eb[v], 0, 0))
    tile = lambda width: pl.BlockSpec((tm, width), lambda v, t, ea, eb, lo, hi, fi: (t[v], 0))
    return pl.pallas_call(
        _moe_kernel,
        grid_spec=pltpu.PrefetchScalarGridSpec(
            num_scalar_prefetch=6,
            grid=(n_visits,),
            in_specs=[tile(d), tile(LANE),
                      pl.BlockSpec((1, d), lambda v, *_: (0, 0)),
                      w_in_a, w_in_a, w_out_a, w_in_b, w_in_b, w_out_b],
            out_specs=tile(d),
            scratch_shapes=[pltpu.VMEM((tm, d), BF16)]),
        out_shape=jax.ShapeDtypeStruct((n, d), F32),
        compiler_params=_cparams(("arbitrary",)),
        name="moe_experts",
    )(*sched, xs, cw, g2, wg, wu, wd, wg, wu, wd)


def _moe_schedule(bucket, tm):
    n = bucket.shape[0]
    n_tiles = n // tm
    n_visits = n_tiles + N_BUCKETS - 1
    counts = jnp.zeros((N_BUCKETS,), jnp.int32).at[bucket].add(1)
    ends = jnp.cumsum(counts)
    starts = ends - counts
    first_tile = starts // tm
    last_tile = jnp.maximum(ends - 1, 0) // tm
    n_vis = jnp.where(counts > 0, last_tile - first_tile + 1, 0)
    vis_end = jnp.cumsum(n_vis)
    vis_start = vis_end - n_vis
    v = jnp.arange(n_visits, dtype=jnp.int32)
    b = jnp.minimum(jnp.searchsorted(vis_end, v, side="right"), N_BUCKETS - 1).astype(jnp.int32)
    live = v < vis_end[-1]
    tile = jnp.where(live, first_tile[b] + v - vis_start[b], n_tiles - 1).astype(jnp.int32)
    lo = jnp.where(live, jnp.clip(starts[b] - tile * tm, 0, tm), 0).astype(jnp.int32)
    hi = jnp.where(live, jnp.clip(ends[b] - tile * tm, 0, tm), 0).astype(jnp.int32)
    b_last = b[jnp.maximum(vis_end[-1] - 1, 0)]
    b = jnp.where(live, b, b_last)
    pair_a = np.array([0, 0, 0, 1, 1, 2], np.int32)
    pair_b = np.array([1, 2, 3, 2, 3, 3], np.int32)
    grp = b // N_PAIRS
    ea = (grp * EXPERTS_PER_GROUP + jnp.asarray(pair_a)[b % N_PAIRS]).astype(jnp.int32)
    eb = (grp * EXPERTS_PER_GROUP + jnp.asarray(pair_b)[b % N_PAIRS]).astype(jnp.int32)
    first = jnp.concatenate([jnp.ones((1,), jnp.int32), (tile[1:] != tile[:-1]).astype(jnp.int32)])
    return tile, ea, eb, lo, hi, first


def _layer(x, layer, p):
    B, S, D = x.shape
    G, dh = NSA_KV_GROUPS, HEAD_DIM
    lambda_init = 0.8 - 0.6 * math.exp(-0.3 * layer)
    x2d = x.reshape(B * S, D)

    proj2d = _in_proj(x2d, p["norm1_g"][None, :], _reorder_w_in(p["w_in"]))
    proj = proj2d.reshape(B, S, C_TOTAL)

    m = S // CMP_STRIDE
    tok = proj[:, :, C_CMP:C_CMP + 2 * G * dh].reshape(B, m, CMP_STRIDE, G, 2, dh)
    tok = tok.transpose(0, 3, 4, 1, 2, 5).reshape(B, G, 2, m, CMP_STRIDE * dh)
    pos = p["cmp_pos"].reshape(2, 2, CMP_STRIDE * dh)
    k_cmp, v_cmp = _compress(tok, pos, p["cmp_w1"].astype(BF16), p["cmp_w2"].astype(BF16),
                             p["nsa_k_norm"][0:1])

    near, bias_cmp = _bias_tables(p["rel_bias"], S)
    nsa_near = near[:NSA_HEADS].reshape(G, NSA_HPG, 2, TQ, 2 * TQ).transpose(0, 2, 1, 3, 4)
    i = np.arange(TQ)
    win_mask = jnp.asarray(np.where(i[None, :] > i[:, None], 0.0, NEG).astype(np.float32))
    expand = jnp.asarray(np.repeat(np.eye(S // SLC_BLOCK, dtype=np.float32), SLC_BLOCK, axis=1), BF16)
    o_nsa = _nsa_attention(proj, k_cmp, v_cmp, p["nsa_q_norm"][None, :], p["nsa_k_norm"],
                           bias_cmp.reshape(G, NSA_HPG, S, LANE), nsa_near, win_mask,
                           jnp.asarray(_overlap_matrix(S)), expand)
    o_diff = _diff_attention(proj, p["diff_q_norm"], p["diff_k_norm"], p["diff_lambda"],
                             p["diff_out_norm"][None, :], near[NSA_HEADS:], lambda_init)

    rw = jnp.concatenate([p["router_group_w"], p["router_expert_w"],
                          jnp.zeros((D, LANE - N_GROUPS - N_EXPERTS), F32)], axis=1)
    rb = jnp.concatenate([p["router_group_b"], p["router_expert_b"],
                          jnp.zeros((LANE - N_GROUPS - N_EXPERTS,), F32)])[None, :]
    x2, route = _merge(x2d, o_nsa.reshape(B * S, -1), o_diff.reshape(B * S, -1), proj2d,
                       p["w_branch_nsa"].astype(BF16), p["w_branch_diff"].astype(BF16),
                       p["w_out"].astype(BF16), p["norm2_g"][None, :], rw, rb)

    tm = 256
    bucket = route[:, N_EXPERTS].astype(jnp.int32)
    perm = jnp.argsort(bucket).astype(jnp.int32)
    inv_perm = jnp.zeros_like(perm).at[perm].set(jnp.arange(perm.shape[0], dtype=jnp.int32))
    sched = _moe_schedule(bucket, tm)
    xs, cw = _gather_rows(perm, [x2, route])
    ys = _moe(sched, xs, cw, p["norm2_g"][None, :], p["expert_w_gate"].astype(BF16),
              p["expert_w_up"].astype(BF16), p["expert_w_down"].astype(BF16), tm)
    (out,) = _gather_rows(inv_perm, [ys])
    return out.reshape(B, S, D)


def kernel(x, norm1_g, w_in, nsa_q_norm, nsa_k_norm, cmp_pos, cmp_w1, cmp_w2, diff_q_norm, diff_k_norm,
           diff_lambda, diff_out_norm, rel_bias, w_branch_nsa, w_branch_diff, w_out, norm2_g,
           router_group_w, router_group_b, router_expert_w, router_expert_b, expert_w_gate,
           expert_w_up, expert_w_down):
    per_layer = dict(norm1_g=norm1_g, w_in=w_in, nsa_q_norm=nsa_q_norm, nsa_k_norm=nsa_k_norm,
                     cmp_pos=cmp_pos, cmp_w1=cmp_w1, cmp_w2=cmp_w2, diff_q_norm=diff_q_norm,
                     diff_k_norm=diff_k_norm, diff_lambda=diff_lambda, diff_out_norm=diff_out_norm,
                     w_branch_nsa=w_branch_nsa, w_branch_diff=w_branch_diff, w_out=w_out,
                     norm2_g=norm2_g, router_group_w=router_group_w, router_group_b=router_group_b,
                     router_expert_w=router_expert_w, router_expert_b=router_expert_b,
                     expert_w_gate=expert_w_gate, expert_w_up=expert_w_up, expert_w_down=expert_w_down)
    for layer in range(norm1_g.shape[0]):
        p = {k: v[layer] for k, v in per_layer.items()}
        p["rel_bias"] = rel_bias
        x = _layer(x, layer, p)
    return x
```

```python
import functools
import math

import numpy as np
import jax
import jax.numpy as jnp
from jax import lax
from jax.experimental import pallas as pl
from jax.experimental.pallas import tpu as pltpu

F32 = jnp.float32
BF16 = jnp.bfloat16

HEAD_DIM = 64
SCALE = HEAD_DIM ** -0.5
NSA_HEADS = 8
NSA_KV_GROUPS = 2
NSA_HPG = NSA_HEADS // NSA_KV_GROUPS
CMP_BLOCK = 32
CMP_STRIDE = 16
CMP_HIDDEN = 256
SLC_BLOCK = 64
SLC_TOP_N = 16
WINDOW = 512
FORCE_SCORE = 1e4
DIFF_HEADS = 4
DIFF_V_DIM = 2 * HEAD_DIM
REL_BUCKETS = 32
REL_MAX_DIST = 128
N_GROUPS = 4
EXPERTS_PER_GROUP = 4
N_EXPERTS = N_GROUPS * EXPERTS_PER_GROUP
EXPERT_HIDDEN = 512
RMS_EPS = 1e-6
NEG = -1e30
LOG2E = math.log2(math.e)

LANE = 128
TQ = 256
N_PAIRS = 6
N_BUCKETS = N_GROUPS * N_PAIRS

C_GM = 0
C_QN = 2048
C_KV = 2560
C_CMP = 3072
C_GATE = 3328
C_QD = 3584
C_KD = 4096
C_VD = 4608
C_TOTAL = 5120

VMEM_LIMIT = 56 * 1024 * 1024


def _cparams(sem):
    return pltpu.CompilerParams(dimension_semantics=sem, vmem_limit_bytes=VMEM_LIMIT)


def _rms(x, gain):
    return x * lax.rsqrt(jnp.mean(x * x, axis=-1, keepdims=True) + RMS_EPS) * gain


def _reorder_w_in(w_in):
    d_model = w_in.shape[0]
    dh, G, Hg = HEAD_DIM, NSA_KV_GROUPS, NSA_HPG
    o_q = 0
    o_kv = o_q + NSA_HEADS * dh
    o_g = o_kv + 3 * 2 * G * dh
    o_qd = o_g + 3 * NSA_HEADS
    o_gm = o_qd + 3 * DIFF_HEADS * 2 * dh

    def cols(start, n):
        return w_in[:, start:start + n]

    def kv(kind, which, g):
        return cols(o_kv + ((kind * 2 + which) * G + g) * dh, dh)

    parts = [cols(o_gm, 2 * d_model), cols(o_q, NSA_HEADS * dh)]
    parts += [kv(kind, which, g) for g in range(G) for kind in (1, 2) for which in (0, 1)]
    parts += [kv(0, which, g) for g in range(G) for which in (0, 1)]
    for g in range(G):
        parts += [cols(o_g + (kind * G + g) * Hg, Hg) for kind in range(3)]
        parts.append(jnp.zeros((d_model, LANE - 3 * Hg), w_in.dtype))
    parts.append(cols(o_qd, 3 * DIFF_HEADS * 2 * dh))
    w = jnp.concatenate(parts, axis=1)
    assert w.shape[1] == C_TOTAL
    return w.astype(BF16)


def _t5_bucket(dist):
    n = jnp.maximum(dist, 0)
    max_exact = REL_BUCKETS // 2
    nf = jnp.maximum(n, 1).astype(F32)
    large = max_exact + (jnp.log(nf / max_exact) / math.log(REL_MAX_DIST / max_exact)
                         * (REL_BUCKETS - max_exact)).astype(jnp.int32)
    large = jnp.minimum(large, REL_BUCKETS - 1)
    return jnp.where(n < max_exact, n, large)


def _toeplitz(v_ext, rows, cols):
    length = v_ext.shape[-1]
    lead = v_ext.shape[:-1]
    flat = jnp.broadcast_to(v_ext[..., None, :], lead + (rows, length)).reshape(lead + (rows * length,))
    return flat[..., :rows * (length - 1)].reshape(lead + (rows, length - 1))[..., :cols]


def _bias_tables(rel_bias, S):
    heads = rel_bias.shape[1]
    far = REL_MAX_DIST
    assert TQ == 2 * far and LANE == far
    bd = rel_bias[_t5_bucket(jnp.arange(2 * far))].T.astype(F32) * LOG2E
    rel = bd - bd[:, far:far + 1]
    neg = lambda *shape: jnp.full((heads,) + shape, NEG, F32)
    zero = lambda *shape: jnp.zeros((heads,) + shape, F32)
    rev = lambda a: a[:, ::-1]
    d0 = _toeplitz(jnp.concatenate([rel[:, 0:1], neg(far), rev(rel[:, 1:far])], axis=1), far, far)
    d1 = _toeplitz(jnp.concatenate([rev(rel[:, 1:far + 1]), zero(far)], axis=1), far, far)
    z, n = zero(far, far), neg(far, far)
    blocks = lambda rows: jnp.concatenate([jnp.concatenate(r, axis=2) for r in rows], axis=1)
    near = jnp.stack([blocks([[d0, n, n, n], [d1, d0, n, n]]),
                      blocks([[z, d1, d0, n], [z, z, d1, d0]])], axis=1)
    hb = rel[:NSA_HEADS]
    zero8 = lambda *shape: jnp.zeros((NSA_HEADS,) + shape, F32)
    neg8 = lambda *shape: jnp.full((NSA_HEADS,) + shape, NEG, F32)
    width = TQ + far - 1
    tp = _toeplitz(jnp.concatenate([rev(hb[:, 0:far]), neg8(TQ + 1), zero8(TQ - 1)], axis=1), TQ, width)
    n_tiles = S // TQ
    per_tile = TQ // CMP_STRIDE
    col0 = per_tile * (n_tiles - 1)
    shift = CMP_STRIDE * col0 - (CMP_BLOCK - 1)
    first = -(-(shift - (far - 1)) // CMP_STRIDE)
    last = (shift + TQ - 1) // CMP_STRIDE
    band = tp[:, :, CMP_STRIDE * first - shift + far - 1::CMP_STRIDE][:, :, :last - first + 1]
    g2 = jnp.concatenate([zero8(TQ, first), band, neg8(TQ, 2 * TQ - last - 1)], axis=2)
    bias_cmp = jnp.stack([g2[:, :, col0 - per_tile * i:col0 - per_tile * i + LANE] for i in range(n_tiles)],
                         axis=1)
    n_cmp = (S - CMP_BLOCK) // CMP_STRIDE + 1
    bias_cmp = jnp.where(np.arange(LANE) < n_cmp, bias_cmp, NEG)
    return near, bias_cmp.reshape(NSA_HEADS, S, LANE)


def _overlap_matrix(S):
    n_cmp = (S - CMP_BLOCK) // CMP_STRIDE + 1
    n_slc = S // SLC_BLOCK
    c_start = np.arange(n_cmp) * CMP_STRIDE
    s_start = np.arange(n_slc) * SLC_BLOCK
    lo = np.maximum(s_start[:, None], c_start[None, :])
    hi = np.minimum(s_start[:, None] + SLC_BLOCK, c_start[None, :] + CMP_BLOCK)
    overlap = (np.maximum(hi - lo, 0) / CMP_BLOCK).astype(np.float32)
    out = np.zeros((n_slc, LANE), np.float32)
    out[:, :n_cmp] = overlap
    return out


def _in_proj_kernel(x_ref, g_ref, w_ref, o_ref, *, chunk):
    h = _rms(x_ref[...], g_ref[...]).astype(BF16)
    for c in range(0, C_TOTAL, chunk):
        o_ref[:, c:c + chunk] = jnp.dot(h, w_ref[:, c:c + chunk],
                                        preferred_element_type=F32).astype(BF16)


def _in_proj(x2d, gain, w):
    n, d = x2d.shape
    tm = 256
    return pl.pallas_call(
        functools.partial(_in_proj_kernel, chunk=512),
        grid=(n // tm,),
        in_specs=[pl.BlockSpec((tm, d), lambda i: (i, 0)),
                  pl.BlockSpec((1, d), lambda i: (0, 0)),
                  pl.BlockSpec((d, C_TOTAL), lambda i: (0, 0))],
        out_specs=pl.BlockSpec((tm, C_TOTAL), lambda i: (i, 0)),
        out_shape=jax.ShapeDtypeStruct((n, C_TOTAL), BF16),
        compiler_params=_cparams(("arbitrary",)),
        name="in_proj",
    )(x2d, gain, w)


def _compress_kernel(t_ref, pos_ref, w1_ref, w2_ref, kg_ref, k_ref, v_ref, tok_scr):
    S, dh = t_ref.shape[0], HEAD_DIM
    m = S // CMP_STRIDE
    tok_scr[0:S, :] = t_ref[...].astype(F32)
    tok_scr[S:, :] = jnp.zeros((CMP_STRIDE, LANE), F32)
    pre = jnp.zeros((m, 2 * CMP_HIDDEN), F32)
    for l in range(CMP_BLOCK):
        x = tok_scr[pl.ds(l, m, stride=CMP_STRIDE), :] + pos_ref[l:l + 1, :]
        pre = pre + jnp.dot(x.astype(BF16), w1_ref[l], preferred_element_type=F32)
    out = jnp.dot(jax.nn.gelu(pre).astype(BF16), w2_ref[...], preferred_element_type=F32)
    k_ref[...] = _rms(out[:, :dh], kg_ref[...]).astype(BF16)
    v_ref[...] = out[:, dh:].astype(BF16)


def _compress(proj, pos, w1, w2, k_gain):
    B, S, _ = proj.shape
    G, m = NSA_KV_GROUPS, S // CMP_STRIDE
    out = jax.ShapeDtypeStruct((B, G, m, HEAD_DIM), BF16)
    ospec = pl.BlockSpec((None, None, m, HEAD_DIM), lambda b, g: (b, g, 0, 0))
    return pl.pallas_call(
        _compress_kernel,
        grid=(B, G),
        in_specs=[pl.BlockSpec((None, S, LANE), lambda b, g: (b, 0, C_CMP // LANE + g)),
                  pl.BlockSpec(pos.shape, lambda b, g: (0, 0)),
                  pl.BlockSpec(w1.shape, lambda b, g: (0, 0, 0)),
                  pl.BlockSpec(w2.shape, lambda b, g: (0, 0)),
                  pl.BlockSpec((1, HEAD_DIM), lambda b, g: (0, 0))],
        out_specs=[ospec, ospec],
        out_shape=[out, out],
        scratch_shapes=[pltpu.VMEM((S + CMP_STRIDE, LANE), F32)],
        compiler_params=_cparams(("arbitrary", "arbitrary")),
        name="nsa_compress",
    )(proj, pos, w1, w2, k_gain)


def _compress_params(cmp_pos, cmp_w1, cmp_w2):
    dh, hid = HEAD_DIM, CMP_HIDDEN
    w1 = cmp_w1.reshape(2, CMP_BLOCK, dh, hid)
    z1 = jnp.zeros((CMP_BLOCK, dh, hid), cmp_w1.dtype)
    w1 = jnp.concatenate([jnp.concatenate([w1[0], z1], axis=2),
                          jnp.concatenate([z1, w1[1]], axis=2)], axis=1)
    z2 = jnp.zeros((hid, dh), cmp_w2.dtype)
    w2 = jnp.concatenate([jnp.concatenate([cmp_w2[0], z2], axis=1),
                          jnp.concatenate([z2, cmp_w2[1]], axis=1)], axis=0)
    pos = jnp.concatenate([cmp_pos[0], cmp_pos[1]], axis=1)
    return pos, w1.astype(BF16), w2.astype(BF16)


def _dot_nt(a, b):
    return lax.dot_general(a, b, (((1,), (1,)), ((), ())), preferred_element_type=F32)


def _softmax_step(s, v, carry):
    m, l, acc = carry
    m_new = jnp.maximum(m, jnp.max(s, axis=-1, keepdims=True))
    alpha = jnp.exp2(m - m_new)
    p = jnp.exp2(s - m_new)
    l = alpha * l + jnp.sum(p, axis=-1, keepdims=True)
    acc = alpha * acc + jnp.dot(p.astype(BF16), v, preferred_element_type=F32)
    return m_new, l, acc


def _softmax_init(rows, dv):
    return (jnp.full((rows, 1), NEG, F32), jnp.zeros((rows, 1), F32), jnp.zeros((rows, dv), F32))


def _far_schedule(qi):
    n_far = jnp.maximum(qi - 1, 0)
    return n_far % 2, n_far // 2


def _q_prep(x, gain):
    return (_rms(x.astype(F32), gain) * (SCALE * LOG2E)).astype(BF16)


def _nsa_kernel(q_ref, kv_ref, kc_ref, vc_ref, gate_ref, qg_ref, kg_ref, bc_ref, nb_ref, wm_ref,
                ov_ref, ex_ref, o_ref, ks_scr, kw_scr, mb_scr, *, seq_len):
    T, dh, Hg = TQ, HEAD_DIM, NSA_HPG
    R = Hg * T
    qi = pl.program_id(2)
    n_slc = seq_len // SLC_BLOCK

    @pl.when(qi == 0)
    def _():
        ks_scr[...] = _rms(kv_ref[:, 0:dh].astype(F32), kg_ref[1:2, :]).astype(BF16)
        kw_scr[...] = _rms(kv_ref[:, 2 * dh:3 * dh].astype(F32), kg_ref[2:3, :]).astype(BF16)

    q = jnp.concatenate([_q_prep(q_ref[:, h * dh:(h + 1) * dh], qg_ref[...]) for h in range(Hg)],
                        axis=0)

    lc = _dot_nt(q, kc_ref[...]) + bc_ref[...].reshape(R, LANE)
    pc = jnp.exp2(lc - jnp.max(lc, axis=-1, keepdims=True))
    row = lax.broadcasted_iota(jnp.int32, (Hg, T, 1), 1).reshape(R, 1)
    visible = (qi * T + row >= CMP_BLOCK - 1).astype(F32)
    pc = pc * (visible / jnp.sum(pc, axis=-1, keepdims=True))
    o_cmp = jnp.dot(pc.astype(BF16), vc_ref[...], preferred_element_type=F32)

    @pl.when((qi + 1) * T <= SLC_TOP_N * SLC_BLOCK)
    def _():
        mb_scr[...] = jnp.zeros_like(mb_scr)

    @pl.when((qi + 1) * T > SLC_TOP_N * SLC_BLOCK)
    def _():
        p_sum = pc[0:T] + pc[T:2 * T] + pc[2 * T:3 * T] + pc[3 * T:4 * T]
        imp = lax.dot_general(ov_ref[...], p_sum, (((1,), (1,)), ((), ())), preferred_element_type=F32,
                              precision=lax.Precision.HIGHEST)
        blk = lax.broadcasted_iota(jnp.int32, (n_slc, T), 0)
        cur = (qi * T + lax.broadcasted_iota(jnp.int32, (n_slc, T), 1)) // SLC_BLOCK
        forced = (blk == 0) | (blk == cur) | (blk == cur - 1)
        score = jnp.where(blk <= cur, imp + jnp.where(forced, FORCE_SCORE, 0.0), NEG)
        rank = jnp.zeros((n_slc, T), jnp.int32)
        for i in range(n_slc):
            other = score[i:i + 1, :]
            ahead = (other > score) | ((other == score) & (blk > i))
            rank = rank + ahead.astype(jnp.int32)
        sel_bias = jnp.where(rank < SLC_TOP_N, 0.0, NEG).T.astype(BF16)
        mb_scr[...] = jnp.dot(sel_bias, ex_ref[...], preferred_element_type=F32)

    near_start = pl.multiple_of(jnp.maximum(qi - 1, 0) * T, T)
    near_bias = nb_ref[jnp.minimum(qi, 1)]
    n_single, n_pair = _far_schedule(qi)

    def sel_step(start, width, bias, carry):
        s = _dot_nt(q, ks_scr[pl.ds(start, width), :]).reshape(Hg, T, width)
        mask = mb_scr[:, pl.ds(start, width)][None]
        s = s + (mask if bias is None else bias + mask)
        return _softmax_step(s.reshape(R, width), kv_ref[pl.ds(start, width), dh:2 * dh], carry)

    carry = sel_step(near_start, 2 * T, near_bias, _softmax_init(R, dh))
    carry = lax.fori_loop(0, n_single, lambda j, c: sel_step(0, T, None, c), carry)
    carry = lax.fori_loop(
        0, n_pair,
        lambda j, c: sel_step(pl.multiple_of((n_single + 2 * j) * T, T), 2 * T, None, c), carry)
    o_slc = carry[2] / carry[1]

    def win_step(start, width, bias, carry):
        s = _dot_nt(q, kw_scr[pl.ds(start, width), :]).reshape(Hg, T, width) + bias
        return _softmax_step(s.reshape(R, width), kv_ref[pl.ds(start, width), 3 * dh:4 * dh], carry)

    carry = win_step(near_start, 2 * T, near_bias, _softmax_init(R, dh))
    carry = lax.fori_loop(
        0, (qi >= 2).astype(jnp.int32),
        lambda j, c: win_step(pl.multiple_of((qi - 2) * T, T), T, wm_ref[...][None], c), carry)
    o_win = carry[2] / carry[1]

    gates = jax.nn.sigmoid(gate_ref[...].astype(F32))
    outs = []
    for h in range(Hg):
        rows = slice(h * T, (h + 1) * T)
        outs.append(gates[:, h:h + 1] * o_cmp[rows]
                    + gates[:, Hg + h:Hg + h + 1] * o_slc[rows]
                    + gates[:, 2 * Hg + h:2 * Hg + h + 1] * o_win[rows])
    o_ref[...] = jnp.concatenate(outs, axis=1).astype(BF16)


def _nsa_attention(proj, k_cmp, v_cmp, q_gain, k_gain, bias_cmp, near_bias, win_mask, overlap, expand):
    B, S, _ = proj.shape
    G, Hg, dh, T = NSA_KV_GROUPS, NSA_HPG, HEAD_DIM, TQ
    assert WINDOW == 2 * T and S % (2 * T) == 0
    n_slc = S // SLC_BLOCK
    return pl.pallas_call(
        functools.partial(_nsa_kernel, seq_len=S),
        grid=(B, G, S // T),
        in_specs=[
            pl.BlockSpec((None, T, Hg * dh), lambda b, g, i: (b, i, C_QN // (Hg * dh) + g)),
            pl.BlockSpec((None, S, 4 * dh), lambda b, g, i: (b, 0, C_KV // (4 * dh) + g)),
            pl.BlockSpec((None, None, LANE, dh), lambda b, g, i: (b, g, 0, 0)),
            pl.BlockSpec((None, None, LANE, dh), lambda b, g, i: (b, g, 0, 0)),
            pl.BlockSpec((None, T, LANE), lambda b, g, i: (b, i, C_GATE // LANE + g)),
            pl.BlockSpec((1, dh), lambda b, g, i: (0, 0)),
            pl.BlockSpec((3, dh), lambda b, g, i: (0, 0)),
            pl.BlockSpec((None, Hg, T, LANE), lambda b, g, i: (g, 0, i, 0)),
            pl.BlockSpec((None, 2, Hg, T, 2 * T), lambda b, g, i: (g, 0, 0, 0, 0)),
            pl.BlockSpec((T, T), lambda b, g, i: (0, 0)),
            pl.BlockSpec((n_slc, LANE), lambda b, g, i: (0, 0)),
            pl.BlockSpec((n_slc, S), lambda b, g, i: (0, 0)),
        ],
        out_specs=pl.BlockSpec((None, T, Hg * dh), lambda b, g, i: (b, i, g)),
        out_shape=jax.ShapeDtypeStruct((B, S, NSA_HEADS * dh), BF16),
        scratch_shapes=[pltpu.VMEM((S, dh), BF16), pltpu.VMEM((S, dh), BF16),
                        pltpu.VMEM((T, S), F32)],
        compiler_params=_cparams(("arbitrary", "arbitrary", "arbitrary")),
        name="nsa_attention",
    )(proj, proj, k_cmp, v_cmp, proj, q_gain, k_gain, bias_cmp, near_bias, win_mask, overlap, expand)


def _diff_kernel(q_ref, k_ref, v_ref, qg_ref, kg_ref, lam_ref, og_ref, nb_ref, o_ref,
                 k1_scr, k2_scr, *, lambda_init):
    T, dh = TQ, HEAD_DIM
    qi = pl.program_id(2)

    @pl.when(qi == 0)
    def _():
        k1_scr[...] = _rms(k_ref[:, 0:dh].astype(F32), kg_ref[0:1, :]).astype(BF16)
        k2_scr[...] = _rms(k_ref[:, dh:2 * dh].astype(F32), kg_ref[1:2, :]).astype(BF16)

    q1 = _q_prep(q_ref[:, 0:dh], qg_ref[0:1, :])
    q2 = _q_prep(q_ref[:, dh:2 * dh], qg_ref[1:2, :])

    def step(start, width, bias, carry):
        c1, c2 = carry
        v = v_ref[pl.ds(start, width), :]
        s1 = _dot_nt(q1, k1_scr[pl.ds(start, width), :])
        s2 = _dot_nt(q2, k2_scr[pl.ds(start, width), :])
        if bias is not None:
            s1, s2 = s1 + bias, s2 + bias
        return _softmax_step(s1, v, c1), _softmax_step(s2, v, c2)

    init = _softmax_init(T, DIFF_V_DIM)
    n_single, n_pair = _far_schedule(qi)
    carry = step(pl.multiple_of(jnp.maximum(qi - 1, 0) * T, T), 2 * T, nb_ref[jnp.minimum(qi, 1)],
                 (init, init))
    carry = lax.fori_loop(0, n_single, lambda j, c: step(0, T, None, c), carry)
    carry = lax.fori_loop(
        0, n_pair, lambda j, c: step(pl.multiple_of((n_single + 2 * j) * T, T), 2 * T, None, c), carry)
    (_, l1, a1), (_, l2, a2) = carry

    lam_p = lam_ref[...]
    lam = (jnp.exp(jnp.sum(lam_p[0:1] * lam_p[1:2], axis=-1, keepdims=True))
           - jnp.exp(jnp.sum(lam_p[2:3] * lam_p[3:4], axis=-1, keepdims=True)) + lambda_init)
    o = a1 / l1 - lam * (a2 / l2)
    o_ref[...] = (_rms(o, og_ref[...]) * (1.0 - lambda_init)).astype(BF16)


def _diff_attention(proj, q_gain, k_gain, lam_p, out_gain, near_bias, lambda_init):
    B, S, _ = proj.shape
    T, dh, H = TQ, HEAD_DIM, DIFF_HEADS
    return pl.pallas_call(
        functools.partial(_diff_kernel, lambda_init=lambda_init),
        grid=(B, H, S // T),
        in_specs=[
            pl.BlockSpec((None, T, 2 * dh), lambda b, h, i: (b, i, C_QD // (2 * dh) + h)),
            pl.BlockSpec((None, S, 2 * dh), lambda b, h, i: (b, 0, C_KD // (2 * dh) + h)),
            pl.BlockSpec((None, S, DIFF_V_DIM), lambda b, h, i: (b, 0, C_VD // DIFF_V_DIM + h)),
            pl.BlockSpec((2, dh), lambda b, h, i: (0, 0)),
            pl.BlockSpec((2, dh), lambda b, h, i: (0, 0)),
            pl.BlockSpec((4, dh), lambda b, h, i: (0, 0)),
            pl.BlockSpec((1, DIFF_V_DIM), lambda b, h, i: (0, 0)),
            pl.BlockSpec((None, 2, T, 2 * T), lambda b, h, i: (h, 0, 0, 0)),
        ],
        out_specs=pl.BlockSpec((None, T, DIFF_V_DIM), lambda b, h, i: (b, i, h)),
        out_shape=jax.ShapeDtypeStruct((B, S, H * DIFF_V_DIM), BF16),
        scratch_shapes=[pltpu.VMEM((S, dh), BF16), pltpu.VMEM((S, dh), BF16)],
        compiler_params=_cparams(("arbitrary", "arbitrary", "arbitrary")),
        name="diff_attention",
    )(proj, proj, proj, q_gain, k_gain, lam_p, out_gain, near_bias)


def _merge_kernel(x_ref, on_ref, od_ref, gm_ref, wn_ref, wd_ref, wo_ref, g2_ref, rwh_ref, rwl_ref, rb_ref,
                  x2_ref, route_ref):
    d = x_ref.shape[-1]
    a = jnp.dot(on_ref[...], wn_ref[...], preferred_element_type=F32)
    b = jnp.dot(od_ref[...], wd_ref[...], preferred_element_type=F32)
    y = (jax.nn.sigmoid(gm_ref[:, :d].astype(F32)) * a
         + jax.nn.sigmoid(gm_ref[:, d:].astype(F32)) * b)
    x2 = x_ref[...] + jnp.dot(y.astype(BF16), wo_ref[...], preferred_element_type=F32)
    x2_ref[...] = x2

    h2 = _rms(x2, g2_ref[...])
    h_hi = h2.astype(BF16)
    h_lo = (h2 - h_hi.astype(F32)).astype(BF16)
    logits = (jnp.dot(h_hi, rwh_ref[...], preferred_element_type=F32)
              + jnp.dot(h_lo, rwh_ref[...], preferred_element_type=F32)
              + jnp.dot(h_hi, rwl_ref[...], preferred_element_type=F32)) + rb_ref[...]
    lane = lax.broadcasted_iota(jnp.int32, logits.shape, 1)
    big = jnp.int32(LANE)

    def first_argmax(vals, mask):
        vals = jnp.where(mask, vals, -jnp.inf)
        top = jnp.max(vals, axis=-1, keepdims=True)
        idx = jnp.min(jnp.where(vals == top, lane, big), axis=-1, keepdims=True)
        return top, idx

    is_group = lane < N_GROUPS
    g_max, g_top = first_argmax(logits, is_group)
    g_top_p = 1.0 / jnp.sum(jnp.where(is_group, jnp.exp(logits - g_max), 0.0), axis=-1, keepdims=True)
    lo = N_GROUPS + g_top * EXPERTS_PER_GROUP
    in_group = (lane >= lo) & (lane < lo + EXPERTS_PER_GROUP)
    m1, i1 = first_argmax(logits, in_group)
    m2, i2 = first_argmax(logits, in_group & (lane != i1))
    e2 = jnp.exp(m2 - m1)
    w1 = g_top_p / (1.0 + e2)
    w2 = g_top_p * e2 / (1.0 + e2)
    la = jnp.minimum(i1, i2) - lo
    lb = jnp.maximum(i1, i2) - lo
    pair = jnp.right_shift(la * (7 - la), 1) + (lb - la - 1)
    bucket = (g_top * N_PAIRS + pair).astype(F32)
    exp_lane = lane + N_GROUPS
    route = (jnp.where(exp_lane == i1, w1, 0.0) + jnp.where(exp_lane == i2, w2, 0.0)
             + jnp.where(lane == N_EXPERTS, bucket, 0.0))
    route_ref[...] = route


def _merge(x2d, o_nsa, o_diff, proj2d, wn, wd, wo, g2, rw, rb):
    n, d = x2d.shape
    tm = 512
    rw_hi = rw.astype(BF16)
    rw_lo = (rw - rw_hi.astype(F32)).astype(BF16)
    full = lambda a: pl.BlockSpec(a.shape, lambda i: (0,) * a.ndim)
    return pl.pallas_call(
        _merge_kernel,
        grid=(n // tm,),
        in_specs=[pl.BlockSpec((tm, d), lambda i: (i, 0)),
                  pl.BlockSpec((tm, o_nsa.shape[1]), lambda i: (i, 0)),
                  pl.BlockSpec((tm, o_diff.shape[1]), lambda i: (i, 0)),
                  pl.BlockSpec((tm, 2 * d), lambda i: (i, C_GM // (2 * d))),
                  full(wn), full(wd), full(wo), full(g2), full(rw_hi), full(rw_lo), full(rb)],
        out_specs=[pl.BlockSpec((tm, d), lambda i: (i, 0)),
                   pl.BlockSpec((tm, LANE), lambda i: (i, 0))],
        out_shape=[jax.ShapeDtypeStruct((n, d), F32), jax.ShapeDtypeStruct((n, LANE), F32)],
        compiler_params=_cparams(("arbitrary",)),
        name="merge_router",
    )(x2d, o_nsa, o_diff, proj2d, wn, wd, wo, g2, rw_hi, rw_lo, rb)


def _gather_kernel(idx_ref, *refs, rows, n_arrays):
    srcs = refs[:n_arrays]
    outs = refs[n_arrays:2 * n_arrays]
    sem = refs[2 * n_arrays]
    base = pl.program_id(0) * rows

    def copy(a, r):
        return pltpu.make_async_copy(srcs[a].at[pl.ds(idx_ref[base + r], 1), :],
                                     outs[a].at[pl.ds(r, 1), :], sem.at[a])

    def start(r, c):
        for a in range(n_arrays):
            copy(a, r).start()
        return c

    def wait(r, c):
        for a in range(n_arrays):
            copy(a, r).wait()
        return c

    lax.fori_loop(0, rows, start, 0, unroll=8)
    lax.fori_loop(0, rows, wait, 0, unroll=8)


def _gather_rows(idx, arrays):
    n = idx.shape[0]
    rows = 256
    k = len(arrays)
    return pl.pallas_call(
        functools.partial(_gather_kernel, rows=rows, n_arrays=k),
        grid_spec=pltpu.PrefetchScalarGridSpec(
            num_scalar_prefetch=1,
            grid=(n // rows,),
            in_specs=[pl.BlockSpec(memory_space=pl.ANY)] * k,
            out_specs=[pl.BlockSpec((rows, a.shape[1]), lambda i, idx_ref: (i, 0)) for a in arrays],
            scratch_shapes=[pltpu.SemaphoreType.DMA((k,))]),
        out_shape=[jax.ShapeDtypeStruct((n, a.shape[1]), a.dtype) for a in arrays],
        compiler_params=_cparams(("arbitrary",)),
        name="row_gather",
    )(idx, *arrays)


def _moe_kernel(tile_ref, ea_ref, eb_ref, lo_ref, hi_ref, first_ref,
                xs_ref, cw_ref, g2_ref, wga_ref, wua_ref, wda_ref, wgb_ref, wub_ref, wdb_ref,
                o_ref, h_scr):
    v = pl.program_id(0)

    @pl.when(first_ref[v] == 1)
    def _():
        xs = xs_ref[...]
        h_scr[...] = _rms(xs, g2_ref[...]).astype(BF16)
        o_ref[...] = xs

    h = h_scr[...]
    cw = cw_ref[...]
    lane = lax.broadcasted_iota(jnp.int32, cw.shape, 1)
    row = lax.broadcasted_iota(jnp.int32, (cw.shape[0], 1), 0)
    in_bucket = (row >= lo_ref[v]) & (row < hi_ref[v])

    def expert(e, wg_ref, wu_ref, wd_ref):
        w = jnp.sum(jnp.where(lane == e, cw, 0.0), axis=-1, keepdims=True)
        w = jnp.where(in_bucket, w, 0.0)
        gate = jnp.dot(h, wg_ref[...], preferred_element_type=F32)
        up = jnp.dot(h, wu_ref[...], preferred_element_type=F32)
        hid = jax.nn.silu(gate) * up * w
        return jnp.dot(hid.astype(BF16), wd_ref[...], preferred_element_type=F32)

    o_ref[...] += (expert(ea_ref[v], wga_ref, wua_ref, wda_ref)
                   + expert(eb_ref[v], wgb_ref, wub_ref, wdb_ref))


def _moe(sched, xs, cw, g2, wg, wu, wd, tm):
    n, d = xs.shape
    f = wg.shape[-1]
    n_visits = sched[0].shape[0]
    w_in_a = pl.BlockSpec((None, d, f), lambda v, t, ea, eb, lo, hi, fi: (ea[v], 0, 0))
    w_in_b = pl.BlockSpec((None, d, f), lambda v, t, ea, eb, lo, hi, fi: (eb[v], 0, 0))
    w_out_a = pl.BlockSpec((None, f, d), lambda v, t, ea, eb, lo, hi, fi: (ea[v], 0, 0))
    w_out_b = pl.BlockSpec((None, f, d), lambda v, t, ea, eb, lo, hi, fi: (eb[v], 0, 0))
    tile = lambda width: pl.BlockSpec((tm, width), lambda v, t, ea, eb, lo, hi, fi: (t[v], 0))
    return pl.pallas_call(
        _moe_kernel,
        grid_spec=pltpu.PrefetchScalarGridSpec(
            num_scalar_prefetch=6,
            grid=(n_visits,),
            in_specs=[tile(d), tile(LANE),
                      pl.BlockSpec((1, d), lambda v, *_: (0, 0)),
                      w_in_a, w_in_a, w_out_a, w_in_b, w_in_b, w_out_b],
            out_specs=tile(d),
            scratch_shapes=[pltpu.VMEM((tm, d), BF16)]),
        out_shape=jax.ShapeDtypeStruct((n, d), F32),
        compiler_params=_cparams(("arbitrary",)),
        name="moe_experts",
    )(*sched, xs, cw, g2, wg, wu, wd, wg, wu, wd)


def _moe_schedule(bucket, tm):
    n = bucket.shape[0]
    n_tiles = n // tm
    n_visits = n_tiles + N_BUCKETS - 1
    counts = jnp.sum((bucket[:, None] == jnp.arange(N_BUCKETS)[None, :]).astype(jnp.int32), axis=0)
    ends = jnp.cumsum(counts)
    starts = ends - counts
    first_tile = starts // tm
    last_tile = jnp.maximum(ends - 1, 0) // tm
    n_vis = jnp.where(counts > 0, last_tile - first_tile + 1, 0)
    vis_end = jnp.cumsum(n_vis)
    vis_start = vis_end - n_vis
    v = jnp.arange(n_visits, dtype=jnp.int32)
    b = jnp.minimum(jnp.searchsorted(vis_end, v, side="right"), N_BUCKETS - 1).astype(jnp.int32)
    live = v < vis_end[-1]
    tile = jnp.where(live, first_tile[b] + v - vis_start[b], n_tiles - 1).astype(jnp.int32)
    lo = jnp.where(live, jnp.clip(starts[b] - tile * tm, 0, tm), 0).astype(jnp.int32)
    hi = jnp.where(live, jnp.clip(ends[b] - tile * tm, 0, tm), 0).astype(jnp.int32)
    b_last = b[jnp.maximum(vis_end[-1] - 1, 0)]
    b = jnp.where(live, b, b_last)
    pair_a = np.array([0, 0, 0, 1, 1, 2], np.int32)
    pair_b = np.array([1, 2, 3, 2, 3, 3], np.int32)
    grp = b // N_PAIRS
    ea = (grp * EXPERTS_PER_GROUP + jnp.asarray(pair_a)[b % N_PAIRS]).astype(jnp.int32)
    eb = (grp * EXPERTS_PER_GROUP + jnp.asarray(pair_b)[b % N_PAIRS]).astype(jnp.int32)
    first = jnp.concatenate([jnp.ones((1,), jnp.int32), (tile[1:] != tile[:-1]).astype(jnp.int32)])
    return tile, ea, eb, lo, hi, first


def _layer(x, layer, p):
    B, S, D = x.shape
    G, dh = NSA_KV_GROUPS, HEAD_DIM
    lambda_init = 0.8 - 0.6 * math.exp(-0.3 * layer)
    x2d = x.reshape(B * S, D)

    proj2d = _in_proj(x2d, p["norm1_g"][None, :], _reorder_w_in(p["w_in"]))
    proj = proj2d.reshape(B, S, C_TOTAL)

    k_cmp, v_cmp = _compress(proj, *_compress_params(p["cmp_pos"], p["cmp_w1"], p["cmp_w2"]),
                             p["nsa_k_norm"][0:1])

    near, bias_cmp = _bias_tables(p["rel_bias"], S)
    nsa_near = near[:NSA_HEADS].reshape(G, NSA_HPG, 2, TQ, 2 * TQ).transpose(0, 2, 1, 3, 4)
    i = np.arange(TQ)
    win_mask = jnp.asarray(np.where(i[None, :] > i[:, None], 0.0, NEG).astype(np.float32))
    expand = jnp.asarray(np.repeat(np.eye(S // SLC_BLOCK, dtype=np.float32), SLC_BLOCK, axis=1), BF16)
    o_nsa = _nsa_attention(proj, k_cmp, v_cmp, p["nsa_q_norm"][None, :], p["nsa_k_norm"],
                           bias_cmp.reshape(G, NSA_HPG, S, LANE), nsa_near, win_mask,
                           jnp.asarray(_overlap_matrix(S)), expand)
    o_diff = _diff_attention(proj, p["diff_q_norm"], p["diff_k_norm"], p["diff_lambda"],
                             p["diff_out_norm"][None, :], near[NSA_HEADS:], lambda_init)

    rw = jnp.concatenate([p["router_group_w"], p["router_expert_w"],
                          jnp.zeros((D, LANE - N_GROUPS - N_EXPERTS), F32)], axis=1)
    rb = jnp.concatenate([p["router_group_b"], p["router_expert_b"],
                          jnp.zeros((LANE - N_GROUPS - N_EXPERTS,), F32)])[None, :]
    x2, route = _merge(x2d, o_nsa.reshape(B * S, -1), o_diff.reshape(B * S, -1), proj2d,
                       p["w_branch_nsa"].astype(BF16), p["w_branch_diff"].astype(BF16),
                       p["w_out"].astype(BF16), p["norm2_g"][None, :], rw, rb)

    tm = 256
    bucket = route[:, N_EXPERTS].astype(jnp.int32)
    perm = jnp.argsort(bucket).astype(jnp.int32)
    inv_perm = jnp.argsort(perm).astype(jnp.int32)
    sched = _moe_schedule(bucket, tm)
    xs, cw = _gather_rows(perm, [x2, route])
    ys = _moe(sched, xs, cw, p["norm2_g"][None, :], p["expert_w_gate"].astype(BF16),
              p["expert_w_up"].astype(BF16), p["expert_w_down"].astype(BF16), tm)
    (out,) = _gather_rows(inv_perm, [ys])
    return out.reshape(B, S, D)


def kernel(x, norm1_g, w_in, nsa_q_norm, nsa_k_norm, cmp_pos, cmp_w1, cmp_w2, diff_q_norm, diff_k_norm,
           diff_lambda, diff_out_norm, rel_bias, w_branch_nsa, w_branch_diff, w_out, norm2_g,
           router_group_w, router_group_b, router_expert_w, router_expert_b, expert_w_gate,
           expert_w_up, expert_w_down):
    per_layer = dict(norm1_g=norm1_g, w_in=w_in, nsa_q_norm=nsa_q_norm, nsa_k_norm=nsa_k_norm,
                     cmp_pos=cmp_pos, cmp_w1=cmp_w1, cmp_w2=cmp_w2, diff_q_norm=diff_q_norm,
                     diff_k_norm=diff_k_norm, diff_lambda=diff_lambda, diff_out_norm=diff_out_norm,
                     w_branch_nsa=w_branch_nsa, w_branch_diff=w_branch_diff, w_out=w_out,
                     norm2_g=norm2_g, router_group_w=router_group_w, router_group_b=router_group_b,
                     router_expert_w=router_expert_w, router_expert_b=router_expert_b,
                     expert_w_gate=expert_w_gate, expert_w_up=expert_w_up, expert_w_down=expert_w_down)
    for layer in range(norm1_g.shape[0]):
        p = {k: v[layer] for k, v in per_layer.items()}
        p["rel_bias"] = rel_bias
        x = _layer(x, layer, p)
    return x
```

```python
import functools
import math

import numpy as np
import jax
import jax.numpy as jnp
from jax import lax
from jax.experimental import pallas as pl
from jax.experimental.pallas import tpu as pltpu

F32 = jnp.float32
BF16 = jnp.bfloat16

HEAD_DIM = 64
SCALE = HEAD_DIM ** -0.5
NSA_HEADS = 8
NSA_KV_GROUPS = 2
NSA_HPG = NSA_HEADS // NSA_KV_GROUPS
CMP_BLOCK = 32
CMP_STRIDE = 16
CMP_HIDDEN = 256
SLC_BLOCK = 64
SLC_TOP_N = 16
WINDOW = 512
FORCE_SCORE = 1e4
DIFF_HEADS = 4
DIFF_V_DIM = 2 * HEAD_DIM
REL_BUCKETS = 32
REL_MAX_DIST = 128
N_GROUPS = 4
EXPERTS_PER_GROUP = 4
N_EXPERTS = N_GROUPS * EXPERTS_PER_GROUP
EXPERT_HIDDEN = 512
RMS_EPS = 1e-6
NEG = -1e30
LOG2E = math.log2(math.e)

LANE = 128
TQ = 256
N_PAIRS = 6
N_BUCKETS = N_GROUPS * N_PAIRS

C_GM = 0
C_QN = 2048
C_KV = 2560
C_CMP = 3072
C_GATE = 3328
C_QD = 3584
C_KD = 4096
C_VD = 4608
C_TOTAL = 5120

VMEM_LIMIT = 56 * 1024 * 1024


def _cparams(sem):
    return pltpu.CompilerParams(dimension_semantics=sem, vmem_limit_bytes=VMEM_LIMIT)


def _rms(x, gain):
    return x * lax.rsqrt(jnp.mean(x * x, axis=-1, keepdims=True) + RMS_EPS) * gain


def _reorder_w_in(w_in):
    d_model = w_in.shape[0]
    dh, G, Hg = HEAD_DIM, NSA_KV_GROUPS, NSA_HPG
    o_q = 0
    o_kv = o_q + NSA_HEADS * dh
    o_g = o_kv + 3 * 2 * G * dh
    o_qd = o_g + 3 * NSA_HEADS
    o_gm = o_qd + 3 * DIFF_HEADS * 2 * dh

    def cols(start, n):
        return w_in[:, start:start + n]

    def kv(kind, which, g):
        return cols(o_kv + ((kind * 2 + which) * G + g) * dh, dh)

    parts = [cols(o_gm, 2 * d_model), cols(o_q, NSA_HEADS * dh)]
    parts += [kv(kind, which, g) for g in range(G) for kind in (1, 2) for which in (0, 1)]
    parts += [kv(0, which, g) for g in range(G) for which in (0, 1)]
    for g in range(G):
        parts += [cols(o_g + (kind * G + g) * Hg, Hg) for kind in range(3)]
        parts.append(jnp.zeros((d_model, LANE - 3 * Hg), w_in.dtype))
    parts.append(cols(o_qd, 3 * DIFF_HEADS * 2 * dh))
    w = jnp.concatenate(parts, axis=1)
    assert w.shape[1] == C_TOTAL
    return w.astype(BF16)


def _t5_bucket(dist):
    n = jnp.maximum(dist, 0)
    max_exact = REL_BUCKETS // 2
    nf = jnp.maximum(n, 1).astype(F32)
    large = max_exact + (jnp.log(nf / max_exact) / math.log(REL_MAX_DIST / max_exact)
                         * (REL_BUCKETS - max_exact)).astype(jnp.int32)
    large = jnp.minimum(large, REL_BUCKETS - 1)
    return jnp.where(n < max_exact, n, large)


def _toeplitz(v_ext, rows, cols):
    length = v_ext.shape[-1]
    lead = v_ext.shape[:-1]
    flat = jnp.broadcast_to(v_ext[..., None, :], lead + (rows, length)).reshape(lead + (rows * length,))
    return flat[..., :rows * (length - 1)].reshape(lead + (rows, length - 1))[..., :cols]


def _bias_tables(rel_bias, S):
    heads = rel_bias.shape[1]
    far = REL_MAX_DIST
    assert TQ == 2 * far and LANE == far
    bd = rel_bias[_t5_bucket(jnp.arange(2 * far))].T.astype(F32) * LOG2E
    rel = bd - bd[:, far:far + 1]
    neg = lambda *shape: jnp.full((heads,) + shape, NEG, F32)
    zero = lambda *shape: jnp.zeros((heads,) + shape, F32)
    rev = lambda a: a[:, ::-1]
    d0 = _toeplitz(jnp.concatenate([rel[:, 0:1], neg(far), rev(rel[:, 1:far])], axis=1), far, far)
    d1 = _toeplitz(jnp.concatenate([rev(rel[:, 1:far + 1]), zero(far)], axis=1), far, far)
    z, n = zero(far, far), neg(far, far)
    blocks = lambda rows: jnp.concatenate([jnp.concatenate(r, axis=2) for r in rows], axis=1)
    near = jnp.stack([blocks([[d0, n, n, n], [d1, d0, n, n]]),
                      blocks([[z, d1, d0, n], [z, z, d1, d0]])], axis=1)
    hb = rel[:NSA_HEADS]
    zero8 = lambda *shape: jnp.zeros((NSA_HEADS,) + shape, F32)
    neg8 = lambda *shape: jnp.full((NSA_HEADS,) + shape, NEG, F32)
    width = TQ + far - 1
    tp = _toeplitz(jnp.concatenate([rev(hb[:, 0:far]), neg8(TQ + 1), zero8(TQ - 1)], axis=1), TQ, width)
    n_tiles = S // TQ
    per_tile = TQ // CMP_STRIDE
    col0 = per_tile * (n_tiles - 1)
    shift = CMP_STRIDE * col0 - (CMP_BLOCK - 1)
    first = -(-(shift - (far - 1)) // CMP_STRIDE)
    last = (shift + TQ - 1) // CMP_STRIDE
    band = tp[:, :, CMP_STRIDE * first - shift + far - 1::CMP_STRIDE][:, :, :last - first + 1]
    g2 = jnp.concatenate([zero8(TQ, first), band, neg8(TQ, 2 * TQ - last - 1)], axis=2)
    bias_cmp = jnp.stack([g2[:, :, col0 - per_tile * i:col0 - per_tile * i + LANE] for i in range(n_tiles)],
                         axis=1)
    n_cmp = (S - CMP_BLOCK) // CMP_STRIDE + 1
    bias_cmp = jnp.where(np.arange(LANE) < n_cmp, bias_cmp, NEG)
    return near, bias_cmp.reshape(NSA_HEADS, S, LANE)


def _overlap_matrix(S):
    n_cmp = (S - CMP_BLOCK) // CMP_STRIDE + 1
    n_slc = S // SLC_BLOCK
    c_start = np.arange(n_cmp) * CMP_STRIDE
    s_start = np.arange(n_slc) * SLC_BLOCK
    lo = np.maximum(s_start[:, None], c_start[None, :])
    hi = np.minimum(s_start[:, None] + SLC_BLOCK, c_start[None, :] + CMP_BLOCK)
    overlap = (np.maximum(hi - lo, 0) / CMP_BLOCK).astype(np.float32)
    out = np.zeros((n_slc, LANE), np.float32)
    out[:, :n_cmp] = overlap
    return out


def _in_proj_kernel(x_ref, g_ref, w_ref, o_ref, *, chunk):
    h = _rms(x_ref[...], g_ref[...]).astype(BF16)
    for c in range(0, C_TOTAL, chunk):
        o_ref[:, c:c + chunk] = jnp.dot(h, w_ref[:, c:c + chunk],
                                        preferred_element_type=F32).astype(BF16)


def _in_proj(x2d, gain, w):
    n, d = x2d.shape
    tm = 256
    return pl.pallas_call(
        functools.partial(_in_proj_kernel, chunk=512),
        grid=(n // tm,),
        in_specs=[pl.BlockSpec((tm, d), lambda i: (i, 0)),
                  pl.BlockSpec((1, d), lambda i: (0, 0)),
                  pl.BlockSpec((d, C_TOTAL), lambda i: (0, 0))],
        out_specs=pl.BlockSpec((tm, C_TOTAL), lambda i: (i, 0)),
        out_shape=jax.ShapeDtypeStruct((n, C_TOTAL), BF16),
        compiler_params=_cparams(("arbitrary",)),
        name="in_proj",
    )(x2d, gain, w)


def _compress_kernel(t_ref, pos_ref, w1_ref, w2_ref, kg_ref, k_ref, v_ref, tok_scr):
    S, dh = t_ref.shape[0], HEAD_DIM
    m = S // CMP_STRIDE
    tok_scr[0:S, :] = t_ref[...].astype(F32)
    tok_scr[S:, :] = jnp.zeros((CMP_STRIDE, LANE), F32)
    pre = jnp.zeros((m, 2 * CMP_HIDDEN), F32)
    for l in range(CMP_BLOCK):
        x = tok_scr[pl.ds(l, m, stride=CMP_STRIDE), :] + pos_ref[l:l + 1, :]
        pre = pre + jnp.dot(x.astype(BF16), w1_ref[l], preferred_element_type=F32)
    out = jnp.dot(jax.nn.gelu(pre).astype(BF16), w2_ref[...], preferred_element_type=F32)
    k_ref[...] = _rms(out[:, :dh], kg_ref[...]).astype(BF16)
    v_ref[...] = out[:, dh:].astype(BF16)


def _compress(proj, pos, w1, w2, k_gain):
    B, S, _ = proj.shape
    G, m = NSA_KV_GROUPS, S // CMP_STRIDE
    out = jax.ShapeDtypeStruct((B, G, m, HEAD_DIM), BF16)
    ospec = pl.BlockSpec((None, None, m, HEAD_DIM), lambda b, g: (b, g, 0, 0))
    return pl.pallas_call(
        _compress_kernel,
        grid=(B, G),
        in_specs=[pl.BlockSpec((None, S, LANE), lambda b, g: (b, 0, C_CMP // LANE + g)),
                  pl.BlockSpec(pos.shape, lambda b, g: (0, 0)),
                  pl.BlockSpec(w1.shape, lambda b, g: (0, 0, 0)),
                  pl.BlockSpec(w2.shape, lambda b, g: (0, 0)),
                  pl.BlockSpec((1, HEAD_DIM), lambda b, g: (0, 0))],
        out_specs=[ospec, ospec],
        out_shape=[out, out],
        scratch_shapes=[pltpu.VMEM((S + CMP_STRIDE, LANE), F32)],
        compiler_params=_cparams(("arbitrary", "arbitrary")),
        name="nsa_compress",
    )(proj, pos, w1, w2, k_gain)


def _compress_params(cmp_pos, cmp_w1, cmp_w2):
    dh, hid = HEAD_DIM, CMP_HIDDEN
    w1 = cmp_w1.reshape(2, CMP_BLOCK, dh, hid)
    z1 = jnp.zeros((CMP_BLOCK, dh, hid), cmp_w1.dtype)
    w1 = jnp.concatenate([jnp.concatenate([w1[0], z1], axis=2),
                          jnp.concatenate([z1, w1[1]], axis=2)], axis=1)
    z2 = jnp.zeros((hid, dh), cmp_w2.dtype)
    w2 = jnp.concatenate([jnp.concatenate([cmp_w2[0], z2], axis=1),
                          jnp.concatenate([z2, cmp_w2[1]], axis=1)], axis=0)
    pos = jnp.concatenate([cmp_pos[0], cmp_pos[1]], axis=1)
    return pos, w1.astype(BF16), w2.astype(BF16)


def _dot_nt(a, b):
    return lax.dot_general(a, b, (((1,), (1,)), ((), ())), preferred_element_type=F32)


def _softmax_step(s, v, carry):
    m, l, acc = carry
    m_new = jnp.maximum(m, jnp.max(s, axis=-1, keepdims=True))
    alpha = jnp.exp2(m - m_new)
    p = jnp.exp2(s - m_new)
    l = alpha * l + jnp.sum(p, axis=-1, keepdims=True)
    acc = alpha * acc + jnp.dot(p.astype(BF16), v, preferred_element_type=F32)
    return m_new, l, acc


def _softmax_init(rows, dv):
    return (jnp.full((rows, 1), NEG, F32), jnp.zeros((rows, 1), F32), jnp.zeros((rows, dv), F32))


def _key_chunks(c):
    T = TQ
    chunks = [(0, T, (0, 0))] if c == 0 else [((c - 1) * T, 2 * T, (1, 0))]
    n_far = max(c - 1, 0)
    if n_far % 2:
        chunks.append((0, T, None))
    chunks += [(s * T, 2 * T, None) for s in range(n_far % 2, n_far, 2)]
    return chunks


def _q_prep(x, gain):
    return (_rms(x.astype(F32), gain) * (SCALE * LOG2E)).astype(BF16)


def _nsa_kernel(q_ref, kv_ref, kc_ref, vc_ref, gate_ref, qg_ref, kg_ref, bc_ref, nb_ref, wm_ref,
                ov_ref, ex_ref, o_ref, ks_scr, kw_scr, mb_scr, osel_scr, *, seq_len):
    T, dh, Hg = TQ, HEAD_DIM, NSA_HPG
    R = Hg * T
    qi = pl.program_id(2)
    n_slc = seq_len // SLC_BLOCK

    @pl.when(qi == 0)
    def _():
        ks_scr[...] = _rms(kv_ref[:, 0:dh].astype(F32), kg_ref[1:2, :]).astype(BF16)
        kw_scr[...] = _rms(kv_ref[:, 2 * dh:3 * dh].astype(F32), kg_ref[2:3, :]).astype(BF16)

    q = jnp.concatenate([_q_prep(q_ref[:, h * dh:(h + 1) * dh], qg_ref[...]) for h in range(Hg)],
                        axis=0)

    lc = _dot_nt(q, kc_ref[...]) + bc_ref[...].reshape(R, LANE)
    pc = jnp.exp2(lc - jnp.max(lc, axis=-1, keepdims=True))
    row = lax.broadcasted_iota(jnp.int32, (Hg, T, 1), 1).reshape(R, 1)
    visible = (qi * T + row >= CMP_BLOCK - 1).astype(F32)
    pc = pc * (visible / jnp.sum(pc, axis=-1, keepdims=True))
    o_cmp = jnp.dot(pc.astype(BF16), vc_ref[...], preferred_element_type=F32)

    n_all = SLC_TOP_N * SLC_BLOCK // T

    @pl.when(qi >= n_all)
    def _():
        p_sum = pc[0:T] + pc[T:2 * T] + pc[2 * T:3 * T] + pc[3 * T:4 * T]
        imp = lax.dot_general(ov_ref[...], p_sum, (((1,), (1,)), ((), ())), preferred_element_type=F32,
                              precision=lax.Precision.HIGHEST)
        blk = lax.broadcasted_iota(jnp.int32, (n_slc, T), 0)
        cur = (qi * T + lax.broadcasted_iota(jnp.int32, (n_slc, T), 1)) // SLC_BLOCK
        forced = (blk == 0) | (blk == cur) | (blk == cur - 1)
        score = jnp.where(blk <= cur, imp + jnp.where(forced, FORCE_SCORE, 0.0), NEG)
        rank = jnp.zeros((n_slc, T), jnp.int32)
        for i in range(n_slc):
            other = score[i:i + 1, :]
            ahead = (other > score) | ((other == score) & (blk > i))
            rank = rank + ahead.astype(jnp.int32)
        sel_bias = jnp.where(rank < SLC_TOP_N, 0.0, NEG).T.astype(BF16)
        mb_scr[...] = jnp.dot(sel_bias, ex_ref[...], preferred_element_type=F32)

    def sel_tile(c):
        carry = _softmax_init(R, dh)
        for start, width, table in _key_chunks(c):
            s = _dot_nt(q, ks_scr[start:start + width, :]).reshape(Hg, T, width)
            if table is not None:
                s = s + nb_ref[table[0], :, :, table[1]:table[1] + width]
            if c >= n_all:
                s = s + mb_scr[:, start:start + width][None]
            carry = _softmax_step(s.reshape(R, width), kv_ref[start:start + width, dh:2 * dh], carry)
        osel_scr[...] = carry[2] / carry[1]

    for c in range(seq_len // T):
        pl.when(qi == c)(functools.partial(sel_tile, c))
    o_slc = osel_scr[...]

    def win_step(start, width, bias, carry):
        s = _dot_nt(q, kw_scr[pl.ds(start, width), :]).reshape(Hg, T, width) + bias
        return _softmax_step(s.reshape(R, width), kv_ref[pl.ds(start, width), 3 * dh:4 * dh], carry)

    carry = win_step(pl.multiple_of(jnp.maximum(qi - 1, 0) * T, T), 2 * T, nb_ref[jnp.minimum(qi, 1)],
                     _softmax_init(R, dh))
    carry = lax.fori_loop(
        0, (qi >= 2).astype(jnp.int32),
        lambda j, c: win_step(pl.multiple_of((qi - 2) * T, T), T, wm_ref[...][None], c), carry)
    o_win = carry[2] / carry[1]

    gates = jax.nn.sigmoid(gate_ref[...].astype(F32))
    outs = []
    for h in range(Hg):
        rows = slice(h * T, (h + 1) * T)
        outs.append(gates[:, h:h + 1] * o_cmp[rows]
                    + gates[:, Hg + h:Hg + h + 1] * o_slc[rows]
                    + gates[:, 2 * Hg + h:2 * Hg + h + 1] * o_win[rows])
    o_ref[...] = jnp.concatenate(outs, axis=1).astype(BF16)


def _nsa_attention(proj, k_cmp, v_cmp, q_gain, k_gain, bias_cmp, near_bias, win_mask, overlap, expand):
    B, S, _ = proj.shape
    G, Hg, dh, T = NSA_KV_GROUPS, NSA_HPG, HEAD_DIM, TQ
    assert WINDOW == 2 * T and S % (2 * T) == 0
    n_slc = S // SLC_BLOCK
    return pl.pallas_call(
        functools.partial(_nsa_kernel, seq_len=S),
        grid=(B, G, S // T),
        in_specs=[
            pl.BlockSpec((None, T, Hg * dh), lambda b, g, i: (b, i, C_QN // (Hg * dh) + g)),
            pl.BlockSpec((None, S, 4 * dh), lambda b, g, i: (b, 0, C_KV // (4 * dh) + g)),
            pl.BlockSpec((None, None, LANE, dh), lambda b, g, i: (b, g, 0, 0)),
            pl.BlockSpec((None, None, LANE, dh), lambda b, g, i: (b, g, 0, 0)),
            pl.BlockSpec((None, T, LANE), lambda b, g, i: (b, i, C_GATE // LANE + g)),
            pl.BlockSpec((1, dh), lambda b, g, i: (0, 0)),
            pl.BlockSpec((3, dh), lambda b, g, i: (0, 0)),
            pl.BlockSpec((None, Hg, T, LANE), lambda b, g, i: (g, 0, i, 0)),
            pl.BlockSpec((None, 2, Hg, T, 2 * T), lambda b, g, i: (g, 0, 0, 0, 0)),
            pl.BlockSpec((T, T), lambda b, g, i: (0, 0)),
            pl.BlockSpec((n_slc, LANE), lambda b, g, i: (0, 0)),
            pl.BlockSpec((n_slc, S), lambda b, g, i: (0, 0)),
        ],
        out_specs=pl.BlockSpec((None, T, Hg * dh), lambda b, g, i: (b, i, g)),
        out_shape=jax.ShapeDtypeStruct((B, S, NSA_HEADS * dh), BF16),
        scratch_shapes=[pltpu.VMEM((S, dh), BF16), pltpu.VMEM((S, dh), BF16),
                        pltpu.VMEM((T, S), F32), pltpu.VMEM((Hg * T, dh), F32)],
        compiler_params=_cparams(("arbitrary", "arbitrary", "arbitrary")),
        name="nsa_attention",
    )(proj, proj, k_cmp, v_cmp, proj, q_gain, k_gain, bias_cmp, near_bias, win_mask, overlap, expand)


def _diff_kernel(q_ref, k_ref, v_ref, qg_ref, kg_ref, lam_ref, og_ref, nb_ref, o_ref,
                 k1_scr, k2_scr, *, lambda_init, n_tiles):
    T, dh = TQ, HEAD_DIM
    qi = pl.program_id(2)

    @pl.when(qi == 0)
    def _():
        k1_scr[...] = _rms(k_ref[:, 0:dh].astype(F32), kg_ref[0:1, :]).astype(BF16)
        k2_scr[...] = _rms(k_ref[:, dh:2 * dh].astype(F32), kg_ref[1:2, :]).astype(BF16)

    q1 = _q_prep(q_ref[:, 0:dh], qg_ref[0:1, :])
    q2 = _q_prep(q_ref[:, dh:2 * dh], qg_ref[1:2, :])

    def step(start, width, bias, carry):
        c1, c2 = carry
        v = v_ref[start:start + width, :]
        s1 = _dot_nt(q1, k1_scr[start:start + width, :])
        s2 = _dot_nt(q2, k2_scr[start:start + width, :])
        if bias is not None:
            s1, s2 = s1 + bias, s2 + bias
        return _softmax_step(s1, v, c1), _softmax_step(s2, v, c2)

    lam_p = lam_ref[...]
    lam = (jnp.exp(jnp.sum(lam_p[0:1] * lam_p[1:2], axis=-1, keepdims=True))
           - jnp.exp(jnp.sum(lam_p[2:3] * lam_p[3:4], axis=-1, keepdims=True)) + lambda_init)

    def tile(c):
        init = _softmax_init(T, DIFF_V_DIM)
        carry = (init, init)
        for start, width, table in _key_chunks(c):
            bias = None if table is None else nb_ref[table[0], :, table[1]:table[1] + width]
            carry = step(start, width, bias, carry)
        (_, l1, a1), (_, l2, a2) = carry
        o = a1 / l1 - lam * (a2 / l2)
        o_ref[...] = (_rms(o, og_ref[...]) * (1.0 - lambda_init)).astype(BF16)

    for c in range(n_tiles):
        pl.when(qi == c)(functools.partial(tile, c))


def _diff_attention(proj, q_gain, k_gain, lam_p, out_gain, near_bias, lambda_init):
    B, S, _ = proj.shape
    T, dh, H = TQ, HEAD_DIM, DIFF_HEADS
    return pl.pallas_call(
        functools.partial(_diff_kernel, lambda_init=lambda_init, n_tiles=S // T),
        grid=(B, H, S // T),
        in_specs=[
            pl.BlockSpec((None, T, 2 * dh), lambda b, h, i: (b, i, C_QD // (2 * dh) + h)),
            pl.BlockSpec((None, S, 2 * dh), lambda b, h, i: (b, 0, C_KD // (2 * dh) + h)),
            pl.BlockSpec((None, S, DIFF_V_DIM), lambda b, h, i: (b, 0, C_VD // DIFF_V_DIM + h)),
            pl.BlockSpec((2, dh), lambda b, h, i: (0, 0)),
            pl.BlockSpec((2, dh), lambda b, h, i: (0, 0)),
            pl.BlockSpec((4, dh), lambda b, h, i: (0, 0)),
            pl.BlockSpec((1, DIFF_V_DIM), lambda b, h, i: (0, 0)),
            pl.BlockSpec((None, 2, T, 2 * T), lambda b, h, i: (h, 0, 0, 0)),
        ],
        out_specs=pl.BlockSpec((None, T, DIFF_V_DIM), lambda b, h, i: (b, i, h)),
        out_shape=jax.ShapeDtypeStruct((B, S, H * DIFF_V_DIM), BF16),
        scratch_shapes=[pltpu.VMEM((S, dh), BF16), pltpu.VMEM((S, dh), BF16)],
        compiler_params=_cparams(("arbitrary", "arbitrary", "arbitrary")),
        name="diff_attention",
    )(proj, proj, proj, q_gain, k_gain, lam_p, out_gain, near_bias)


def _merge_kernel(x_ref, on_ref, od_ref, gm_ref, wn_ref, wd_ref, wo_ref, g2_ref, rwh_ref, rwl_ref, rb_ref,
                  x2_ref, route_ref):
    d = x_ref.shape[-1]
    a = jnp.dot(on_ref[...], wn_ref[...], preferred_element_type=F32)
    b = jnp.dot(od_ref[...], wd_ref[...], preferred_element_type=F32)
    y = (jax.nn.sigmoid(gm_ref[:, :d].astype(F32)) * a
         + jax.nn.sigmoid(gm_ref[:, d:].astype(F32)) * b)
    x2 = x_ref[...] + jnp.dot(y.astype(BF16), wo_ref[...], preferred_element_type=F32)
    x2_ref[...] = x2

    h2 = _rms(x2, g2_ref[...])
    h_hi = h2.astype(BF16)
    h_lo = (h2 - h_hi.astype(F32)).astype(BF16)
    logits = (jnp.dot(h_hi, rwh_ref[...], preferred_element_type=F32)
              + jnp.dot(h_lo, rwh_ref[...], preferred_element_type=F32)
              + jnp.dot(h_hi, rwl_ref[...], preferred_element_type=F32)) + rb_ref[...]
    lane = lax.broadcasted_iota(jnp.int32, logits.shape, 1)
    big = jnp.int32(LANE)

    def first_argmax(vals, mask):
        vals = jnp.where(mask, vals, -jnp.inf)
        top = jnp.max(vals, axis=-1, keepdims=True)
        idx = jnp.min(jnp.where(vals == top, lane, big), axis=-1, keepdims=True)
        return top, idx

    is_group = lane < N_GROUPS
    g_max, g_top = first_argmax(logits, is_group)
    g_top_p = 1.0 / jnp.sum(jnp.where(is_group, jnp.exp(logits - g_max), 0.0), axis=-1, keepdims=True)
    lo = N_GROUPS + g_top * EXPERTS_PER_GROUP
    in_group = (lane >= lo) & (lane < lo + EXPERTS_PER_GROUP)
    m1, i1 = first_argmax(logits, in_group)
    m2, i2 = first_argmax(logits, in_group & (lane != i1))
    e2 = jnp.exp(m2 - m1)
    w1 = g_top_p / (1.0 + e2)
    w2 = g_top_p * e2 / (1.0 + e2)
    la = jnp.minimum(i1, i2) - lo
    lb = jnp.maximum(i1, i2) - lo
    pair = jnp.right_shift(la * (7 - la), 1) + (lb - la - 1)
    bucket = (g_top * N_PAIRS + pair).astype(F32)
    exp_lane = lane + N_GROUPS
    route = (jnp.where(exp_lane == i1, w1, 0.0) + jnp.where(exp_lane == i2, w2, 0.0)
             + jnp.where(lane == N_EXPERTS, bucket, 0.0))
    route_ref[...] = route


def _merge(x2d, o_nsa, o_diff, proj2d, wn, wd, wo, g2, rw, rb):
    n, d = x2d.shape
    tm = 512
    rw_hi = rw.astype(BF16)
    rw_lo = (rw - rw_hi.astype(F32)).astype(BF16)
    full = lambda a: pl.BlockSpec(a.shape, lambda i: (0,) * a.ndim)
    return pl.pallas_call(
        _merge_kernel,
        grid=(n // tm,),
        in_specs=[pl.BlockSpec((tm, d), lambda i: (i, 0)),
                  pl.BlockSpec((tm, o_nsa.shape[1]), lambda i: (i, 0)),
                  pl.BlockSpec((tm, o_diff.shape[1]), lambda i: (i, 0)),
                  pl.BlockSpec((tm, 2 * d), lambda i: (i, C_GM // (2 * d))),
                  full(wn), full(wd), full(wo), full(g2), full(rw_hi), full(rw_lo), full(rb)],
        out_specs=[pl.BlockSpec((tm, d), lambda i: (i, 0)),
                   pl.BlockSpec((tm, LANE), lambda i: (i, 0))],
        out_shape=[jax.ShapeDtypeStruct((n, d), F32), jax.ShapeDtypeStruct((n, LANE), F32)],
        compiler_params=_cparams(("arbitrary",)),
        name="merge_router",
    )(x2d, o_nsa, o_diff, proj2d, wn, wd, wo, g2, rw_hi, rw_lo, rb)


def _gather_kernel(idx_ref, *refs, rows, n_arrays):
    srcs = refs[:n_arrays]
    outs = refs[n_arrays:2 * n_arrays]
    sem = refs[2 * n_arrays]
    base = pl.program_id(0) * rows

    def copy(a, r):
        return pltpu.make_async_copy(srcs[a].at[pl.ds(idx_ref[base + r], 1), :],
                                     outs[a].at[pl.ds(r, 1), :], sem.at[a])

    def start(r, c):
        for a in range(n_arrays):
            copy(a, r).start()
        return c

    def wait(r, c):
        for a in range(n_arrays):
            copy(a, r).wait()
        return c

    lax.fori_loop(0, rows, start, 0, unroll=8)
    lax.fori_loop(0, rows, wait, 0, unroll=8)


def _gather_rows(idx, arrays):
    n = idx.shape[0]
    rows = 256
    k = len(arrays)
    return pl.pallas_call(
        functools.partial(_gather_kernel, rows=rows, n_arrays=k),
        grid_spec=pltpu.PrefetchScalarGridSpec(
            num_scalar_prefetch=1,
            grid=(n // rows,),
            in_specs=[pl.BlockSpec(memory_space=pl.ANY)] * k,
            out_specs=[pl.BlockSpec((rows, a.shape[1]), lambda i, idx_ref: (i, 0)) for a in arrays],
            scratch_shapes=[pltpu.SemaphoreType.DMA((k,))]),
        out_shape=[jax.ShapeDtypeStruct((n, a.shape[1]), a.dtype) for a in arrays],
        compiler_params=_cparams(("arbitrary",)),
        name="row_gather",
    )(idx, *arrays)


def _moe_kernel(tile_ref, ea_ref, eb_ref, lo_ref, hi_ref, first_ref,
                xs_ref, cw_ref, g2_ref, wga_ref, wua_ref, wda_ref, wgb_ref, wub_ref, wdb_ref,
                o_ref, h_scr):
    v = pl.program_id(0)

    @pl.when(first_ref[v] == 1)
    def _():
        xs = xs_ref[...]
        h_scr[...] = _rms(xs, g2_ref[...]).astype(BF16)
        o_ref[...] = xs

    h = h_scr[...]
    cw = cw_ref[...]
    lane = lax.broadcasted_iota(jnp.int32, cw.shape, 1)
    row = lax.broadcasted_iota(jnp.int32, (cw.shape[0], 1), 0)
    in_bucket = (row >= lo_ref[v]) & (row < hi_ref[v])

    def expert(e, wg_ref, wu_ref, wd_ref):
        w = jnp.sum(jnp.where(lane == e, cw, 0.0), axis=-1, keepdims=True)
        w = jnp.where(in_bucket, w, 0.0)
        gate = jnp.dot(h, wg_ref[...], preferred_element_type=F32)
        up = jnp.dot(h, wu_ref[...], preferred_element_type=F32)
        hid = jax.nn.silu(gate) * up * w
        return jnp.dot(hid.astype(BF16), wd_ref[...], preferred_element_type=F32)

    o_ref[...] += (expert(ea_ref[v], wga_ref, wua_ref, wda_ref)
                   + expert(eb_ref[v], wgb_ref, wub_ref, wdb_ref))


def _moe(sched, xs, cw, g2, wg, wu, wd, tm):
    n, d = xs.shape
    f = wg.shape[-1]
    n_visits = sched[0].shape[0]
    w_in_a = pl.BlockSpec((None, d, f), lambda v, t, ea, eb, lo, hi, fi: (ea[v], 0, 0))
    w_in_b = pl.BlockSpec((None, d, f), lambda v, t, ea, eb, lo, hi, fi: (eb[v], 0, 0))
    w_out_a = pl.BlockSpec((None, f, d), lambda v, t, ea, eb, lo, hi, fi: (ea[v], 0, 0))
    w_out_b = pl.BlockSpec((None, f, d), lambda v, t, ea, eb, lo, hi, fi: (eb[v], 0, 0))
    tile = lambda width: pl.BlockSpec((tm, width), lambda v, t, ea, eb, lo, hi, fi: (t[v], 0))
    return pl.pallas_call(
        _moe_kernel,
        grid_spec=pltpu.PrefetchScalarGridSpec(
            num_scalar_prefetch=6,
            grid=(n_visits,),
            in_specs=[tile(d), tile(LANE),
                      pl.BlockSpec((1, d), lambda v, *_: (0, 0)),
                      w_in_a, w_in_a, w_out_a, w_in_b, w_in_b, w_out_b],
            out_specs=tile(d),
            scratch_shapes=[pltpu.VMEM((tm, d), BF16)]),
        out_shape=jax.ShapeDtypeStruct((n, d), F32),
        compiler_params=_cparams(("arbitrary",)),
        name="moe_experts",
    )(*sched, xs, cw, g2, wg, wu, wd, wg, wu, wd)


def _moe_schedule(bucket, tm):
    n = bucket.shape[0]
    n_tiles = n // tm
    n_visits = n_tiles + N_BUCKETS - 1
    counts = jnp.sum((bucket[:, None] == jnp.arange(N_BUCKETS)[None, :]).astype(jnp.int32), axis=0)
    ends = jnp.cumsum(counts)
    starts = ends - counts
    first_tile = starts // tm
    last_tile = jnp.maximum(ends - 1, 0) // tm
    n_vis = jnp.where(counts > 0, last_tile - first_tile + 1, 0)
    vis_end = jnp.cumsum(n_vis)
    vis_start = vis_end - n_vis
    v = jnp.arange(n_visits, dtype=jnp.int32)
    b = jnp.minimum(jnp.searchsorted(vis_end, v, side="right"), N_BUCKETS - 1).astype(jnp.int32)
    live = v < vis_end[-1]
    tile = jnp.where(live, first_tile[b] + v - vis_start[b], n_tiles - 1).astype(jnp.int32)
    lo = jnp.where(live, jnp.clip(starts[b] - tile * tm, 0, tm), 0).astype(jnp.int32)
    hi = jnp.where(live, jnp.clip(ends[b] - tile * tm, 0, tm), 0).astype(jnp.int32)
    b_last = b[jnp.maximum(vis_end[-1] - 1, 0)]
    b = jnp.where(live, b, b_last)
    pair_a = np.array([0, 0, 0, 1, 1, 2], np.int32)
    pair_b = np.array([1, 2, 3, 2, 3, 3], np.int32)
    grp = b // N_PAIRS
    ea = (grp * EXPERTS_PER_GROUP + jnp.asarray(pair_a)[b % N_PAIRS]).astype(jnp.int32)
    eb = (grp * EXPERTS_PER_GROUP + jnp.asarray(pair_b)[b % N_PAIRS]).astype(jnp.int32)
    first = jnp.concatenate([jnp.ones((1,), jnp.int32), (tile[1:] != tile[:-1]).astype(jnp.int32)])
    return tile, ea, eb, lo, hi, first


def _layer(x, layer, p):
    B, S, D = x.shape
    G, dh = NSA_KV_GROUPS, HEAD_DIM
    lambda_init = 0.8 - 0.6 * math.exp(-0.3 * layer)
    x2d = x.reshape(B * S, D)

    proj2d = _in_proj(x2d, p["norm1_g"][None, :], _reorder_w_in(p["w_in"]))
    proj = proj2d.reshape(B, S, C_TOTAL)

    k_cmp, v_cmp = _compress(proj, *_compress_params(p["cmp_pos"], p["cmp_w1"], p["cmp_w2"]),
                             p["nsa_k_norm"][0:1])

    near, bias_cmp = _bias_tables(p["rel_bias"], S)
    nsa_near = near[:NSA_HEADS].reshape(G, NSA_HPG, 2, TQ, 2 * TQ).transpose(0, 2, 1, 3, 4)
    i = np.arange(TQ)
    win_mask = jnp.asarray(np.where(i[None, :] > i[:, None], 0.0, NEG).astype(np.float32))
    expand = jnp.asarray(np.repeat(np.eye(S // SLC_BLOCK, dtype=np.float32), SLC_BLOCK, axis=1), BF16)
    o_nsa = _nsa_attention(proj, k_cmp, v_cmp, p["nsa_q_norm"][None, :], p["nsa_k_norm"],
                           bias_cmp.reshape(G, NSA_HPG, S, LANE), nsa_near, win_mask,
                           jnp.asarray(_overlap_matrix(S)), expand)
    o_diff = _diff_attention(proj, p["diff_q_norm"], p["diff_k_norm"], p["diff_lambda"],
                             p["diff_out_norm"][None, :], near[NSA_HEADS:], lambda_init)

    rw = jnp.concatenate([p["router_group_w"], p["router_expert_w"],
                          jnp.zeros((D, LANE - N_GROUPS - N_EXPERTS), F32)], axis=1)
    rb = jnp.concatenate([p["router_group_b"], p["router_expert_b"],
                          jnp.zeros((LANE - N_GROUPS - N_EXPERTS,), F32)])[None, :]
    x2, route = _merge(x2d, o_nsa.reshape(B * S, -1), o_diff.reshape(B * S, -1), proj2d,
                       p["w_branch_nsa"].astype(BF16), p["w_branch_diff"].astype(BF16),
                       p["w_out"].astype(BF16), p["norm2_g"][None, :], rw, rb)

    tm = 256
    bucket = route[:, N_EXPERTS].astype(jnp.int32)
    perm = jnp.argsort(bucket).astype(jnp.int32)
    inv_perm = jnp.argsort(perm).astype(jnp.int32)
    sched = _moe_schedule(bucket, tm)
    xs, cw = _gather_rows(perm, [x2, route])
    ys = _moe(sched, xs, cw, p["norm2_g"][None, :], p["expert_w_gate"].astype(BF16),
              p["expert_w_up"].astype(BF16), p["expert_w_down"].astype(BF16), tm)
    (out,) = _gather_rows(inv_perm, [ys])
    return out.reshape(B, S, D)


def kernel(x, norm1_g, w_in, nsa_q_norm, nsa_k_norm, cmp_pos, cmp_w1, cmp_w2, diff_q_norm, diff_k_norm,
           diff_lambda, diff_out_norm, rel_bias, w_branch_nsa, w_branch_diff, w_out, norm2_g,
           router_group_w, router_group_b, router_expert_w, router_expert_b, expert_w_gate,
           expert_w_up, expert_w_down):
    per_layer = dict(norm1_g=norm1_g, w_in=w_in, nsa_q_norm=nsa_q_norm, nsa_k_norm=nsa_k_norm,
                     cmp_pos=cmp_pos, cmp_w1=cmp_w1, cmp_w2=cmp_w2, diff_q_norm=diff_q_norm,
                     diff_k_norm=diff_k_norm, diff_lambda=diff_lambda, diff_out_norm=diff_out_norm,
                     w_branch_nsa=w_branch_nsa, w_branch_diff=w_branch_diff, w_out=w_out,
                     norm2_g=norm2_g, router_group_w=router_group_w, router_group_b=router_group_b,
                     router_expert_w=router_expert_w, router_expert_b=router_expert_b,
                     expert_w_gate=expert_w_gate, expert_w_up=expert_w_up, expert_w_down=expert_w_down)
    for layer in range(norm1_g.shape[0]):
        p = {k: v[layer] for k, v in per_layer.items()}
        p["rel_bias"] = rel_bias
        x = _layer(x, layer, p)
    return x
```

```python
import functools
import math

import numpy as np
import jax
import jax.numpy as jnp
from jax import lax
from jax.experimental import pallas as pl
from jax.experimental.pallas import tpu as pltpu

F32 = jnp.float32
BF16 = jnp.bfloat16

HEAD_DIM = 64
SCALE = HEAD_DIM ** -0.5
NSA_HEADS = 8
NSA_KV_GROUPS = 2
NSA_HPG = NSA_HEADS // NSA_KV_GROUPS
CMP_BLOCK = 32
CMP_STRIDE = 16
CMP_HIDDEN = 256
SLC_BLOCK = 64
SLC_TOP_N = 16
WINDOW = 512
FORCE_SCORE = 1e4
DIFF_HEADS = 4
DIFF_V_DIM = 2 * HEAD_DIM
REL_BUCKETS = 32
REL_MAX_DIST = 128
N_GROUPS = 4
EXPERTS_PER_GROUP = 4
N_EXPERTS = N_GROUPS * EXPERTS_PER_GROUP
EXPERT_HIDDEN = 512
RMS_EPS = 1e-6
NEG = -1e30
LOG2E = math.log2(math.e)

LANE = 128
TQ = 256
N_PAIRS = 6
N_BUCKETS = N_GROUPS * N_PAIRS

C_GM = 0
C_QN = 2048
C_KV = 2560
C_CMP = 3072
C_GATE = 3328
C_QD = 3584
C_KD = 4096
C_VD = 4608
C_TOTAL = 5120

VMEM_LIMIT = 56 * 1024 * 1024


def _cparams(sem):
    return pltpu.CompilerParams(dimension_semantics=sem, vmem_limit_bytes=VMEM_LIMIT)


def _rms(x, gain):
    return x * lax.rsqrt(jnp.mean(x * x, axis=-1, keepdims=True) + RMS_EPS) * gain


def _reorder_w_in(w_in):
    d_model = w_in.shape[0]
    dh, G, Hg = HEAD_DIM, NSA_KV_GROUPS, NSA_HPG
    o_q = 0
    o_kv = o_q + NSA_HEADS * dh
    o_g = o_kv + 3 * 2 * G * dh
    o_qd = o_g + 3 * NSA_HEADS
    o_gm = o_qd + 3 * DIFF_HEADS * 2 * dh

    def cols(start, n):
        return w_in[:, start:start + n]

    def kv(kind, which, g):
        return cols(o_kv + ((kind * 2 + which) * G + g) * dh, dh)

    parts = [cols(o_gm, 2 * d_model), cols(o_q, NSA_HEADS * dh)]
    parts += [kv(kind, which, g) for g in range(G) for kind in (1, 2) for which in (0, 1)]
    parts += [kv(0, which, g) for g in range(G) for which in (0, 1)]
    for g in range(G):
        parts += [cols(o_g + (kind * G + g) * Hg, Hg) for kind in range(3)]
        parts.append(jnp.zeros((d_model, LANE - 3 * Hg), w_in.dtype))
    parts.append(cols(o_qd, 3 * DIFF_HEADS * 2 * dh))
    w = jnp.concatenate(parts, axis=1)
    assert w.shape[1] == C_TOTAL
    return w.astype(BF16)


def _t5_bucket(dist):
    n = jnp.maximum(dist, 0)
    max_exact = REL_BUCKETS // 2
    nf = jnp.maximum(n, 1).astype(F32)
    large = max_exact + (jnp.log(nf / max_exact) / math.log(REL_MAX_DIST / max_exact)
                         * (REL_BUCKETS - max_exact)).astype(jnp.int32)
    large = jnp.minimum(large, REL_BUCKETS - 1)
    return jnp.where(n < max_exact, n, large)


def _toeplitz(v_ext, rows, cols):
    length = v_ext.shape[-1]
    lead = v_ext.shape[:-1]
    flat = jnp.broadcast_to(v_ext[..., None, :], lead + (rows, length)).reshape(lead + (rows * length,))
    return flat[..., :rows * (length - 1)].reshape(lead + (rows, length - 1))[..., :cols]


def _bias_tables(rel_bias, S):
    heads = rel_bias.shape[1]
    far = REL_MAX_DIST
    assert TQ == 2 * far and LANE == far
    bd = rel_bias[_t5_bucket(jnp.arange(2 * far))].T.astype(F32) * LOG2E
    rel = bd - bd[:, far:far + 1]
    neg = lambda *shape: jnp.full((heads,) + shape, NEG, F32)
    zero = lambda *shape: jnp.zeros((heads,) + shape, F32)
    rev = lambda a: a[:, ::-1]
    d0 = _toeplitz(jnp.concatenate([rel[:, 0:1], neg(far), rev(rel[:, 1:far])], axis=1), far, far)
    d1 = _toeplitz(jnp.concatenate([rev(rel[:, 1:far + 1]), zero(far)], axis=1), far, far)
    z, n = zero(far, far), neg(far, far)
    blocks = lambda rows: jnp.concatenate([jnp.concatenate(r, axis=2) for r in rows], axis=1)
    near = jnp.stack([blocks([[d0, n, n, n], [d1, d0, n, n]]),
                      blocks([[z, d1, d0, n], [z, z, d1, d0]])], axis=1)
    hb = rel[:NSA_HEADS]
    zero8 = lambda *shape: jnp.zeros((NSA_HEADS,) + shape, F32)
    neg8 = lambda *shape: jnp.full((NSA_HEADS,) + shape, NEG, F32)
    width = TQ + far - 1
    tp = _toeplitz(jnp.concatenate([rev(hb[:, 0:far]), neg8(TQ + 1), zero8(TQ - 1)], axis=1), TQ, width)
    n_tiles = S // TQ
    per_tile = TQ // CMP_STRIDE
    col0 = per_tile * (n_tiles - 1)
    shift = CMP_STRIDE * col0 - (CMP_BLOCK - 1)
    first = -(-(shift - (far - 1)) // CMP_STRIDE)
    last = (shift + TQ - 1) // CMP_STRIDE
    band = tp[:, :, CMP_STRIDE * first - shift + far - 1::CMP_STRIDE][:, :, :last - first + 1]
    g2 = jnp.concatenate([zero8(TQ, first), band, neg8(TQ, 2 * TQ - last - 1)], axis=2)
    bias_cmp = jnp.stack([g2[:, :, col0 - per_tile * i:col0 - per_tile * i + LANE] for i in range(n_tiles)],
                         axis=1)
    n_cmp = (S - CMP_BLOCK) // CMP_STRIDE + 1
    bias_cmp = jnp.where(np.arange(LANE) < n_cmp, bias_cmp, NEG)
    return near, bias_cmp.reshape(NSA_HEADS, S, LANE)


def _overlap_matrix(S):
    n_cmp = (S - CMP_BLOCK) // CMP_STRIDE + 1
    n_slc = S // SLC_BLOCK
    c_start = np.arange(n_cmp) * CMP_STRIDE
    s_start = np.arange(n_slc) * SLC_BLOCK
    lo = np.maximum(s_start[:, None], c_start[None, :])
    hi = np.minimum(s_start[:, None] + SLC_BLOCK, c_start[None, :] + CMP_BLOCK)
    overlap = (np.maximum(hi - lo, 0) / CMP_BLOCK).astype(np.float32)
    out = np.zeros((n_slc, LANE), np.float32)
    out[:, :n_cmp] = overlap
    return out


def _in_proj_kernel(x_ref, g_ref, w_ref, o_ref, *, chunk):
    h = _rms(x_ref[...], g_ref[...]).astype(BF16)
    for c in range(0, C_TOTAL, chunk):
        o_ref[:, c:c + chunk] = jnp.dot(h, w_ref[:, c:c + chunk],
                                        preferred_element_type=F32).astype(BF16)


def _in_proj(x2d, gain, w):
    n, d = x2d.shape
    tm = 256
    return pl.pallas_call(
        functools.partial(_in_proj_kernel, chunk=512),
        grid=(n // tm,),
        in_specs=[pl.BlockSpec((tm, d), lambda i: (i, 0)),
                  pl.BlockSpec((1, d), lambda i: (0, 0)),
                  pl.BlockSpec((d, C_TOTAL), lambda i: (0, 0))],
        out_specs=pl.BlockSpec((tm, C_TOTAL), lambda i: (i, 0)),
        out_shape=jax.ShapeDtypeStruct((n, C_TOTAL), BF16),
        compiler_params=_cparams(("arbitrary",)),
        name="in_proj",
    )(x2d, gain, w)


def _compress_kernel(t_ref, pos_ref, w1_ref, w2_ref, kg_ref, k_ref, v_ref, tok_scr):
    S, dh = t_ref.shape[0], HEAD_DIM
    m = S // CMP_STRIDE
    tok_scr[0:S, :] = t_ref[...].astype(F32)
    tok_scr[S:, :] = jnp.zeros((CMP_STRIDE, LANE), F32)
    pre = jnp.zeros((m, 2 * CMP_HIDDEN), F32)
    for l in range(CMP_BLOCK):
        x = tok_scr[pl.ds(l, m, stride=CMP_STRIDE), :] + pos_ref[l:l + 1, :]
        pre = pre + jnp.dot(x.astype(BF16), w1_ref[l], preferred_element_type=F32)
    out = jnp.dot(jax.nn.gelu(pre).astype(BF16), w2_ref[...], preferred_element_type=F32)
    k_ref[...] = _rms(out[:, :dh], kg_ref[...]).astype(BF16)
    v_ref[...] = out[:, dh:].astype(BF16)


def _compress(proj, pos, w1, w2, k_gain):
    B, S, _ = proj.shape
    G, m = NSA_KV_GROUPS, S // CMP_STRIDE
    out = jax.ShapeDtypeStruct((B, G, m, HEAD_DIM), BF16)
    ospec = pl.BlockSpec((None, None, m, HEAD_DIM), lambda b, g: (b, g, 0, 0))
    return pl.pallas_call(
        _compress_kernel,
        grid=(B, G),
        in_specs=[pl.BlockSpec((None, S, LANE), lambda b, g: (b, 0, C_CMP // LANE + g)),
                  pl.BlockSpec(pos.shape, lambda b, g: (0, 0)),
                  pl.BlockSpec(w1.shape, lambda b, g: (0, 0, 0)),
                  pl.BlockSpec(w2.shape, lambda b, g: (0, 0)),
                  pl.BlockSpec((1, HEAD_DIM), lambda b, g: (0, 0))],
        out_specs=[ospec, ospec],
        out_shape=[out, out],
        scratch_shapes=[pltpu.VMEM((S + CMP_STRIDE, LANE), F32)],
        compiler_params=_cparams(("arbitrary", "arbitrary")),
        name="nsa_compress",
    )(proj, pos, w1, w2, k_gain)


def _compress_params(cmp_pos, cmp_w1, cmp_w2):
    dh, hid = HEAD_DIM, CMP_HIDDEN
    w1 = cmp_w1.reshape(2, CMP_BLOCK, dh, hid)
    z1 = jnp.zeros((CMP_BLOCK, dh, hid), cmp_w1.dtype)
    w1 = jnp.concatenate([jnp.concatenate([w1[0], z1], axis=2),
                          jnp.concatenate([z1, w1[1]], axis=2)], axis=1)
    z2 = jnp.zeros((hid, dh), cmp_w2.dtype)
    w2 = jnp.concatenate([jnp.concatenate([cmp_w2[0], z2], axis=1),
                          jnp.concatenate([z2, cmp_w2[1]], axis=1)], axis=0)
    pos = jnp.concatenate([cmp_pos[0], cmp_pos[1]], axis=1)
    return pos, w1.astype(BF16), w2.astype(BF16)


def _dot_nt(a, b):
    return lax.dot_general(a, b, (((1,), (1,)), ((), ())), preferred_element_type=F32)


def _softmax_step(s, v, carry):
    m, l, acc = carry
    m_new = jnp.maximum(m, jnp.max(s, axis=-1, keepdims=True))
    alpha = jnp.exp2(m - m_new)
    p = jnp.exp2(s - m_new)
    l = alpha * l + jnp.sum(p, axis=-1, keepdims=True)
    acc = alpha * acc + jnp.dot(p.astype(BF16), v, preferred_element_type=F32)
    return m_new, l, acc


def _softmax_init(rows, dv):
    return (jnp.full((rows, 1), NEG, F32), jnp.zeros((rows, 1), F32), jnp.zeros((rows, dv), F32))


def _key_chunks(c):
    T = TQ
    chunks = [(0, T, (0, 0))] if c == 0 else [((c - 1) * T, 2 * T, (1, 0))]
    n_far = max(c - 1, 0)
    if n_far % 2:
        chunks.append((0, T, None))
    chunks += [(s * T, 2 * T, None) for s in range(n_far % 2, n_far, 2)]
    return chunks


def _q_prep(x, gain):
    return (_rms(x.astype(F32), gain) * (SCALE * LOG2E)).astype(BF16)


def _nsa_kernel(q_ref, kv_ref, kc_ref, vc_ref, gate_ref, qg_ref, kg_ref, bc_ref, nb_ref, wm_ref,
                ov_ref, ex_ref, o_ref, ks_scr, kw_scr, mb_scr, osel_scr, *, seq_len):
    T, dh, Hg = TQ, HEAD_DIM, NSA_HPG
    R = Hg * T
    qi = pl.program_id(2)
    n_slc = seq_len // SLC_BLOCK

    @pl.when(qi == 0)
    def _():
        ks_scr[...] = _rms(kv_ref[:, 0:dh].astype(F32), kg_ref[1:2, :]).astype(BF16)
        kw_scr[...] = _rms(kv_ref[:, 2 * dh:3 * dh].astype(F32), kg_ref[2:3, :]).astype(BF16)

    q = jnp.concatenate([_q_prep(q_ref[:, h * dh:(h + 1) * dh], qg_ref[...]) for h in range(Hg)],
                        axis=0)

    lc = _dot_nt(q, kc_ref[...]) + bc_ref[...].reshape(R, LANE)
    pc = jnp.exp2(lc - jnp.max(lc, axis=-1, keepdims=True))
    row = lax.broadcasted_iota(jnp.int32, (Hg, T, 1), 1).reshape(R, 1)
    visible = (qi * T + row >= CMP_BLOCK - 1).astype(F32)
    pc = pc * (visible / jnp.sum(pc, axis=-1, keepdims=True))
    o_cmp = jnp.dot(pc.astype(BF16), vc_ref[...], preferred_element_type=F32)

    n_all = SLC_TOP_N * SLC_BLOCK // T

    @pl.when(qi >= n_all)
    def _():
        p_sum = pc[0:T] + pc[T:2 * T] + pc[2 * T:3 * T] + pc[3 * T:4 * T]
        imp = lax.dot_general(ov_ref[...], p_sum, (((1,), (1,)), ((), ())), preferred_element_type=F32,
                              precision=lax.Precision.HIGHEST)
        blk = lax.broadcasted_iota(jnp.int32, (n_slc, T), 0)
        cur = (qi * T + lax.broadcasted_iota(jnp.int32, (n_slc, T), 1)) // SLC_BLOCK
        forced = (blk == 0) | (blk == cur) | (blk == cur - 1)
        score = jnp.where(blk <= cur, imp + jnp.where(forced, FORCE_SCORE, 0.0), NEG)
        rank = jnp.zeros((n_slc, T), jnp.int32)
        for i in range(n_slc):
            other = score[i:i + 1, :]
            ahead = (other > score) | ((other == score) & (blk > i))
            rank = rank + ahead.astype(jnp.int32)
        sel_bias = jnp.where(rank < SLC_TOP_N, 0.0, NEG).T.astype(BF16)
        mb_scr[...] = jnp.dot(sel_bias, ex_ref[...], preferred_element_type=F32)

    def sel_tile(c):
        carry = _softmax_init(R, dh)
        for start, width, table in _key_chunks(c):
            s = _dot_nt(q, ks_scr[start:start + width, :]).reshape(Hg, T, width)
            if table is not None:
                s = s + nb_ref[table[0], :, :, table[1]:table[1] + width]
            if c >= n_all:
                s = s + mb_scr[:, start:start + width][None]
            carry = _softmax_step(s.reshape(R, width), kv_ref[start:start + width, dh:2 * dh], carry)
        osel_scr[...] = carry[2] / carry[1]

    for c in range(seq_len // T):
        pl.when(qi == c)(functools.partial(sel_tile, c))
    o_slc = osel_scr[...]

    def win_step(start, width, bias, carry):
        s = _dot_nt(q, kw_scr[pl.ds(start, width), :]).reshape(Hg, T, width) + bias
        return _softmax_step(s.reshape(R, width), kv_ref[pl.ds(start, width), 3 * dh:4 * dh], carry)

    carry = win_step(pl.multiple_of(jnp.maximum(qi - 1, 0) * T, T), 2 * T, nb_ref[jnp.minimum(qi, 1)],
                     _softmax_init(R, dh))
    carry = lax.fori_loop(
        0, (qi >= 2).astype(jnp.int32),
        lambda j, c: win_step(pl.multiple_of((qi - 2) * T, T), T, wm_ref[...][None], c), carry)
    o_win = carry[2] / carry[1]

    gates = jax.nn.sigmoid(gate_ref[...].astype(F32))
    outs = []
    for h in range(Hg):
        rows = slice(h * T, (h + 1) * T)
        outs.append(gates[:, h:h + 1] * o_cmp[rows]
                    + gates[:, Hg + h:Hg + h + 1] * o_slc[rows]
                    + gates[:, 2 * Hg + h:2 * Hg + h + 1] * o_win[rows])
    o_ref[...] = jnp.concatenate(outs, axis=1).astype(BF16)


def _nsa_attention(proj, k_cmp, v_cmp, q_gain, k_gain, bias_cmp, near_bias, win_mask, overlap, expand):
    B, S, _ = proj.shape
    G, Hg, dh, T = NSA_KV_GROUPS, NSA_HPG, HEAD_DIM, TQ
    assert WINDOW == 2 * T and S % (2 * T) == 0
    n_slc = S // SLC_BLOCK
    return pl.pallas_call(
        functools.partial(_nsa_kernel, seq_len=S),
        grid=(B, G, S // T),
        in_specs=[
            pl.BlockSpec((None, T, Hg * dh), lambda b, g, i: (b, i, C_QN // (Hg * dh) + g)),
            pl.BlockSpec((None, S, 4 * dh), lambda b, g, i: (b, 0, C_KV // (4 * dh) + g)),
            pl.BlockSpec((None, None, LANE, dh), lambda b, g, i: (b, g, 0, 0)),
            pl.BlockSpec((None, None, LANE, dh), lambda b, g, i: (b, g, 0, 0)),
            pl.BlockSpec((None, T, LANE), lambda b, g, i: (b, i, C_GATE // LANE + g)),
            pl.BlockSpec((1, dh), lambda b, g, i: (0, 0)),
            pl.BlockSpec((3, dh), lambda b, g, i: (0, 0)),
            pl.BlockSpec((None, Hg, T, LANE), lambda b, g, i: (g, 0, i, 0)),
            pl.BlockSpec((None, 2, Hg, T, 2 * T), lambda b, g, i: (g, 0, 0, 0, 0)),
            pl.BlockSpec((T, T), lambda b, g, i: (0, 0)),
            pl.BlockSpec((n_slc, LANE), lambda b, g, i: (0, 0)),
            pl.BlockSpec((n_slc, S), lambda b, g, i: (0, 0)),
        ],
        out_specs=pl.BlockSpec((None, T, Hg * dh), lambda b, g, i: (b, i, g)),
        out_shape=jax.ShapeDtypeStruct((B, S, NSA_HEADS * dh), BF16),
        scratch_shapes=[pltpu.VMEM((S, dh), BF16), pltpu.VMEM((S, dh), BF16),
                        pltpu.VMEM((T, S), F32), pltpu.VMEM((Hg * T, dh), F32)],
        compiler_params=_cparams(("arbitrary", "arbitrary", "arbitrary")),
        name="nsa_attention",
    )(proj, proj, k_cmp, v_cmp, proj, q_gain, k_gain, bias_cmp, near_bias, win_mask, overlap, expand)


def _diff_kernel(q_ref, k_ref, v_ref, qg_ref, kg_ref, lam_ref, og_ref, nb_ref, o_ref,
                 k1_scr, k2_scr, *, lambda_init, n_tiles):
    T, dh = TQ, HEAD_DIM
    qi = pl.program_id(2)

    @pl.when(qi == 0)
    def _():
        k1_scr[...] = _rms(k_ref[:, 0:dh].astype(F32), kg_ref[0:1, :]).astype(BF16)
        k2_scr[...] = _rms(k_ref[:, dh:2 * dh].astype(F32), kg_ref[1:2, :]).astype(BF16)

    q1 = _q_prep(q_ref[:, 0:dh], qg_ref[0:1, :])
    q2 = _q_prep(q_ref[:, dh:2 * dh], qg_ref[1:2, :])

    def step(start, width, bias, carry):
        c1, c2 = carry
        v = v_ref[start:start + width, :]
        s1 = _dot_nt(q1, k1_scr[start:start + width, :])
        s2 = _dot_nt(q2, k2_scr[start:start + width, :])
        if bias is not None:
            s1, s2 = s1 + bias, s2 + bias
        return _softmax_step(s1, v, c1), _softmax_step(s2, v, c2)

    lam_p = lam_ref[...]
    lam = (jnp.exp(jnp.sum(lam_p[0:1] * lam_p[1:2], axis=-1, keepdims=True))
           - jnp.exp(jnp.sum(lam_p[2:3] * lam_p[3:4], axis=-1, keepdims=True)) + lambda_init)

    def tile(c):
        init = _softmax_init(T, DIFF_V_DIM)
        carry = (init, init)
        for start, width, table in _key_chunks(c):
            bias = None if table is None else nb_ref[table[0], :, table[1]:table[1] + width]
            carry = step(start, width, bias, carry)
        (_, l1, a1), (_, l2, a2) = carry
        o = a1 / l1 - lam * (a2 / l2)
        o_ref[...] = (_rms(o, og_ref[...]) * (1.0 - lambda_init)).astype(BF16)

    for c in range(n_tiles):
        pl.when(qi == c)(functools.partial(tile, c))


def _diff_attention(proj, q_gain, k_gain, lam_p, out_gain, near_bias, lambda_init):
    B, S, _ = proj.shape
    T, dh, H = TQ, HEAD_DIM, DIFF_HEADS
    return pl.pallas_call(
        functools.partial(_diff_kernel, lambda_init=lambda_init, n_tiles=S // T),
        grid=(B, H, S // T),
        in_specs=[
            pl.BlockSpec((None, T, 2 * dh), lambda b, h, i: (b, i, C_QD // (2 * dh) + h)),
            pl.BlockSpec((None, S, 2 * dh), lambda b, h, i: (b, 0, C_KD // (2 * dh) + h)),
            pl.BlockSpec((None, S, DIFF_V_DIM), lambda b, h, i: (b, 0, C_VD // DIFF_V_DIM + h)),
            pl.BlockSpec((2, dh), lambda b, h, i: (0, 0)),
            pl.BlockSpec((2, dh), lambda b, h, i: (0, 0)),
            pl.BlockSpec((4, dh), lambda b, h, i: (0, 0)),
            pl.BlockSpec((1, DIFF_V_DIM), lambda b, h, i: (0, 0)),
            pl.BlockSpec((None, 2, T, 2 * T), lambda b, h, i: (h, 0, 0, 0)),
        ],
        out_specs=pl.BlockSpec((None, T, DIFF_V_DIM), lambda b, h, i: (b, i, h)),
        out_shape=jax.ShapeDtypeStruct((B, S, H * DIFF_V_DIM), BF16),
        scratch_shapes=[pltpu.VMEM((S, dh), BF16), pltpu.VMEM((S, dh), BF16)],
        compiler_params=_cparams(("arbitrary", "arbitrary", "arbitrary")),
        name="diff_attention",
    )(proj, proj, proj, q_gain, k_gain, lam_p, out_gain, near_bias)


def _merge_kernel(x_ref, on_ref, od_ref, gm_ref, wn_ref, wd_ref, wo_ref, g2_ref, rwh_ref, rwl_ref, rb_ref,
                  o_ref):
    d = x_ref.shape[-1]
    a = jnp.dot(on_ref[...], wn_ref[...], preferred_element_type=F32)
    b = jnp.dot(od_ref[...], wd_ref[...], preferred_element_type=F32)
    y = (jax.nn.sigmoid(gm_ref[:, :d].astype(F32)) * a
         + jax.nn.sigmoid(gm_ref[:, d:].astype(F32)) * b)
    x2 = x_ref[...] + jnp.dot(y.astype(BF16), wo_ref[...], preferred_element_type=F32)
    o_ref[:, :d] = x2

    h2 = _rms(x2, g2_ref[...])
    h_hi = h2.astype(BF16)
    h_lo = (h2 - h_hi.astype(F32)).astype(BF16)
    logits = (jnp.dot(h_hi, rwh_ref[...], preferred_element_type=F32)
              + jnp.dot(h_lo, rwh_ref[...], preferred_element_type=F32)
              + jnp.dot(h_hi, rwl_ref[...], preferred_element_type=F32)) + rb_ref[...]
    lane = lax.broadcasted_iota(jnp.int32, logits.shape, 1)
    big = jnp.int32(LANE)

    def first_argmax(vals, mask):
        vals = jnp.where(mask, vals, -jnp.inf)
        top = jnp.max(vals, axis=-1, keepdims=True)
        idx = jnp.min(jnp.where(vals == top, lane, big), axis=-1, keepdims=True)
        return top, idx

    is_group = lane < N_GROUPS
    g_max, g_top = first_argmax(logits, is_group)
    g_top_p = 1.0 / jnp.sum(jnp.where(is_group, jnp.exp(logits - g_max), 0.0), axis=-1, keepdims=True)
    lo = N_GROUPS + g_top * EXPERTS_PER_GROUP
    in_group = (lane >= lo) & (lane < lo + EXPERTS_PER_GROUP)
    m1, i1 = first_argmax(logits, in_group)
    m2, i2 = first_argmax(logits, in_group & (lane != i1))
    e2 = jnp.exp(m2 - m1)
    w1 = g_top_p / (1.0 + e2)
    w2 = g_top_p * e2 / (1.0 + e2)
    la = jnp.minimum(i1, i2) - lo
    lb = jnp.maximum(i1, i2) - lo
    pair = jnp.right_shift(la * (7 - la), 1) + (lb - la - 1)
    bucket = (g_top * N_PAIRS + pair).astype(F32)
    exp_lane = lane + N_GROUPS
    route = (jnp.where(exp_lane == i1, w1, 0.0) + jnp.where(exp_lane == i2, w2, 0.0)
             + jnp.where(lane == N_EXPERTS, bucket, 0.0))
    o_ref[:, d:] = route


def _merge(x2d, o_nsa, o_diff, proj2d, wn, wd, wo, g2, rw, rb):
    n, d = x2d.shape
    tm = 512
    rw_hi = rw.astype(BF16)
    rw_lo = (rw - rw_hi.astype(F32)).astype(BF16)
    full = lambda a: pl.BlockSpec(a.shape, lambda i: (0,) * a.ndim)
    return pl.pallas_call(
        _merge_kernel,
        grid=(n // tm,),
        in_specs=[pl.BlockSpec((tm, d), lambda i: (i, 0)),
                  pl.BlockSpec((tm, o_nsa.shape[1]), lambda i: (i, 0)),
                  pl.BlockSpec((tm, o_diff.shape[1]), lambda i: (i, 0)),
                  pl.BlockSpec((tm, 2 * d), lambda i: (i, C_GM // (2 * d))),
                  full(wn), full(wd), full(wo), full(g2), full(rw_hi), full(rw_lo), full(rb)],
        out_specs=pl.BlockSpec((tm, d + LANE), lambda i: (i, 0)),
        out_shape=jax.ShapeDtypeStruct((n, d + LANE), F32),
        compiler_params=_cparams(("arbitrary",)),
        name="merge_router",
    )(x2d, o_nsa, o_diff, proj2d, wn, wd, wo, g2, rw_hi, rw_lo, rb)


def _moe_kernel(tile_ref, ea_ref, eb_ref, lo_ref, hi_ref, first_ref, last_ref, perm_ref,
                x_hbm, g2_ref, wga_ref, wua_ref, wda_ref, wgb_ref, wub_ref, wdb_ref,
                o_hbm, x_buf, acc_buf, h_scr, g_sem, s_sem, *, tm, n_tiles):
    d = o_hbm.shape[1]
    v = pl.program_id(0)
    t = tile_ref[v]
    slot = t % 2

    def gather_copy(tt, sl, r):
        return pltpu.make_async_copy(x_hbm.at[pl.ds(perm_ref[tt * tm + r], 1), :],
                                     x_buf.at[sl, pl.ds(r, 1), :], g_sem.at[sl])

    def scatter_copy(tt, sl, r):
        return pltpu.make_async_copy(acc_buf.at[sl, pl.ds(r, 1), :],
                                     o_hbm.at[pl.ds(perm_ref[tt * tm + r], 1), :], s_sem.at[sl])

    def for_rows(fn):
        for r in range(tm):
            fn(r)

    @pl.when(v == 0)
    def _():
        for_rows(lambda r: gather_copy(0, 0, r).start())

    @pl.when(first_ref[v] == 1)
    def _():
        for_rows(lambda r: gather_copy(t, slot, r).wait())

        @pl.when(t + 1 < n_tiles)
        def _():
            for_rows(lambda r: gather_copy(t + 1, 1 - slot, r).start())

        @pl.when(t >= 2)
        def _():
            for_rows(lambda r: scatter_copy(t - 2, slot, r).wait())

        xs = x_buf[slot, :, :d]
        h_scr[...] = _rms(xs, g2_ref[...]).astype(BF16)
        acc_buf[slot] = xs

    h = h_scr[...]
    cw = x_buf[slot, :, d:]
    lane = lax.broadcasted_iota(jnp.int32, cw.shape, 1)
    row = lax.broadcasted_iota(jnp.int32, (tm, 1), 0)
    in_bucket = (row >= lo_ref[v]) & (row < hi_ref[v])

    def expert(e, wg_ref, wu_ref, wd_ref):
        w = jnp.sum(jnp.where(lane == e, cw, 0.0), axis=-1, keepdims=True)
        w = jnp.where(in_bucket, w, 0.0)
        gate = jnp.dot(h, wg_ref[...], preferred_element_type=F32)
        up = jnp.dot(h, wu_ref[...], preferred_element_type=F32)
        hid = jax.nn.silu(gate) * up * w
        return jnp.dot(hid.astype(BF16), wd_ref[...], preferred_element_type=F32)

    acc_buf[slot] += (expert(ea_ref[v], wga_ref, wua_ref, wda_ref)
                      + expert(eb_ref[v], wgb_ref, wub_ref, wdb_ref))

    @pl.when(last_ref[v] == 1)
    def _():
        for_rows(lambda r: scatter_copy(t, slot, r).start())

    @pl.when(v == pl.num_programs(0) - 1)
    def _():
        for_rows(lambda r: scatter_copy(t - 1, 1 - slot, r).wait())
        for_rows(lambda r: scatter_copy(t, slot, r).wait())


def _moe(sched, perm, x2r, g2, wg, wu, wd, tm):
    n = x2r.shape[0]
    d, f = wg.shape[1], wg.shape[2]
    n_visits = sched[0].shape[0]
    n_tiles = n // tm
    assert n_tiles >= 2
    w_in_a = pl.BlockSpec((None, d, f), lambda v, t, ea, eb, *_: (ea[v], 0, 0))
    w_in_b = pl.BlockSpec((None, d, f), lambda v, t, ea, eb, *_: (eb[v], 0, 0))
    w_out_a = pl.BlockSpec((None, f, d), lambda v, t, ea, eb, *_: (ea[v], 0, 0))
    w_out_b = pl.BlockSpec((None, f, d), lambda v, t, ea, eb, *_: (eb[v], 0, 0))
    return pl.pallas_call(
        functools.partial(_moe_kernel, tm=tm, n_tiles=n_tiles),
        grid_spec=pltpu.PrefetchScalarGridSpec(
            num_scalar_prefetch=8,
            grid=(n_visits,),
            in_specs=[pl.BlockSpec(memory_space=pl.ANY),
                      pl.BlockSpec((1, d), lambda v, *_: (0, 0)),
                      w_in_a, w_in_a, w_out_a, w_in_b, w_in_b, w_out_b],
            out_specs=pl.BlockSpec(memory_space=pl.ANY),
            scratch_shapes=[pltpu.VMEM((2, tm, x2r.shape[1]), F32), pltpu.VMEM((2, tm, d), F32),
                            pltpu.VMEM((tm, d), BF16),
                            pltpu.SemaphoreType.DMA((2,)), pltpu.SemaphoreType.DMA((2,))]),
        out_shape=jax.ShapeDtypeStruct((n, d), F32),
        compiler_params=_cparams(("arbitrary",)),
        name="moe_experts",
    )(*sched, perm, x2r, g2, wg, wu, wd, wg, wu, wd)


def _moe_schedule(bucket, tm):
    n = bucket.shape[0]
    n_tiles = n // tm
    n_visits = n_tiles + N_BUCKETS - 1
    counts = jnp.sum((bucket[:, None] == jnp.arange(N_BUCKETS)[None, :]).astype(jnp.int32), axis=0)
    ends = jnp.cumsum(counts)
    starts = ends - counts
    first_tile = starts // tm
    last_tile = jnp.maximum(ends - 1, 0) // tm
    n_vis = jnp.where(counts > 0, last_tile - first_tile + 1, 0)
    vis_end = jnp.cumsum(n_vis)
    vis_start = vis_end - n_vis
    v = jnp.arange(n_visits, dtype=jnp.int32)
    b = jnp.minimum(jnp.searchsorted(vis_end, v, side="right"), N_BUCKETS - 1).astype(jnp.int32)
    live = v < vis_end[-1]
    tile = jnp.where(live, first_tile[b] + v - vis_start[b], n_tiles - 1).astype(jnp.int32)
    lo = jnp.where(live, jnp.clip(starts[b] - tile * tm, 0, tm), 0).astype(jnp.int32)
    hi = jnp.where(live, jnp.clip(ends[b] - tile * tm, 0, tm), 0).astype(jnp.int32)
    b_last = b[jnp.maximum(vis_end[-1] - 1, 0)]
    b = jnp.where(live, b, b_last)
    pair_a = np.array([0, 0, 0, 1, 1, 2], np.int32)
    pair_b = np.array([1, 2, 3, 2, 3, 3], np.int32)
    grp = b // N_PAIRS
    ea = (grp * EXPERTS_PER_GROUP + jnp.asarray(pair_a)[b % N_PAIRS]).astype(jnp.int32)
    eb = (grp * EXPERTS_PER_GROUP + jnp.asarray(pair_b)[b % N_PAIRS]).astype(jnp.int32)
    changed = (tile[1:] != tile[:-1]).astype(jnp.int32)
    one = jnp.ones((1,), jnp.int32)
    first = jnp.concatenate([one, changed])
    last = jnp.concatenate([changed, one])
    return tile, ea, eb, lo, hi, first, last


def _layer(x, layer, p):
    B, S, D = x.shape
    G, dh = NSA_KV_GROUPS, HEAD_DIM
    lambda_init = 0.8 - 0.6 * math.exp(-0.3 * layer)
    x2d = x.reshape(B * S, D)

    proj2d = _in_proj(x2d, p["norm1_g"][None, :], _reorder_w_in(p["w_in"]))
    proj = proj2d.reshape(B, S, C_TOTAL)

    k_cmp, v_cmp = _compress(proj, *_compress_params(p["cmp_pos"], p["cmp_w1"], p["cmp_w2"]),
                             p["nsa_k_norm"][0:1])

    near, bias_cmp = _bias_tables(p["rel_bias"], S)
    nsa_near = near[:NSA_HEADS].reshape(G, NSA_HPG, 2, TQ, 2 * TQ).transpose(0, 2, 1, 3, 4)
    i = np.arange(TQ)
    win_mask = jnp.asarray(np.where(i[None, :] > i[:, None], 0.0, NEG).astype(np.float32))
    expand = jnp.asarray(np.repeat(np.eye(S // SLC_BLOCK, dtype=np.float32), SLC_BLOCK, axis=1), BF16)
    o_nsa = _nsa_attention(proj, k_cmp, v_cmp, p["nsa_q_norm"][None, :], p["nsa_k_norm"],
                           bias_cmp.reshape(G, NSA_HPG, S, LANE), nsa_near, win_mask,
                           jnp.asarray(_overlap_matrix(S)), expand)
    o_diff = _diff_attention(proj, p["diff_q_norm"], p["diff_k_norm"], p["diff_lambda"],
                             p["diff_out_norm"][None, :], near[NSA_HEADS:], lambda_init)

    rw = jnp.concatenate([p["router_group_w"], p["router_expert_w"],
                          jnp.zeros((D, LANE - N_GROUPS - N_EXPERTS), F32)], axis=1)
    rb = jnp.concatenate([p["router_group_b"], p["router_expert_b"],
                          jnp.zeros((LANE - N_GROUPS - N_EXPERTS,), F32)])[None, :]
    x2r = _merge(x2d, o_nsa.reshape(B * S, -1), o_diff.reshape(B * S, -1), proj2d,
                 p["w_branch_nsa"].astype(BF16), p["w_branch_diff"].astype(BF16),
                 p["w_out"].astype(BF16), p["norm2_g"][None, :], rw, rb)

    tm = 256
    bucket = x2r[:, D + N_EXPERTS].astype(jnp.int32)
    perm = jnp.argsort(bucket).astype(jnp.int32)
    out = _moe(_moe_schedule(bucket, tm), perm, x2r, p["norm2_g"][None, :],
               p["expert_w_gate"].astype(BF16), p["expert_w_up"].astype(BF16),
               p["expert_w_down"].astype(BF16), tm)
    return out.reshape(B, S, D)


def kernel(x, norm1_g, w_in, nsa_q_norm, nsa_k_norm, cmp_pos, cmp_w1, cmp_w2, diff_q_norm, diff_k_norm,
           diff_lambda, diff_out_norm, rel_bias, w_branch_nsa, w_branch_diff, w_out, norm2_g,
           router_group_w, router_group_b, router_expert_w, router_expert_b, expert_w_gate,
           expert_w_up, expert_w_down):
    per_layer = dict(norm1_g=norm1_g, w_in=w_in, nsa_q_norm=nsa_q_norm, nsa_k_norm=nsa_k_norm,
                     cmp_pos=cmp_pos, cmp_w1=cmp_w1, cmp_w2=cmp_w2, diff_q_norm=diff_q_norm,
                     diff_k_norm=diff_k_norm, diff_lambda=diff_lambda, diff_out_norm=diff_out_norm,
                     w_branch_nsa=w_branch_nsa, w_branch_diff=w_branch_diff, w_out=w_out,
                     norm2_g=norm2_g, router_group_w=router_group_w, router_group_b=router_group_b,
                     router_expert_w=router_expert_w, router_expert_b=router_expert_b,
                     expert_w_gate=expert_w_gate, expert_w_up=expert_w_up, expert_w_down=expert_w_down)
    for layer in range(norm1_g.shape[0]):
        p = {k: v[layer] for k, v in per_layer.items()}
        p["rel_bias"] = rel_bias
        x = _layer(x, layer, p)
    return x
```

```python
import functools
import math

import numpy as np
import jax
import jax.numpy as jnp
from jax import lax
from jax.experimental import pallas as pl
from jax.experimental.pallas import tpu as pltpu

F32 = jnp.float32
BF16 = jnp.bfloat16

HEAD_DIM = 64
SCALE = HEAD_DIM ** -0.5
NSA_HEADS = 8
NSA_KV_GROUPS = 2
NSA_HPG = NSA_HEADS // NSA_KV_GROUPS
CMP_BLOCK = 32
CMP_STRIDE = 16
CMP_HIDDEN = 256
SLC_BLOCK = 64
SLC_TOP_N = 16
WINDOW = 512
FORCE_SCORE = 1e4
DIFF_HEADS = 4
DIFF_V_DIM = 2 * HEAD_DIM
REL_BUCKETS = 32
REL_MAX_DIST = 128
N_GROUPS = 4
EXPERTS_PER_GROUP = 4
N_EXPERTS = N_GROUPS * EXPERTS_PER_GROUP
EXPERT_HIDDEN = 512
RMS_EPS = 1e-6
NEG = -1e30
LOG2E = math.log2(math.e)

LANE = 128
TQ = 256
N_PAIRS = 6
N_BUCKETS = N_GROUPS * N_PAIRS

C_GM = 0
C_QN = 2048
C_KV = 2560
C_CMP = 3072
C_GATE = 3328
C_QD = 3584
C_KD = 4096
C_VD = 4608
C_TOTAL = 5120

VMEM_LIMIT = 56 * 1024 * 1024


def _cparams(sem):
    return pltpu.CompilerParams(dimension_semantics=sem, vmem_limit_bytes=VMEM_LIMIT)


def _rms(x, gain):
    return x * lax.rsqrt(jnp.mean(x * x, axis=-1, keepdims=True) + RMS_EPS) * gain


def _reorder_w_in(w_in):
    d_model = w_in.shape[0]
    dh, G, Hg = HEAD_DIM, NSA_KV_GROUPS, NSA_HPG
    o_q = 0
    o_kv = o_q + NSA_HEADS * dh
    o_g = o_kv + 3 * 2 * G * dh
    o_qd = o_g + 3 * NSA_HEADS
    o_gm = o_qd + 3 * DIFF_HEADS * 2 * dh

    def cols(start, n):
        return w_in[:, start:start + n]

    def kv(kind, which, g):
        return cols(o_kv + ((kind * 2 + which) * G + g) * dh, dh)

    parts = [cols(o_gm, 2 * d_model), cols(o_q, NSA_HEADS * dh)]
    parts += [kv(kind, which, g) for g in range(G) for which in (0, 1) for kind in (1, 2)]
    parts += [kv(0, which, g) for g in range(G) for which in (0, 1)]
    for g in range(G):
        parts += [cols(o_g + (kind * G + g) * Hg, Hg) for kind in range(3)]
        parts.append(jnp.zeros((d_model, LANE - 3 * Hg), w_in.dtype))
    parts.append(cols(o_qd, 3 * DIFF_HEADS * 2 * dh))
    w = jnp.concatenate(parts, axis=1)
    assert w.shape[1] == C_TOTAL
    return w.astype(BF16)


def _head_norm_rows(p):
    dh, G = HEAD_DIM, NSA_KV_GROUPS
    qs = SCALE * LOG2E
    zeros = lambda n: jnp.zeros((n,), F32)
    kn = p["nsa_k_norm"]
    parts = [zeros(C_QN - C_GM), jnp.tile(p["nsa_q_norm"] * qs, NSA_HEADS)]
    parts += [jnp.concatenate([kn[1], kn[2], zeros(2 * dh)])] * G
    parts += [zeros(C_QD - C_CMP), jnp.tile((p["diff_q_norm"] * qs).reshape(-1), DIFF_HEADS),
              jnp.tile(p["diff_k_norm"].reshape(-1), DIFF_HEADS), zeros(C_TOTAL - C_VD)]
    gain = jnp.concatenate(parts)
    assert gain.shape[0] == C_TOTAL
    return gain[None, :]


def _t5_bucket(dist):
    n = jnp.maximum(dist, 0)
    max_exact = REL_BUCKETS // 2
    nf = jnp.maximum(n, 1).astype(F32)
    large = max_exact + (jnp.log(nf / max_exact) / math.log(REL_MAX_DIST / max_exact)
                         * (REL_BUCKETS - max_exact)).astype(jnp.int32)
    large = jnp.minimum(large, REL_BUCKETS - 1)
    return jnp.where(n < max_exact, n, large)


def _toeplitz(v_ext, rows, cols):
    length = v_ext.shape[-1]
    lead = v_ext.shape[:-1]
    flat = jnp.broadcast_to(v_ext[..., None, :], lead + (rows, length)).reshape(lead + (rows * length,))
    return flat[..., :rows * (length - 1)].reshape(lead + (rows, length - 1))[..., :cols]


def _bias_tables(rel_bias, S):
    heads = rel_bias.shape[1]
    far = REL_MAX_DIST
    assert TQ == 2 * far and LANE == far
    bd = rel_bias[_t5_bucket(jnp.arange(2 * far))].T.astype(F32) * LOG2E
    rel = bd - bd[:, far:far + 1]
    neg = lambda *shape: jnp.full((heads,) + shape, NEG, F32)
    zero = lambda *shape: jnp.zeros((heads,) + shape, F32)
    rev = lambda a: a[:, ::-1]
    d0 = _toeplitz(jnp.concatenate([rel[:, 0:1], neg(far), rev(rel[:, 1:far])], axis=1), far, far)
    d1 = _toeplitz(jnp.concatenate([rev(rel[:, 1:far + 1]), zero(far)], axis=1), far, far)
    z, n = zero(far, far), neg(far, far)
    blocks = lambda rows: jnp.concatenate([jnp.concatenate(r, axis=2) for r in rows], axis=1)
    near = jnp.stack([blocks([[d0, n, n, n], [d1, d0, n, n]]),
                      blocks([[z, d1, d0, n], [z, z, d1, d0]])], axis=1)
    hb = rel[:NSA_HEADS]
    zero8 = lambda *shape: jnp.zeros((NSA_HEADS,) + shape, F32)
    neg8 = lambda *shape: jnp.full((NSA_HEADS,) + shape, NEG, F32)
    width = TQ + far - 1
    tp = _toeplitz(jnp.concatenate([rev(hb[:, 0:far]), neg8(TQ + 1), zero8(TQ - 1)], axis=1), TQ, width)
    n_tiles = S // TQ
    per_tile = TQ // CMP_STRIDE
    col0 = per_tile * (n_tiles - 1)
    shift = CMP_STRIDE * col0 - (CMP_BLOCK - 1)
    first = -(-(shift - (far - 1)) // CMP_STRIDE)
    last = (shift + TQ - 1) // CMP_STRIDE
    band = tp[:, :, CMP_STRIDE * first - shift + far - 1::CMP_STRIDE][:, :, :last - first + 1]
    g2 = jnp.concatenate([zero8(TQ, first), band, neg8(TQ, 2 * TQ - last - 1)], axis=2)
    bias_cmp = jnp.stack([g2[:, :, col0 - per_tile * i:col0 - per_tile * i + LANE] for i in range(n_tiles)],
                         axis=1)
    n_cmp = (S - CMP_BLOCK) // CMP_STRIDE + 1
    bias_cmp = jnp.where(np.arange(LANE) < n_cmp, bias_cmp, NEG)
    return near, bias_cmp.reshape(NSA_HEADS, S, LANE)


def _overlap_matrix(S):
    n_cmp = (S - CMP_BLOCK) // CMP_STRIDE + 1
    n_slc = S // SLC_BLOCK
    c_start = np.arange(n_cmp) * CMP_STRIDE
    s_start = np.arange(n_slc) * SLC_BLOCK
    lo = np.maximum(s_start[:, None], c_start[None, :])
    hi = np.minimum(s_start[:, None] + SLC_BLOCK, c_start[None, :] + CMP_BLOCK)
    overlap = (np.maximum(hi - lo, 0) / CMP_BLOCK).astype(np.float32)
    out = np.zeros((n_slc, LANE), np.float32)
    out[:, :n_cmp] = overlap
    return out


IN_CHUNK = 512
HEAD_NORM_CHUNKS = {C_QN // IN_CHUNK: "all", C_KV // IN_CHUNK: "keys",
                    C_QD // IN_CHUNK: "all", C_KD // IN_CHUNK: "all"}


def _in_proj_kernel(x_ref, g_ref, w_ref, hg_ref, seg_ref, o_ref):
    dh = HEAD_DIM
    h = _rms(x_ref[...], g_ref[...]).astype(BF16)
    for c in range(C_TOTAL // IN_CHUNK):
        cols = slice(c * IN_CHUNK, (c + 1) * IN_CHUNK)
        y = jnp.dot(h, w_ref[:, cols], preferred_element_type=F32)
        mode = HEAD_NORM_CHUNKS.get(c)
        if mode is not None:
            ss = jnp.dot((y * y).astype(BF16), seg_ref[...], preferred_element_type=F32)
            scale = lax.rsqrt(ss * (1.0 / dh) + RMS_EPS) * hg_ref[:, cols]
            if mode == "keys":
                lane = lax.broadcasted_iota(jnp.int32, (1, IN_CHUNK), 1)
                scale = jnp.where(lane % (4 * dh) < 2 * dh, scale, 1.0)
            y = y * scale
        o_ref[:, cols] = y.astype(BF16)


def _in_proj(x2d, gain, w, head_gain):
    n, d = x2d.shape
    tm = 256
    seg = np.arange(IN_CHUNK) // HEAD_DIM
    seg_ones = jnp.asarray((seg[:, None] == seg[None, :]).astype(np.float32), BF16)
    return pl.pallas_call(
        _in_proj_kernel,
        grid=(n // tm,),
        in_specs=[pl.BlockSpec((tm, d), lambda i: (i, 0)),
                  pl.BlockSpec((1, d), lambda i: (0, 0)),
                  pl.BlockSpec((d, C_TOTAL), lambda i: (0, 0)),
                  pl.BlockSpec((1, C_TOTAL), lambda i: (0, 0)),
                  pl.BlockSpec((IN_CHUNK, IN_CHUNK), lambda i: (0, 0))],
        out_specs=pl.BlockSpec((tm, C_TOTAL), lambda i: (i, 0)),
        out_shape=jax.ShapeDtypeStruct((n, C_TOTAL), BF16),
        compiler_params=_cparams(("arbitrary",)),
        name="in_proj",
    )(x2d, gain, w, head_gain, seg_ones)


def _compress_kernel(t_ref, pos_ref, w1_ref, w2_ref, kg_ref, k_ref, v_ref, tok_scr):
    S, dh = t_ref.shape[0], HEAD_DIM
    m = S // CMP_STRIDE
    tok_scr[0:S, :] = t_ref[...].astype(F32)
    tok_scr[S:, :] = jnp.zeros((CMP_STRIDE, LANE), F32)
    pre = jnp.zeros((m, 2 * CMP_HIDDEN), F32)
    for l in range(CMP_BLOCK):
        x = tok_scr[pl.ds(l, m, stride=CMP_STRIDE), :] + pos_ref[l:l + 1, :]
        pre = pre + jnp.dot(x.astype(BF16), w1_ref[l], preferred_element_type=F32)
    out = jnp.dot(jax.nn.gelu(pre).astype(BF16), w2_ref[...], preferred_element_type=F32)
    k_ref[...] = jnp.concatenate([_rms(out[:, :dh], kg_ref[...]), jnp.zeros((m, LANE - dh), F32)],
                                 axis=1).astype(BF16)
    v_ref[...] = out[:, dh:].astype(BF16)


def _compress(proj, pos, w1, w2, k_gain):
    B, S, _ = proj.shape
    G, m = NSA_KV_GROUPS, S // CMP_STRIDE
    outs = [jax.ShapeDtypeStruct((B, G, m, width), BF16) for width in (LANE, HEAD_DIM)]
    ospecs = [pl.BlockSpec((None, None, m, width), lambda b, g: (b, g, 0, 0)) for width in (LANE, HEAD_DIM)]
    return pl.pallas_call(
        _compress_kernel,
        grid=(B, G),
        in_specs=[pl.BlockSpec((None, S, LANE), lambda b, g: (b, 0, C_CMP // LANE + g)),
                  pl.BlockSpec(pos.shape, lambda b, g: (0, 0)),
                  pl.BlockSpec(w1.shape, lambda b, g: (0, 0, 0)),
                  pl.BlockSpec(w2.shape, lambda b, g: (0, 0)),
                  pl.BlockSpec((1, HEAD_DIM), lambda b, g: (0, 0))],
        out_specs=ospecs,
        out_shape=outs,
        scratch_shapes=[pltpu.VMEM((S + CMP_STRIDE, LANE), F32)],
        compiler_params=_cparams(("arbitrary", "arbitrary")),
        name="nsa_compress",
    )(proj, pos, w1, w2, k_gain)


def _compress_params(cmp_pos, cmp_w1, cmp_w2):
    dh, hid = HEAD_DIM, CMP_HIDDEN
    w1 = cmp_w1.reshape(2, CMP_BLOCK, dh, hid)
    z1 = jnp.zeros((CMP_BLOCK, dh, hid), cmp_w1.dtype)
    w1 = jnp.concatenate([jnp.concatenate([w1[0], z1], axis=2),
                          jnp.concatenate([z1, w1[1]], axis=2)], axis=1)
    z2 = jnp.zeros((hid, dh), cmp_w2.dtype)
    w2 = jnp.concatenate([jnp.concatenate([cmp_w2[0], z2], axis=1),
                          jnp.concatenate([z2, cmp_w2[1]], axis=1)], axis=0)
    pos = jnp.concatenate([cmp_pos[0], cmp_pos[1]], axis=1)
    return pos, w1.astype(BF16), w2.astype(BF16)


def _dot_nt(a, b):
    return lax.dot_general(a, b, (((1,), (1,)), ((), ())), preferred_element_type=F32)


def _softmax_step(s, v, carry):
    m, l, acc = carry
    m_new = jnp.maximum(m, jnp.max(s, axis=-1, keepdims=True))
    alpha = jnp.exp2(m - m_new)
    p = jnp.exp2(s - m_new)
    l = alpha * l + jnp.sum(p, axis=-1, keepdims=True)
    acc = alpha * acc + jnp.dot(p.astype(BF16), v, preferred_element_type=F32)
    return m_new, l, acc


def _softmax_init(rows, dv):
    return (jnp.full((rows, 1), NEG, F32), jnp.zeros((rows, 1), F32), jnp.zeros((rows, dv), F32))


def _key_chunks(c):
    T = TQ
    chunks = [(0, T, (0, 0))] if c == 0 else [((c - 1) * T, 2 * T, (1, 0))]
    n_far = max(c - 1, 0)
    if n_far % 2:
        chunks.append((0, T, None))
    chunks += [(s * T, 2 * T, None) for s in range(n_far % 2, n_far, 2)]
    return chunks


def _nsa_kernel(q_ref, kv_ref, kc_ref, vc_ref, gate_ref, bc_ref, nb_ref, wm_ref, ov_ref, ex_ref, gx_ref,
                o_ref, mb_scr, osel_scr, *, seq_len):
    T, dh, Hg = TQ, HEAD_DIM, NSA_HPG
    R = Hg * T
    qi = pl.program_id(2)
    n_slc = seq_len // SLC_BLOCK

    lower = lax.broadcasted_iota(jnp.int32, (T, 2 * dh), 1) < dh
    q_lo, q_hi = [], []
    for pair in range(Hg // 2):
        pq = q_ref[:, pair * 2 * dh:(pair + 1) * 2 * dh]
        sw = jnp.concatenate([pq[:, dh:], pq[:, :dh]], axis=1)
        zero = jnp.zeros_like(pq)
        q_lo += [jnp.where(lower, pq, zero), jnp.where(lower, sw, zero)]
        q_hi += [jnp.where(lower, zero, sw), jnp.where(lower, zero, pq)]
    q = jnp.concatenate(q_lo, axis=0)
    q_win = jnp.concatenate(q_hi, axis=0)

    lc = _dot_nt(q, kc_ref[...]) + bc_ref[...].reshape(R, LANE)
    pc = jnp.exp2(lc - jnp.max(lc, axis=-1, keepdims=True))
    row = lax.broadcasted_iota(jnp.int32, (Hg, T, 1), 1).reshape(R, 1)
    visible = (qi * T + row >= CMP_BLOCK - 1).astype(F32)
    pc = pc * (visible / jnp.sum(pc, axis=-1, keepdims=True))
    o_cmp = jnp.dot(pc.astype(BF16), vc_ref[...], preferred_element_type=F32)

    n_all = SLC_TOP_N * SLC_BLOCK // T

    @pl.when(qi >= n_all)
    def _():
        p_sum = pc[0:T] + pc[T:2 * T] + pc[2 * T:3 * T] + pc[3 * T:4 * T]
        imp = lax.dot_general(ov_ref[...], p_sum, (((1,), (1,)), ((), ())), preferred_element_type=F32,
                              precision=lax.Precision.HIGHEST)
        blk = lax.broadcasted_iota(jnp.int32, (n_slc, T), 0)
        cur = (qi * T + lax.broadcasted_iota(jnp.int32, (n_slc, T), 1)) // SLC_BLOCK
        forced = (blk == 0) | (blk == cur) | (blk == cur - 1)
        score = jnp.where(blk <= cur, imp + jnp.where(forced, FORCE_SCORE, 0.0), NEG)
        rank = jnp.zeros((n_slc, T), jnp.int32)
        for i in range(n_slc):
            other = score[i:i + 1, :]
            ahead = (other > score) | ((other == score) & (blk > i))
            rank = rank + ahead.astype(jnp.int32)
        sel_bias = jnp.where(rank < SLC_TOP_N, 0.0, NEG).T.astype(BF16)
        mb_scr[...] = jnp.dot(sel_bias, ex_ref[...], preferred_element_type=F32)

    def sel_tile(c):
        carry = _softmax_init(R, 2 * dh)
        for start, width, table in _key_chunks(c):
            s = _dot_nt(q, kv_ref[start:start + width, 0:2 * dh]).reshape(Hg, T, width)
            if table is not None:
                s = s + nb_ref[table[0], :, :, table[1]:table[1] + width]
            if c >= n_all:
                s = s + mb_scr[:, start:start + width][None]
            carry = _softmax_step(s.reshape(R, width), kv_ref[start:start + width, 2 * dh:4 * dh], carry)
        osel_scr[...] = carry[2] / carry[1]

    for c in range(seq_len // T):
        pl.when(qi == c)(functools.partial(sel_tile, c))
    o_slc = osel_scr[...]

    def win_step(start, width, bias, carry):
        s = _dot_nt(q_win, kv_ref[pl.ds(start, width), 0:2 * dh]).reshape(Hg, T, width) + bias
        return _softmax_step(s.reshape(R, width), kv_ref[pl.ds(start, width), 2 * dh:4 * dh], carry)

    carry = win_step(pl.multiple_of(jnp.maximum(qi - 1, 0) * T, T), 2 * T, nb_ref[jnp.minimum(qi, 1)],
                     _softmax_init(R, 2 * dh))
    carry = lax.fori_loop(
        0, (qi >= 2).astype(jnp.int32),
        lambda j, c: win_step(pl.multiple_of((qi - 2) * T, T), T, wm_ref[...][None], c), carry)
    o_win = carry[2] / carry[1]

    gates = jax.nn.sigmoid(gate_ref[...].astype(F32))
    g_hi = gates.astype(BF16)
    g_lo = (gates - g_hi.astype(F32)).astype(BF16)
    g_wide = (jnp.dot(g_hi, gx_ref[...], preferred_element_type=F32)
              + jnp.dot(g_lo, gx_ref[...], preferred_element_type=F32))
    by_head = lambda o, lo: jnp.concatenate([o[h * T:(h + 1) * T, lo:lo + dh] for h in range(Hg)], axis=1)
    w = Hg * dh
    out = (g_wide[:, 0:w] * by_head(o_cmp, 0) + g_wide[:, w:2 * w] * by_head(o_slc, 0)
           + g_wide[:, 2 * w:3 * w] * by_head(o_win, dh))
    o_ref[...] = out.astype(BF16)


def _nsa_attention(proj, k_cmp, v_cmp, bias_cmp, near_bias, win_mask, overlap, expand):
    B, S, _ = proj.shape
    G, Hg, dh, T = NSA_KV_GROUPS, NSA_HPG, HEAD_DIM, TQ
    assert WINDOW == 2 * T and S % (2 * T) == 0
    n_slc = S // SLC_BLOCK
    gate_expand = np.zeros((LANE, 3 * Hg * dh), np.float32)
    for j in range(3 * Hg):
        gate_expand[j, j * dh:(j + 1) * dh] = 1.0
    gate_expand = jnp.asarray(gate_expand, BF16)
    return pl.pallas_call(
        functools.partial(_nsa_kernel, seq_len=S),
        grid=(B, G, S // T),
        in_specs=[
            pl.BlockSpec((None, T, Hg * dh), lambda b, g, i: (b, i, C_QN // (Hg * dh) + g)),
            pl.BlockSpec((None, S, 4 * dh), lambda b, g, i: (b, 0, C_KV // (4 * dh) + g)),
            pl.BlockSpec((None, None, LANE, LANE), lambda b, g, i: (b, g, 0, 0)),
            pl.BlockSpec((None, None, LANE, dh), lambda b, g, i: (b, g, 0, 0)),
            pl.BlockSpec((None, T, LANE), lambda b, g, i: (b, i, C_GATE // LANE + g)),
            pl.BlockSpec((None, Hg, T, LANE), lambda b, g, i: (g, 0, i, 0)),
            pl.BlockSpec((None, 2, Hg, T, 2 * T), lambda b, g, i: (g, 0, 0, 0, 0)),
            pl.BlockSpec((T, T), lambda b, g, i: (0, 0)),
            pl.BlockSpec((n_slc, LANE), lambda b, g, i: (0, 0)),
            pl.BlockSpec((n_slc, S), lambda b, g, i: (0, 0)),
            pl.BlockSpec(gate_expand.shape, lambda b, g, i: (0, 0)),
        ],
        out_specs=pl.BlockSpec((None, T, Hg * dh), lambda b, g, i: (b, i, g)),
        out_shape=jax.ShapeDtypeStruct((B, S, NSA_HEADS * dh), BF16),
        scratch_shapes=[pltpu.VMEM((T, S), F32), pltpu.VMEM((Hg * T, 2 * dh), F32)],
        compiler_params=_cparams(("arbitrary", "arbitrary", "arbitrary")),
        name="nsa_attention",
    )(proj, proj, k_cmp, v_cmp, proj, bias_cmp, near_bias, win_mask, overlap, expand, gate_expand)


def _diff_kernel(q_ref, k_ref, v_ref, lam_ref, og_ref, nb_ref, o_ref, *, lambda_init, n_tiles):
    T, dh = TQ, HEAD_DIM
    qi = pl.program_id(2)
    q = q_ref[...]
    lower = lax.broadcasted_iota(jnp.int32, q.shape, 1) < dh
    q1 = jnp.where(lower, q, jnp.zeros_like(q))
    q2 = jnp.where(lower, jnp.zeros_like(q), q)

    def step(start, width, bias, carry):
        c1, c2 = carry
        v = v_ref[start:start + width, :]
        k = k_ref[start:start + width, :]
        s1 = _dot_nt(q1, k)
        s2 = _dot_nt(q2, k)
        if bias is not None:
            s1, s2 = s1 + bias, s2 + bias
        return _softmax_step(s1, v, c1), _softmax_step(s2, v, c2)

    lam_p = lam_ref[...]
    lam = (jnp.exp(jnp.sum(lam_p[0:1] * lam_p[1:2], axis=-1, keepdims=True))
           - jnp.exp(jnp.sum(lam_p[2:3] * lam_p[3:4], axis=-1, keepdims=True)) + lambda_init)

    def tile(c):
        init = _softmax_init(T, DIFF_V_DIM)
        carry = (init, init)
        for start, width, table in _key_chunks(c):
            bias = None if table is None else nb_ref[table[0], :, table[1]:table[1] + width]
            carry = step(start, width, bias, carry)
        (_, l1, a1), (_, l2, a2) = carry
        o = a1 / l1 - lam * (a2 / l2)
        o_ref[...] = (_rms(o, og_ref[...]) * (1.0 - lambda_init)).astype(BF16)

    for c in range(n_tiles):
        pl.when(qi == c)(functools.partial(tile, c))


def _diff_attention(proj, lam_p, out_gain, near_bias, lambda_init):
    B, S, _ = proj.shape
    T, dh, H = TQ, HEAD_DIM, DIFF_HEADS
    return pl.pallas_call(
        functools.partial(_diff_kernel, lambda_init=lambda_init, n_tiles=S // T),
        grid=(B, H, S // T),
        in_specs=[
            pl.BlockSpec((None, T, 2 * dh), lambda b, h, i: (b, i, C_QD // (2 * dh) + h)),
            pl.BlockSpec((None, S, 2 * dh), lambda b, h, i: (b, 0, C_KD // (2 * dh) + h)),
            pl.BlockSpec((None, S, DIFF_V_DIM), lambda b, h, i: (b, 0, C_VD // DIFF_V_DIM + h)),
            pl.BlockSpec((4, dh), lambda b, h, i: (0, 0)),
            pl.BlockSpec((1, DIFF_V_DIM), lambda b, h, i: (0, 0)),
            pl.BlockSpec((None, 2, T, 2 * T), lambda b, h, i: (h, 0, 0, 0)),
        ],
        out_specs=pl.BlockSpec((None, T, DIFF_V_DIM), lambda b, h, i: (b, i, h)),
        out_shape=jax.ShapeDtypeStruct((B, S, H * DIFF_V_DIM), BF16),
        compiler_params=_cparams(("arbitrary", "arbitrary", "arbitrary")),
        name="diff_attention",
    )(proj, proj, proj, lam_p, out_gain, near_bias)


def _merge_kernel(x_ref, on_ref, od_ref, gm_ref, wn_ref, wd_ref, wo_ref, g2_ref, rwh_ref, rwl_ref, rb_ref,
                  o_ref):
    d = x_ref.shape[-1]
    a = jnp.dot(on_ref[...], wn_ref[...], preferred_element_type=F32)
    b = jnp.dot(od_ref[...], wd_ref[...], preferred_element_type=F32)
    y = (jax.nn.sigmoid(gm_ref[:, :d].astype(F32)) * a
         + jax.nn.sigmoid(gm_ref[:, d:].astype(F32)) * b)
    x2 = x_ref[...] + jnp.dot(y.astype(BF16), wo_ref[...], preferred_element_type=F32)
    o_ref[:, :d] = x2

    h2 = _rms(x2, g2_ref[...])
    h_hi = h2.astype(BF16)
    h_lo = (h2 - h_hi.astype(F32)).astype(BF16)
    logits = (jnp.dot(h_hi, rwh_ref[...], preferred_element_type=F32)
              + jnp.dot(h_lo, rwh_ref[...], preferred_element_type=F32)
              + jnp.dot(h_hi, rwl_ref[...], preferred_element_type=F32)) + rb_ref[...]
    lane = lax.broadcasted_iota(jnp.int32, logits.shape, 1)
    big = jnp.int32(LANE)

    def first_argmax(vals, mask):
        vals = jnp.where(mask, vals, -jnp.inf)
        top = jnp.max(vals, axis=-1, keepdims=True)
        idx = jnp.min(jnp.where(vals == top, lane, big), axis=-1, keepdims=True)
        return top, idx

    is_group = lane < N_GROUPS
    g_max, g_top = first_argmax(logits, is_group)
    g_top_p = 1.0 / jnp.sum(jnp.where(is_group, jnp.exp(logits - g_max), 0.0), axis=-1, keepdims=True)
    lo = N_GROUPS + g_top * EXPERTS_PER_GROUP
    in_group = (lane >= lo) & (lane < lo + EXPERTS_PER_GROUP)
    m1, i1 = first_argmax(logits, in_group)
    m2, i2 = first_argmax(logits, in_group & (lane != i1))
    e2 = jnp.exp(m2 - m1)
    w1 = g_top_p / (1.0 + e2)
    w2 = g_top_p * e2 / (1.0 + e2)
    la = jnp.minimum(i1, i2) - lo
    lb = jnp.maximum(i1, i2) - lo
    pair = jnp.right_shift(la * (7 - la), 1) + (lb - la - 1)
    bucket = (g_top * N_PAIRS + pair).astype(F32)
    exp_lane = lane + N_GROUPS
    route = (jnp.where(exp_lane == i1, w1, 0.0) + jnp.where(exp_lane == i2, w2, 0.0)
             + jnp.where(lane == N_EXPERTS, bucket, 0.0))
    o_ref[:, d:] = route


def _merge(x2d, o_nsa, o_diff, proj2d, wn, wd, wo, g2, rw, rb):
    n, d = x2d.shape
    tm = 512
    rw_hi = rw.astype(BF16)
    rw_lo = (rw - rw_hi.astype(F32)).astype(BF16)
    full = lambda a: pl.BlockSpec(a.shape, lambda i: (0,) * a.ndim)
    return pl.pallas_call(
        _merge_kernel,
        grid=(n // tm,),
        in_specs=[pl.BlockSpec((tm, d), lambda i: (i, 0)),
                  pl.BlockSpec((tm, o_nsa.shape[1]), lambda i: (i, 0)),
                  pl.BlockSpec((tm, o_diff.shape[1]), lambda i: (i, 0)),
                  pl.BlockSpec((tm, 2 * d), lambda i: (i, C_GM // (2 * d))),
                  full(wn), full(wd), full(wo), full(g2), full(rw_hi), full(rw_lo), full(rb)],
        out_specs=pl.BlockSpec((tm, d + LANE), lambda i: (i, 0)),
        out_shape=jax.ShapeDtypeStruct((n, d + LANE), F32),
        compiler_params=_cparams(("arbitrary",)),
        name="merge_router",
    )(x2d, o_nsa, o_diff, proj2d, wn, wd, wo, g2, rw_hi, rw_lo, rb)


def _moe_kernel(tile_ref, ea_ref, eb_ref, lo_ref, hi_ref, first_ref, last_ref, perm_ref,
                x_hbm, g2_ref, wga_ref, wua_ref, wda_ref, wgb_ref, wub_ref, wdb_ref,
                o_hbm, x_buf, acc_buf, h_scr, g_sem, s_sem, *, tm, n_tiles):
    d = o_hbm.shape[1]
    v = pl.program_id(0)
    t = tile_ref[v]
    slot = t % 2

    def gather_copy(tt, sl, r):
        return pltpu.make_async_copy(x_hbm.at[pl.ds(perm_ref[tt * tm + r], 1), :],
                                     x_buf.at[sl, pl.ds(r, 1), :], g_sem.at[sl])

    def scatter_copy(tt, sl, r):
        return pltpu.make_async_copy(acc_buf.at[sl, pl.ds(r, 1), :],
                                     o_hbm.at[pl.ds(perm_ref[tt * tm + r], 1), :], s_sem.at[sl])

    def for_rows(fn):
        for r in range(tm):
            fn(r)

    @pl.when(v == 0)
    def _():
        for_rows(lambda r: gather_copy(0, 0, r).start())

    @pl.when(first_ref[v] == 1)
    def _():
        for_rows(lambda r: gather_copy(t, slot, r).wait())

        @pl.when(t + 1 < n_tiles)
        def _():
            for_rows(lambda r: gather_copy(t + 1, 1 - slot, r).start())

        @pl.when(t >= 2)
        def _():
            for_rows(lambda r: scatter_copy(t - 2, slot, r).wait())

        xs = x_buf[slot, :, :d]
        h_scr[...] = _rms(xs, g2_ref[...]).astype(BF16)
        acc_buf[slot] = xs

    h = h_scr[...]
    cw = x_buf[slot, :, d:]
    lane = lax.broadcasted_iota(jnp.int32, cw.shape, 1)
    row = lax.broadcasted_iota(jnp.int32, (tm, 1), 0)
    in_bucket = (row >= lo_ref[v]) & (row < hi_ref[v])

    def expert(e, wg_ref, wu_ref, wd_ref):
        w = jnp.sum(jnp.where(lane == e, cw, 0.0), axis=-1, keepdims=True)
        w = jnp.where(in_bucket, w, 0.0)
        gate = jnp.dot(h, wg_ref[...], preferred_element_type=F32)
        up = jnp.dot(h, wu_ref[...], preferred_element_type=F32)
        hid = jax.nn.silu(gate) * up * w
        return jnp.dot(hid.astype(BF16), wd_ref[...], preferred_element_type=F32)

    acc_buf[slot] += (expert(ea_ref[v], wga_ref, wua_ref, wda_ref)
                      + expert(eb_ref[v], wgb_ref, wub_ref, wdb_ref))

    @pl.when(last_ref[v] == 1)
    def _():
        for_rows(lambda r: scatter_copy(t, slot, r).start())

    @pl.when(v == pl.num_programs(0) - 1)
    def _():
        for_rows(lambda r: scatter_copy(t - 1, 1 - slot, r).wait())
        for_rows(lambda r: scatter_copy(t, slot, r).wait())


def _moe(sched, perm, x2r, g2, wg, wu, wd, tm):
    n = x2r.shape[0]
    d, f = wg.shape[1], wg.shape[2]
    n_visits = sched[0].shape[0]
    n_tiles = n // tm
    assert n_tiles >= 2
    w_in_a = pl.BlockSpec((None, d, f), lambda v, t, ea, eb, *_: (ea[v], 0, 0))
    w_in_b = pl.BlockSpec((None, d, f), lambda v, t, ea, eb, *_: (eb[v], 0, 0))
    w_out_a = pl.BlockSpec((None, f, d), lambda v, t, ea, eb, *_: (ea[v], 0, 0))
    w_out_b = pl.BlockSpec((None, f, d), lambda v, t, ea, eb, *_: (eb[v], 0, 0))
    return pl.pallas_call(
        functools.partial(_moe_kernel, tm=tm, n_tiles=n_tiles),
        grid_spec=pltpu.PrefetchScalarGridSpec(
            num_scalar_prefetch=8,
            grid=(n_visits,),
            in_specs=[pl.BlockSpec(memory_space=pl.ANY),
                      pl.BlockSpec((1, d), lambda v, *_: (0, 0)),
                      w_in_a, w_in_a, w_out_a, w_in_b, w_in_b, w_out_b],
            out_specs=pl.BlockSpec(memory_space=pl.ANY),
            scratch_shapes=[pltpu.VMEM((2, tm, x2r.shape[1]), F32), pltpu.VMEM((2, tm, d), F32),
                            pltpu.VMEM((tm, d), BF16),
                            pltpu.SemaphoreType.DMA((2,)), pltpu.SemaphoreType.DMA((2,))]),
        out_shape=jax.ShapeDtypeStruct((n, d), F32),
        compiler_params=_cparams(("arbitrary",)),
        name="moe_experts",
    )(*sched, perm, x2r, g2, wg, wu, wd, wg, wu, wd)


def _moe_schedule(bucket, tm):
    n = bucket.shape[0]
    n_tiles = n // tm
    n_visits = n_tiles + N_BUCKETS - 1
    counts = jnp.sum((bucket[:, None] == jnp.arange(N_BUCKETS)[None, :]).astype(jnp.int32), axis=0)
    ends = jnp.cumsum(counts)
    starts = ends - counts
    first_tile = starts // tm
    last_tile = jnp.maximum(ends - 1, 0) // tm
    n_vis = jnp.where(counts > 0, last_tile - first_tile + 1, 0)
    vis_end = jnp.cumsum(n_vis)
    vis_start = vis_end - n_vis
    v = jnp.arange(n_visits, dtype=jnp.int32)
    b = jnp.minimum(jnp.searchsorted(vis_end, v, side="right"), N_BUCKETS - 1).astype(jnp.int32)
    live = v < vis_end[-1]
    tile = jnp.where(live, first_tile[b] + v - vis_start[b], n_tiles - 1).astype(jnp.int32)
    lo = jnp.where(live, jnp.clip(starts[b] - tile * tm, 0, tm), 0).astype(jnp.int32)
    hi = jnp.where(live, jnp.clip(ends[b] - tile * tm, 0, tm), 0).astype(jnp.int32)
    b_last = b[jnp.maximum(vis_end[-1] - 1, 0)]
    b = jnp.where(live, b, b_last)
    pair_a = np.array([0, 0, 0, 1, 1, 2], np.int32)
    pair_b = np.array([1, 2, 3, 2, 3, 3], np.int32)
    grp = b // N_PAIRS
    ea = (grp * EXPERTS_PER_GROUP + jnp.asarray(pair_a)[b % N_PAIRS]).astype(jnp.int32)
    eb = (grp * EXPERTS_PER_GROUP + jnp.asarray(pair_b)[b % N_PAIRS]).astype(jnp.int32)
    changed = (tile[1:] != tile[:-1]).astype(jnp.int32)
    one = jnp.ones((1,), jnp.int32)
    first = jnp.concatenate([one, changed])
    last = jnp.concatenate([changed, one])
    return tile, ea, eb, lo, hi, first, last


def _layer(x, layer, p):
    B, S, D = x.shape
    G, dh = NSA_KV_GROUPS, HEAD_DIM
    lambda_init = 0.8 - 0.6 * math.exp(-0.3 * layer)
    x2d = x.reshape(B * S, D)

    proj2d = _in_proj(x2d, p["norm1_g"][None, :], _reorder_w_in(p["w_in"]), _head_norm_rows(p))
    proj = proj2d.reshape(B, S, C_TOTAL)

    k_cmp, v_cmp = _compress(proj, *_compress_params(p["cmp_pos"], p["cmp_w1"], p["cmp_w2"]),
                             p["nsa_k_norm"][0:1])

    near, bias_cmp = _bias_tables(p["rel_bias"], S)
    nsa_near = near[:NSA_HEADS].reshape(G, NSA_HPG, 2, TQ, 2 * TQ).transpose(0, 2, 1, 3, 4)
    i = np.arange(TQ)
    win_mask = jnp.asarray(np.where(i[None, :] > i[:, None], 0.0, NEG).astype(np.float32))
    expand = jnp.asarray(np.repeat(np.eye(S // SLC_BLOCK, dtype=np.float32), SLC_BLOCK, axis=1), BF16)
    o_nsa = _nsa_attention(proj, k_cmp, v_cmp, bias_cmp.reshape(G, NSA_HPG, S, LANE), nsa_near, win_mask,
                           jnp.asarray(_overlap_matrix(S)), expand)
    o_diff = _diff_attention(proj, p["diff_lambda"], p["diff_out_norm"][None, :], near[NSA_HEADS:],
                             lambda_init)

    rw = jnp.concatenate([p["router_group_w"], p["router_expert_w"],
                          jnp.zeros((D, LANE - N_GROUPS - N_EXPERTS), F32)], axis=1)
    rb = jnp.concatenate([p["router_group_b"], p["router_expert_b"],
                          jnp.zeros((LANE - N_GROUPS - N_EXPERTS,), F32)])[None, :]
    x2r = _merge(x2d, o_nsa.reshape(B * S, -1), o_diff.reshape(B * S, -1), proj2d,
                 p["w_branch_nsa"].astype(BF16), p["w_branch_diff"].astype(BF16),
                 p["w_out"].astype(BF16), p["norm2_g"][None, :], rw, rb)

    tm = 256
    bucket = x2r[:, D + N_EXPERTS].astype(jnp.int32)
    perm = jnp.argsort(bucket).astype(jnp.int32)
    out = _moe(_moe_schedule(bucket, tm), perm, x2r, p["norm2_g"][None, :],
               p["expert_w_gate"].astype(BF16), p["expert_w_up"].astype(BF16),
               p["expert_w_down"].astype(BF16), tm)
    return out.reshape(B, S, D)


def kernel(x, norm1_g, w_in, nsa_q_norm, nsa_k_norm, cmp_pos, cmp_w1, cmp_w2, diff_q_norm, diff_k_norm,
           diff_lambda, diff_out_norm, rel_bias, w_branch_nsa, w_branch_diff, w_out, norm2_g,
           router_group_w, router_group_b, router_expert_w, router_expert_b, expert_w_gate,
           expert_w_up, expert_w_down):
    per_layer = dict(norm1_g=norm1_g, w_in=w_in, nsa_q_norm=nsa_q_norm, nsa_k_norm=nsa_k_norm,
                     cmp_pos=cmp_pos, cmp_w1=cmp_w1, cmp_w2=cmp_w2, diff_q_norm=diff_q_norm,
                     diff_k_norm=diff_k_norm, diff_lambda=diff_lambda, diff_out_norm=diff_out_norm,
                     w_branch_nsa=w_branch_nsa, w_branch_diff=w_branch_diff, w_out=w_out,
                     norm2_g=norm2_g, router_group_w=router_group_w, router_group_b=router_group_b,
                     router_expert_w=router_expert_w, router_expert_b=router_expert_b,
                     expert_w_gate=expert_w_gate, expert_w_up=expert_w_up, expert_w_down=expert_w_down)
    for layer in range(norm1_g.shape[0]):
        p = {k: v[layer] for k, v in per_layer.items()}
        p["rel_bias"] = rel_bias
        x = _layer(x, layer, p)
    return x
```

```python
import functools
import math

import numpy as np
import jax
import jax.numpy as jnp
from jax import lax
from jax.experimental import pallas as pl
from jax.experimental.pallas import tpu as pltpu

F32 = jnp.float32
BF16 = jnp.bfloat16

HEAD_DIM = 64
SCALE = HEAD_DIM ** -0.5
NSA_HEADS = 8
NSA_KV_GROUPS = 2
NSA_HPG = NSA_HEADS // NSA_KV_GROUPS
CMP_BLOCK = 32
CMP_STRIDE = 16
CMP_HIDDEN = 256
SLC_BLOCK = 64
SLC_TOP_N = 16
WINDOW = 512
FORCE_SCORE = 1e4
DIFF_HEADS = 4
DIFF_V_DIM = 2 * HEAD_DIM
REL_BUCKETS = 32
REL_MAX_DIST = 128
N_GROUPS = 4
EXPERTS_PER_GROUP = 4
N_EXPERTS = N_GROUPS * EXPERTS_PER_GROUP
EXPERT_HIDDEN = 512
RMS_EPS = 1e-6
NEG = -1e30
LOG2E = math.log2(math.e)

LANE = 128
MXU_DIM = 256
TQ = 256
N_PAIRS = 6
N_BUCKETS = N_GROUPS * N_PAIRS

C_GM = 0
C_QN = 2048
C_KV = 2560
C_CMP = 3072
C_GATE = 3328
C_QD = 3584
C_KD = 4096
C_VD = 4608
C_TOTAL = 5120

VMEM_LIMIT = 56 * 1024 * 1024


def _cparams(sem):
    return pltpu.CompilerParams(dimension_semantics=sem, vmem_limit_bytes=VMEM_LIMIT)


def _rms(x, gain):
    return x * lax.rsqrt(jnp.mean(x * x, axis=-1, keepdims=True) + RMS_EPS) * gain


def _reorder_w_in(w_in):
    d_model = w_in.shape[0]
    dh, G, Hg = HEAD_DIM, NSA_KV_GROUPS, NSA_HPG
    o_q = 0
    o_kv = o_q + NSA_HEADS * dh
    o_g = o_kv + 3 * 2 * G * dh
    o_qd = o_g + 3 * NSA_HEADS
    o_gm = o_qd + 3 * DIFF_HEADS * 2 * dh

    def cols(start, n):
        return w_in[:, start:start + n]

    def kv(kind, which, g):
        return cols(o_kv + ((kind * 2 + which) * G + g) * dh, dh)

    parts = [cols(o_gm, 2 * d_model), cols(o_q, NSA_HEADS * dh)]
    parts += [kv(kind, which, g) for g in range(G) for which in (0, 1) for kind in (1, 2)]
    parts += [kv(0, which, g) for g in range(G) for which in (0, 1)]
    for g in range(G):
        parts += [cols(o_g + (kind * G + g) * Hg, Hg) for kind in range(3)]
        parts.append(jnp.zeros((d_model, LANE - 3 * Hg), w_in.dtype))
    parts.append(cols(o_qd, 3 * DIFF_HEADS * 2 * dh))
    w = jnp.concatenate(parts, axis=1)
    assert w.shape[1] == C_TOTAL
    return w.astype(BF16)


def _head_norm_rows(p):
    dh, G = HEAD_DIM, NSA_KV_GROUPS
    qs = SCALE * LOG2E
    zeros = lambda n: jnp.zeros((n,), F32)
    kn = p["nsa_k_norm"]
    parts = [zeros(C_QN - C_GM), jnp.tile(p["nsa_q_norm"] * qs, NSA_HEADS)]
    parts += [jnp.concatenate([kn[1], kn[2], zeros(2 * dh)])] * G
    parts += [zeros(C_QD - C_CMP), jnp.tile((p["diff_q_norm"] * qs).reshape(-1), DIFF_HEADS),
              jnp.tile(p["diff_k_norm"].reshape(-1), DIFF_HEADS), zeros(C_TOTAL - C_VD)]
    gain = jnp.concatenate(parts)
    assert gain.shape[0] == C_TOTAL
    return gain[None, :]


def _t5_bucket(dist):
    n = jnp.maximum(dist, 0)
    max_exact = REL_BUCKETS // 2
    nf = jnp.maximum(n, 1).astype(F32)
    large = max_exact + (jnp.log(nf / max_exact) / math.log(REL_MAX_DIST / max_exact)
                         * (REL_BUCKETS - max_exact)).astype(jnp.int32)
    large = jnp.minimum(large, REL_BUCKETS - 1)
    return jnp.where(n < max_exact, n, large)


def _toeplitz(v_ext, rows, cols):
    length = v_ext.shape[-1]
    lead = v_ext.shape[:-1]
    flat = jnp.broadcast_to(v_ext[..., None, :], lead + (rows, length)).reshape(lead + (rows * length,))
    return flat[..., :rows * (length - 1)].reshape(lead + (rows, length - 1))[..., :cols]


def _bias_tables(rel_bias, S):
    heads = rel_bias.shape[1]
    far = REL_MAX_DIST
    assert TQ == 2 * far and LANE == far
    bd = rel_bias[_t5_bucket(jnp.arange(2 * far))].T.astype(F32) * LOG2E
    rel = bd - bd[:, far:far + 1]
    neg = lambda *shape: jnp.full((heads,) + shape, NEG, F32)
    zero = lambda *shape: jnp.zeros((heads,) + shape, F32)
    rev = lambda a: a[:, ::-1]
    d0 = _toeplitz(jnp.concatenate([rel[:, 0:1], neg(far), rev(rel[:, 1:far])], axis=1), far, far)
    d1 = _toeplitz(jnp.concatenate([rev(rel[:, 1:far + 1]), zero(far)], axis=1), far, far)
    z, n = zero(far, far), neg(far, far)
    blocks = lambda rows: jnp.concatenate([jnp.concatenate(r, axis=2) for r in rows], axis=1)
    near = jnp.stack([blocks([[d0, n, n, n], [d1, d0, n, n]]),
                      blocks([[z, d1, d0, n], [z, z, d1, d0]])], axis=1)
    hb = rel[:NSA_HEADS]
    zero8 = lambda *shape: jnp.zeros((NSA_HEADS,) + shape, F32)
    neg8 = lambda *shape: jnp.full((NSA_HEADS,) + shape, NEG, F32)
    width = TQ + far - 1
    tp = _toeplitz(jnp.concatenate([rev(hb[:, 0:far]), neg8(TQ + 1), zero8(TQ - 1)], axis=1), TQ, width)
    n_tiles = S // TQ
    per_tile = TQ // CMP_STRIDE
    col0 = per_tile * (n_tiles - 1)
    shift = CMP_STRIDE * col0 - (CMP_BLOCK - 1)
    first = -(-(shift - (far - 1)) // CMP_STRIDE)
    last = (shift + TQ - 1) // CMP_STRIDE
    band = tp[:, :, CMP_STRIDE * first - shift + far - 1::CMP_STRIDE][:, :, :last - first + 1]
    g2 = jnp.concatenate([zero8(TQ, first), band, neg8(TQ, 2 * TQ - last - 1)], axis=2)
    bias_cmp = jnp.stack([g2[:, :, col0 - per_tile * i:col0 - per_tile * i + LANE] for i in range(n_tiles)],
                         axis=1)
    n_cmp = (S - CMP_BLOCK) // CMP_STRIDE + 1
    bias_cmp = jnp.where(np.arange(LANE) < n_cmp, bias_cmp, NEG)
    return near, bias_cmp.reshape(NSA_HEADS, S, LANE)


def _overlap_matrix(S):
    n_cmp = (S - CMP_BLOCK) // CMP_STRIDE + 1
    n_slc = S // SLC_BLOCK
    c_start = np.arange(n_cmp) * CMP_STRIDE
    s_start = np.arange(n_slc) * SLC_BLOCK
    lo = np.maximum(s_start[:, None], c_start[None, :])
    hi = np.minimum(s_start[:, None] + SLC_BLOCK, c_start[None, :] + CMP_BLOCK)
    overlap = (np.maximum(hi - lo, 0) / CMP_BLOCK).astype(np.float32)
    out = np.zeros((n_slc, LANE), np.float32)
    out[:, :n_cmp] = overlap
    return out


IN_CHUNK = 512
HEAD_NORM_CHUNKS = {C_QN // IN_CHUNK: "all", C_KV // IN_CHUNK: "keys",
                    C_QD // IN_CHUNK: "all", C_KD // IN_CHUNK: "all"}


def _in_proj_kernel(x_ref, g_ref, w_ref, hg_ref, seg_ref, o_ref):
    dh = HEAD_DIM
    h = _rms(x_ref[...], g_ref[...]).astype(BF16)
    for c in range(C_TOTAL // IN_CHUNK):
        cols = slice(c * IN_CHUNK, (c + 1) * IN_CHUNK)
        y = jnp.dot(h, w_ref[:, cols], preferred_element_type=F32)
        mode = HEAD_NORM_CHUNKS.get(c)
        if mode is not None:
            sq = (y * y).astype(BF16)
            seg_w = seg_ref.shape[0]
            ss = jnp.concatenate([jnp.dot(sq[:, j:j + seg_w], seg_ref[...], preferred_element_type=F32)
                                  for j in range(0, IN_CHUNK, seg_w)], axis=1)
            scale = lax.rsqrt(ss * (1.0 / dh) + RMS_EPS) * hg_ref[:, cols]
            if mode == "keys":
                lane = lax.broadcasted_iota(jnp.int32, (1, IN_CHUNK), 1)
                scale = jnp.where(lane % (4 * dh) < 2 * dh, scale, 1.0)
            y = y * scale
        o_ref[:, cols] = y.astype(BF16)


def _in_proj(x2d, gain, w, head_gain):
    n, d = x2d.shape
    tm = 256
    seg = np.arange(MXU_DIM) // HEAD_DIM
    seg_ones = jnp.asarray((seg[:, None] == seg[None, :]).astype(np.float32), BF16)
    return pl.pallas_call(
        _in_proj_kernel,
        grid=(n // tm,),
        in_specs=[pl.BlockSpec((tm, d), lambda i: (i, 0)),
                  pl.BlockSpec((1, d), lambda i: (0, 0)),
                  pl.BlockSpec((d, C_TOTAL), lambda i: (0, 0)),
                  pl.BlockSpec((1, C_TOTAL), lambda i: (0, 0)),
                  pl.BlockSpec((MXU_DIM, MXU_DIM), lambda i: (0, 0))],
        out_specs=pl.BlockSpec((tm, C_TOTAL), lambda i: (i, 0)),
        out_shape=jax.ShapeDtypeStruct((n, C_TOTAL), BF16),
        compiler_params=_cparams(("arbitrary",)),
        name="in_proj",
    )(x2d, gain, w, head_gain, seg_ones)


def _compress_kernel(t_ref, pos_ref, w1_ref, w2_ref, kg_ref, k_ref, v_ref, tok_scr):
    S, dh = t_ref.shape[0], HEAD_DIM
    m = S // CMP_STRIDE
    tok_scr[0:S, :] = t_ref[...].astype(F32)
    tok_scr[S:, :] = jnp.zeros((CMP_STRIDE, LANE), F32)
    pre = jnp.zeros((m, 2 * CMP_HIDDEN), F32)
    for l in range(CMP_BLOCK):
        x = tok_scr[pl.ds(l, m, stride=CMP_STRIDE), :] + pos_ref[l:l + 1, :]
        pre = pre + jnp.dot(x.astype(BF16), w1_ref[l], preferred_element_type=F32)
    out = jnp.dot(jax.nn.gelu(pre).astype(BF16), w2_ref[...], preferred_element_type=F32)
    k_ref[...] = jnp.concatenate([_rms(out[:, :dh], kg_ref[...]), jnp.zeros((m, LANE - dh), F32)],
                                 axis=1).astype(BF16)
    v_ref[...] = out[:, dh:].astype(BF16)


def _compress(proj, pos, w1, w2, k_gain):
    B, S, _ = proj.shape
    G, m = NSA_KV_GROUPS, S // CMP_STRIDE
    outs = [jax.ShapeDtypeStruct((B, G, m, width), BF16) for width in (LANE, HEAD_DIM)]
    ospecs = [pl.BlockSpec((None, None, m, width), lambda b, g: (b, g, 0, 0)) for width in (LANE, HEAD_DIM)]
    return pl.pallas_call(
        _compress_kernel,
        grid=(B, G),
        in_specs=[pl.BlockSpec((None, S, LANE), lambda b, g: (b, 0, C_CMP // LANE + g)),
                  pl.BlockSpec(pos.shape, lambda b, g: (0, 0)),
                  pl.BlockSpec(w1.shape, lambda b, g: (0, 0, 0)),
                  pl.BlockSpec(w2.shape, lambda b, g: (0, 0)),
                  pl.BlockSpec((1, HEAD_DIM), lambda b, g: (0, 0))],
        out_specs=ospecs,
        out_shape=outs,
        scratch_shapes=[pltpu.VMEM((S + CMP_STRIDE, LANE), F32)],
        compiler_params=_cparams(("arbitrary", "arbitrary")),
        name="nsa_compress",
    )(proj, pos, w1, w2, k_gain)


def _compress_params(cmp_pos, cmp_w1, cmp_w2):
    dh, hid = HEAD_DIM, CMP_HIDDEN
    w1 = cmp_w1.reshape(2, CMP_BLOCK, dh, hid)
    z1 = jnp.zeros((CMP_BLOCK, dh, hid), cmp_w1.dtype)
    w1 = jnp.concatenate([jnp.concatenate([w1[0], z1], axis=2),
                          jnp.concatenate([z1, w1[1]], axis=2)], axis=1)
    z2 = jnp.zeros((hid, dh), cmp_w2.dtype)
    w2 = jnp.concatenate([jnp.concatenate([cmp_w2[0], z2], axis=1),
                          jnp.concatenate([z2, cmp_w2[1]], axis=1)], axis=0)
    pos = jnp.concatenate([cmp_pos[0], cmp_pos[1]], axis=1)
    return pos, w1.astype(BF16), w2.astype(BF16)


def _dot_nt(a, b):
    return lax.dot_general(a, b, (((1,), (1,)), ((), ())), preferred_element_type=F32)


def _softmax_step(s, v, carry):
    m, l, acc = carry
    m_new = jnp.maximum(m, jnp.max(s, axis=-1, keepdims=True))
    alpha = jnp.exp2(m - m_new)
    p = jnp.exp2(s - m_new)
    l = alpha * l + jnp.sum(p, axis=-1, keepdims=True)
    acc = alpha * acc + jnp.dot(p.astype(BF16), v, preferred_element_type=F32)
    return m_new, l, acc


def _softmax_init(rows, dv):
    return (jnp.full((rows, 1), NEG, F32), jnp.zeros((rows, 1), F32), jnp.zeros((rows, dv), F32))


def _key_chunks(c):
    T = TQ
    chunks = [(0, T, (0, 0))] if c == 0 else [((c - 1) * T, 2 * T, (1, 0))]
    n_far = max(c - 1, 0)
    if n_far % 2:
        chunks.append((0, T, None))
    chunks += [(s * T, 2 * T, None) for s in range(n_far % 2, n_far, 2)]
    return chunks


def _nsa_kernel(q_ref, kv_ref, kc_ref, vc_ref, gate_ref, bc_ref, nb_ref, wm_ref, ov_ref, ex_ref, gx_ref,
                o_ref, mb_scr, osel_scr, *, seq_len):
    T, dh, Hg = TQ, HEAD_DIM, NSA_HPG
    R = Hg * T
    qi = pl.program_id(2)
    n_slc = seq_len // SLC_BLOCK

    lower = lax.broadcasted_iota(jnp.int32, (T, 2 * dh), 1) < dh
    q_lo, q_hi = [], []
    for pair in range(Hg // 2):
        pq = q_ref[:, pair * 2 * dh:(pair + 1) * 2 * dh]
        sw = jnp.concatenate([pq[:, dh:], pq[:, :dh]], axis=1)
        zero = jnp.zeros_like(pq)
        q_lo += [jnp.where(lower, pq, zero), jnp.where(lower, sw, zero)]
        q_hi += [jnp.where(lower, zero, sw), jnp.where(lower, zero, pq)]
    q = jnp.concatenate(q_lo, axis=0)
    q_win = jnp.concatenate(q_hi, axis=0)

    lc = _dot_nt(q, kc_ref[...]) + bc_ref[...].reshape(R, LANE)
    pc = jnp.exp2(lc - jnp.max(lc, axis=-1, keepdims=True))
    row = lax.broadcasted_iota(jnp.int32, (Hg, T, 1), 1).reshape(R, 1)
    visible = (qi * T + row >= CMP_BLOCK - 1).astype(F32)
    pc = pc * (visible / jnp.sum(pc, axis=-1, keepdims=True))
    o_cmp = jnp.dot(pc.astype(BF16), vc_ref[...], preferred_element_type=F32)

    n_all = SLC_TOP_N * SLC_BLOCK // T

    @pl.when(qi >= n_all)
    def _():
        p_sum = pc[0:T] + pc[T:2 * T] + pc[2 * T:3 * T] + pc[3 * T:4 * T]
        imp = lax.dot_general(ov_ref[...], p_sum, (((1,), (1,)), ((), ())), preferred_element_type=F32,
                              precision=lax.Precision.HIGHEST)
        blk = lax.broadcasted_iota(jnp.int32, (n_slc, T), 0)
        cur = (qi * T + lax.broadcasted_iota(jnp.int32, (n_slc, T), 1)) // SLC_BLOCK
        forced = (blk == 0) | (blk == cur) | (blk == cur - 1)
        score = jnp.where(blk <= cur, imp + jnp.where(forced, FORCE_SCORE, 0.0), NEG)
        rank = jnp.zeros((n_slc, T), jnp.int32)
        for i in range(n_slc):
            other = score[i:i + 1, :]
            ahead = (other > score) | ((other == score) & (blk > i))
            rank = rank + ahead.astype(jnp.int32)
        sel_bias = jnp.where(rank < SLC_TOP_N, 0.0, NEG).T.astype(BF16)
        mb_scr[...] = jnp.dot(sel_bias, ex_ref[...], preferred_element_type=F32)

    def sel_tile(c):
        carry = _softmax_init(R, 2 * dh)
        for start, width, table in _key_chunks(c):
            s = _dot_nt(q, kv_ref[start:start + width, 0:2 * dh]).reshape(Hg, T, width)
            if table is not None:
                s = s + nb_ref[table[0], :, :, table[1]:table[1] + width]
            if c >= n_all:
                s = s + mb_scr[:, start:start + width][None]
            carry = _softmax_step(s.reshape(R, width), kv_ref[start:start + width, 2 * dh:4 * dh], carry)
        osel_scr[...] = carry[2] / carry[1]

    for c in range(seq_len // T):
        pl.when(qi == c)(functools.partial(sel_tile, c))
    o_slc = osel_scr[...]

    def win_step(start, width, bias, carry):
        s = _dot_nt(q_win, kv_ref[pl.ds(start, width), 0:2 * dh]).reshape(Hg, T, width) + bias
        return _softmax_step(s.reshape(R, width), kv_ref[pl.ds(start, width), 2 * dh:4 * dh], carry)

    carry = win_step(pl.multiple_of(jnp.maximum(qi - 1, 0) * T, T), 2 * T, nb_ref[jnp.minimum(qi, 1)],
                     _softmax_init(R, 2 * dh))
    carry = lax.fori_loop(
        0, (qi >= 2).astype(jnp.int32),
        lambda j, c: win_step(pl.multiple_of((qi - 2) * T, T), T, wm_ref[...][None], c), carry)
    o_win = carry[2] / carry[1]

    gates = jax.nn.sigmoid(gate_ref[...].astype(F32))
    g_hi = gates.astype(BF16)
    g_lo = (gates - g_hi.astype(F32)).astype(BF16)
    g_wide = (jnp.dot(g_hi, gx_ref[...], preferred_element_type=F32)
              + jnp.dot(g_lo, gx_ref[...], preferred_element_type=F32))
    by_head = lambda o, lo: jnp.concatenate([o[h * T:(h + 1) * T, lo:lo + dh] for h in range(Hg)], axis=1)
    w = Hg * dh
    out = (g_wide[:, 0:w] * by_head(o_cmp, 0) + g_wide[:, w:2 * w] * by_head(o_slc, 0)
           + g_wide[:, 2 * w:3 * w] * by_head(o_win, dh))
    o_ref[...] = out.astype(BF16)


def _nsa_attention(proj, k_cmp, v_cmp, bias_cmp, near_bias, win_mask, overlap, expand):
    B, S, _ = proj.shape
    G, Hg, dh, T = NSA_KV_GROUPS, NSA_HPG, HEAD_DIM, TQ
    assert WINDOW == 2 * T and S % (2 * T) == 0
    n_slc = S // SLC_BLOCK
    gate_expand = np.zeros((LANE, 3 * Hg * dh), np.float32)
    for j in range(3 * Hg):
        gate_expand[j, j * dh:(j + 1) * dh] = 1.0
    gate_expand = jnp.asarray(gate_expand, BF16)
    return pl.pallas_call(
        functools.partial(_nsa_kernel, seq_len=S),
        grid=(B, G, S // T),
        in_specs=[
            pl.BlockSpec((None, T, Hg * dh), lambda b, g, i: (b, i, C_QN // (Hg * dh) + g)),
            pl.BlockSpec((None, S, 4 * dh), lambda b, g, i: (b, 0, C_KV // (4 * dh) + g)),
            pl.BlockSpec((None, None, LANE, LANE), lambda b, g, i: (b, g, 0, 0)),
            pl.BlockSpec((None, None, LANE, dh), lambda b, g, i: (b, g, 0, 0)),
            pl.BlockSpec((None, T, LANE), lambda b, g, i: (b, i, C_GATE // LANE + g)),
            pl.BlockSpec((None, Hg, T, LANE), lambda b, g, i: (g, 0, i, 0)),
            pl.BlockSpec((None, 2, Hg, T, 2 * T), lambda b, g, i: (g, 0, 0, 0, 0)),
            pl.BlockSpec((T, T), lambda b, g, i: (0, 0)),
            pl.BlockSpec((n_slc, LANE), lambda b, g, i: (0, 0)),
            pl.BlockSpec((n_slc, S), lambda b, g, i: (0, 0)),
            pl.BlockSpec(gate_expand.shape, lambda b, g, i: (0, 0)),
        ],
        out_specs=pl.BlockSpec((None, T, Hg * dh), lambda b, g, i: (b, i, g)),
        out_shape=jax.ShapeDtypeStruct((B, S, NSA_HEADS * dh), BF16),
        scratch_shapes=[pltpu.VMEM((T, S), F32), pltpu.VMEM((Hg * T, 2 * dh), F32)],
        compiler_params=_cparams(("arbitrary", "arbitrary", "arbitrary")),
        name="nsa_attention",
    )(proj, proj, k_cmp, v_cmp, proj, bias_cmp, near_bias, win_mask, overlap, expand, gate_expand)


def _diff_kernel(q_ref, k_ref, v_ref, lam_ref, og_ref, nb_ref, o_ref, *, lambda_init, n_tiles):
    T, dh = TQ, HEAD_DIM
    qi = pl.program_id(2)
    q = q_ref[...]
    lower = lax.broadcasted_iota(jnp.int32, q.shape, 1) < dh
    q1 = jnp.where(lower, q, jnp.zeros_like(q))
    q2 = jnp.where(lower, jnp.zeros_like(q), q)

    def step(start, width, bias, carry):
        c1, c2 = carry
        v = v_ref[start:start + width, :]
        k = k_ref[start:start + width, :]
        s1 = _dot_nt(q1, k)
        s2 = _dot_nt(q2, k)
        if bias is not None:
            s1, s2 = s1 + bias, s2 + bias
        return _softmax_step(s1, v, c1), _softmax_step(s2, v, c2)

    lam_p = lam_ref[...]
    lam = (jnp.exp(jnp.sum(lam_p[0:1] * lam_p[1:2], axis=-1, keepdims=True))
           - jnp.exp(jnp.sum(lam_p[2:3] * lam_p[3:4], axis=-1, keepdims=True)) + lambda_init)

    def tile(c):
        init = _softmax_init(T, DIFF_V_DIM)
        carry = (init, init)
        for start, width, table in _key_chunks(c):
            bias = None if table is None else nb_ref[table[0], :, table[1]:table[1] + width]
            carry = step(start, width, bias, carry)
        (_, l1, a1), (_, l2, a2) = carry
        o = a1 / l1 - lam * (a2 / l2)
        o_ref[...] = (_rms(o, og_ref[...]) * (1.0 - lambda_init)).astype(BF16)

    for c in range(n_tiles):
        pl.when(qi == c)(functools.partial(tile, c))


def _diff_attention(proj, lam_p, out_gain, near_bias, lambda_init):
    B, S, _ = proj.shape
    T, dh, H = TQ, HEAD_DIM, DIFF_HEADS
    return pl.pallas_call(
        functools.partial(_diff_kernel, lambda_init=lambda_init, n_tiles=S // T),
        grid=(B, H, S // T),
        in_specs=[
            pl.BlockSpec((None, T, 2 * dh), lambda b, h, i: (b, i, C_QD // (2 * dh) + h)),
            pl.BlockSpec((None, S, 2 * dh), lambda b, h, i: (b, 0, C_KD // (2 * dh) + h)),
            pl.BlockSpec((None, S, DIFF_V_DIM), lambda b, h, i: (b, 0, C_VD // DIFF_V_DIM + h)),
            pl.BlockSpec((4, dh), lambda b, h, i: (0, 0)),
            pl.BlockSpec((1, DIFF_V_DIM), lambda b, h, i: (0, 0)),
            pl.BlockSpec((None, 2, T, 2 * T), lambda b, h, i: (h, 0, 0, 0)),
        ],
        out_specs=pl.BlockSpec((None, T, DIFF_V_DIM), lambda b, h, i: (b, i, h)),
        out_shape=jax.ShapeDtypeStruct((B, S, H * DIFF_V_DIM), BF16),
        compiler_params=_cparams(("arbitrary", "arbitrary", "arbitrary")),
        name="diff_attention",
    )(proj, proj, proj, lam_p, out_gain, near_bias)


def _merge_kernel(x_ref, on_ref, od_ref, gm_ref, wn_ref, wd_ref, wo_ref, g2_ref, rwh_ref, rwl_ref, rb_ref,
                  o_ref):
    d = x_ref.shape[-1]
    a = jnp.dot(on_ref[...], wn_ref[...], preferred_element_type=F32)
    b = jnp.dot(od_ref[...], wd_ref[...], preferred_element_type=F32)
    y = (jax.nn.sigmoid(gm_ref[:, :d].astype(F32)) * a
         + jax.nn.sigmoid(gm_ref[:, d:].astype(F32)) * b)
    x2 = x_ref[...] + jnp.dot(y.astype(BF16), wo_ref[...], preferred_element_type=F32)
    o_ref[:, :d] = x2

    h2 = _rms(x2, g2_ref[...])
    h_hi = h2.astype(BF16)
    h_lo = (h2 - h_hi.astype(F32)).astype(BF16)
    logits = (jnp.dot(h_hi, rwh_ref[...], preferred_element_type=F32)
              + jnp.dot(h_lo, rwh_ref[...], preferred_element_type=F32)
              + jnp.dot(h_hi, rwl_ref[...], preferred_element_type=F32)) + rb_ref[...]
    lane = lax.broadcasted_iota(jnp.int32, logits.shape, 1)
    big = jnp.int32(LANE)

    def first_argmax(vals, mask):
        vals = jnp.where(mask, vals, -jnp.inf)
        top = jnp.max(vals, axis=-1, keepdims=True)
        idx = jnp.min(jnp.where(vals == top, lane, big), axis=-1, keepdims=True)
        return top, idx

    is_group = lane < N_GROUPS
    g_max, g_top = first_argmax(logits, is_group)
    g_top_p = 1.0 / jnp.sum(jnp.where(is_group, jnp.exp(logits - g_max), 0.0), axis=-1, keepdims=True)
    lo = N_GROUPS + g_top * EXPERTS_PER_GROUP
    in_group = (lane >= lo) & (lane < lo + EXPERTS_PER_GROUP)
    m1, i1 = first_argmax(logits, in_group)
    m2, i2 = first_argmax(logits, in_group & (lane != i1))
    e2 = jnp.exp(m2 - m1)
    w1 = g_top_p / (1.0 + e2)
    w2 = g_top_p * e2 / (1.0 + e2)
    la = jnp.minimum(i1, i2) - lo
    lb = jnp.maximum(i1, i2) - lo
    pair = jnp.right_shift(la * (7 - la), 1) + (lb - la - 1)
    bucket = (g_top * N_PAIRS + pair).astype(F32)
    exp_lane = lane + N_GROUPS
    route = (jnp.where(exp_lane == i1, w1, 0.0) + jnp.where(exp_lane == i2, w2, 0.0)
             + jnp.where(lane == N_EXPERTS, bucket, 0.0))
    o_ref[:, d:] = route


def _merge(x2d, o_nsa, o_diff, proj2d, wn, wd, wo, g2, rw, rb):
    n, d = x2d.shape
    tm = 512
    rw_hi = rw.astype(BF16)
    rw_lo = (rw - rw_hi.astype(F32)).astype(BF16)
    full = lambda a: pl.BlockSpec(a.shape, lambda i: (0,) * a.ndim)
    return pl.pallas_call(
        _merge_kernel,
        grid=(n // tm,),
        in_specs=[pl.BlockSpec((tm, d), lambda i: (i, 0)),
                  pl.BlockSpec((tm, o_nsa.shape[1]), lambda i: (i, 0)),
                  pl.BlockSpec((tm, o_diff.shape[1]), lambda i: (i, 0)),
                  pl.BlockSpec((tm, 2 * d), lambda i: (i, C_GM // (2 * d))),
                  full(wn), full(wd), full(wo), full(g2), full(rw_hi), full(rw_lo), full(rb)],
        out_specs=pl.BlockSpec((tm, d + LANE), lambda i: (i, 0)),
        out_shape=jax.ShapeDtypeStruct((n, d + LANE), F32),
        compiler_params=_cparams(("arbitrary",)),
        name="merge_router",
    )(x2d, o_nsa, o_diff, proj2d, wn, wd, wo, g2, rw_hi, rw_lo, rb)


def _moe_kernel(tile_ref, ea_ref, eb_ref, lo_ref, hi_ref, first_ref, perm_ref,
                x_hbm, g2_ref, wga_ref, wua_ref, wda_ref, wgb_ref, wub_ref, wdb_ref,
                o_hbm, xb0, xb1, ac0, ac1, h_scr, g_sem, s_sem, *, tm, n_tiles):
    d = o_hbm.shape[1]
    v = pl.program_id(0)
    t = tile_ref[v]
    x_bufs, acc_bufs = (xb0, xb1), (ac0, ac1)
    last_slot = (n_tiles - 1) % 2

    def gather_copy(tt, sl, r):
        return pltpu.make_async_copy(x_hbm.at[pl.ds(perm_ref[tt * tm + r], 1), :],
                                     x_bufs[sl].at[pl.ds(r, 1), :], g_sem.at[sl])

    def scatter_copy(tt, sl, r):
        return pltpu.make_async_copy(acc_bufs[sl].at[pl.ds(r, 1), :],
                                     o_hbm.at[pl.ds(perm_ref[tt * tm + r], 1), :], s_sem.at[sl])

    def for_rows(fn):
        for r in range(tm):
            fn(r)

    def experts(sl, dma_starts):
        n_stage = 6
        per_stage = -(-len(dma_starts) // n_stage)

        def issue(stage):
            for start in dma_starts[stage * per_stage:(stage + 1) * per_stage]:
                start()

        h = h_scr[...]
        cw = x_bufs[sl][:, d:]
        lane = lax.broadcasted_iota(jnp.int32, cw.shape, 1)
        row = lax.broadcasted_iota(jnp.int32, (tm, 1), 0)
        in_bucket = (row >= lo_ref[v]) & (row < hi_ref[v])
        out = None
        for i, (e, wg_ref, wu_ref, wd_ref) in enumerate(((ea_ref[v], wga_ref, wua_ref, wda_ref),
                                                         (eb_ref[v], wgb_ref, wub_ref, wdb_ref))):
            w = jnp.sum(jnp.where(lane == e, cw, 0.0), axis=-1, keepdims=True)
            w = jnp.where(in_bucket, w, 0.0)
            gate = jnp.dot(h, wg_ref[...], preferred_element_type=F32)
            issue(3 * i)
            up = jnp.dot(h, wu_ref[...], preferred_element_type=F32)
            issue(3 * i + 1)
            hid = jax.nn.silu(gate) * up * w
            y = jnp.dot(hid.astype(BF16), wd_ref[...], preferred_element_type=F32)
            issue(3 * i + 2)
            out = y if out is None else out + y
        acc_bufs[sl][...] += out

    def first_visit(sl, scatter_previous):
        for_rows(lambda r: gather_copy(t, sl, r).wait())

        @pl.when(t >= 2)
        def _():
            for_rows(lambda r: scatter_copy(t - 2, sl, r).wait())

        xs = x_bufs[sl][:, :d]
        h_scr[...] = _rms(xs, g2_ref[...]).astype(BF16)
        acc_bufs[sl][...] = xs
        nxt = jnp.minimum(t + 1, n_tiles - 1)
        starts = [functools.partial(lambda r: gather_copy(nxt, 1 - sl, r).start(), r) for r in range(tm)]
        if scatter_previous:
            starts += [functools.partial(lambda r: scatter_copy(t - 1, 1 - sl, r).start(), r)
                       for r in range(tm)]
        experts(sl, starts)

    @pl.when(v == 0)
    def _():
        for_rows(lambda r: gather_copy(0, 0, r).start())
        first_visit(0, scatter_previous=False)

    for sl in range(2):
        pl.when((v > 0) & (first_ref[v] == 1) & (t % 2 == sl))(
            functools.partial(first_visit, sl, scatter_previous=True))
        pl.when((first_ref[v] == 0) & (t % 2 == sl))(functools.partial(experts, sl, []))

    @pl.when(v == pl.num_programs(0) - 1)
    def _():
        for_rows(lambda r: scatter_copy(n_tiles - 1, last_slot, r).start())
        for_rows(lambda r: gather_copy(n_tiles - 1, 1 - last_slot, r).wait())
        for_rows(lambda r: scatter_copy(n_tiles - 2, 1 - last_slot, r).wait())
        for_rows(lambda r: scatter_copy(n_tiles - 1, last_slot, r).wait())


def _moe(sched, perm, x2r, g2, wg, wu, wd, tm):
    n = x2r.shape[0]
    d, f = wg.shape[1], wg.shape[2]
    n_visits = sched[0].shape[0]
    n_tiles = n // tm
    assert n_tiles >= 2
    w_in_a = pl.BlockSpec((None, d, f), lambda v, t, ea, eb, *_: (ea[v], 0, 0))
    w_in_b = pl.BlockSpec((None, d, f), lambda v, t, ea, eb, *_: (eb[v], 0, 0))
    w_out_a = pl.BlockSpec((None, f, d), lambda v, t, ea, eb, *_: (ea[v], 0, 0))
    w_out_b = pl.BlockSpec((None, f, d), lambda v, t, ea, eb, *_: (eb[v], 0, 0))
    return pl.pallas_call(
        functools.partial(_moe_kernel, tm=tm, n_tiles=n_tiles),
        grid_spec=pltpu.PrefetchScalarGridSpec(
            num_scalar_prefetch=7,
            grid=(n_visits,),
            in_specs=[pl.BlockSpec(memory_space=pl.ANY),
                      pl.BlockSpec((1, d), lambda v, *_: (0, 0)),
                      w_in_a, w_in_a, w_out_a, w_in_b, w_in_b, w_out_b],
            out_specs=pl.BlockSpec(memory_space=pl.ANY),
            scratch_shapes=[pltpu.VMEM((tm, x2r.shape[1]), F32), pltpu.VMEM((tm, x2r.shape[1]), F32),
                            pltpu.VMEM((tm, d), F32), pltpu.VMEM((tm, d), F32),
                            pltpu.VMEM((tm, d), BF16),
                            pltpu.SemaphoreType.DMA((2,)), pltpu.SemaphoreType.DMA((2,))]),
        out_shape=jax.ShapeDtypeStruct((n, d), F32),
        compiler_params=_cparams(("arbitrary",)),
        name="moe_experts",
    )(*sched, perm, x2r, g2, wg, wu, wd, wg, wu, wd)


def _moe_schedule(bucket, tm):
    n = bucket.shape[0]
    n_tiles = n // tm
    n_visits = n_tiles + N_BUCKETS - 1
    counts = jnp.sum((bucket[:, None] == jnp.arange(N_BUCKETS)[None, :]).astype(jnp.int32), axis=0)
    ends = jnp.cumsum(counts)
    starts = ends - counts
    first_tile = starts // tm
    last_tile = jnp.maximum(ends - 1, 0) // tm
    n_vis = jnp.where(counts > 0, last_tile - first_tile + 1, 0)
    vis_end = jnp.cumsum(n_vis)
    vis_start = vis_end - n_vis
    v = jnp.arange(n_visits, dtype=jnp.int32)
    b = jnp.minimum(jnp.searchsorted(vis_end, v, side="right"), N_BUCKETS - 1).astype(jnp.int32)
    live = v < vis_end[-1]
    tile = jnp.where(live, first_tile[b] + v - vis_start[b], n_tiles - 1).astype(jnp.int32)
    lo = jnp.where(live, jnp.clip(starts[b] - tile * tm, 0, tm), 0).astype(jnp.int32)
    hi = jnp.where(live, jnp.clip(ends[b] - tile * tm, 0, tm), 0).astype(jnp.int32)
    b_last = b[jnp.maximum(vis_end[-1] - 1, 0)]
    b = jnp.where(live, b, b_last)
    pair_a = np.array([0, 0, 0, 1, 1, 2], np.int32)
    pair_b = np.array([1, 2, 3, 2, 3, 3], np.int32)
    grp = b // N_PAIRS
    ea = (grp * EXPERTS_PER_GROUP + jnp.asarray(pair_a)[b % N_PAIRS]).astype(jnp.int32)
    eb = (grp * EXPERTS_PER_GROUP + jnp.asarray(pair_b)[b % N_PAIRS]).astype(jnp.int32)
    first = jnp.concatenate([jnp.ones((1,), jnp.int32), (tile[1:] != tile[:-1]).astype(jnp.int32)])
    return tile, ea, eb, lo, hi, first


def _layer(x, layer, p):
    B, S, D = x.shape
    G, dh = NSA_KV_GROUPS, HEAD_DIM
    lambda_init = 0.8 - 0.6 * math.exp(-0.3 * layer)
    x2d = x.reshape(B * S, D)

    proj2d = _in_proj(x2d, p["norm1_g"][None, :], _reorder_w_in(p["w_in"]), _head_norm_rows(p))
    proj = proj2d.reshape(B, S, C_TOTAL)

    k_cmp, v_cmp = _compress(proj, *_compress_params(p["cmp_pos"], p["cmp_w1"], p["cmp_w2"]),
                             p["nsa_k_norm"][0:1])

    near, bias_cmp = _bias_tables(p["rel_bias"], S)
    nsa_near = near[:NSA_HEADS].reshape(G, NSA_HPG, 2, TQ, 2 * TQ).transpose(0, 2, 1, 3, 4)
    i = np.arange(TQ)
    win_mask = jnp.asarray(np.where(i[None, :] > i[:, None], 0.0, NEG).astype(np.float32))
    expand = jnp.asarray(np.repeat(np.eye(S // SLC_BLOCK, dtype=np.float32), SLC_BLOCK, axis=1), BF16)
    o_nsa = _nsa_attention(proj, k_cmp, v_cmp, bias_cmp.reshape(G, NSA_HPG, S, LANE), nsa_near, win_mask,
                           jnp.asarray(_overlap_matrix(S)), expand)
    o_diff = _diff_attention(proj, p["diff_lambda"], p["diff_out_norm"][None, :], near[NSA_HEADS:],
                             lambda_init)

    rw = jnp.concatenate([p["router_group_w"], p["router_expert_w"],
                          jnp.zeros((D, LANE - N_GROUPS - N_EXPERTS), F32)], axis=1)
    rb = jnp.concatenate([p["router_group_b"], p["router_expert_b"],
                          jnp.zeros((LANE - N_GROUPS - N_EXPERTS,), F32)])[None, :]
    x2r = _merge(x2d, o_nsa.reshape(B * S, -1), o_diff.reshape(B * S, -1), proj2d,
                 p["w_branch_nsa"].astype(BF16), p["w_branch_diff"].astype(BF16),
                 p["w_out"].astype(BF16), p["norm2_g"][None, :], rw, rb)

    tm = 256
    bucket = x2r[:, D + N_EXPERTS].astype(jnp.int32)
    perm = jnp.argsort(bucket).astype(jnp.int32)
    out = _moe(_moe_schedule(bucket, tm), perm, x2r, p["norm2_g"][None, :],
               p["expert_w_gate"].astype(BF16), p["expert_w_up"].astype(BF16),
               p["expert_w_down"].astype(BF16), tm)
    return out.reshape(B, S, D)


def kernel(x, norm1_g, w_in, nsa_q_norm, nsa_k_norm, cmp_pos, cmp_w1, cmp_w2, diff_q_norm, diff_k_norm,
           diff_lambda, diff_out_norm, rel_bias, w_branch_nsa, w_branch_diff, w_out, norm2_g,
           router_group_w, router_group_b, router_expert_w, router_expert_b, expert_w_gate,
           expert_w_up, expert_w_down):
    per_layer = dict(norm1_g=norm1_g, w_in=w_in, nsa_q_norm=nsa_q_norm, nsa_k_norm=nsa_k_norm,
                     cmp_pos=cmp_pos, cmp_w1=cmp_w1, cmp_w2=cmp_w2, diff_q_norm=diff_q_norm,
                     diff_k_norm=diff_k_norm, diff_lambda=diff_lambda, diff_out_norm=diff_out_norm,
                     w_branch_nsa=w_branch_nsa, w_branch_diff=w_branch_diff, w_out=w_out,
                     norm2_g=norm2_g, router_group_w=router_group_w, router_group_b=router_group_b,
                     router_expert_w=router_expert_w, router_expert_b=router_expert_b,
                     expert_w_gate=expert_w_gate, expert_w_up=expert_w_up, expert_w_down=expert_w_down)
    for layer in range(norm1_g.shape[0]):
        p = {k: v[layer] for k, v in per_layer.items()}
        p["rel_bias"] = rel_bias
        x = _layer(x, layer, p)
    return x
```

```python
import functools
import math

import numpy as np
import jax
import jax.numpy as jnp
from jax import lax
from jax.experimental import pallas as pl
from jax.experimental.pallas import tpu as pltpu

F32 = jnp.float32
BF16 = jnp.bfloat16

HEAD_DIM = 64
SCALE = HEAD_DIM ** -0.5
NSA_HEADS = 8
NSA_KV_GROUPS = 2
NSA_HPG = NSA_HEADS // NSA_KV_GROUPS
CMP_BLOCK = 32
CMP_STRIDE = 16
CMP_HIDDEN = 256
SLC_BLOCK = 64
SLC_TOP_N = 16
WINDOW = 512
FORCE_SCORE = 1e4
DIFF_HEADS = 4
DIFF_V_DIM = 2 * HEAD_DIM
REL_BUCKETS = 32
REL_MAX_DIST = 128
N_GROUPS = 4
EXPERTS_PER_GROUP = 4
N_EXPERTS = N_GROUPS * EXPERTS_PER_GROUP
EXPERT_HIDDEN = 512
RMS_EPS = 1e-6
NEG = -1e30
LOG2E = math.log2(math.e)

LANE = 128
MXU_DIM = 256
TQ = 256
N_PAIRS = 6
N_BUCKETS = N_GROUPS * N_PAIRS

C_GM = 0
C_QN = 2048
C_KV = 2560
C_CMP = 3072
C_GATE = 3328
C_QD = 3584
C_KD = 4096
C_VD = 4608
C_TOTAL = 5120

VMEM_LIMIT = 56 * 1024 * 1024


def _cparams(sem):
    return pltpu.CompilerParams(dimension_semantics=sem, vmem_limit_bytes=VMEM_LIMIT)


def _rms(x, gain):
    return x * lax.rsqrt(jnp.mean(x * x, axis=-1, keepdims=True) + RMS_EPS) * gain


def _reorder_w_in(w_in):
    d_model = w_in.shape[0]
    dh, G, Hg = HEAD_DIM, NSA_KV_GROUPS, NSA_HPG
    o_q = 0
    o_kv = o_q + NSA_HEADS * dh
    o_g = o_kv + 3 * 2 * G * dh
    o_qd = o_g + 3 * NSA_HEADS
    o_gm = o_qd + 3 * DIFF_HEADS * 2 * dh

    def cols(start, n):
        return w_in[:, start:start + n]

    def kv(kind, which, g):
        return cols(o_kv + ((kind * 2 + which) * G + g) * dh, dh)

    parts = [cols(o_gm, 2 * d_model), cols(o_q, NSA_HEADS * dh)]
    parts += [kv(kind, which, g) for g in range(G) for which in (0, 1) for kind in (1, 2)]
    parts += [kv(0, which, g) for g in range(G) for which in (0, 1)]
    for g in range(G):
        parts += [cols(o_g + (kind * G + g) * Hg, Hg) for kind in range(3)]
        parts.append(jnp.zeros((d_model, LANE - 3 * Hg), w_in.dtype))
    parts.append(cols(o_qd, 3 * DIFF_HEADS * 2 * dh))
    w = jnp.concatenate(parts, axis=1)
    assert w.shape[1] == C_TOTAL
    return w.astype(BF16)


def _head_norm_rows(p):
    dh, G = HEAD_DIM, NSA_KV_GROUPS
    qs = SCALE * LOG2E
    zeros = lambda n: jnp.zeros((n,), F32)
    kn = p["nsa_k_norm"]
    parts = [zeros(C_QN - C_GM), jnp.tile(p["nsa_q_norm"] * qs, NSA_HEADS)]
    parts += [jnp.concatenate([kn[1], kn[2], zeros(2 * dh)])] * G
    parts += [zeros(C_QD - C_CMP), jnp.tile((p["diff_q_norm"] * qs).reshape(-1), DIFF_HEADS),
              jnp.tile(p["diff_k_norm"].reshape(-1), DIFF_HEADS), zeros(C_TOTAL - C_VD)]
    gain = jnp.concatenate(parts)
    assert gain.shape[0] == C_TOTAL
    return gain[None, :]


def _t5_bucket(dist):
    n = jnp.maximum(dist, 0)
    max_exact = REL_BUCKETS // 2
    nf = jnp.maximum(n, 1).astype(F32)
    large = max_exact + (jnp.log(nf / max_exact) / math.log(REL_MAX_DIST / max_exact)
                         * (REL_BUCKETS - max_exact)).astype(jnp.int32)
    large = jnp.minimum(large, REL_BUCKETS - 1)
    return jnp.where(n < max_exact, n, large)


def _toeplitz(v_ext, rows, cols):
    length = v_ext.shape[-1]
    lead = v_ext.shape[:-1]
    flat = jnp.broadcast_to(v_ext[..., None, :], lead + (rows, length)).reshape(lead + (rows * length,))
    return flat[..., :rows * (length - 1)].reshape(lead + (rows, length - 1))[..., :cols]


def _bias_tables(rel_bias, S):
    heads = rel_bias.shape[1]
    far = REL_MAX_DIST
    assert TQ == 2 * far and LANE == far
    bd = rel_bias[_t5_bucket(jnp.arange(2 * far))].T.astype(F32) * LOG2E
    rel = bd - bd[:, far:far + 1]
    neg = lambda *shape: jnp.full((heads,) + shape, NEG, F32)
    zero = lambda *shape: jnp.zeros((heads,) + shape, F32)
    rev = lambda a: a[:, ::-1]
    d0 = _toeplitz(jnp.concatenate([rel[:, 0:1], neg(far), rev(rel[:, 1:far])], axis=1), far, far)
    d1 = _toeplitz(jnp.concatenate([rev(rel[:, 1:far + 1]), zero(far)], axis=1), far, far)
    z, n = zero(far, far), neg(far, far)
    blocks = lambda rows: jnp.concatenate([jnp.concatenate(r, axis=2) for r in rows], axis=1)
    near = jnp.stack([blocks([[d0, n, n, n], [d1, d0, n, n]]),
                      blocks([[z, d1, d0, n], [z, z, d1, d0]])], axis=1)
    hb = rel[:NSA_HEADS]
    zero8 = lambda *shape: jnp.zeros((NSA_HEADS,) + shape, F32)
    neg8 = lambda *shape: jnp.full((NSA_HEADS,) + shape, NEG, F32)
    width = TQ + far - 1
    tp = _toeplitz(jnp.concatenate([rev(hb[:, 0:far]), neg8(TQ + 1), zero8(TQ - 1)], axis=1), TQ, width)
    n_tiles = S // TQ
    per_tile = TQ // CMP_STRIDE
    col0 = per_tile * (n_tiles - 1)
    shift = CMP_STRIDE * col0 - (CMP_BLOCK - 1)
    first = -(-(shift - (far - 1)) // CMP_STRIDE)
    last = (shift + TQ - 1) // CMP_STRIDE
    band = tp[:, :, CMP_STRIDE * first - shift + far - 1::CMP_STRIDE][:, :, :last - first + 1]
    g2 = jnp.concatenate([zero8(TQ, first), band, neg8(TQ, 2 * TQ - last - 1)], axis=2)
    bias_cmp = jnp.stack([g2[:, :, col0 - per_tile * i:col0 - per_tile * i + LANE] for i in range(n_tiles)],
                         axis=1)
    n_cmp = (S - CMP_BLOCK) // CMP_STRIDE + 1
    bias_cmp = jnp.where(np.arange(LANE) < n_cmp, bias_cmp, NEG)
    return near, bias_cmp.reshape(NSA_HEADS, S, LANE)


def _overlap_matrix(S):
    n_cmp = (S - CMP_BLOCK) // CMP_STRIDE + 1
    n_slc = S // SLC_BLOCK
    c_start = np.arange(n_cmp) * CMP_STRIDE
    s_start = np.arange(n_slc) * SLC_BLOCK
    lo = np.maximum(s_start[:, None], c_start[None, :])
    hi = np.minimum(s_start[:, None] + SLC_BLOCK, c_start[None, :] + CMP_BLOCK)
    overlap = (np.maximum(hi - lo, 0) / CMP_BLOCK).astype(np.float32)
    out = np.zeros((n_slc, LANE), np.float32)
    out[:, :n_cmp] = overlap
    return out


IN_CHUNK = 512
HEAD_NORM_CHUNKS = {C_QN // IN_CHUNK: "all", C_KV // IN_CHUNK: "keys",
                    C_QD // IN_CHUNK: "all", C_KD // IN_CHUNK: "all"}


def _in_proj_kernel(x_ref, g_ref, w_ref, hg_ref, seg_ref, o_ref):
    dh = HEAD_DIM
    h = _rms(x_ref[...], g_ref[...]).astype(BF16)
    for c in range(C_TOTAL // IN_CHUNK):
        cols = slice(c * IN_CHUNK, (c + 1) * IN_CHUNK)
        y = jnp.dot(h, w_ref[:, cols], preferred_element_type=F32)
        mode = HEAD_NORM_CHUNKS.get(c)
        if mode is not None:
            sq = (y * y).astype(BF16)
            seg_w = seg_ref.shape[0]
            ss = jnp.concatenate([jnp.dot(sq[:, j:j + seg_w], seg_ref[...], preferred_element_type=F32)
                                  for j in range(0, IN_CHUNK, seg_w)], axis=1)
            scale = lax.rsqrt(ss * (1.0 / dh) + RMS_EPS) * hg_ref[:, cols]
            if mode == "keys":
                lane = lax.broadcasted_iota(jnp.int32, (1, IN_CHUNK), 1)
                scale = jnp.where(lane % (4 * dh) < 2 * dh, scale, 1.0)
            y = y * scale
        o_ref[:, cols] = y.astype(BF16)


def _in_proj(x2d, gain, w, head_gain):
    n, d = x2d.shape
    tm = 256
    seg = np.arange(MXU_DIM) // HEAD_DIM
    seg_ones = jnp.asarray((seg[:, None] == seg[None, :]).astype(np.float32), BF16)
    return pl.pallas_call(
        _in_proj_kernel,
        grid=(n // tm,),
        in_specs=[pl.BlockSpec((tm, d), lambda i: (i, 0)),
                  pl.BlockSpec((1, d), lambda i: (0, 0)),
                  pl.BlockSpec((d, C_TOTAL), lambda i: (0, 0)),
                  pl.BlockSpec((1, C_TOTAL), lambda i: (0, 0)),
                  pl.BlockSpec((MXU_DIM, MXU_DIM), lambda i: (0, 0))],
        out_specs=pl.BlockSpec((tm, C_TOTAL), lambda i: (i, 0)),
        out_shape=jax.ShapeDtypeStruct((n, C_TOTAL), BF16),
        compiler_params=_cparams(("arbitrary",)),
        name="in_proj",
    )(x2d, gain, w, head_gain, seg_ones)


def _compress_kernel(t_ref, pos_ref, w1_ref, w2_ref, kg_ref, k_ref, v_ref, tok_scr):
    S, dh = t_ref.shape[0], HEAD_DIM
    m = S // CMP_STRIDE
    tok_scr[0:S, :] = t_ref[...].astype(F32)
    tok_scr[S:, :] = jnp.zeros((CMP_STRIDE, LANE), F32)
    pre = jnp.zeros((m, 2 * CMP_HIDDEN), F32)
    for l in range(CMP_BLOCK):
        x = tok_scr[pl.ds(l, m, stride=CMP_STRIDE), :] + pos_ref[l:l + 1, :]
        pre = pre + jnp.dot(x.astype(BF16), w1_ref[l], preferred_element_type=F32)
    out = jnp.dot(jax.nn.gelu(pre).astype(BF16), w2_ref[...], preferred_element_type=F32)
    k_ref[...] = jnp.concatenate([_rms(out[:, :dh], kg_ref[...]), jnp.zeros((m, LANE - dh), F32)],
                                 axis=1).astype(BF16)
    v_ref[...] = out[:, dh:].astype(BF16)


def _compress(proj, pos, w1, w2, k_gain):
    B, S, _ = proj.shape
    G, m = NSA_KV_GROUPS, S // CMP_STRIDE
    outs = [jax.ShapeDtypeStruct((B, G, m, width), BF16) for width in (LANE, HEAD_DIM)]
    ospecs = [pl.BlockSpec((None, None, m, width), lambda b, g: (b, g, 0, 0)) for width in (LANE, HEAD_DIM)]
    return pl.pallas_call(
        _compress_kernel,
        grid=(B, G),
        in_specs=[pl.BlockSpec((None, S, LANE), lambda b, g: (b, 0, C_CMP // LANE + g)),
                  pl.BlockSpec(pos.shape, lambda b, g: (0, 0)),
                  pl.BlockSpec(w1.shape, lambda b, g: (0, 0, 0)),
                  pl.BlockSpec(w2.shape, lambda b, g: (0, 0)),
                  pl.BlockSpec((1, HEAD_DIM), lambda b, g: (0, 0))],
        out_specs=ospecs,
        out_shape=outs,
        scratch_shapes=[pltpu.VMEM((S + CMP_STRIDE, LANE), F32)],
        compiler_params=_cparams(("arbitrary", "arbitrary")),
        name="nsa_compress",
    )(proj, pos, w1, w2, k_gain)


def _compress_params(cmp_pos, cmp_w1, cmp_w2):
    dh, hid = HEAD_DIM, CMP_HIDDEN
    w1 = cmp_w1.reshape(2, CMP_BLOCK, dh, hid)
    z1 = jnp.zeros((CMP_BLOCK, dh, hid), cmp_w1.dtype)
    w1 = jnp.concatenate([jnp.concatenate([w1[0], z1], axis=2),
                          jnp.concatenate([z1, w1[1]], axis=2)], axis=1)
    z2 = jnp.zeros((hid, dh), cmp_w2.dtype)
    w2 = jnp.concatenate([jnp.concatenate([cmp_w2[0], z2], axis=1),
                          jnp.concatenate([z2, cmp_w2[1]], axis=1)], axis=0)
    pos = jnp.concatenate([cmp_pos[0], cmp_pos[1]], axis=1)
    return pos, w1.astype(BF16), w2.astype(BF16)


def _dot_nt(a, b):
    return lax.dot_general(a, b, (((1,), (1,)), ((), ())), preferred_element_type=F32)


def _softmax_step(s, v, carry):
    m, l, acc = carry
    m_new = jnp.maximum(m, jnp.max(s, axis=-1, keepdims=True))
    alpha = jnp.exp2(m - m_new)
    p = jnp.exp2(s - m_new)
    l = alpha * l + jnp.sum(p, axis=-1, keepdims=True)
    acc = alpha * acc + jnp.dot(p.astype(BF16), v, preferred_element_type=F32)
    return m_new, l, acc


def _softmax_init(rows, dv):
    return (jnp.full((rows, 1), NEG, F32), jnp.zeros((rows, 1), F32), jnp.zeros((rows, dv), F32))


def _key_chunks(c):
    T = TQ
    chunks = [(0, T, (0, 0))] if c == 0 else [((c - 1) * T, 2 * T, (1, 0))]
    n_far = max(c - 1, 0)
    if n_far % 2:
        chunks.append((0, T, None))
    chunks += [(s * T, 2 * T, None) for s in range(n_far % 2, n_far, 2)]
    return chunks


def _nsa_kernel(q_ref, kv_ref, kc_ref, vc_ref, gate_ref, bc_ref, nb_ref, wm_ref, ov_ref, ex_ref, gx_ref,
                o_ref, mb_scr, osel_scr, *, seq_len):
    T, dh, Hg = TQ, HEAD_DIM, NSA_HPG
    R = Hg * T
    qi = pl.program_id(2)
    n_slc = seq_len // SLC_BLOCK

    lower = lax.broadcasted_iota(jnp.int32, (T, 2 * dh), 1) < dh
    q_lo, q_hi = [], []
    for pair in range(Hg // 2):
        pq = q_ref[:, pair * 2 * dh:(pair + 1) * 2 * dh]
        sw = jnp.concatenate([pq[:, dh:], pq[:, :dh]], axis=1)
        zero = jnp.zeros_like(pq)
        q_lo += [jnp.where(lower, pq, zero), jnp.where(lower, sw, zero)]
        q_hi += [jnp.where(lower, zero, sw), jnp.where(lower, zero, pq)]
    q = jnp.concatenate(q_lo, axis=0)
    q_win = jnp.concatenate(q_hi, axis=0)

    lc = _dot_nt(q, kc_ref[...]) + bc_ref[...].reshape(R, LANE)
    pc = jnp.exp2(lc - jnp.max(lc, axis=-1, keepdims=True))
    row = lax.broadcasted_iota(jnp.int32, (Hg, T, 1), 1).reshape(R, 1)
    visible = (qi * T + row >= CMP_BLOCK - 1).astype(F32)
    pc = pc * (visible / jnp.sum(pc, axis=-1, keepdims=True))
    o_cmp = jnp.dot(pc.astype(BF16), vc_ref[...], preferred_element_type=F32)

    n_all = SLC_TOP_N * SLC_BLOCK // T

    @pl.when(qi >= n_all)
    def _():
        p_sum = pc[0:T] + pc[T:2 * T] + pc[2 * T:3 * T] + pc[3 * T:4 * T]
        imp = lax.dot_general(ov_ref[...], p_sum, (((1,), (1,)), ((), ())), preferred_element_type=F32,
                              precision=lax.Precision.HIGHEST)
        blk = lax.broadcasted_iota(jnp.int32, (n_slc, T), 0)
        cur = (qi * T + lax.broadcasted_iota(jnp.int32, (n_slc, T), 1)) // SLC_BLOCK
        forced = (blk == 0) | (blk == cur) | (blk == cur - 1)
        score = jnp.where(blk <= cur, imp + jnp.where(forced, FORCE_SCORE, 0.0), NEG)
        rank = jnp.zeros((n_slc, T), jnp.int32)
        for i in range(n_slc):
            other = score[i:i + 1, :]
            ahead = (other > score) | ((other == score) & (blk > i))
            rank = rank + ahead.astype(jnp.int32)
        sel_bias = jnp.where(rank < SLC_TOP_N, 0.0, NEG).T.astype(BF16)
        mb_scr[...] = jnp.dot(sel_bias, ex_ref[...], preferred_element_type=F32)

    def sel_tile(c):
        carry = _softmax_init(R, 2 * dh)
        for start, width, table in _key_chunks(c):
            s = _dot_nt(q, kv_ref[start:start + width, 0:2 * dh]).reshape(Hg, T, width)
            if table is not None:
                s = s + nb_ref[table[0], :, :, table[1]:table[1] + width]
            if c >= n_all:
                s = s + mb_scr[:, start:start + width][None]
            carry = _softmax_step(s.reshape(R, width), kv_ref[start:start + width, 2 * dh:4 * dh], carry)
        osel_scr[...] = carry[2] / carry[1]

    for c in range(seq_len // T):
        pl.when(qi == c)(functools.partial(sel_tile, c))
    o_slc = osel_scr[...]

    def win_step(start, width, bias, carry):
        s = _dot_nt(q_win, kv_ref[pl.ds(start, width), 0:2 * dh]).reshape(Hg, T, width) + bias
        return _softmax_step(s.reshape(R, width), kv_ref[pl.ds(start, width), 2 * dh:4 * dh], carry)

    carry = win_step(pl.multiple_of(jnp.maximum(qi - 1, 0) * T, T), 2 * T, nb_ref[jnp.minimum(qi, 1)],
                     _softmax_init(R, 2 * dh))
    carry = lax.fori_loop(
        0, (qi >= 2).astype(jnp.int32),
        lambda j, c: win_step(pl.multiple_of((qi - 2) * T, T), T, wm_ref[...][None], c), carry)
    o_win = carry[2] / carry[1]

    gates = jax.nn.sigmoid(gate_ref[...].astype(F32))
    g_hi = gates.astype(BF16)
    g_lo = (gates - g_hi.astype(F32)).astype(BF16)
    g_wide = (jnp.dot(g_hi, gx_ref[...], preferred_element_type=F32)
              + jnp.dot(g_lo, gx_ref[...], preferred_element_type=F32))
    by_head = lambda o, lo: jnp.concatenate([o[h * T:(h + 1) * T, lo:lo + dh] for h in range(Hg)], axis=1)
    w = Hg * dh
    out = (g_wide[:, 0:w] * by_head(o_cmp, 0) + g_wide[:, w:2 * w] * by_head(o_slc, 0)
           + g_wide[:, 2 * w:3 * w] * by_head(o_win, dh))
    o_ref[...] = out.astype(BF16)


def _nsa_attention(proj, k_cmp, v_cmp, bias_cmp, near_bias, win_mask, overlap, expand):
    B, S, _ = proj.shape
    G, Hg, dh, T = NSA_KV_GROUPS, NSA_HPG, HEAD_DIM, TQ
    assert WINDOW == 2 * T and S % (2 * T) == 0
    n_slc = S // SLC_BLOCK
    gate_expand = np.zeros((LANE, 3 * Hg * dh), np.float32)
    for j in range(3 * Hg):
        gate_expand[j, j * dh:(j + 1) * dh] = 1.0
    gate_expand = jnp.asarray(gate_expand, BF16)
    return pl.pallas_call(
        functools.partial(_nsa_kernel, seq_len=S),
        grid=(B, G, S // T),
        in_specs=[
            pl.BlockSpec((None, T, Hg * dh), lambda b, g, i: (b, i, C_QN // (Hg * dh) + g)),
            pl.BlockSpec((None, S, 4 * dh), lambda b, g, i: (b, 0, C_KV // (4 * dh) + g)),
            pl.BlockSpec((None, None, LANE, LANE), lambda b, g, i: (b, g, 0, 0)),
            pl.BlockSpec((None, None, LANE, dh), lambda b, g, i: (b, g, 0, 0)),
            pl.BlockSpec((None, T, LANE), lambda b, g, i: (b, i, C_GATE // LANE + g)),
            pl.BlockSpec((None, Hg, T, LANE), lambda b, g, i: (g, 0, i, 0)),
            pl.BlockSpec((None, 2, Hg, T, 2 * T), lambda b, g, i: (g, 0, 0, 0, 0)),
            pl.BlockSpec((T, T), lambda b, g, i: (0, 0)),
            pl.BlockSpec((n_slc, LANE), lambda b, g, i: (0, 0)),
            pl.BlockSpec((n_slc, S), lambda b, g, i: (0, 0)),
            pl.BlockSpec(gate_expand.shape, lambda b, g, i: (0, 0)),
        ],
        out_specs=pl.BlockSpec((None, T, Hg * dh), lambda b, g, i: (b, i, g)),
        out_shape=jax.ShapeDtypeStruct((B, S, NSA_HEADS * dh), BF16),
        scratch_shapes=[pltpu.VMEM((T, S), F32), pltpu.VMEM((Hg * T, 2 * dh), F32)],
        compiler_params=_cparams(("arbitrary", "arbitrary", "arbitrary")),
        name="nsa_attention",
    )(proj, proj, k_cmp, v_cmp, proj, bias_cmp, near_bias, win_mask, overlap, expand, gate_expand)


def _diff_kernel(q_ref, k_ref, v_ref, lam_ref, og_ref, nb_ref, o_ref, *, lambda_init, n_tiles):
    T, dh = TQ, HEAD_DIM
    qi = pl.program_id(2)
    q = q_ref[...]
    lower = lax.broadcasted_iota(jnp.int32, q.shape, 1) < dh
    q1 = jnp.where(lower, q, jnp.zeros_like(q))
    q2 = jnp.where(lower, jnp.zeros_like(q), q)

    def step(start, width, bias, carry):
        c1, c2 = carry
        v = v_ref[start:start + width, :]
        k = k_ref[start:start + width, :]
        s1 = _dot_nt(q1, k)
        s2 = _dot_nt(q2, k)
        if bias is not None:
            s1, s2 = s1 + bias, s2 + bias
        return _softmax_step(s1, v, c1), _softmax_step(s2, v, c2)

    lam_p = lam_ref[...]
    lam = (jnp.exp(jnp.sum(lam_p[0:1] * lam_p[1:2], axis=-1, keepdims=True))
           - jnp.exp(jnp.sum(lam_p[2:3] * lam_p[3:4], axis=-1, keepdims=True)) + lambda_init)

    def tile(c):
        init = _softmax_init(T, DIFF_V_DIM)
        carry = (init, init)
        for start, width, table in _key_chunks(c):
            bias = None if table is None else nb_ref[table[0], :, table[1]:table[1] + width]
            carry = step(start, width, bias, carry)
        (_, l1, a1), (_, l2, a2) = carry
        o = a1 / l1 - lam * (a2 / l2)
        o_ref[...] = (_rms(o, og_ref[...]) * (1.0 - lambda_init)).astype(BF16)

    for c in range(n_tiles):
        pl.when(qi == c)(functools.partial(tile, c))


def _diff_attention(proj, lam_p, out_gain, near_bias, lambda_init):
    B, S, _ = proj.shape
    T, dh, H = TQ, HEAD_DIM, DIFF_HEADS
    return pl.pallas_call(
        functools.partial(_diff_kernel, lambda_init=lambda_init, n_tiles=S // T),
        grid=(B, H, S // T),
        in_specs=[
            pl.BlockSpec((None, T, 2 * dh), lambda b, h, i: (b, i, C_QD // (2 * dh) + h)),
            pl.BlockSpec((None, S, 2 * dh), lambda b, h, i: (b, 0, C_KD // (2 * dh) + h)),
            pl.BlockSpec((None, S, DIFF_V_DIM), lambda b, h, i: (b, 0, C_VD // DIFF_V_DIM + h)),
            pl.BlockSpec((4, dh), lambda b, h, i: (0, 0)),
            pl.BlockSpec((1, DIFF_V_DIM), lambda b, h, i: (0, 0)),
            pl.BlockSpec((None, 2, T, 2 * T), lambda b, h, i: (h, 0, 0, 0)),
        ],
        out_specs=pl.BlockSpec((None, T, DIFF_V_DIM), lambda b, h, i: (b, i, h)),
        out_shape=jax.ShapeDtypeStruct((B, S, H * DIFF_V_DIM), BF16),
        compiler_params=_cparams(("arbitrary", "arbitrary", "arbitrary")),
        name="diff_attention",
    )(proj, proj, proj, lam_p, out_gain, near_bias)


def _merge_kernel(x_ref, on_ref, od_ref, gm_ref, wn_ref, wd_ref, wo_ref, g2_ref, rw_ref, rb_ref,
                  x2_ref, route_ref, *, tm):
    d = x_ref.shape[-1]
    a = jnp.dot(on_ref[...], wn_ref[...], preferred_element_type=F32)
    b = jnp.dot(od_ref[...], wd_ref[...], preferred_element_type=F32)
    sigmoid = lambda z: 0.5 * jnp.tanh(0.5 * z) + 0.5
    y = sigmoid(gm_ref[:, :d].astype(F32)) * a + sigmoid(gm_ref[:, d:].astype(F32)) * b
    x2 = x_ref[...] + jnp.dot(y.astype(BF16), wo_ref[...], preferred_element_type=F32)
    n_chunk = d // LANE
    for k in range(n_chunk):
        x2_ref[pl.ds(k, tm, stride=n_chunk), :] = x2[:, k * LANE:(k + 1) * LANE]

    h2 = _rms(x2, g2_ref[...])
    h_hi = h2.astype(BF16)
    h_lo = (h2 - h_hi.astype(F32)).astype(BF16)
    part = jnp.dot(jnp.concatenate([h_hi, h_lo], axis=0), rw_ref[...], preferred_element_type=F32)
    logits = part[:tm, :LANE] + part[:tm, LANE:] + part[tm:, :LANE] + part[tm:, LANE:] + rb_ref[...]
    lane = lax.broadcasted_iota(jnp.int32, logits.shape, 1)
    big = jnp.int32(LANE)

    def first_argmax(vals, mask):
        vals = jnp.where(mask, vals, -jnp.inf)
        top = jnp.max(vals, axis=-1, keepdims=True)
        idx = jnp.min(jnp.where(vals == top, lane, big), axis=-1, keepdims=True)
        return top, idx

    is_group = lane < N_GROUPS
    g_max, g_top = first_argmax(logits, is_group)
    g_top_p = 1.0 / jnp.sum(jnp.where(is_group, jnp.exp(logits - g_max), 0.0), axis=-1, keepdims=True)
    lo = N_GROUPS + g_top * EXPERTS_PER_GROUP
    in_group = (lane >= lo) & (lane < lo + EXPERTS_PER_GROUP)
    m1, i1 = first_argmax(logits, in_group)
    m2, i2 = first_argmax(logits, in_group & (lane != i1))
    e2 = jnp.exp(m2 - m1)
    w1 = g_top_p / (1.0 + e2)
    w2 = g_top_p * e2 / (1.0 + e2)
    la = jnp.minimum(i1, i2) - lo
    lb = jnp.maximum(i1, i2) - lo
    pair = jnp.right_shift(la * (7 - la), 1) + (lb - la - 1)
    bucket = (g_top * N_PAIRS + pair).astype(F32)
    exp_lane = lane + N_GROUPS
    route = (jnp.where(exp_lane == i1, w1, 0.0) + jnp.where(exp_lane == i2, w2, 0.0)
             + jnp.where(lane == N_EXPERTS, bucket, 0.0))
    route_ref[...] = route


def _merge(x2d, o_nsa, o_diff, proj2d, wn, wd, wo, g2, rw, rb):
    n, d = x2d.shape
    tm = 512
    rows = d // LANE
    rw_hi = rw.astype(BF16)
    rw_cat = jnp.concatenate([rw_hi, (rw - rw_hi.astype(F32)).astype(BF16)], axis=1)
    full = lambda a: pl.BlockSpec(a.shape, lambda i: (0,) * a.ndim)
    return pl.pallas_call(
        functools.partial(_merge_kernel, tm=tm),
        grid=(n // tm,),
        in_specs=[pl.BlockSpec((tm, d), lambda i: (i, 0)),
                  pl.BlockSpec((tm, o_nsa.shape[1]), lambda i: (i, 0)),
                  pl.BlockSpec((tm, o_diff.shape[1]), lambda i: (i, 0)),
                  pl.BlockSpec((tm, 2 * d), lambda i: (i, C_GM // (2 * d))),
                  full(wn), full(wd), full(wo), full(g2), full(rw_cat), full(rb)],
        out_specs=[pl.BlockSpec((tm * rows, LANE), lambda i: (i, 0)),
                   pl.BlockSpec((tm, LANE), lambda i: (i, 0))],
        out_shape=[jax.ShapeDtypeStruct((n * rows, LANE), F32), jax.ShapeDtypeStruct((n, LANE), F32)],
        compiler_params=_cparams(("arbitrary",)),
        name="merge_router",
    )(x2d, o_nsa, o_diff, proj2d, wn, wd, wo, g2, rw_cat, rb)


def _moe_kernel(tile_ref, ea_ref, eb_ref, lo_ref, hi_ref, first_ref, perm_ref,
                x_hbm, r_hbm, g2_ref, wga_ref, wua_ref, wda_ref, wgb_ref, wub_ref, wdb_ref,
                o_hbm, xb0, xb1, rb0, rb1, ac0, ac1, h_scr, g_sem, r_sem, s_sem, *, tm, n_tiles):
    d = o_hbm.shape[1]
    n_chunk = d // LANE
    v = pl.program_id(0)
    t = tile_ref[v]
    x_bufs, r_bufs, acc_bufs = (xb0, xb1), (rb0, rb1), (ac0, ac1)
    last_slot = (n_tiles - 1) % 2

    def gather_copies(tt, sl, r):
        token = perm_ref[tt * tm + r]
        return (pltpu.make_async_copy(x_hbm.at[pl.ds(pl.multiple_of(token * n_chunk, n_chunk), n_chunk), :],
                                      x_bufs[sl].at[pl.ds(r * n_chunk, n_chunk), :], g_sem.at[sl]),
                pltpu.make_async_copy(r_hbm.at[pl.ds(token, 1), :],
                                      r_bufs[sl].at[pl.ds(r, 1), :], r_sem.at[sl]))

    class gather_copy:
        def __init__(self, tt, sl, r):
            self.copies = gather_copies(tt, sl, r)

        def start(self):
            for c in self.copies:
                c.start()

        def wait(self):
            for c in self.copies:
                c.wait()

    def scatter_copy(tt, sl, r):
        return pltpu.make_async_copy(acc_bufs[sl].at[pl.ds(r, 1), :],
                                     o_hbm.at[pl.ds(perm_ref[tt * tm + r], 1), :], s_sem.at[sl])

    def for_rows(fn):
        for r in range(tm):
            fn(r)

    def experts(sl, dma_starts):
        n_stage = 6
        per_stage = -(-len(dma_starts) // n_stage)

        def issue(stage):
            for start in dma_starts[stage * per_stage:(stage + 1) * per_stage]:
                start()

        h = h_scr[...]
        cw = r_bufs[sl][...]
        lane = lax.broadcasted_iota(jnp.int32, cw.shape, 1)
        row = lax.broadcasted_iota(jnp.int32, (tm, 1), 0)
        in_bucket = (row >= lo_ref[v]) & (row < hi_ref[v])
        out = None
        for i, (e, wg_ref, wu_ref, wd_ref) in enumerate(((ea_ref[v], wga_ref, wua_ref, wda_ref),
                                                         (eb_ref[v], wgb_ref, wub_ref, wdb_ref))):
            w = jnp.sum(jnp.where(lane == e, cw, 0.0), axis=-1, keepdims=True)
            w = jnp.where(in_bucket, w, 0.0)
            gate = jnp.dot(h, wg_ref[...], preferred_element_type=F32)
            issue(3 * i)
            up = jnp.dot(h, wu_ref[...], preferred_element_type=F32)
            issue(3 * i + 1)
            hid = jax.nn.silu(gate) * up * w
            y = jnp.dot(hid.astype(BF16), wd_ref[...], preferred_element_type=F32)
            issue(3 * i + 2)
            out = y if out is None else out + y
        acc_bufs[sl][...] += out

    def first_visit(sl, scatter_previous):
        for_rows(lambda r: gather_copy(t, sl, r).wait())

        @pl.when(t >= 2)
        def _():
            for_rows(lambda r: scatter_copy(t - 2, sl, r).wait())

        chunks = [x_bufs[sl][pl.ds(k, tm, stride=n_chunk), :] for k in range(n_chunk)]
        ssq = chunks[0] * chunks[0]
        for c in chunks[1:]:
            ssq = ssq + c * c
        inv = lax.rsqrt(jnp.sum(ssq, axis=-1, keepdims=True) * (1.0 / d) + RMS_EPS)
        for k, c in enumerate(chunks):
            cols = slice(k * LANE, (k + 1) * LANE)
            h_scr[:, cols] = (c * inv * g2_ref[:, cols]).astype(BF16)
            acc_bufs[sl][:, cols] = c
        nxt = jnp.minimum(t + 1, n_tiles - 1)
        starts = [functools.partial(lambda r: gather_copy(nxt, 1 - sl, r).start(), r) for r in range(tm)]
        if scatter_previous:
            starts += [functools.partial(lambda r: scatter_copy(t - 1, 1 - sl, r).start(), r)
                       for r in range(tm)]
        experts(sl, starts)

    @pl.when(v == 0)
    def _():
        for_rows(lambda r: gather_copy(0, 0, r).start())
        first_visit(0, scatter_previous=False)

    for sl in range(2):
        pl.when((v > 0) & (first_ref[v] == 1) & (t % 2 == sl))(
            functools.partial(first_visit, sl, scatter_previous=True))
        pl.when((first_ref[v] == 0) & (t % 2 == sl))(functools.partial(experts, sl, []))

    @pl.when(v == pl.num_programs(0) - 1)
    def _():
        for_rows(lambda r: scatter_copy(n_tiles - 1, last_slot, r).start())
        for_rows(lambda r: gather_copy(n_tiles - 1, 1 - last_slot, r).wait())
        for_rows(lambda r: scatter_copy(n_tiles - 2, 1 - last_slot, r).wait())
        for_rows(lambda r: scatter_copy(n_tiles - 1, last_slot, r).wait())


def _moe(sched, perm, x_slabs, route, g2, wg, wu, wd, tm):
    n = route.shape[0]
    d, f = wg.shape[1], wg.shape[2]
    n_visits = sched[0].shape[0]
    n_tiles = n // tm
    assert n_tiles >= 2
    w_in_a = pl.BlockSpec((None, d, f), lambda v, t, ea, eb, *_: (ea[v], 0, 0))
    w_in_b = pl.BlockSpec((None, d, f), lambda v, t, ea, eb, *_: (eb[v], 0, 0))
    w_out_a = pl.BlockSpec((None, f, d), lambda v, t, ea, eb, *_: (ea[v], 0, 0))
    w_out_b = pl.BlockSpec((None, f, d), lambda v, t, ea, eb, *_: (eb[v], 0, 0))
    return pl.pallas_call(
        functools.partial(_moe_kernel, tm=tm, n_tiles=n_tiles),
        grid_spec=pltpu.PrefetchScalarGridSpec(
            num_scalar_prefetch=7,
            grid=(n_visits,),
            in_specs=[pl.BlockSpec(memory_space=pl.ANY), pl.BlockSpec(memory_space=pl.ANY),
                      pl.BlockSpec((1, d), lambda v, *_: (0, 0)),
                      w_in_a, w_in_a, w_out_a, w_in_b, w_in_b, w_out_b],
            out_specs=pl.BlockSpec(memory_space=pl.ANY),
            scratch_shapes=[pltpu.VMEM((tm * d // LANE, LANE), F32), pltpu.VMEM((tm * d // LANE, LANE), F32),
                            pltpu.VMEM((tm, LANE), F32), pltpu.VMEM((tm, LANE), F32),
                            pltpu.VMEM((tm, d), F32), pltpu.VMEM((tm, d), F32),
                            pltpu.VMEM((tm, d), BF16),
                            pltpu.SemaphoreType.DMA((2,)), pltpu.SemaphoreType.DMA((2,)),
                            pltpu.SemaphoreType.DMA((2,))]),
        out_shape=jax.ShapeDtypeStruct((n, d), F32),
        compiler_params=_cparams(("arbitrary",)),
        name="moe_experts",
    )(*sched, perm, x_slabs, route, g2, wg, wu, wd, wg, wu, wd)


def _moe_schedule(bucket, tm):
    n = bucket.shape[0]
    n_tiles = n // tm
    n_visits = n_tiles + N_BUCKETS - 1
    counts = jnp.sum((bucket[:, None] == jnp.arange(N_BUCKETS)[None, :]).astype(jnp.int32), axis=0)
    ends = jnp.cumsum(counts)
    starts = ends - counts
    first_tile = starts // tm
    last_tile = jnp.maximum(ends - 1, 0) // tm
    n_vis = jnp.where(counts > 0, last_tile - first_tile + 1, 0)
    vis_end = jnp.cumsum(n_vis)
    vis_start = vis_end - n_vis
    v = jnp.arange(n_visits, dtype=jnp.int32)
    b = jnp.minimum(jnp.searchsorted(vis_end, v, side="right"), N_BUCKETS - 1).astype(jnp.int32)
    live = v < vis_end[-1]
    tile = jnp.where(live, first_tile[b] + v - vis_start[b], n_tiles - 1).astype(jnp.int32)
    lo = jnp.where(live, jnp.clip(starts[b] - tile * tm, 0, tm), 0).astype(jnp.int32)
    hi = jnp.where(live, jnp.clip(ends[b] - tile * tm, 0, tm), 0).astype(jnp.int32)
    b_last = b[jnp.maximum(vis_end[-1] - 1, 0)]
    b = jnp.where(live, b, b_last)
    pair_a = np.array([0, 0, 0, 1, 1, 2], np.int32)
    pair_b = np.array([1, 2, 3, 2, 3, 3], np.int32)
    grp = b // N_PAIRS
    ea = (grp * EXPERTS_PER_GROUP + jnp.asarray(pair_a)[b % N_PAIRS]).astype(jnp.int32)
    eb = (grp * EXPERTS_PER_GROUP + jnp.asarray(pair_b)[b % N_PAIRS]).astype(jnp.int32)
    first = jnp.concatenate([jnp.ones((1,), jnp.int32), (tile[1:] != tile[:-1]).astype(jnp.int32)])
    return tile, ea, eb, lo, hi, first


def _layer(x, layer, p):
    B, S, D = x.shape
    G, dh = NSA_KV_GROUPS, HEAD_DIM
    lambda_init = 0.8 - 0.6 * math.exp(-0.3 * layer)
    x2d = x.reshape(B * S, D)

    proj2d = _in_proj(x2d, p["norm1_g"][None, :], _reorder_w_in(p["w_in"]), _head_norm_rows(p))
    proj = proj2d.reshape(B, S, C_TOTAL)

    k_cmp, v_cmp = _compress(proj, *_compress_params(p["cmp_pos"], p["cmp_w1"], p["cmp_w2"]),
                             p["nsa_k_norm"][0:1])

    near, bias_cmp = _bias_tables(p["rel_bias"], S)
    nsa_near = near[:NSA_HEADS].reshape(G, NSA_HPG, 2, TQ, 2 * TQ).transpose(0, 2, 1, 3, 4)
    i = np.arange(TQ)
    win_mask = jnp.asarray(np.where(i[None, :] > i[:, None], 0.0, NEG).astype(np.float32))
    expand = jnp.asarray(np.repeat(np.eye(S // SLC_BLOCK, dtype=np.float32), SLC_BLOCK, axis=1), BF16)
    o_nsa = _nsa_attention(proj, k_cmp, v_cmp, bias_cmp.reshape(G, NSA_HPG, S, LANE), nsa_near, win_mask,
                           jnp.asarray(_overlap_matrix(S)), expand)
    o_diff = _diff_attention(proj, p["diff_lambda"], p["diff_out_norm"][None, :], near[NSA_HEADS:],
                             lambda_init)

    rw = jnp.concatenate([p["router_group_w"], p["router_expert_w"],
                          jnp.zeros((D, LANE - N_GROUPS - N_EXPERTS), F32)], axis=1)
    rb = jnp.concatenate([p["router_group_b"], p["router_expert_b"],
                          jnp.zeros((LANE - N_GROUPS - N_EXPERTS,), F32)])[None, :]
    x_slabs, route = _merge(x2d, o_nsa.reshape(B * S, -1), o_diff.reshape(B * S, -1), proj2d,
                            p["w_branch_nsa"].astype(BF16), p["w_branch_diff"].astype(BF16),
                            p["w_out"].astype(BF16), p["norm2_g"][None, :], rw, rb)

    tm = 256
    bucket = route[:, N_EXPERTS].astype(jnp.int32)
    perm = jnp.argsort(bucket).astype(jnp.int32)
    out = _moe(_moe_schedule(bucket, tm), perm, x_slabs, route, p["norm2_g"][None, :],
               p["expert_w_gate"].astype(BF16), p["expert_w_up"].astype(BF16),
               p["expert_w_down"].astype(BF16), tm)
    return out.reshape(B, S, D)


def kernel(x, norm1_g, w_in, nsa_q_norm, nsa_k_norm, cmp_pos, cmp_w1, cmp_w2, diff_q_norm, diff_k_norm,
           diff_lambda, diff_out_norm, rel_bias, w_branch_nsa, w_branch_diff, w_out, norm2_g,
           router_group_w, router_group_b, router_expert_w, router_expert_b, expert_w_gate,
           expert_w_up, expert_w_down):
    per_layer = dict(norm1_g=norm1_g, w_in=w_in, nsa_q_norm=nsa_q_norm, nsa_k_norm=nsa_k_norm,
                     cmp_pos=cmp_pos, cmp_w1=cmp_w1, cmp_w2=cmp_w2, diff_q_norm=diff_q_norm,
                     diff_k_norm=diff_k_norm, diff_lambda=diff_lambda, diff_out_norm=diff_out_norm,
                     w_branch_nsa=w_branch_nsa, w_branch_diff=w_branch_diff, w_out=w_out,
                     norm2_g=norm2_g, router_group_w=router_group_w, router_group_b=router_group_b,
                     router_expert_w=router_expert_w, router_expert_b=router_expert_b,
                     expert_w_gate=expert_w_gate, expert_w_up=expert_w_up, expert_w_down=expert_w_down)
    for layer in range(norm1_g.shape[0]):
        p = {k: v[layer] for k, v in per_layer.items()}
        p["rel_bias"] = rel_bias
        x = _layer(x, layer, p)
    return x
```

```python
import functools
import math

import numpy as np
import jax
import jax.numpy as jnp
from jax import lax
from jax.experimental import pallas as pl
from jax.experimental.pallas import tpu as pltpu

F32 = jnp.float32
BF16 = jnp.bfloat16

HEAD_DIM = 64
SCALE = HEAD_DIM ** -0.5
NSA_HEADS = 8
NSA_KV_GROUPS = 2
NSA_HPG = NSA_HEADS // NSA_KV_GROUPS
CMP_BLOCK = 32
CMP_STRIDE = 16
CMP_HIDDEN = 256
SLC_BLOCK = 64
SLC_TOP_N = 16
WINDOW = 512
FORCE_SCORE = 1e4
DIFF_HEADS = 4
DIFF_V_DIM = 2 * HEAD_DIM
REL_BUCKETS = 32
REL_MAX_DIST = 128
N_GROUPS = 4
EXPERTS_PER_GROUP = 4
N_EXPERTS = N_GROUPS * EXPERTS_PER_GROUP
EXPERT_HIDDEN = 512
RMS_EPS = 1e-6
NEG = -1e30
LOG2E = math.log2(math.e)

LANE = 128
MXU_DIM = 256
TQ = 256
N_PAIRS = 6
N_BUCKETS = N_GROUPS * N_PAIRS

C_GM = 0
C_QN = 2048
C_KV = 2560
C_CMP = 3072
C_GATE = 3328
C_QD = 3584
C_KD = 4096
C_VD = 4608
C_TOTAL = 5120

VMEM_LIMIT = 56 * 1024 * 1024


def _cparams(sem):
    return pltpu.CompilerParams(dimension_semantics=sem, vmem_limit_bytes=VMEM_LIMIT)


def _rms(x, gain):
    return x * lax.rsqrt(jnp.mean(x * x, axis=-1, keepdims=True) + RMS_EPS) * gain


def _reorder_w_in(w_in):
    d_model = w_in.shape[0]
    dh, G, Hg = HEAD_DIM, NSA_KV_GROUPS, NSA_HPG
    o_q = 0
    o_kv = o_q + NSA_HEADS * dh
    o_g = o_kv + 3 * 2 * G * dh
    o_qd = o_g + 3 * NSA_HEADS
    o_gm = o_qd + 3 * DIFF_HEADS * 2 * dh

    def cols(start, n):
        return w_in[:, start:start + n]

    def kv(kind, which, g):
        return cols(o_kv + ((kind * 2 + which) * G + g) * dh, dh)

    parts = [cols(o_gm, 2 * d_model), cols(o_q, NSA_HEADS * dh)]
    parts += [kv(kind, which, g) for g in range(G) for which in (0, 1) for kind in (1, 2)]
    parts += [kv(0, which, g) for g in range(G) for which in (0, 1)]
    for g in range(G):
        parts += [cols(o_g + (kind * G + g) * Hg, Hg) for kind in range(3)]
        parts.append(jnp.zeros((d_model, LANE - 3 * Hg), w_in.dtype))
    parts.append(cols(o_qd, 3 * DIFF_HEADS * 2 * dh))
    w = jnp.concatenate(parts, axis=1)
    assert w.shape[1] == C_TOTAL
    return w.astype(BF16)


def _head_norm_rows(p):
    dh, G = HEAD_DIM, NSA_KV_GROUPS
    qs = SCALE * LOG2E
    zeros = lambda n: jnp.zeros((n,), F32)
    kn = p["nsa_k_norm"]
    parts = [zeros(C_QN - C_GM), jnp.tile(p["nsa_q_norm"] * qs, NSA_HEADS)]
    parts += [jnp.concatenate([kn[1], kn[2], zeros(2 * dh)])] * G
    parts += [zeros(C_QD - C_CMP), jnp.tile((p["diff_q_norm"] * qs).reshape(-1), DIFF_HEADS),
              jnp.tile(p["diff_k_norm"].reshape(-1), DIFF_HEADS), zeros(C_TOTAL - C_VD)]
    gain = jnp.concatenate(parts)
    assert gain.shape[0] == C_TOTAL
    return gain[None, :]


def _t5_bucket(dist):
    n = jnp.maximum(dist, 0)
    max_exact = REL_BUCKETS // 2
    nf = jnp.maximum(n, 1).astype(F32)
    large = max_exact + (jnp.log(nf / max_exact) / math.log(REL_MAX_DIST / max_exact)
                         * (REL_BUCKETS - max_exact)).astype(jnp.int32)
    large = jnp.minimum(large, REL_BUCKETS - 1)
    return jnp.where(n < max_exact, n, large)


def _toeplitz(v_ext, rows, cols):
    length = v_ext.shape[-1]
    lead = v_ext.shape[:-1]
    flat = jnp.broadcast_to(v_ext[..., None, :], lead + (rows, length)).reshape(lead + (rows * length,))
    return flat[..., :rows * (length - 1)].reshape(lead + (rows, length - 1))[..., :cols]


def _bias_tables(rel_bias, S):
    heads = rel_bias.shape[1]
    far = REL_MAX_DIST
    assert TQ == 2 * far and LANE == far
    bd = rel_bias[_t5_bucket(jnp.arange(2 * far))].T.astype(F32) * LOG2E
    rel = bd - bd[:, far:far + 1]
    neg = lambda *shape: jnp.full((heads,) + shape, NEG, F32)
    zero = lambda *shape: jnp.zeros((heads,) + shape, F32)
    rev = lambda a: a[:, ::-1]
    d0 = _toeplitz(jnp.concatenate([rel[:, 0:1], neg(far), rev(rel[:, 1:far])], axis=1), far, far)
    d1 = _toeplitz(jnp.concatenate([rev(rel[:, 1:far + 1]), zero(far)], axis=1), far, far)
    z, n = zero(far, far), neg(far, far)
    blocks = lambda rows: jnp.concatenate([jnp.concatenate(r, axis=2) for r in rows], axis=1)
    near = jnp.stack([blocks([[d0, n, n, n], [d1, d0, n, n]]),
                      blocks([[z, d1, d0, n], [z, z, d1, d0]])], axis=1)
    hb = rel[:NSA_HEADS]
    zero8 = lambda *shape: jnp.zeros((NSA_HEADS,) + shape, F32)
    neg8 = lambda *shape: jnp.full((NSA_HEADS,) + shape, NEG, F32)
    width = TQ + far - 1
    tp = _toeplitz(jnp.concatenate([rev(hb[:, 0:far]), neg8(TQ + 1), zero8(TQ - 1)], axis=1), TQ, width)
    n_tiles = S // TQ
    per_tile = TQ // CMP_STRIDE
    col0 = per_tile * (n_tiles - 1)
    shift = CMP_STRIDE * col0 - (CMP_BLOCK - 1)
    first = -(-(shift - (far - 1)) // CMP_STRIDE)
    last = (shift + TQ - 1) // CMP_STRIDE
    band = tp[:, :, CMP_STRIDE * first - shift + far - 1::CMP_STRIDE][:, :, :last - first + 1]
    g2 = jnp.concatenate([zero8(TQ, first), band, neg8(TQ, 2 * TQ - last - 1)], axis=2)
    bias_cmp = jnp.stack([g2[:, :, col0 - per_tile * i:col0 - per_tile * i + LANE] for i in range(n_tiles)],
                         axis=1)
    n_cmp = (S - CMP_BLOCK) // CMP_STRIDE + 1
    bias_cmp = jnp.where(np.arange(LANE) < n_cmp, bias_cmp, NEG)
    return near, bias_cmp.reshape(NSA_HEADS, S, LANE)


def _overlap_matrix(S):
    n_cmp = (S - CMP_BLOCK) // CMP_STRIDE + 1
    n_slc = S // SLC_BLOCK
    c_start = np.arange(n_cmp) * CMP_STRIDE
    s_start = np.arange(n_slc) * SLC_BLOCK
    lo = np.maximum(s_start[:, None], c_start[None, :])
    hi = np.minimum(s_start[:, None] + SLC_BLOCK, c_start[None, :] + CMP_BLOCK)
    overlap = (np.maximum(hi - lo, 0) / CMP_BLOCK).astype(np.float32)
    out = np.zeros((n_slc, LANE), np.float32)
    out[:, :n_cmp] = overlap
    return out


IN_CHUNK = 512
HEAD_NORM_CHUNKS = {C_QN // IN_CHUNK: "all", C_KV // IN_CHUNK: "keys",
                    C_QD // IN_CHUNK: "all", C_KD // IN_CHUNK: "all"}


def _in_proj_kernel(x_ref, g_ref, w_ref, hg_ref, seg_ref, o_ref):
    dh = HEAD_DIM
    h = _rms(x_ref[...], g_ref[...]).astype(BF16)
    for c in range(C_TOTAL // IN_CHUNK):
        cols = slice(c * IN_CHUNK, (c + 1) * IN_CHUNK)
        y = jnp.dot(h, w_ref[:, cols], preferred_element_type=F32)
        mode = HEAD_NORM_CHUNKS.get(c)
        if mode is not None:
            sq = (y * y).astype(BF16)
            seg_w = seg_ref.shape[0]
            ss = jnp.concatenate([jnp.dot(sq[:, j:j + seg_w], seg_ref[...], preferred_element_type=F32)
                                  for j in range(0, IN_CHUNK, seg_w)], axis=1)
            scale = lax.rsqrt(ss * (1.0 / dh) + RMS_EPS) * hg_ref[:, cols]
            if mode == "keys":
                lane = lax.broadcasted_iota(jnp.int32, (1, IN_CHUNK), 1)
                scale = jnp.where(lane % (4 * dh) < 2 * dh, scale, 1.0)
            y = y * scale
        o_ref[:, cols] = y.astype(BF16)


def _in_proj(x2d, gain, w, head_gain):
    n, d = x2d.shape
    tm = 256
    seg = np.arange(MXU_DIM) // HEAD_DIM
    seg_ones = jnp.asarray((seg[:, None] == seg[None, :]).astype(np.float32), BF16)
    return pl.pallas_call(
        _in_proj_kernel,
        grid=(n // tm,),
        in_specs=[pl.BlockSpec((tm, d), lambda i: (i, 0)),
                  pl.BlockSpec((1, d), lambda i: (0, 0)),
                  pl.BlockSpec((d, C_TOTAL), lambda i: (0, 0)),
                  pl.BlockSpec((1, C_TOTAL), lambda i: (0, 0)),
                  pl.BlockSpec((MXU_DIM, MXU_DIM), lambda i: (0, 0))],
        out_specs=pl.BlockSpec((tm, C_TOTAL), lambda i: (i, 0)),
        out_shape=jax.ShapeDtypeStruct((n, C_TOTAL), BF16),
        compiler_params=_cparams(("arbitrary",)),
        name="in_proj",
    )(x2d, gain, w, head_gain, seg_ones)


def _compress_kernel(t_ref, pos_ref, w1_ref, w2_ref, kg_ref, k_ref, v_ref, tok_scr):
    S, dh = t_ref.shape[0], HEAD_DIM
    m = S // CMP_STRIDE
    tok_scr[0:S, :] = t_ref[...].astype(F32)
    tok_scr[S:, :] = jnp.zeros((CMP_STRIDE, LANE), F32)
    pre = jnp.zeros((m, 2 * CMP_HIDDEN), F32)
    for l in range(CMP_BLOCK):
        x = tok_scr[pl.ds(l, m, stride=CMP_STRIDE), :] + pos_ref[l:l + 1, :]
        pre = pre + jnp.dot(x.astype(BF16), w1_ref[l], preferred_element_type=F32)
    out = jnp.dot(jax.nn.gelu(pre).astype(BF16), w2_ref[...], preferred_element_type=F32)
    k_ref[...] = jnp.concatenate([_rms(out[:, :dh], kg_ref[...]), jnp.zeros((m, LANE - dh), F32)],
                                 axis=1).astype(BF16)
    v_ref[...] = out[:, dh:].astype(BF16)


def _compress(proj, pos, w1, w2, k_gain):
    B, S, _ = proj.shape
    G, m = NSA_KV_GROUPS, S // CMP_STRIDE
    outs = [jax.ShapeDtypeStruct((B, G, m, width), BF16) for width in (LANE, HEAD_DIM)]
    ospecs = [pl.BlockSpec((None, None, m, width), lambda b, g: (b, g, 0, 0)) for width in (LANE, HEAD_DIM)]
    return pl.pallas_call(
        _compress_kernel,
        grid=(B, G),
        in_specs=[pl.BlockSpec((None, S, LANE), lambda b, g: (b, 0, C_CMP // LANE + g)),
                  pl.BlockSpec(pos.shape, lambda b, g: (0, 0)),
                  pl.BlockSpec(w1.shape, lambda b, g: (0, 0, 0)),
                  pl.BlockSpec(w2.shape, lambda b, g: (0, 0)),
                  pl.BlockSpec((1, HEAD_DIM), lambda b, g: (0, 0))],
        out_specs=ospecs,
        out_shape=outs,
        scratch_shapes=[pltpu.VMEM((S + CMP_STRIDE, LANE), F32)],
        compiler_params=_cparams(("arbitrary", "arbitrary")),
        name="nsa_compress",
    )(proj, pos, w1, w2, k_gain)


def _compress_params(cmp_pos, cmp_w1, cmp_w2):
    dh, hid = HEAD_DIM, CMP_HIDDEN
    w1 = cmp_w1.reshape(2, CMP_BLOCK, dh, hid)
    z1 = jnp.zeros((CMP_BLOCK, dh, hid), cmp_w1.dtype)
    w1 = jnp.concatenate([jnp.concatenate([w1[0], z1], axis=2),
                          jnp.concatenate([z1, w1[1]], axis=2)], axis=1)
    z2 = jnp.zeros((hid, dh), cmp_w2.dtype)
    w2 = jnp.concatenate([jnp.concatenate([cmp_w2[0], z2], axis=1),
                          jnp.concatenate([z2, cmp_w2[1]], axis=1)], axis=0)
    pos = jnp.concatenate([cmp_pos[0], cmp_pos[1]], axis=1)
    return pos, w1.astype(BF16), w2.astype(BF16)


def _dot_nt(a, b):
    return lax.dot_general(a, b, (((1,), (1,)), ((), ())), preferred_element_type=F32)


def _softmax_step(s, v, carry):
    m, l, acc = carry
    m_new = jnp.maximum(m, jnp.max(s, axis=-1, keepdims=True))
    alpha = jnp.exp2(m - m_new)
    p = jnp.exp2(s - m_new)
    l = alpha * l + jnp.sum(p, axis=-1, keepdims=True)
    acc = alpha * acc + jnp.dot(p.astype(BF16), v, preferred_element_type=F32)
    return m_new, l, acc


def _softmax_init(rows, dv):
    return (jnp.full((rows, 1), NEG, F32), jnp.zeros((rows, 1), F32), jnp.zeros((rows, dv), F32))


def _softmax_step_ones(s, v_ones, carry):
    m, acc = carry
    m_new = jnp.maximum(m, jnp.max(s, axis=-1, keepdims=True))
    p = jnp.exp2((s - m_new).astype(BF16))
    acc = jnp.exp2(m - m_new) * acc + jnp.dot(p, v_ones, preferred_element_type=F32)
    return m_new, acc


def _softmax_init_ones(rows, width):
    return jnp.full((rows, 1), NEG, F32), jnp.zeros((rows, width), F32)


def _key_chunks(c):
    T = TQ
    chunks = [(0, T, (0, 0))] if c == 0 else [((c - 1) * T, 2 * T, (1, 0))]
    n_far = max(c - 1, 0)
    if n_far % 2:
        chunks.append((0, T, None))
    chunks += [(s * T, 2 * T, None) for s in range(n_far % 2, n_far, 2)]
    return chunks


def _nsa_kernel(q_ref, kv_ref, kc_ref, vc_ref, gate_ref, bc_ref, nb_ref, wm_ref, ov_ref, ex_ref, gx_ref,
                o_ref, mb_scr, osel_scr, *, seq_len):
    T, dh, Hg = TQ, HEAD_DIM, NSA_HPG
    R = Hg * T
    qi = pl.program_id(2)
    n_slc = seq_len // SLC_BLOCK

    lower = lax.broadcasted_iota(jnp.int32, (T, 2 * dh), 1) < dh
    q_lo, q_hi = [], []
    for pair in range(Hg // 2):
        pq = q_ref[:, pair * 2 * dh:(pair + 1) * 2 * dh]
        sw = jnp.concatenate([pq[:, dh:], pq[:, :dh]], axis=1)
        zero = jnp.zeros_like(pq)
        q_lo += [jnp.where(lower, pq, zero), jnp.where(lower, sw, zero)]
        q_hi += [jnp.where(lower, zero, sw), jnp.where(lower, zero, pq)]
    q = jnp.concatenate(q_lo, axis=0)
    q_win = jnp.concatenate(q_hi, axis=0)

    lc = _dot_nt(q, kc_ref[...]) + bc_ref[...].reshape(R, LANE)
    pc = jnp.exp2(lc - jnp.max(lc, axis=-1, keepdims=True))
    row = lax.broadcasted_iota(jnp.int32, (Hg, T, 1), 1).reshape(R, 1)
    visible = (qi * T + row >= CMP_BLOCK - 1).astype(F32)
    pc = pc * (visible / jnp.sum(pc, axis=-1, keepdims=True))
    o_cmp = jnp.dot(pc.astype(BF16), vc_ref[...], preferred_element_type=F32)

    n_all = SLC_TOP_N * SLC_BLOCK // T

    @pl.when(qi >= n_all)
    def _():
        p_sum = pc[0:T] + pc[T:2 * T] + pc[2 * T:3 * T] + pc[3 * T:4 * T]
        imp = lax.dot_general(ov_ref[...], p_sum, (((1,), (1,)), ((), ())), preferred_element_type=F32,
                              precision=lax.Precision.HIGHEST)
        blk = lax.broadcasted_iota(jnp.int32, (n_slc, T), 0)
        cur = (qi * T + lax.broadcasted_iota(jnp.int32, (n_slc, T), 1)) // SLC_BLOCK
        forced = (blk == 0) | (blk == cur) | (blk == cur - 1)
        score = jnp.where(blk <= cur, imp + jnp.where(forced, FORCE_SCORE, 0.0), NEG)
        rank = jnp.zeros((n_slc, T), jnp.int32)
        for i in range(n_slc):
            other = score[i:i + 1, :]
            ahead = (other > score) | ((other == score) & (blk > i))
            rank = rank + ahead.astype(jnp.int32)
        sel_bias = jnp.where(rank < SLC_TOP_N, 0.0, NEG).T.astype(BF16)
        mb_scr[...] = jnp.dot(sel_bias, ex_ref[...], preferred_element_type=F32)

    def values(rows, ones_upper):
        v = kv_ref[rows, 2 * dh:4 * dh]
        keep = lax.broadcasted_iota(jnp.int32, v.shape, 1) < dh
        return jnp.where(keep if ones_upper else ~keep, v, jnp.ones_like(v))

    def sel_tile(c):
        carry = _softmax_init_ones(R, 2 * dh)
        for start, width, table in _key_chunks(c):
            rows = slice(start, start + width)
            s = _dot_nt(q, kv_ref[rows, 0:2 * dh]).reshape(Hg, T, width)
            if table is not None:
                s = s + nb_ref[table[0], :, :, table[1]:table[1] + width]
            if c >= n_all:
                s = s + mb_scr[:, rows][None]
            carry = _softmax_step_ones(s.reshape(R, width), values(rows, True), carry)
        osel_scr[...] = carry[1]

    for c in range(seq_len // T):
        pl.when(qi == c)(functools.partial(sel_tile, c))
    acc_s = osel_scr[...]
    o_slc = acc_s[:, :dh] / acc_s[:, dh:]

    def win_step(start, width, bias, carry):
        rows = pl.ds(start, width)
        s = _dot_nt(q_win, kv_ref[rows, 0:2 * dh]).reshape(Hg, T, width) + bias
        return _softmax_step_ones(s.reshape(R, width), values(rows, False), carry)

    carry = win_step(pl.multiple_of(jnp.maximum(qi - 1, 0) * T, T), 2 * T, nb_ref[jnp.minimum(qi, 1)],
                     _softmax_init_ones(R, 2 * dh))
    carry = lax.fori_loop(
        0, (qi >= 2).astype(jnp.int32),
        lambda j, c: win_step(pl.multiple_of((qi - 2) * T, T), T, wm_ref[...][None], c), carry)
    acc_w = carry[1]
    o_win = acc_w[:, dh:] / acc_w[:, :dh]

    gates = jax.nn.sigmoid(gate_ref[...].astype(F32))
    g_hi = gates.astype(BF16)
    g_lo = (gates - g_hi.astype(F32)).astype(BF16)
    g_wide = (jnp.dot(g_hi, gx_ref[...], preferred_element_type=F32)
              + jnp.dot(g_lo, gx_ref[...], preferred_element_type=F32))
    by_head = lambda o: jnp.concatenate([o[h * T:(h + 1) * T] for h in range(Hg)], axis=1)
    w = Hg * dh
    out = (g_wide[:, 0:w] * by_head(o_cmp) + g_wide[:, w:2 * w] * by_head(o_slc)
           + g_wide[:, 2 * w:3 * w] * by_head(o_win))
    o_ref[...] = out.astype(BF16)


def _nsa_attention(proj, k_cmp, v_cmp, bias_cmp, near_bias, win_mask, overlap, expand):
    B, S, _ = proj.shape
    G, Hg, dh, T = NSA_KV_GROUPS, NSA_HPG, HEAD_DIM, TQ
    assert WINDOW == 2 * T and S % (2 * T) == 0
    n_slc = S // SLC_BLOCK
    gate_expand = np.zeros((LANE, 3 * Hg * dh), np.float32)
    for j in range(3 * Hg):
        gate_expand[j, j * dh:(j + 1) * dh] = 1.0
    gate_expand = jnp.asarray(gate_expand, BF16)
    return pl.pallas_call(
        functools.partial(_nsa_kernel, seq_len=S),
        grid=(B, G, S // T),
        in_specs=[
            pl.BlockSpec((None, T, Hg * dh), lambda b, g, i: (b, i, C_QN // (Hg * dh) + g)),
            pl.BlockSpec((None, S, 4 * dh), lambda b, g, i: (b, 0, C_KV // (4 * dh) + g)),
            pl.BlockSpec((None, None, LANE, LANE), lambda b, g, i: (b, g, 0, 0)),
            pl.BlockSpec((None, None, LANE, dh), lambda b, g, i: (b, g, 0, 0)),
            pl.BlockSpec((None, T, LANE), lambda b, g, i: (b, i, C_GATE // LANE + g)),
            pl.BlockSpec((None, Hg, T, LANE), lambda b, g, i: (g, 0, i, 0)),
            pl.BlockSpec((None, 2, Hg, T, 2 * T), lambda b, g, i: (g, 0, 0, 0, 0)),
            pl.BlockSpec((T, T), lambda b, g, i: (0, 0)),
            pl.BlockSpec((n_slc, LANE), lambda b, g, i: (0, 0)),
            pl.BlockSpec((n_slc, S), lambda b, g, i: (0, 0)),
            pl.BlockSpec(gate_expand.shape, lambda b, g, i: (0, 0)),
        ],
        out_specs=pl.BlockSpec((None, T, Hg * dh), lambda b, g, i: (b, i, g)),
        out_shape=jax.ShapeDtypeStruct((B, S, NSA_HEADS * dh), BF16),
        scratch_shapes=[pltpu.VMEM((T, S), F32), pltpu.VMEM((Hg * T, 2 * dh), F32)],
        compiler_params=_cparams(("arbitrary", "arbitrary", "arbitrary")),
        name="nsa_attention",
    )(proj, proj, k_cmp, v_cmp, proj, bias_cmp, near_bias, win_mask, overlap, expand, gate_expand)


def _diff_kernel(q_ref, k_ref, v_ref, lam_ref, og_ref, nb_ref, o_ref, *, lambda_init, n_tiles):
    T, dh = TQ, HEAD_DIM
    qi = pl.program_id(2)
    q = q_ref[...]
    lower = lax.broadcasted_iota(jnp.int32, q.shape, 1) < dh
    q1 = jnp.where(lower, q, jnp.zeros_like(q))
    q2 = jnp.where(lower, jnp.zeros_like(q), q)

    def step(start, width, bias, carry):
        c1, c2 = carry
        v = v_ref[start:start + width, :]
        k = k_ref[start:start + width, :]
        s1 = _dot_nt(q1, k)
        s2 = _dot_nt(q2, k)
        if bias is not None:
            s1, s2 = s1 + bias, s2 + bias
        return _softmax_step(s1, v, c1), _softmax_step(s2, v, c2)

    lam_p = lam_ref[...]
    lam = (jnp.exp(jnp.sum(lam_p[0:1] * lam_p[1:2], axis=-1, keepdims=True))
           - jnp.exp(jnp.sum(lam_p[2:3] * lam_p[3:4], axis=-1, keepdims=True)) + lambda_init)

    def tile(c):
        init = _softmax_init(T, DIFF_V_DIM)
        carry = (init, init)
        for start, width, table in _key_chunks(c):
            bias = None if table is None else nb_ref[table[0], :, table[1]:table[1] + width]
            carry = step(start, width, bias, carry)
        (_, l1, a1), (_, l2, a2) = carry
        o = a1 / l1 - lam * (a2 / l2)
        o_ref[...] = (_rms(o, og_ref[...]) * (1.0 - lambda_init)).astype(BF16)

    for c in range(n_tiles):
        pl.when(qi == c)(functools.partial(tile, c))


def _diff_attention(proj, lam_p, out_gain, near_bias, lambda_init):
    B, S, _ = proj.shape
    T, dh, H = TQ, HEAD_DIM, DIFF_HEADS
    return pl.pallas_call(
        functools.partial(_diff_kernel, lambda_init=lambda_init, n_tiles=S // T),
        grid=(B, H, S // T),
        in_specs=[
            pl.BlockSpec((None, T, 2 * dh), lambda b, h, i: (b, i, C_QD // (2 * dh) + h)),
            pl.BlockSpec((None, S, 2 * dh), lambda b, h, i: (b, 0, C_KD // (2 * dh) + h)),
            pl.BlockSpec((None, S, DIFF_V_DIM), lambda b, h, i: (b, 0, C_VD // DIFF_V_DIM + h)),
            pl.BlockSpec((4, dh), lambda b, h, i: (0, 0)),
            pl.BlockSpec((1, DIFF_V_DIM), lambda b, h, i: (0, 0)),
            pl.BlockSpec((None, 2, T, 2 * T), lambda b, h, i: (h, 0, 0, 0)),
        ],
        out_specs=pl.BlockSpec((None, T, DIFF_V_DIM), lambda b, h, i: (b, i, h)),
        out_shape=jax.ShapeDtypeStruct((B, S, H * DIFF_V_DIM), BF16),
        compiler_params=_cparams(("arbitrary", "arbitrary", "arbitrary")),
        name="diff_attention",
    )(proj, proj, proj, lam_p, out_gain, near_bias)


def _merge_kernel(x_ref, on_ref, od_ref, gm_ref, wn_ref, wd_ref, wo_ref, g2_ref, rw_ref, rb_ref, o_ref):
    tm, d = x_ref.shape
    a = jnp.dot(on_ref[...], wn_ref[...], preferred_element_type=F32)
    b = jnp.dot(od_ref[...], wd_ref[...], preferred_element_type=F32)
    sigmoid = lambda z: 0.5 * jnp.tanh(0.5 * z) + 0.5
    y = sigmoid(gm_ref[:, :d].astype(F32)) * a + sigmoid(gm_ref[:, d:].astype(F32)) * b
    x2 = x_ref[...] + jnp.dot(y.astype(BF16), wo_ref[...], preferred_element_type=F32)
    o_ref[:, :d] = x2

    h2 = _rms(x2, g2_ref[...])
    h_hi = h2.astype(BF16)
    h_lo = (h2 - h_hi.astype(F32)).astype(BF16)
    part = jnp.dot(jnp.concatenate([h_hi, h_lo], axis=0), rw_ref[...], preferred_element_type=F32)
    logits = part[:tm, :LANE] + part[:tm, LANE:] + part[tm:, :LANE] + part[tm:, LANE:] + rb_ref[...]
    lane = lax.broadcasted_iota(jnp.int32, logits.shape, 1)
    big = jnp.int32(LANE)

    def first_argmax(vals, mask):
        vals = jnp.where(mask, vals, -jnp.inf)
        top = jnp.max(vals, axis=-1, keepdims=True)
        idx = jnp.min(jnp.where(vals == top, lane, big), axis=-1, keepdims=True)
        return top, idx

    is_group = lane < N_GROUPS
    g_max, g_top = first_argmax(logits, is_group)
    g_top_p = 1.0 / jnp.sum(jnp.where(is_group, jnp.exp(logits - g_max), 0.0), axis=-1, keepdims=True)
    lo = N_GROUPS + g_top * EXPERTS_PER_GROUP
    in_group = (lane >= lo) & (lane < lo + EXPERTS_PER_GROUP)
    m1, i1 = first_argmax(logits, in_group)
    m2, i2 = first_argmax(logits, in_group & (lane != i1))
    e2 = jnp.exp(m2 - m1)
    w1 = g_top_p / (1.0 + e2)
    w2 = g_top_p * e2 / (1.0 + e2)
    la = jnp.minimum(i1, i2) - lo
    lb = jnp.maximum(i1, i2) - lo
    pair = jnp.right_shift(la * (7 - la), 1) + (lb - la - 1)
    bucket = (g_top * N_PAIRS + pair).astype(F32)
    exp_lane = lane + N_GROUPS
    route = (jnp.where(exp_lane == i1, w1, 0.0) + jnp.where(exp_lane == i2, w2, 0.0)
             + jnp.where(lane == N_EXPERTS, bucket, 0.0))
    o_ref[:, d:] = route


def _merge(x2d, o_nsa, o_diff, proj2d, wn, wd, wo, g2, rw, rb):
    n, d = x2d.shape
    tm = 512
    rw_hi = rw.astype(BF16)
    rw_cat = jnp.concatenate([rw_hi, (rw - rw_hi.astype(F32)).astype(BF16)], axis=1)
    full = lambda a: pl.BlockSpec(a.shape, lambda i: (0,) * a.ndim)
    return pl.pallas_call(
        _merge_kernel,
        grid=(n // tm,),
        in_specs=[pl.BlockSpec((tm, d), lambda i: (i, 0)),
                  pl.BlockSpec((tm, o_nsa.shape[1]), lambda i: (i, 0)),
                  pl.BlockSpec((tm, o_diff.shape[1]), lambda i: (i, 0)),
                  pl.BlockSpec((tm, 2 * d), lambda i: (i, C_GM // (2 * d))),
                  full(wn), full(wd), full(wo), full(g2), full(rw_cat), full(rb)],
        out_specs=pl.BlockSpec((tm, d + LANE), lambda i: (i, 0)),
        out_shape=jax.ShapeDtypeStruct((n, d + LANE), F32),
        compiler_params=_cparams(("arbitrary",)),
        name="merge_router",
    )(x2d, o_nsa, o_diff, proj2d, wn, wd, wo, g2, rw_cat, rb)


def _moe_kernel(tile_ref, ea_ref, eb_ref, lo_ref, hi_ref, first_ref, perm_ref,
                x_hbm, g2_ref, wga_ref, wua_ref, wda_ref, wgb_ref, wub_ref, wdb_ref,
                o_hbm, xb0, xb1, ac0, ac1, h_scr, g_sem, s_sem, *, tm, n_tiles):
    d = o_hbm.shape[1]
    v = pl.program_id(0)
    t = tile_ref[v]
    x_bufs, acc_bufs = (xb0, xb1), (ac0, ac1)
    last_slot = (n_tiles - 1) % 2

    def gather_copy(tt, sl, r):
        return pltpu.make_async_copy(x_hbm.at[pl.ds(perm_ref[tt * tm + r], 1), :],
                                     x_bufs[sl].at[pl.ds(r, 1), :], g_sem.at[sl])

    def scatter_copy(tt, sl, r):
        return pltpu.make_async_copy(acc_bufs[sl].at[pl.ds(r, 1), :],
                                     o_hbm.at[pl.ds(perm_ref[tt * tm + r], 1), :], s_sem.at[sl])

    def for_rows(fn):
        for r in range(tm):
            fn(r)

    def experts(sl, dma_starts):
        n_stage = 6
        per_stage = -(-len(dma_starts) // n_stage)

        def issue(stage):
            for start in dma_starts[stage * per_stage:(stage + 1) * per_stage]:
                start()

        h = h_scr[...]
        cw = x_bufs[sl][:, d:]
        lane = lax.broadcasted_iota(jnp.int32, cw.shape, 1)
        row = lax.broadcasted_iota(jnp.int32, (tm, 1), 0)
        in_bucket = (row >= lo_ref[v]) & (row < hi_ref[v])
        out = None
        for i, (e, wg_ref, wu_ref, wd_ref) in enumerate(((ea_ref[v], wga_ref, wua_ref, wda_ref),
                                                         (eb_ref[v], wgb_ref, wub_ref, wdb_ref))):
            w = jnp.sum(jnp.where(lane == e, cw, 0.0), axis=-1, keepdims=True)
            w = jnp.where(in_bucket, w, 0.0)
            gate = jnp.dot(h, wg_ref[...], preferred_element_type=F32)
            issue(3 * i)
            up = jnp.dot(h, wu_ref[...], preferred_element_type=F32)
            issue(3 * i + 1)
            hid = jax.nn.silu(gate) * up * w
            y = jnp.dot(hid.astype(BF16), wd_ref[...], preferred_element_type=F32)
            issue(3 * i + 2)
            out = y if out is None else out + y
        acc_bufs[sl][...] += out

    def first_visit(sl, scatter_previous):
        for_rows(lambda r: gather_copy(t, sl, r).wait())

        @pl.when(t >= 2)
        def _():
            for_rows(lambda r: scatter_copy(t - 2, sl, r).wait())

        xs = x_bufs[sl][:, :d]
        h_scr[...] = _rms(xs, g2_ref[...]).astype(BF16)
        acc_bufs[sl][...] = xs
        nxt = jnp.minimum(t + 1, n_tiles - 1)
        starts = [functools.partial(lambda r: gather_copy(nxt, 1 - sl, r).start(), r) for r in range(tm)]
        if scatter_previous:
            starts += [functools.partial(lambda r: scatter_copy(t - 1, 1 - sl, r).start(), r)
                       for r in range(tm)]
        experts(sl, starts)

    @pl.when(v == 0)
    def _():
        for_rows(lambda r: gather_copy(0, 0, r).start())
        first_visit(0, scatter_previous=False)

    for sl in range(2):
        pl.when((v > 0) & (first_ref[v] == 1) & (t % 2 == sl))(
            functools.partial(first_visit, sl, scatter_previous=True))
        pl.when((first_ref[v] == 0) & (t % 2 == sl))(functools.partial(experts, sl, []))

    @pl.when(v == pl.num_programs(0) - 1)
    def _():
        for_rows(lambda r: scatter_copy(n_tiles - 1, last_slot, r).start())
        for_rows(lambda r: gather_copy(n_tiles - 1, 1 - last_slot, r).wait())
        for_rows(lambda r: scatter_copy(n_tiles - 2, 1 - last_slot, r).wait())
        for_rows(lambda r: scatter_copy(n_tiles - 1, last_slot, r).wait())


def _moe(sched, perm, x2r, g2, wg, wu, wd, tm):
    n = x2r.shape[0]
    d, f = wg.shape[1], wg.shape[2]
    n_visits = sched[0].shape[0]
    n_tiles = n // tm
    assert n_tiles >= 2
    w_in_a = pl.BlockSpec((None, d, f), lambda v, t, ea, eb, *_: (ea[v], 0, 0))
    w_in_b = pl.BlockSpec((None, d, f), lambda v, t, ea, eb, *_: (eb[v], 0, 0))
    w_out_a = pl.BlockSpec((None, f, d), lambda v, t, ea, eb, *_: (ea[v], 0, 0))
    w_out_b = pl.BlockSpec((None, f, d), lambda v, t, ea, eb, *_: (eb[v], 0, 0))
    return pl.pallas_call(
        functools.partial(_moe_kernel, tm=tm, n_tiles=n_tiles),
        grid_spec=pltpu.PrefetchScalarGridSpec(
            num_scalar_prefetch=7,
            grid=(n_visits,),
            in_specs=[pl.BlockSpec(memory_space=pl.ANY),
                      pl.BlockSpec((1, d), lambda v, *_: (0, 0)),
                      w_in_a, w_in_a, w_out_a, w_in_b, w_in_b, w_out_b],
            out_specs=pl.BlockSpec(memory_space=pl.ANY),
            scratch_shapes=[pltpu.VMEM((tm, x2r.shape[1]), F32), pltpu.VMEM((tm, x2r.shape[1]), F32),
                            pltpu.VMEM((tm, d), F32), pltpu.VMEM((tm, d), F32),
                            pltpu.VMEM((tm, d), BF16),
                            pltpu.SemaphoreType.DMA((2,)), pltpu.SemaphoreType.DMA((2,))]),
        out_shape=jax.ShapeDtypeStruct((n, d), F32),
        compiler_params=_cparams(("arbitrary",)),
        name="moe_experts",
    )(*sched, perm, x2r, g2, wg, wu, wd, wg, wu, wd)


def _moe_schedule(bucket, tm):
    n = bucket.shape[0]
    n_tiles = n // tm
    n_visits = n_tiles + N_BUCKETS - 1
    counts = jnp.sum((bucket[:, None] == jnp.arange(N_BUCKETS)[None, :]).astype(jnp.int32), axis=0)
    ends = jnp.cumsum(counts)
    starts = ends - counts
    first_tile = starts // tm
    last_tile = jnp.maximum(ends - 1, 0) // tm
    n_vis = jnp.where(counts > 0, last_tile - first_tile + 1, 0)
    vis_end = jnp.cumsum(n_vis)
    vis_start = vis_end - n_vis
    v = jnp.arange(n_visits, dtype=jnp.int32)
    b = jnp.minimum(jnp.searchsorted(vis_end, v, side="right"), N_BUCKETS - 1).astype(jnp.int32)
    live = v < vis_end[-1]
    tile = jnp.where(live, first_tile[b] + v - vis_start[b], n_tiles - 1).astype(jnp.int32)
    lo = jnp.where(live, jnp.clip(starts[b] - tile * tm, 0, tm), 0).astype(jnp.int32)
    hi = jnp.where(live, jnp.clip(ends[b] - tile * tm, 0, tm), 0).astype(jnp.int32)
    b_last = b[jnp.maximum(vis_end[-1] - 1, 0)]
    b = jnp.where(live, b, b_last)
    pair_a = np.array([0, 0, 0, 1, 1, 2], np.int32)
    pair_b = np.array([1, 2, 3, 2, 3, 3], np.int32)
    grp = b // N_PAIRS
    ea = (grp * EXPERTS_PER_GROUP + jnp.asarray(pair_a)[b % N_PAIRS]).astype(jnp.int32)
    eb = (grp * EXPERTS_PER_GROUP + jnp.asarray(pair_b)[b % N_PAIRS]).astype(jnp.int32)
    first = jnp.concatenate([jnp.ones((1,), jnp.int32), (tile[1:] != tile[:-1]).astype(jnp.int32)])
    return tile, ea, eb, lo, hi, first


def _layer(x, layer, p):
    B, S, D = x.shape
    G, dh = NSA_KV_GROUPS, HEAD_DIM
    lambda_init = 0.8 - 0.6 * math.exp(-0.3 * layer)
    x2d = x.reshape(B * S, D)

    proj2d = _in_proj(x2d, p["norm1_g"][None, :], _reorder_w_in(p["w_in"]), _head_norm_rows(p))
    proj = proj2d.reshape(B, S, C_TOTAL)

    k_cmp, v_cmp = _compress(proj, *_compress_params(p["cmp_pos"], p["cmp_w1"], p["cmp_w2"]),
                             p["nsa_k_norm"][0:1])

    near, bias_cmp = _bias_tables(p["rel_bias"], S)
    nsa_near = near[:NSA_HEADS].reshape(G, NSA_HPG, 2, TQ, 2 * TQ).transpose(0, 2, 1, 3, 4)
    i = np.arange(TQ)
    win_mask = jnp.asarray(np.where(i[None, :] > i[:, None], 0.0, NEG).astype(np.float32))
    expand = jnp.asarray(np.repeat(np.eye(S // SLC_BLOCK, dtype=np.float32), SLC_BLOCK, axis=1), BF16)
    o_nsa = _nsa_attention(proj, k_cmp, v_cmp, bias_cmp.reshape(G, NSA_HPG, S, LANE), nsa_near, win_mask,
                           jnp.asarray(_overlap_matrix(S)), expand)
    o_diff = _diff_attention(proj, p["diff_lambda"], p["diff_out_norm"][None, :], near[NSA_HEADS:],
                             lambda_init)

    rw = jnp.concatenate([p["router_group_w"], p["router_expert_w"],
                          jnp.zeros((D, LANE - N_GROUPS - N_EXPERTS), F32)], axis=1)
    rb = jnp.concatenate([p["router_group_b"], p["router_expert_b"],
                          jnp.zeros((LANE - N_GROUPS - N_EXPERTS,), F32)])[None, :]
    x2r = _merge(x2d, o_nsa.reshape(B * S, -1), o_diff.reshape(B * S, -1), proj2d,
                 p["w_branch_nsa"].astype(BF16), p["w_branch_diff"].astype(BF16),
                 p["w_out"].astype(BF16), p["norm2_g"][None, :], rw, rb)

    tm = 256
    bucket = x2r[:, D + N_EXPERTS].astype(jnp.int32)
    perm = jnp.argsort(bucket).astype(jnp.int32)
    out = _moe(_moe_schedule(bucket, tm), perm, x2r, p["norm2_g"][None, :],
               p["expert_w_gate"].astype(BF16), p["expert_w_up"].astype(BF16),
               p["expert_w_down"].astype(BF16), tm)
    return out.reshape(B, S, D)


def kernel(x, norm1_g, w_in, nsa_q_norm, nsa_k_norm, cmp_pos, cmp_w1, cmp_w2, diff_q_norm, diff_k_norm,
           diff_lambda, diff_out_norm, rel_bias, w_branch_nsa, w_branch_diff, w_out, norm2_g,
           router_group_w, router_group_b, router_expert_w, router_expert_b, expert_w_gate,
           expert_w_up, expert_w_down):
    per_layer = dict(norm1_g=norm1_g, w_in=w_in, nsa_q_norm=nsa_q_norm, nsa_k_norm=nsa_k_norm,
                     cmp_pos=cmp_pos, cmp_w1=cmp_w1, cmp_w2=cmp_w2, diff_q_norm=diff_q_norm,
                     diff_k_norm=diff_k_norm, diff_lambda=diff_lambda, diff_out_norm=diff_out_norm,
                     w_branch_nsa=w_branch_nsa, w_branch_diff=w_branch_diff, w_out=w_out,
                     norm2_g=norm2_g, router_group_w=router_group_w, router_group_b=router_group_b,
                     router_expert_w=router_expert_w, router_expert_b=router_expert_b,
                     expert_w_gate=expert_w_gate, expert_w_up=expert_w_up, expert_w_down=expert_w_down)
    for layer in range(norm1_g.shape[0]):
        p = {k: v[layer] for k, v in per_layer.items()}
        p["rel_bias"] = rel_bias
        x = _layer(x, layer, p)
    return x
```

```python
import functools
import math

import numpy as np
import jax
import jax.numpy as jnp
from jax import lax
from jax.experimental import pallas as pl
from jax.experimental.pallas import tpu as pltpu

F32 = jnp.float32
BF16 = jnp.bfloat16

HEAD_DIM = 64
SCALE = HEAD_DIM ** -0.5
NSA_HEADS = 8
NSA_KV_GROUPS = 2
NSA_HPG = NSA_HEADS // NSA_KV_GROUPS
CMP_BLOCK = 32
CMP_STRIDE = 16
CMP_HIDDEN = 256
SLC_BLOCK = 64
SLC_TOP_N = 16
WINDOW = 512
FORCE_SCORE = 1e4
DIFF_HEADS = 4
DIFF_V_DIM = 2 * HEAD_DIM
REL_BUCKETS = 32
REL_MAX_DIST = 128
N_GROUPS = 4
EXPERTS_PER_GROUP = 4
N_EXPERTS = N_GROUPS * EXPERTS_PER_GROUP
EXPERT_HIDDEN = 512
RMS_EPS = 1e-6
NEG = -1e30
LOG2E = math.log2(math.e)

LANE = 128
MXU_DIM = 256
TQ = 256
N_PAIRS = 6
N_BUCKETS = N_GROUPS * N_PAIRS

C_GM = 0
C_QN = 2048
C_KV = 2560
C_CMP = 3072
C_GATE = 3328
C_QD = 3584
C_KD = 4096
C_VD = 4608
C_TOTAL = 5120

VMEM_LIMIT = 56 * 1024 * 1024


def _cparams(sem):
    return pltpu.CompilerParams(dimension_semantics=sem, vmem_limit_bytes=VMEM_LIMIT)


def _rms(x, gain):
    return x * lax.rsqrt(jnp.mean(x * x, axis=-1, keepdims=True) + RMS_EPS) * gain


def _reorder_w_in(w_in):
    d_model = w_in.shape[0]
    dh, G, Hg = HEAD_DIM, NSA_KV_GROUPS, NSA_HPG
    o_q = 0
    o_kv = o_q + NSA_HEADS * dh
    o_g = o_kv + 3 * 2 * G * dh
    o_qd = o_g + 3 * NSA_HEADS
    o_gm = o_qd + 3 * DIFF_HEADS * 2 * dh

    def cols(start, n):
        return w_in[:, start:start + n]

    def kv(kind, which, g):
        return cols(o_kv + ((kind * 2 + which) * G + g) * dh, dh)

    parts = [cols(o_gm, 2 * d_model), cols(o_q, NSA_HEADS * dh)]
    parts += [kv(kind, which, g) for g in range(G) for which in (0, 1) for kind in (1, 2)]
    parts += [kv(0, which, g) for g in range(G) for which in (0, 1)]
    for g in range(G):
        parts += [cols(o_g + (kind * G + g) * Hg, Hg) for kind in range(3)]
        parts.append(jnp.zeros((d_model, LANE - 3 * Hg), w_in.dtype))
    parts.append(cols(o_qd, 3 * DIFF_HEADS * 2 * dh))
    w = jnp.concatenate(parts, axis=1)
    assert w.shape[1] == C_TOTAL
    return w.astype(BF16)


def _head_norm_rows(p):
    dh, G = HEAD_DIM, NSA_KV_GROUPS
    qs = SCALE * LOG2E
    zeros = lambda n: jnp.zeros((n,), F32)
    kn = p["nsa_k_norm"]
    parts = [zeros(C_QN - C_GM), jnp.tile(p["nsa_q_norm"] * qs, NSA_HEADS)]
    parts += [jnp.concatenate([kn[1], kn[2], zeros(2 * dh)])] * G
    parts += [zeros(C_QD - C_CMP), jnp.tile((p["diff_q_norm"] * qs).reshape(-1), DIFF_HEADS),
              jnp.tile(p["diff_k_norm"].reshape(-1), DIFF_HEADS), zeros(C_TOTAL - C_VD)]
    gain = jnp.concatenate(parts)
    assert gain.shape[0] == C_TOTAL
    return gain[None, :]


def _t5_bucket(dist):
    n = jnp.maximum(dist, 0)
    max_exact = REL_BUCKETS // 2
    nf = jnp.maximum(n, 1).astype(F32)
    large = max_exact + (jnp.log(nf / max_exact) / math.log(REL_MAX_DIST / max_exact)
                         * (REL_BUCKETS - max_exact)).astype(jnp.int32)
    large = jnp.minimum(large, REL_BUCKETS - 1)
    return jnp.where(n < max_exact, n, large)


def _toeplitz(v_ext, rows, cols):
    length = v_ext.shape[-1]
    lead = v_ext.shape[:-1]
    flat = jnp.broadcast_to(v_ext[..., None, :], lead + (rows, length)).reshape(lead + (rows * length,))
    return flat[..., :rows * (length - 1)].reshape(lead + (rows, length - 1))[..., :cols]


def _bias_tables(rel_bias, S):
    heads = rel_bias.shape[1]
    far = REL_MAX_DIST
    assert TQ == 2 * far and LANE == far
    bd = rel_bias[_t5_bucket(jnp.arange(2 * far))].T.astype(F32) * LOG2E
    rel = bd - bd[:, far:far + 1]
    neg = lambda *shape: jnp.full((heads,) + shape, NEG, F32)
    zero = lambda *shape: jnp.zeros((heads,) + shape, F32)
    rev = lambda a: a[:, ::-1]
    d0 = _toeplitz(jnp.concatenate([rel[:, 0:1], neg(far), rev(rel[:, 1:far])], axis=1), far, far)
    d1 = _toeplitz(jnp.concatenate([rev(rel[:, 1:far + 1]), zero(far)], axis=1), far, far)
    z, n = zero(far, far), neg(far, far)
    blocks = lambda rows: jnp.concatenate([jnp.concatenate(r, axis=2) for r in rows], axis=1)
    near = jnp.stack([blocks([[d0, n, n, n], [d1, d0, n, n]]),
                      blocks([[z, d1, d0, n], [z, z, d1, d0]])], axis=1)
    hb = rel[:NSA_HEADS]
    zero8 = lambda *shape: jnp.zeros((NSA_HEADS,) + shape, F32)
    neg8 = lambda *shape: jnp.full((NSA_HEADS,) + shape, NEG, F32)
    width = TQ + far - 1
    tp = _toeplitz(jnp.concatenate([rev(hb[:, 0:far]), neg8(TQ + 1), zero8(TQ - 1)], axis=1), TQ, width)
    n_tiles = S // TQ
    per_tile = TQ // CMP_STRIDE
    col0 = per_tile * (n_tiles - 1)
    shift = CMP_STRIDE * col0 - (CMP_BLOCK - 1)
    first = -(-(shift - (far - 1)) // CMP_STRIDE)
    last = (shift + TQ - 1) // CMP_STRIDE
    band = tp[:, :, CMP_STRIDE * first - shift + far - 1::CMP_STRIDE][:, :, :last - first + 1]
    g2 = jnp.concatenate([zero8(TQ, first), band, neg8(TQ, 2 * TQ - last - 1)], axis=2)
    bias_cmp = jnp.stack([g2[:, :, col0 - per_tile * i:col0 - per_tile * i + LANE] for i in range(n_tiles)],
                         axis=1)
    n_cmp = (S - CMP_BLOCK) // CMP_STRIDE + 1
    bias_cmp = jnp.where(np.arange(LANE) < n_cmp, bias_cmp, NEG)
    return near, bias_cmp.reshape(NSA_HEADS, S, LANE)


def _overlap_matrix(S):
    n_cmp = (S - CMP_BLOCK) // CMP_STRIDE + 1
    n_slc = S // SLC_BLOCK
    c_start = np.arange(n_cmp) * CMP_STRIDE
    s_start = np.arange(n_slc) * SLC_BLOCK
    lo = np.maximum(s_start[:, None], c_start[None, :])
    hi = np.minimum(s_start[:, None] + SLC_BLOCK, c_start[None, :] + CMP_BLOCK)
    overlap = (np.maximum(hi - lo, 0) / CMP_BLOCK).astype(np.float32)
    out = np.zeros((n_slc, LANE), np.float32)
    out[:, :n_cmp] = overlap
    return out


IN_CHUNK = 512
HEAD_NORM_CHUNKS = {C_QN // IN_CHUNK: "all", C_KV // IN_CHUNK: "keys",
                    C_QD // IN_CHUNK: "all", C_KD // IN_CHUNK: "all"}


def _in_proj_kernel(x_ref, g_ref, w_ref, hg_ref, seg_ref, o_ref):
    dh = HEAD_DIM
    h = _rms(x_ref[...], g_ref[...]).astype(BF16)
    for c in range(C_TOTAL // IN_CHUNK):
        cols = slice(c * IN_CHUNK, (c + 1) * IN_CHUNK)
        y = jnp.dot(h, w_ref[:, cols], preferred_element_type=F32)
        mode = HEAD_NORM_CHUNKS.get(c)
        if mode is not None:
            sq = (y * y).astype(BF16)
            seg_w = seg_ref.shape[0]
            ss = jnp.concatenate([jnp.dot(sq[:, j:j + seg_w], seg_ref[...], preferred_element_type=F32)
                                  for j in range(0, IN_CHUNK, seg_w)], axis=1)
            scale = lax.rsqrt(ss * (1.0 / dh) + RMS_EPS) * hg_ref[:, cols]
            if mode == "keys":
                lane = lax.broadcasted_iota(jnp.int32, (1, IN_CHUNK), 1)
                scale = jnp.where(lane % (4 * dh) < 2 * dh, scale, 1.0)
            y = y * scale
        o_ref[:, cols] = y.astype(BF16)


def _in_proj(x2d, gain, w, head_gain):
    n, d = x2d.shape
    tm = 256
    seg = np.arange(MXU_DIM) // HEAD_DIM
    seg_ones = jnp.asarray((seg[:, None] == seg[None, :]).astype(np.float32), BF16)
    return pl.pallas_call(
        _in_proj_kernel,
        grid=(n // tm,),
        in_specs=[pl.BlockSpec((tm, d), lambda i: (i, 0)),
                  pl.BlockSpec((1, d), lambda i: (0, 0)),
                  pl.BlockSpec((d, C_TOTAL), lambda i: (0, 0)),
                  pl.BlockSpec((1, C_TOTAL), lambda i: (0, 0)),
                  pl.BlockSpec((MXU_DIM, MXU_DIM), lambda i: (0, 0))],
        out_specs=pl.BlockSpec((tm, C_TOTAL), lambda i: (i, 0)),
        out_shape=jax.ShapeDtypeStruct((n, C_TOTAL), BF16),
        compiler_params=_cparams(("arbitrary",)),
        name="in_proj",
    )(x2d, gain, w, head_gain, seg_ones)


def _compress_kernel(t_ref, pos_ref, w1_ref, w2_ref, kg_ref, k_ref, v_ref, tok_scr):
    S, dh = t_ref.shape[0], HEAD_DIM
    m = S // CMP_STRIDE
    tok_scr[0:S, :] = t_ref[...].astype(F32)
    tok_scr[S:, :] = jnp.zeros((CMP_STRIDE, LANE), F32)
    pre = jnp.zeros((m, 2 * CMP_HIDDEN), F32)
    for l in range(CMP_BLOCK):
        x = tok_scr[pl.ds(l, m, stride=CMP_STRIDE), :] + pos_ref[l:l + 1, :]
        pre = pre + jnp.dot(x.astype(BF16), w1_ref[l], preferred_element_type=F32)
    out = jnp.dot(jax.nn.gelu(pre).astype(BF16), w2_ref[...], preferred_element_type=F32)
    k_ref[...] = jnp.concatenate([_rms(out[:, :dh], kg_ref[...]), jnp.zeros((m, LANE - dh), F32)],
                                 axis=1).astype(BF16)
    v_ref[...] = out[:, dh:].astype(BF16)


def _compress(proj, pos, w1, w2, k_gain):
    B, S, _ = proj.shape
    G, m = NSA_KV_GROUPS, S // CMP_STRIDE
    outs = [jax.ShapeDtypeStruct((B, G, m, width), BF16) for width in (LANE, HEAD_DIM)]
    ospecs = [pl.BlockSpec((None, None, m, width), lambda b, g: (b, g, 0, 0)) for width in (LANE, HEAD_DIM)]
    return pl.pallas_call(
        _compress_kernel,
        grid=(B, G),
        in_specs=[pl.BlockSpec((None, S, LANE), lambda b, g: (b, 0, C_CMP // LANE + g)),
                  pl.BlockSpec(pos.shape, lambda b, g: (0, 0)),
                  pl.BlockSpec(w1.shape, lambda b, g: (0, 0, 0)),
                  pl.BlockSpec(w2.shape, lambda b, g: (0, 0)),
                  pl.BlockSpec((1, HEAD_DIM), lambda b, g: (0, 0))],
        out_specs=ospecs,
        out_shape=outs,
        scratch_shapes=[pltpu.VMEM((S + CMP_STRIDE, LANE), F32)],
        compiler_params=_cparams(("arbitrary", "arbitrary")),
        name="nsa_compress",
    )(proj, pos, w1, w2, k_gain)


def _compress_params(cmp_pos, cmp_w1, cmp_w2):
    dh, hid = HEAD_DIM, CMP_HIDDEN
    w1 = cmp_w1.reshape(2, CMP_BLOCK, dh, hid)
    z1 = jnp.zeros((CMP_BLOCK, dh, hid), cmp_w1.dtype)
    w1 = jnp.concatenate([jnp.concatenate([w1[0], z1], axis=2),
                          jnp.concatenate([z1, w1[1]], axis=2)], axis=1)
    z2 = jnp.zeros((hid, dh), cmp_w2.dtype)
    w2 = jnp.concatenate([jnp.concatenate([cmp_w2[0], z2], axis=1),
                          jnp.concatenate([z2, cmp_w2[1]], axis=1)], axis=0)
    pos = jnp.concatenate([cmp_pos[0], cmp_pos[1]], axis=1)
    return pos, w1.astype(BF16), w2.astype(BF16)


def _dot_nt(a, b):
    return lax.dot_general(a, b, (((1,), (1,)), ((), ())), preferred_element_type=F32)


def _softmax_step(s, v, carry):
    m, l, acc = carry
    m_new = jnp.maximum(m, jnp.max(s, axis=-1, keepdims=True))
    alpha = jnp.exp2(m - m_new)
    p = jnp.exp2(s - m_new)
    l = alpha * l + jnp.sum(p, axis=-1, keepdims=True)
    acc = alpha * acc + jnp.dot(p.astype(BF16), v, preferred_element_type=F32)
    return m_new, l, acc


def _softmax_init(rows, dv):
    return (jnp.full((rows, 1), NEG, F32), jnp.zeros((rows, 1), F32), jnp.zeros((rows, dv), F32))


def _softmax_step_ones(s, v_ones, carry):
    m, acc = carry
    m_new = jnp.maximum(m, jnp.max(s, axis=-1, keepdims=True))
    p = jnp.exp2((s - m_new).astype(BF16))
    acc = jnp.exp2(m - m_new) * acc + jnp.dot(p, v_ones, preferred_element_type=F32)
    return m_new, acc


def _softmax_init_ones(rows, width):
    return jnp.full((rows, 1), NEG, F32), jnp.zeros((rows, width), F32)


def _key_chunks(c):
    T = TQ
    chunks = [(0, T, (0, 0))] if c == 0 else [((c - 1) * T, 2 * T, (1, 0))]
    n_far = max(c - 1, 0)
    if n_far % 2:
        chunks.append((0, T, None))
    chunks += [(s * T, 2 * T, None) for s in range(n_far % 2, n_far, 2)]
    return chunks


def _nsa_kernel(q_ref, kv_ref, kc_ref, vc_ref, gate_ref, bc_ref, nb_ref, wm_ref, ov_ref, ex_ref, gx_ref,
                o_ref, mb_scr, osel_scr, owin_scr, *, seq_len):
    T, dh, Hg = TQ, HEAD_DIM, NSA_HPG
    R = Hg * T
    qi = pl.program_id(2)
    n_slc = seq_len // SLC_BLOCK

    lower = lax.broadcasted_iota(jnp.int32, (T, 2 * dh), 1) < dh
    q_lo, q_hi = [], []
    for pair in range(Hg // 2):
        pq = q_ref[:, pair * 2 * dh:(pair + 1) * 2 * dh]
        sw = jnp.concatenate([pq[:, dh:], pq[:, :dh]], axis=1)
        zero = jnp.zeros_like(pq)
        q_lo += [jnp.where(lower, pq, zero), jnp.where(lower, sw, zero)]
        q_hi += [jnp.where(lower, zero, sw), jnp.where(lower, zero, pq)]
    q = jnp.concatenate(q_lo, axis=0)
    q_win = jnp.concatenate(q_hi, axis=0)

    lc = _dot_nt(q, kc_ref[...]) + bc_ref[...].reshape(R, LANE)
    pc = jnp.exp2(lc - jnp.max(lc, axis=-1, keepdims=True))
    row = lax.broadcasted_iota(jnp.int32, (Hg, T, 1), 1).reshape(R, 1)
    visible = (qi * T + row >= CMP_BLOCK - 1).astype(F32)
    pc = pc * (visible / jnp.sum(pc, axis=-1, keepdims=True))
    o_cmp = jnp.dot(pc.astype(BF16), vc_ref[...], preferred_element_type=F32)

    n_all = SLC_TOP_N * SLC_BLOCK // T

    @pl.when(qi >= n_all)
    def _():
        p_sum = pc[0:T] + pc[T:2 * T] + pc[2 * T:3 * T] + pc[3 * T:4 * T]
        imp = lax.dot_general(ov_ref[...], p_sum, (((1,), (1,)), ((), ())), preferred_element_type=F32,
                              precision=lax.Precision.HIGHEST)
        blk = lax.broadcasted_iota(jnp.int32, (n_slc, T), 0)
        cur = (qi * T + lax.broadcasted_iota(jnp.int32, (n_slc, T), 1)) // SLC_BLOCK
        forced = (blk == 0) | (blk == cur) | (blk == cur - 1)
        score = jnp.where(blk <= cur, imp + jnp.where(forced, FORCE_SCORE, 0.0), NEG)
        rank = jnp.zeros((n_slc, T), jnp.int32)
        for i in range(n_slc):
            other = score[i:i + 1, :]
            ahead = (other > score) | ((other == score) & (blk > i))
            rank = rank + ahead.astype(jnp.int32)
        sel_bias = jnp.where(rank < SLC_TOP_N, 0.0, NEG).T.astype(BF16)
        mb_scr[...] = jnp.dot(sel_bias, ex_ref[...], preferred_element_type=F32)

    def values(rows, ones_upper):
        v = kv_ref[rows, 2 * dh:4 * dh]
        keep = lax.broadcasted_iota(jnp.int32, v.shape, 1) < dh
        return jnp.where(keep if ones_upper else ~keep, v, jnp.ones_like(v))

    def tile(c):
        chunks = _key_chunks(c)
        carry = _softmax_init_ones(R, 2 * dh)
        for start, width, table in chunks:
            rows = slice(start, start + width)
            s = _dot_nt(q, kv_ref[rows, 0:2 * dh]).reshape(Hg, T, width)
            if table is not None:
                s = s + nb_ref[table[0], :, :, table[1]:table[1] + width]
            if c >= n_all:
                s = s + mb_scr[:, rows][None]
            carry = _softmax_step_ones(s.reshape(R, width), values(rows, True), carry)
        osel_scr[...] = carry[1]
        start, width, table = chunks[0]
        win_chunks = [(start, width, nb_ref[table[0], :, :, table[1]:table[1] + width])]
        if c >= 2:
            win_chunks.append(((c - 2) * T, T, wm_ref[...][None]))
        carry = _softmax_init_ones(R, 2 * dh)
        for start, width, bias in win_chunks:
            rows = slice(start, start + width)
            s = _dot_nt(q_win, kv_ref[rows, 0:2 * dh]).reshape(Hg, T, width) + bias
            carry = _softmax_step_ones(s.reshape(R, width), values(rows, False), carry)
        owin_scr[...] = carry[1]

    for c in range(seq_len // T):
        pl.when(qi == c)(functools.partial(tile, c))
    acc_s = osel_scr[...]
    o_slc = acc_s[:, :dh] / acc_s[:, dh:]
    acc_w = owin_scr[...]
    o_win = acc_w[:, dh:] / acc_w[:, :dh]

    gates = jax.nn.sigmoid(gate_ref[...].astype(F32))
    g_hi = gates.astype(BF16)
    g_lo = (gates - g_hi.astype(F32)).astype(BF16)
    g_wide = (jnp.dot(g_hi, gx_ref[...], preferred_element_type=F32)
              + jnp.dot(g_lo, gx_ref[...], preferred_element_type=F32))
    by_head = lambda o: jnp.concatenate([o[h * T:(h + 1) * T] for h in range(Hg)], axis=1)
    w = Hg * dh
    out = (g_wide[:, 0:w] * by_head(o_cmp) + g_wide[:, w:2 * w] * by_head(o_slc)
           + g_wide[:, 2 * w:3 * w] * by_head(o_win))
    o_ref[...] = out.astype(BF16)


def _nsa_attention(proj, k_cmp, v_cmp, bias_cmp, near_bias, win_mask, overlap, expand):
    B, S, _ = proj.shape
    G, Hg, dh, T = NSA_KV_GROUPS, NSA_HPG, HEAD_DIM, TQ
    assert WINDOW == 2 * T and S % (2 * T) == 0
    n_slc = S // SLC_BLOCK
    gate_expand = np.zeros((LANE, 3 * Hg * dh), np.float32)
    for j in range(3 * Hg):
        gate_expand[j, j * dh:(j + 1) * dh] = 1.0
    gate_expand = jnp.asarray(gate_expand, BF16)
    return pl.pallas_call(
        functools.partial(_nsa_kernel, seq_len=S),
        grid=(B, G, S // T),
        in_specs=[
            pl.BlockSpec((None, T, Hg * dh), lambda b, g, i: (b, i, C_QN // (Hg * dh) + g)),
            pl.BlockSpec((None, S, 4 * dh), lambda b, g, i: (b, 0, C_KV // (4 * dh) + g)),
            pl.BlockSpec((None, None, LANE, LANE), lambda b, g, i: (b, g, 0, 0)),
            pl.BlockSpec((None, None, LANE, dh), lambda b, g, i: (b, g, 0, 0)),
            pl.BlockSpec((None, T, LANE), lambda b, g, i: (b, i, C_GATE // LANE + g)),
            pl.BlockSpec((None, Hg, T, LANE), lambda b, g, i: (g, 0, i, 0)),
            pl.BlockSpec((None, 2, Hg, T, 2 * T), lambda b, g, i: (g, 0, 0, 0, 0)),
            pl.BlockSpec((T, T), lambda b, g, i: (0, 0)),
            pl.BlockSpec((n_slc, LANE), lambda b, g, i: (0, 0)),
            pl.BlockSpec((n_slc, S), lambda b, g, i: (0, 0)),
            pl.BlockSpec(gate_expand.shape, lambda b, g, i: (0, 0)),
        ],
        out_specs=pl.BlockSpec((None, T, Hg * dh), lambda b, g, i: (b, i, g)),
        out_shape=jax.ShapeDtypeStruct((B, S, NSA_HEADS * dh), BF16),
        scratch_shapes=[pltpu.VMEM((T, S), F32), pltpu.VMEM((Hg * T, 2 * dh), F32),
                        pltpu.VMEM((Hg * T, 2 * dh), F32)],
        compiler_params=_cparams(("arbitrary", "arbitrary", "arbitrary")),
        name="nsa_attention",
    )(proj, proj, k_cmp, v_cmp, proj, bias_cmp, near_bias, win_mask, overlap, expand, gate_expand)


def _diff_kernel(q_ref, k_ref, v_ref, lam_ref, og_ref, nb_ref, o_ref, *, lambda_init, n_tiles):
    T, dh = TQ, HEAD_DIM
    qi = pl.program_id(2)
    q = q_ref[...]
    lower = lax.broadcasted_iota(jnp.int32, q.shape, 1) < dh
    zero = jnp.zeros_like(q)
    q12 = jnp.concatenate([jnp.where(lower, q, zero), jnp.where(lower, zero, q)], axis=0)

    def step(start, width, bias, carry):
        s = _dot_nt(q12, k_ref[start:start + width, :])
        if bias is not None:
            s = (s.reshape(2, T, width) + bias[None]).reshape(2 * T, width)
        return _softmax_step(s, v_ref[start:start + width, :], carry)

    lam_p = lam_ref[...]
    lam = (jnp.exp(jnp.sum(lam_p[0:1] * lam_p[1:2], axis=-1, keepdims=True))
           - jnp.exp(jnp.sum(lam_p[2:3] * lam_p[3:4], axis=-1, keepdims=True)) + lambda_init)

    def tile(c):
        carry = _softmax_init(2 * T, DIFF_V_DIM)
        for start, width, table in _key_chunks(c):
            bias = None if table is None else nb_ref[table[0], :, table[1]:table[1] + width]
            carry = step(start, width, bias, carry)
        a = carry[2] / carry[1]
        o = a[:T] - lam * a[T:]
        o_ref[...] = (_rms(o, og_ref[...]) * (1.0 - lambda_init)).astype(BF16)

    for c in range(n_tiles):
        pl.when(qi == c)(functools.partial(tile, c))


def _diff_attention(proj, lam_p, out_gain, near_bias, lambda_init):
    B, S, _ = proj.shape
    T, dh, H = TQ, HEAD_DIM, DIFF_HEADS
    return pl.pallas_call(
        functools.partial(_diff_kernel, lambda_init=lambda_init, n_tiles=S // T),
        grid=(B, H, S // T),
        in_specs=[
            pl.BlockSpec((None, T, 2 * dh), lambda b, h, i: (b, i, C_QD // (2 * dh) + h)),
            pl.BlockSpec((None, S, 2 * dh), lambda b, h, i: (b, 0, C_KD // (2 * dh) + h)),
            pl.BlockSpec((None, S, DIFF_V_DIM), lambda b, h, i: (b, 0, C_VD // DIFF_V_DIM + h)),
            pl.BlockSpec((4, dh), lambda b, h, i: (0, 0)),
            pl.BlockSpec((1, DIFF_V_DIM), lambda b, h, i: (0, 0)),
            pl.BlockSpec((None, 2, T, 2 * T), lambda b, h, i: (h, 0, 0, 0)),
        ],
        out_specs=pl.BlockSpec((None, T, DIFF_V_DIM), lambda b, h, i: (b, i, h)),
        out_shape=jax.ShapeDtypeStruct((B, S, H * DIFF_V_DIM), BF16),
        compiler_params=_cparams(("arbitrary", "arbitrary", "arbitrary")),
        name="diff_attention",
    )(proj, proj, proj, lam_p, out_gain, near_bias)


def _merge_kernel(x_ref, on_ref, od_ref, gm_ref, wn_ref, wd_ref, wo_ref, g2_ref, rw_ref, rb_ref, o_ref):
    tm, d = x_ref.shape
    a = jnp.dot(on_ref[...], wn_ref[...], preferred_element_type=F32)
    b = jnp.dot(od_ref[...], wd_ref[...], preferred_element_type=F32)
    sigmoid = lambda z: 0.5 * jnp.tanh(0.5 * z) + 0.5
    y = sigmoid(gm_ref[:, :d].astype(F32)) * a + sigmoid(gm_ref[:, d:].astype(F32)) * b
    x2 = x_ref[...] + jnp.dot(y.astype(BF16), wo_ref[...], preferred_element_type=F32)
    o_ref[:, :d] = x2

    h2 = _rms(x2, g2_ref[...])
    h_hi = h2.astype(BF16)
    h_lo = (h2 - h_hi.astype(F32)).astype(BF16)
    part = jnp.dot(jnp.concatenate([h_hi, h_lo], axis=0), rw_ref[...], preferred_element_type=F32)
    logits = part[:tm, :LANE] + part[:tm, LANE:] + part[tm:, :LANE] + part[tm:, LANE:] + rb_ref[...]
    lane = lax.broadcasted_iota(jnp.int32, logits.shape, 1)
    big = jnp.int32(LANE)

    def first_argmax(vals, mask):
        vals = jnp.where(mask, vals, -jnp.inf)
        top = jnp.max(vals, axis=-1, keepdims=True)
        idx = jnp.min(jnp.where(vals == top, lane, big), axis=-1, keepdims=True)
        return top, idx

    is_group = lane < N_GROUPS
    g_max, g_top = first_argmax(logits, is_group)
    g_top_p = 1.0 / jnp.sum(jnp.where(is_group, jnp.exp(logits - g_max), 0.0), axis=-1, keepdims=True)
    lo = N_GROUPS + g_top * EXPERTS_PER_GROUP
    in_group = (lane >= lo) & (lane < lo + EXPERTS_PER_GROUP)
    m1, i1 = first_argmax(logits, in_group)
    m2, i2 = first_argmax(logits, in_group & (lane != i1))
    e2 = jnp.exp(m2 - m1)
    w1 = g_top_p / (1.0 + e2)
    w2 = g_top_p * e2 / (1.0 + e2)
    la = jnp.minimum(i1, i2) - lo
    lb = jnp.maximum(i1, i2) - lo
    pair = jnp.right_shift(la * (7 - la), 1) + (lb - la - 1)
    bucket = (g_top * N_PAIRS + pair).astype(F32)
    exp_lane = lane + N_GROUPS
    route = (jnp.where(exp_lane == i1, w1, 0.0) + jnp.where(exp_lane == i2, w2, 0.0)
             + jnp.where(lane == N_EXPERTS, bucket, 0.0))
    o_ref[:, d:] = route


def _merge(x2d, o_nsa, o_diff, proj2d, wn, wd, wo, g2, rw, rb):
    n, d = x2d.shape
    tm = 512
    rw_hi = rw.astype(BF16)
    rw_cat = jnp.concatenate([rw_hi, (rw - rw_hi.astype(F32)).astype(BF16)], axis=1)
    full = lambda a: pl.BlockSpec(a.shape, lambda i: (0,) * a.ndim)
    return pl.pallas_call(
        _merge_kernel,
        grid=(n // tm,),
        in_specs=[pl.BlockSpec((tm, d), lambda i: (i, 0)),
                  pl.BlockSpec((tm, o_nsa.shape[1]), lambda i: (i, 0)),
                  pl.BlockSpec((tm, o_diff.shape[1]), lambda i: (i, 0)),
                  pl.BlockSpec((tm, 2 * d), lambda i: (i, C_GM // (2 * d))),
                  full(wn), full(wd), full(wo), full(g2), full(rw_cat), full(rb)],
        out_specs=pl.BlockSpec((tm, d + LANE), lambda i: (i, 0)),
        out_shape=jax.ShapeDtypeStruct((n, d + LANE), F32),
        compiler_params=_cparams(("arbitrary",)),
        name="merge_router",
    )(x2d, o_nsa, o_diff, proj2d, wn, wd, wo, g2, rw_cat, rb)


def _moe_kernel(tile_ref, ea_ref, eb_ref, lo_ref, hi_ref, first_ref, perm_ref,
                x_hbm, g2_ref, wga_ref, wua_ref, wda_ref, wgb_ref, wub_ref, wdb_ref,
                o_hbm, xb0, xb1, ac0, ac1, h_scr, g_sem, s_sem, *, tm, n_tiles):
    d = o_hbm.shape[1]
    v = pl.program_id(0)
    t = tile_ref[v]
    x_bufs, acc_bufs = (xb0, xb1), (ac0, ac1)
    last_slot = (n_tiles - 1) % 2

    def gather_copy(tt, sl, r):
        return pltpu.make_async_copy(x_hbm.at[pl.ds(perm_ref[tt * tm + r], 1), :],
                                     x_bufs[sl].at[pl.ds(r, 1), :], g_sem.at[sl])

    def scatter_copy(tt, sl, r):
        return pltpu.make_async_copy(acc_bufs[sl].at[pl.ds(r, 1), :],
                                     o_hbm.at[pl.ds(perm_ref[tt * tm + r], 1), :], s_sem.at[sl])

    def for_rows(fn):
        for r in range(tm):
            fn(r)

    def experts(sl, dma_starts):
        n_stage = 6
        per_stage = -(-len(dma_starts) // n_stage)

        def issue(stage):
            for start in dma_starts[stage * per_stage:(stage + 1) * per_stage]:
                start()

        h = h_scr[...]
        cw = x_bufs[sl][:, d:]
        lane = lax.broadcasted_iota(jnp.int32, cw.shape, 1)
        row = lax.broadcasted_iota(jnp.int32, (tm, 1), 0)
        in_bucket = (row >= lo_ref[v]) & (row < hi_ref[v])
        out = None
        for i, (e, wg_ref, wu_ref, wd_ref) in enumerate(((ea_ref[v], wga_ref, wua_ref, wda_ref),
                                                         (eb_ref[v], wgb_ref, wub_ref, wdb_ref))):
            w = jnp.sum(jnp.where(lane == e, cw, 0.0), axis=-1, keepdims=True)
            w = jnp.where(in_bucket, w, 0.0)
            gate = jnp.dot(h, wg_ref[...], preferred_element_type=F32)
            issue(3 * i)
            up = jnp.dot(h, wu_ref[...], preferred_element_type=F32)
            issue(3 * i + 1)
            hid = jax.nn.silu(gate) * up * w
            y = jnp.dot(hid.astype(BF16), wd_ref[...], preferred_element_type=F32)
            issue(3 * i + 2)
            out = y if out is None else out + y
        acc_bufs[sl][...] += out

    def first_visit(sl, scatter_previous):
        for_rows(lambda r: gather_copy(t, sl, r).wait())

        @pl.when(t >= 2)
        def _():
            for_rows(lambda r: scatter_copy(t - 2, sl, r).wait())

        xs = x_bufs[sl][:, :d]
        h_scr[...] = _rms(xs, g2_ref[...]).astype(BF16)
        acc_bufs[sl][...] = xs
        nxt = jnp.minimum(t + 1, n_tiles - 1)
        starts = [functools.partial(lambda r: gather_copy(nxt, 1 - sl, r).start(), r) for r in range(tm)]
        if scatter_previous:
            starts += [functools.partial(lambda r: scatter_copy(t - 1, 1 - sl, r).start(), r)
                       for r in range(tm)]
        experts(sl, starts)

    @pl.when(v == 0)
    def _():
        for_rows(lambda r: gather_copy(0, 0, r).start())
        first_visit(0, scatter_previous=False)

    for sl in range(2):
        pl.when((v > 0) & (first_ref[v] == 1) & (t % 2 == sl))(
            functools.partial(first_visit, sl, scatter_previous=True))
        pl.when((first_ref[v] == 0) & (t % 2 == sl))(functools.partial(experts, sl, []))

    @pl.when(v == pl.num_programs(0) - 1)
    def _():
        for_rows(lambda r: scatter_copy(n_tiles - 1, last_slot, r).start())
        for_rows(lambda r: gather_copy(n_tiles - 1, 1 - last_slot, r).wait())
        for_rows(lambda r: scatter_copy(n_tiles - 2, 1 - last_slot, r).wait())
        for_rows(lambda r: scatter_copy(n_tiles - 1, last_slot, r).wait())


def _moe(sched, perm, x2r, g2, wg, wu, wd, tm):
    n = x2r.shape[0]
    d, f = wg.shape[1], wg.shape[2]
    n_visits = sched[0].shape[0]
    n_tiles = n // tm
    assert n_tiles >= 2
    w_in_a = pl.BlockSpec((None, d, f), lambda v, t, ea, eb, *_: (ea[v], 0, 0))
    w_in_b = pl.BlockSpec((None, d, f), lambda v, t, ea, eb, *_: (eb[v], 0, 0))
    w_out_a = pl.BlockSpec((None, f, d), lambda v, t, ea, eb, *_: (ea[v], 0, 0))
    w_out_b = pl.BlockSpec((None, f, d), lambda v, t, ea, eb, *_: (eb[v], 0, 0))
    return pl.pallas_call(
        functools.partial(_moe_kernel, tm=tm, n_tiles=n_tiles),
        grid_spec=pltpu.PrefetchScalarGridSpec(
            num_scalar_prefetch=7,
            grid=(n_visits,),
            in_specs=[pl.BlockSpec(memory_space=pl.ANY),
                      pl.BlockSpec((1, d), lambda v, *_: (0, 0)),
                      w_in_a, w_in_a, w_out_a, w_in_b, w_in_b, w_out_b],
            out_specs=pl.BlockSpec(memory_space=pl.ANY),
            scratch_shapes=[pltpu.VMEM((tm, x2r.shape[1]), F32), pltpu.VMEM((tm, x2r.shape[1]), F32),
                            pltpu.VMEM((tm, d), F32), pltpu.VMEM((tm, d), F32),
                            pltpu.VMEM((tm, d), BF16),
                            pltpu.SemaphoreType.DMA((2,)), pltpu.SemaphoreType.DMA((2,))]),
        out_shape=jax.ShapeDtypeStruct((n, d), F32),
        compiler_params=_cparams(("arbitrary",)),
        name="moe_experts",
    )(*sched, perm, x2r, g2, wg, wu, wd, wg, wu, wd)


def _moe_schedule(bucket, tm):
    n = bucket.shape[0]
    n_tiles = n // tm
    n_visits = n_tiles + N_BUCKETS - 1
    counts = jnp.sum((bucket[:, None] == jnp.arange(N_BUCKETS)[None, :]).astype(jnp.int32), axis=0)
    ends = jnp.cumsum(counts)
    starts = ends - counts
    first_tile = starts // tm
    last_tile = jnp.maximum(ends - 1, 0) // tm
    n_vis = jnp.where(counts > 0, last_tile - first_tile + 1, 0)
    vis_end = jnp.cumsum(n_vis)
    vis_start = vis_end - n_vis
    v = jnp.arange(n_visits, dtype=jnp.int32)
    b = jnp.minimum(jnp.searchsorted(vis_end, v, side="right"), N_BUCKETS - 1).astype(jnp.int32)
    live = v < vis_end[-1]
    tile = jnp.where(live, first_tile[b] + v - vis_start[b], n_tiles - 1).astype(jnp.int32)
    lo = jnp.where(live, jnp.clip(starts[b] - tile * tm, 0, tm), 0).astype(jnp.int32)
    hi = jnp.where(live, jnp.clip(ends[b] - tile * tm, 0, tm), 0).astype(jnp.int32)
    b_last = b[jnp.maximum(vis_end[-1] - 1, 0)]
    b = jnp.where(live, b, b_last)
    pair_a = np.array([0, 0, 0, 1, 1, 2], np.int32)
    pair_b = np.array([1, 2, 3, 2, 3, 3], np.int32)
    grp = b // N_PAIRS
    ea = (grp * EXPERTS_PER_GROUP + jnp.asarray(pair_a)[b % N_PAIRS]).astype(jnp.int32)
    eb = (grp * EXPERTS_PER_GROUP + jnp.asarray(pair_b)[b % N_PAIRS]).astype(jnp.int32)
    first = jnp.concatenate([jnp.ones((1,), jnp.int32), (tile[1:] != tile[:-1]).astype(jnp.int32)])
    return tile, ea, eb, lo, hi, first


def _layer(x, layer, p):
    B, S, D = x.shape
    G, dh = NSA_KV_GROUPS, HEAD_DIM
    lambda_init = 0.8 - 0.6 * math.exp(-0.3 * layer)
    x2d = x.reshape(B * S, D)

    proj2d = _in_proj(x2d, p["norm1_g"][None, :], _reorder_w_in(p["w_in"]), _head_norm_rows(p))
    proj = proj2d.reshape(B, S, C_TOTAL)

    k_cmp, v_cmp = _compress(proj, *_compress_params(p["cmp_pos"], p["cmp_w1"], p["cmp_w2"]),
                             p["nsa_k_norm"][0:1])

    near, bias_cmp = _bias_tables(p["rel_bias"], S)
    nsa_near = near[:NSA_HEADS].reshape(G, NSA_HPG, 2, TQ, 2 * TQ).transpose(0, 2, 1, 3, 4)
    i = np.arange(TQ)
    win_mask = jnp.asarray(np.where(i[None, :] > i[:, None], 0.0, NEG).astype(np.float32))
    expand = jnp.asarray(np.repeat(np.eye(S // SLC_BLOCK, dtype=np.float32), SLC_BLOCK, axis=1), BF16)
    o_nsa = _nsa_attention(proj, k_cmp, v_cmp, bias_cmp.reshape(G, NSA_HPG, S, LANE), nsa_near, win_mask,
                           jnp.asarray(_overlap_matrix(S)), expand)
    o_diff = _diff_attention(proj, p["diff_lambda"], p["diff_out_norm"][None, :], near[NSA_HEADS:],
                             lambda_init)

    rw = jnp.concatenate([p["router_group_w"], p["router_expert_w"],
                          jnp.zeros((D, LANE - N_GROUPS - N_EXPERTS), F32)], axis=1)
    rb = jnp.concatenate([p["router_group_b"], p["router_expert_b"],
                          jnp.zeros((LANE - N_GROUPS - N_EXPERTS,), F32)])[None, :]
    x2r = _merge(x2d, o_nsa.reshape(B * S, -1), o_diff.reshape(B * S, -1), proj2d,
                 p["w_branch_nsa"].astype(BF16), p["w_branch_diff"].astype(BF16),
                 p["w_out"].astype(BF16), p["norm2_g"][None, :], rw, rb)

    tm = 256
    bucket = x2r[:, D + N_EXPERTS].astype(jnp.int32)
    perm = jnp.argsort(bucket).astype(jnp.int32)
    out = _moe(_moe_schedule(bucket, tm), perm, x2r, p["norm2_g"][None, :],
               p["expert_w_gate"].astype(BF16), p["expert_w_up"].astype(BF16),
               p["expert_w_down"].astype(BF16), tm)
    return out.reshape(B, S, D)


def kernel(x, norm1_g, w_in, nsa_q_norm, nsa_k_norm, cmp_pos, cmp_w1, cmp_w2, diff_q_norm, diff_k_norm,
           diff_lambda, diff_out_norm, rel_bias, w_branch_nsa, w_branch_diff, w_out, norm2_g,
           router_group_w, router_group_b, router_expert_w, router_expert_b, expert_w_gate,
           expert_w_up, expert_w_down):
    per_layer = dict(norm1_g=norm1_g, w_in=w_in, nsa_q_norm=nsa_q_norm, nsa_k_norm=nsa_k_norm,
                     cmp_pos=cmp_pos, cmp_w1=cmp_w1, cmp_w2=cmp_w2, diff_q_norm=diff_q_norm,
                     diff_k_norm=diff_k_norm, diff_lambda=diff_lambda, diff_out_norm=diff_out_norm,
                     w_branch_nsa=w_branch_nsa, w_branch_diff=w_branch_diff, w_out=w_out,
                     norm2_g=norm2_g, router_group_w=router_group_w, router_group_b=router_group_b,
                     router_expert_w=router_expert_w, router_expert_b=router_expert_b,
                     expert_w_gate=expert_w_gate, expert_w_up=expert_w_up, expert_w_down=expert_w_down)
    for layer in range(norm1_g.shape[0]):
        p = {k: v[layer] for k, v in per_layer.items()}
        p["rel_bias"] = rel_bias
        x = _layer(x, layer, p)
    return x
```

```python
import functools
import math

import numpy as np
import jax
import jax.numpy as jnp
from jax import lax
from jax.experimental import pallas as pl
from jax.experimental.pallas import tpu as pltpu

F32 = jnp.float32
BF16 = jnp.bfloat16

HEAD_DIM = 64
SCALE = HEAD_DIM ** -0.5
NSA_HEADS = 8
NSA_KV_GROUPS = 2
NSA_HPG = NSA_HEADS // NSA_KV_GROUPS
CMP_BLOCK = 32
CMP_STRIDE = 16
CMP_HIDDEN = 256
SLC_BLOCK = 64
SLC_TOP_N = 16
WINDOW = 512
FORCE_SCORE = 1e4
DIFF_HEADS = 4
DIFF_V_DIM = 2 * HEAD_DIM
REL_BUCKETS = 32
REL_MAX_DIST = 128
N_GROUPS = 4
EXPERTS_PER_GROUP = 4
N_EXPERTS = N_GROUPS * EXPERTS_PER_GROUP
EXPERT_HIDDEN = 512
RMS_EPS = 1e-6
NEG = -1e30
LOG2E = math.log2(math.e)

LANE = 128
MXU_DIM = 256
TQ = 256
N_PAIRS = 6
N_BUCKETS = N_GROUPS * N_PAIRS

C_GM = 0
C_QN = 2048
C_KV = 2560
C_CMP = 3072
C_GATE = 3328
C_QD = 3584
C_KD = 4096
C_VD = 4608
C_TOTAL = 5120

VMEM_LIMIT = 56 * 1024 * 1024


def _cparams(sem):
    return pltpu.CompilerParams(dimension_semantics=sem, vmem_limit_bytes=VMEM_LIMIT)


def _rms(x, gain):
    return x * lax.rsqrt(jnp.mean(x * x, axis=-1, keepdims=True) + RMS_EPS) * gain


def _reorder_w_in(w_in):
    d_model = w_in.shape[0]
    dh, G, Hg = HEAD_DIM, NSA_KV_GROUPS, NSA_HPG
    o_q = 0
    o_kv = o_q + NSA_HEADS * dh
    o_g = o_kv + 3 * 2 * G * dh
    o_qd = o_g + 3 * NSA_HEADS
    o_gm = o_qd + 3 * DIFF_HEADS * 2 * dh

    def cols(start, n):
        return w_in[:, start:start + n]

    def kv(kind, which, g):
        return cols(o_kv + ((kind * 2 + which) * G + g) * dh, dh)

    parts = [cols(o_gm, 2 * d_model), cols(o_q, NSA_HEADS * dh)]
    parts += [kv(kind, which, g) for g in range(G) for which in (0, 1) for kind in (1, 2)]
    parts += [kv(0, which, g) for g in range(G) for which in (0, 1)]
    for g in range(G):
        parts += [cols(o_g + (kind * G + g) * Hg, Hg) for kind in range(3)]
        parts.append(jnp.zeros((d_model, LANE - 3 * Hg), w_in.dtype))
    parts.append(cols(o_qd, 3 * DIFF_HEADS * 2 * dh))
    w = jnp.concatenate(parts, axis=1)
    assert w.shape[1] == C_TOTAL
    return w.astype(BF16)


def _head_norm_rows(p):
    dh, G = HEAD_DIM, NSA_KV_GROUPS
    qs = SCALE * LOG2E
    zeros = lambda n: jnp.zeros((n,), F32)
    kn = p["nsa_k_norm"]
    parts = [zeros(C_QN - C_GM), jnp.tile(p["nsa_q_norm"] * qs, NSA_HEADS)]
    parts += [jnp.concatenate([kn[1], kn[2], zeros(2 * dh)])] * G
    parts += [zeros(C_QD - C_CMP), jnp.tile((p["diff_q_norm"] * qs).reshape(-1), DIFF_HEADS),
              jnp.tile(p["diff_k_norm"].reshape(-1), DIFF_HEADS), zeros(C_TOTAL - C_VD)]
    gain = jnp.concatenate(parts)
    assert gain.shape[0] == C_TOTAL
    return gain[None, :]


def _t5_bucket(dist):
    n = jnp.maximum(dist, 0)
    max_exact = REL_BUCKETS // 2
    nf = jnp.maximum(n, 1).astype(F32)
    large = max_exact + (jnp.log(nf / max_exact) / math.log(REL_MAX_DIST / max_exact)
                         * (REL_BUCKETS - max_exact)).astype(jnp.int32)
    large = jnp.minimum(large, REL_BUCKETS - 1)
    return jnp.where(n < max_exact, n, large)


def _toeplitz(v_ext, rows, cols):
    length = v_ext.shape[-1]
    lead = v_ext.shape[:-1]
    flat = jnp.broadcast_to(v_ext[..., None, :], lead + (rows, length)).reshape(lead + (rows * length,))
    return flat[..., :rows * (length - 1)].reshape(lead + (rows, length - 1))[..., :cols]


def _bias_tables(rel_bias, S):
    heads = rel_bias.shape[1]
    far = REL_MAX_DIST
    assert TQ == 2 * far and LANE == far
    bd = rel_bias[_t5_bucket(jnp.arange(2 * far))].T.astype(F32) * LOG2E
    rel = bd - bd[:, far:far + 1]
    neg = lambda *shape: jnp.full((heads,) + shape, NEG, F32)
    zero = lambda *shape: jnp.zeros((heads,) + shape, F32)
    rev = lambda a: a[:, ::-1]
    d0 = _toeplitz(jnp.concatenate([rel[:, 0:1], neg(far), rev(rel[:, 1:far])], axis=1), far, far)
    d1 = _toeplitz(jnp.concatenate([rev(rel[:, 1:far + 1]), zero(far)], axis=1), far, far)
    z, n = zero(far, far), neg(far, far)
    blocks = lambda rows: jnp.concatenate([jnp.concatenate(r, axis=2) for r in rows], axis=1)
    near = jnp.stack([blocks([[d0, n, n, n], [d1, d0, n, n]]),
                      blocks([[z, d1, d0, n], [z, z, d1, d0]])], axis=1)
    hb = rel[:NSA_HEADS]
    zero8 = lambda *shape: jnp.zeros((NSA_HEADS,) + shape, F32)
    neg8 = lambda *shape: jnp.full((NSA_HEADS,) + shape, NEG, F32)
    width = TQ + far - 1
    tp = _toeplitz(jnp.concatenate([rev(hb[:, 0:far]), neg8(TQ + 1), zero8(TQ - 1)], axis=1), TQ, width)
    n_tiles = S // TQ
    per_tile = TQ // CMP_STRIDE
    col0 = per_tile * (n_tiles - 1)
    shift = CMP_STRIDE * col0 - (CMP_BLOCK - 1)
    first = -(-(shift - (far - 1)) // CMP_STRIDE)
    last = (shift + TQ - 1) // CMP_STRIDE
    band = tp[:, :, CMP_STRIDE * first - shift + far - 1::CMP_STRIDE][:, :, :last - first + 1]
    g2 = jnp.concatenate([zero8(TQ, first), band, neg8(TQ, 2 * TQ - last - 1)], axis=2)
    bias_cmp = jnp.stack([g2[:, :, col0 - per_tile * i:col0 - per_tile * i + LANE] for i in range(n_tiles)],
                         axis=1)
    n_cmp = (S - CMP_BLOCK) // CMP_STRIDE + 1
    bias_cmp = jnp.where(np.arange(LANE) < n_cmp, bias_cmp, NEG)
    return near, bias_cmp.reshape(NSA_HEADS, S, LANE)


def _overlap_matrix(S):
    n_cmp = (S - CMP_BLOCK) // CMP_STRIDE + 1
    n_slc = S // SLC_BLOCK
    c_start = np.arange(n_cmp) * CMP_STRIDE
    s_start = np.arange(n_slc) * SLC_BLOCK
    lo = np.maximum(s_start[:, None], c_start[None, :])
    hi = np.minimum(s_start[:, None] + SLC_BLOCK, c_start[None, :] + CMP_BLOCK)
    overlap = (np.maximum(hi - lo, 0) / CMP_BLOCK).astype(np.float32)
    out = np.zeros((n_slc, LANE), np.float32)
    out[:, :n_cmp] = overlap
    return out


IN_CHUNK = 512
HEAD_NORM_CHUNKS = {C_QN // IN_CHUNK: "all", C_KV // IN_CHUNK: "keys",
                    C_QD // IN_CHUNK: "all", C_KD // IN_CHUNK: "all"}


def _in_proj_kernel(x_ref, g_ref, w_ref, hg_ref, seg_ref, o_ref):
    dh = HEAD_DIM
    h = _rms(x_ref[...], g_ref[...]).astype(BF16)
    for c in range(C_TOTAL // IN_CHUNK):
        cols = slice(c * IN_CHUNK, (c + 1) * IN_CHUNK)
        y = jnp.dot(h, w_ref[:, cols], preferred_element_type=F32)
        mode = HEAD_NORM_CHUNKS.get(c)
        if mode is not None:
            sq = (y * y).astype(BF16)
            seg_w = seg_ref.shape[0]
            ss = jnp.concatenate([jnp.dot(sq[:, j:j + seg_w], seg_ref[...], preferred_element_type=F32)
                                  for j in range(0, IN_CHUNK, seg_w)], axis=1)
            scale = lax.rsqrt(ss * (1.0 / dh) + RMS_EPS) * hg_ref[:, cols]
            if mode == "keys":
                lane = lax.broadcasted_iota(jnp.int32, (1, IN_CHUNK), 1)
                scale = jnp.where(lane % (4 * dh) < 2 * dh, scale, 1.0)
            y = y * scale
        o_ref[:, cols] = y.astype(BF16)


def _in_proj(x2d, gain, w, head_gain):
    n, d = x2d.shape
    tm = 256
    seg = np.arange(MXU_DIM) // HEAD_DIM
    seg_ones = jnp.asarray((seg[:, None] == seg[None, :]).astype(np.float32), BF16)
    return pl.pallas_call(
        _in_proj_kernel,
        grid=(n // tm,),
        in_specs=[pl.BlockSpec((tm, d), lambda i: (i, 0)),
                  pl.BlockSpec((1, d), lambda i: (0, 0)),
                  pl.BlockSpec((d, C_TOTAL), lambda i: (0, 0)),
                  pl.BlockSpec((1, C_TOTAL), lambda i: (0, 0)),
                  pl.BlockSpec((MXU_DIM, MXU_DIM), lambda i: (0, 0))],
        out_specs=pl.BlockSpec((tm, C_TOTAL), lambda i: (i, 0)),
        out_shape=jax.ShapeDtypeStruct((n, C_TOTAL), BF16),
        compiler_params=_cparams(("arbitrary",)),
        name="in_proj",
    )(x2d, gain, w, head_gain, seg_ones)


def _compress_kernel(t_ref, pos_ref, w1_ref, w2_ref, kg_ref, k_ref, v_ref, tok_scr):
    S, dh = t_ref.shape[0], HEAD_DIM
    m = S // CMP_STRIDE
    tok_scr[0:S, :] = t_ref[...].astype(F32)
    tok_scr[S:, :] = jnp.zeros((CMP_STRIDE, LANE), F32)
    pre = jnp.zeros((m, 2 * CMP_HIDDEN), F32)
    for l in range(CMP_BLOCK):
        x = tok_scr[pl.ds(l, m, stride=CMP_STRIDE), :] + pos_ref[l:l + 1, :]
        pre = pre + jnp.dot(x.astype(BF16), w1_ref[l], preferred_element_type=F32)
    out = jnp.dot(jax.nn.gelu(pre).astype(BF16), w2_ref[...], preferred_element_type=F32)
    k_ref[...] = jnp.concatenate([_rms(out[:, :dh], kg_ref[...]), jnp.zeros((m, LANE - dh), F32)],
                                 axis=1).astype(BF16)
    v_ref[...] = out[:, dh:].astype(BF16)


def _compress(proj, pos, w1, w2, k_gain):
    B, S, _ = proj.shape
    G, m = NSA_KV_GROUPS, S // CMP_STRIDE
    outs = [jax.ShapeDtypeStruct((B, G, m, width), BF16) for width in (LANE, HEAD_DIM)]
    ospecs = [pl.BlockSpec((None, None, m, width), lambda b, g: (b, g, 0, 0)) for width in (LANE, HEAD_DIM)]
    return pl.pallas_call(
        _compress_kernel,
        grid=(B, G),
        in_specs=[pl.BlockSpec((None, S, LANE), lambda b, g: (b, 0, C_CMP // LANE + g)),
                  pl.BlockSpec(pos.shape, lambda b, g: (0, 0)),
                  pl.BlockSpec(w1.shape, lambda b, g: (0, 0, 0)),
                  pl.BlockSpec(w2.shape, lambda b, g: (0, 0)),
                  pl.BlockSpec((1, HEAD_DIM), lambda b, g: (0, 0))],
        out_specs=ospecs,
        out_shape=outs,
        scratch_shapes=[pltpu.VMEM((S + CMP_STRIDE, LANE), F32)],
        compiler_params=_cparams(("arbitrary", "arbitrary")),
        name="nsa_compress",
    )(proj, pos, w1, w2, k_gain)


def _compress_params(cmp_pos, cmp_w1, cmp_w2):
    dh, hid = HEAD_DIM, CMP_HIDDEN
    w1 = cmp_w1.reshape(2, CMP_BLOCK, dh, hid)
    z1 = jnp.zeros((CMP_BLOCK, dh, hid), cmp_w1.dtype)
    w1 = jnp.concatenate([jnp.concatenate([w1[0], z1], axis=2),
                          jnp.concatenate([z1, w1[1]], axis=2)], axis=1)
    z2 = jnp.zeros((hid, dh), cmp_w2.dtype)
    w2 = jnp.concatenate([jnp.concatenate([cmp_w2[0], z2], axis=1),
                          jnp.concatenate([z2, cmp_w2[1]], axis=1)], axis=0)
    pos = jnp.concatenate([cmp_pos[0], cmp_pos[1]], axis=1)
    return pos, w1.astype(BF16), w2.astype(BF16)


def _dot_nt(a, b):
    return lax.dot_general(a, b, (((1,), (1,)), ((), ())), preferred_element_type=F32)


def _softmax_step(s, v, carry):
    m, l, acc = carry
    m_new = jnp.maximum(m, jnp.max(s, axis=-1, keepdims=True))
    alpha = jnp.exp2(m - m_new)
    p = jnp.exp2(s - m_new)
    l = alpha * l + jnp.sum(p, axis=-1, keepdims=True)
    acc = alpha * acc + jnp.dot(p.astype(BF16), v, preferred_element_type=F32)
    return m_new, l, acc


def _softmax_init(rows, dv):
    return (jnp.full((rows, 1), NEG, F32), jnp.zeros((rows, 1), F32), jnp.zeros((rows, dv), F32))


def _softmax_step_ones(s, v_ones, carry):
    m, acc = carry
    m_new = jnp.maximum(m, jnp.max(s, axis=-1, keepdims=True))
    p = jnp.exp2((s - m_new).astype(BF16))
    acc = jnp.exp2(m - m_new) * acc + jnp.dot(p, v_ones, preferred_element_type=F32)
    return m_new, acc


def _softmax_init_ones(rows, width):
    return jnp.full((rows, 1), NEG, F32), jnp.zeros((rows, width), F32)


def _key_chunks(c):
    T = TQ
    chunks = [(0, T, (0, 0))] if c == 0 else [((c - 1) * T, 2 * T, (1, 0))]
    n_far = max(c - 1, 0)
    if n_far % 2:
        chunks.append((0, T, None))
    chunks += [(s * T, 2 * T, None) for s in range(n_far % 2, n_far, 2)]
    return chunks


def _nsa_kernel(q_ref, kv_ref, kc_ref, vc_ref, gate_ref, bc_ref, nb_ref, wm_ref, ov_ref, ex_ref, gx_ref,
                o_ref, mb_scr, osel_scr, owin_scr, *, seq_len):
    T, dh, Hg = TQ, HEAD_DIM, NSA_HPG
    R = Hg * T
    qi = pl.program_id(2)
    n_slc = seq_len // SLC_BLOCK

    lower = lax.broadcasted_iota(jnp.int32, (T, 2 * dh), 1) < dh
    q_lo, q_hi = [], []
    for pair in range(Hg // 2):
        pq = q_ref[:, pair * 2 * dh:(pair + 1) * 2 * dh]
        sw = jnp.concatenate([pq[:, dh:], pq[:, :dh]], axis=1)
        zero = jnp.zeros_like(pq)
        q_lo += [jnp.where(lower, pq, zero), jnp.where(lower, sw, zero)]
        q_hi += [jnp.where(lower, zero, sw), jnp.where(lower, zero, pq)]
    q = jnp.concatenate(q_lo, axis=0)
    q_win = jnp.concatenate(q_hi, axis=0)

    lc = _dot_nt(q, kc_ref[...]) + bc_ref[...].reshape(R, LANE)
    pc = jnp.exp2(lc - jnp.max(lc, axis=-1, keepdims=True))
    row = lax.broadcasted_iota(jnp.int32, (Hg, T, 1), 1).reshape(R, 1)
    visible = (qi * T + row >= CMP_BLOCK - 1).astype(F32)
    pc = pc * (visible / jnp.sum(pc, axis=-1, keepdims=True))
    o_cmp = jnp.dot(pc.astype(BF16), vc_ref[...], preferred_element_type=F32)

    n_all = SLC_TOP_N * SLC_BLOCK // T

    @pl.when(qi >= n_all)
    def _():
        p_sum = pc[0:T] + pc[T:2 * T] + pc[2 * T:3 * T] + pc[3 * T:4 * T]
        imp = lax.dot_general(ov_ref[...], p_sum, (((1,), (1,)), ((), ())), preferred_element_type=F32,
                              precision=lax.Precision.HIGHEST)
        blk = lax.broadcasted_iota(jnp.int32, (n_slc, T), 0)
        cur = (qi * T + lax.broadcasted_iota(jnp.int32, (n_slc, T), 1)) // SLC_BLOCK
        forced = (blk == 0) | (blk == cur) | (blk == cur - 1)
        score = jnp.where(blk <= cur, imp + jnp.where(forced, FORCE_SCORE, 0.0), NEG)
        rank = jnp.zeros((n_slc, T), jnp.int32)
        for i in range(n_slc):
            other = score[i:i + 1, :]
            ahead = (other > score) | ((other == score) & (blk > i))
            rank = rank + ahead.astype(jnp.int32)
        sel_bias = jnp.where(rank < SLC_TOP_N, 0.0, NEG).T.astype(BF16)
        mb_scr[...] = jnp.dot(sel_bias, ex_ref[...], preferred_element_type=F32)

    def values(rows, ones_upper):
        v = kv_ref[rows, 2 * dh:4 * dh]
        keep = lax.broadcasted_iota(jnp.int32, v.shape, 1) < dh
        return jnp.where(keep if ones_upper else ~keep, v, jnp.ones_like(v))

    def tile(c):
        chunks = _key_chunks(c)
        carry = _softmax_init_ones(R, 2 * dh)
        for start, width, table in chunks:
            rows = slice(start, start + width)
            s = _dot_nt(q, kv_ref[rows, 0:2 * dh]).reshape(Hg, T, width)
            if table is not None:
                s = s + nb_ref[table[0], :, :, table[1]:table[1] + width]
            if c >= n_all:
                s = s + mb_scr[:, rows][None]
            carry = _softmax_step_ones(s.reshape(R, width), values(rows, True), carry)
        osel_scr[...] = carry[1]
        start, width, table = chunks[0]
        win_chunks = [(start, width, nb_ref[table[0], :, :, table[1]:table[1] + width])]
        if c >= 2:
            win_chunks.append(((c - 2) * T, T, wm_ref[...][None]))
        carry = _softmax_init_ones(R, 2 * dh)
        for start, width, bias in win_chunks:
            rows = slice(start, start + width)
            s = _dot_nt(q_win, kv_ref[rows, 0:2 * dh]).reshape(Hg, T, width) + bias
            carry = _softmax_step_ones(s.reshape(R, width), values(rows, False), carry)
        owin_scr[...] = carry[1]

    for c in range(seq_len // T):
        pl.when(qi == c)(functools.partial(tile, c))
    acc_s = osel_scr[...]
    o_slc = acc_s[:, :dh] / acc_s[:, dh:]
    acc_w = owin_scr[...]
    o_win = acc_w[:, dh:] / acc_w[:, :dh]

    gates = jax.nn.sigmoid(gate_ref[...].astype(F32))
    g_hi = gates.astype(BF16)
    g_lo = (gates - g_hi.astype(F32)).astype(BF16)
    g_wide = (jnp.dot(g_hi, gx_ref[...], preferred_element_type=F32)
              + jnp.dot(g_lo, gx_ref[...], preferred_element_type=F32))
    by_head = lambda o: jnp.concatenate([o[h * T:(h + 1) * T] for h in range(Hg)], axis=1)
    w = Hg * dh
    out = (g_wide[:, 0:w] * by_head(o_cmp) + g_wide[:, w:2 * w] * by_head(o_slc)
           + g_wide[:, 2 * w:3 * w] * by_head(o_win))
    o_ref[...] = out.astype(BF16)


def _nsa_attention(proj, k_cmp, v_cmp, bias_cmp, near_bias, win_mask, overlap, expand):
    B, S, _ = proj.shape
    G, Hg, dh, T = NSA_KV_GROUPS, NSA_HPG, HEAD_DIM, TQ
    assert WINDOW == 2 * T and S % (2 * T) == 0
    n_slc = S // SLC_BLOCK
    gate_expand = np.zeros((LANE, 3 * Hg * dh), np.float32)
    for j in range(3 * Hg):
        gate_expand[j, j * dh:(j + 1) * dh] = 1.0
    gate_expand = jnp.asarray(gate_expand, BF16)
    return pl.pallas_call(
        functools.partial(_nsa_kernel, seq_len=S),
        grid=(B, G, S // T),
        in_specs=[
            pl.BlockSpec((None, T, Hg * dh), lambda b, g, i: (b, i, C_QN // (Hg * dh) + g)),
            pl.BlockSpec((None, S, 4 * dh), lambda b, g, i: (b, 0, C_KV // (4 * dh) + g)),
            pl.BlockSpec((None, None, LANE, LANE), lambda b, g, i: (b, g, 0, 0)),
            pl.BlockSpec((None, None, LANE, dh), lambda b, g, i: (b, g, 0, 0)),
            pl.BlockSpec((None, T, LANE), lambda b, g, i: (b, i, C_GATE // LANE + g)),
            pl.BlockSpec((None, Hg, T, LANE), lambda b, g, i: (g, 0, i, 0)),
            pl.BlockSpec((None, 2, Hg, T, 2 * T), lambda b, g, i: (g, 0, 0, 0, 0)),
            pl.BlockSpec((T, T), lambda b, g, i: (0, 0)),
            pl.BlockSpec((n_slc, LANE), lambda b, g, i: (0, 0)),
            pl.BlockSpec((n_slc, S), lambda b, g, i: (0, 0)),
            pl.BlockSpec(gate_expand.shape, lambda b, g, i: (0, 0)),
        ],
        out_specs=pl.BlockSpec((None, T, Hg * dh), lambda b, g, i: (b, i, g)),
        out_shape=jax.ShapeDtypeStruct((B, S, NSA_HEADS * dh), BF16),
        scratch_shapes=[pltpu.VMEM((T, S), F32), pltpu.VMEM((Hg * T, 2 * dh), F32),
                        pltpu.VMEM((Hg * T, 2 * dh), F32)],
        compiler_params=_cparams(("arbitrary", "arbitrary", "arbitrary")),
        name="nsa_attention",
    )(proj, proj, k_cmp, v_cmp, proj, bias_cmp, near_bias, win_mask, overlap, expand, gate_expand)


def _diff_kernel(q_ref, k_ref, v_ref, lam_ref, og_ref, nb_ref, o_ref, *, lambda_init, n_tiles):
    T, dh = TQ, HEAD_DIM
    qi = pl.program_id(2)
    q = q_ref[...]
    lower = lax.broadcasted_iota(jnp.int32, q.shape, 1) < dh
    zero = jnp.zeros_like(q)
    q12 = jnp.concatenate([jnp.where(lower, q, zero), jnp.where(lower, zero, q)], axis=0)

    def step(start, width, bias, carry):
        s = _dot_nt(q12, k_ref[start:start + width, :])
        if bias is not None:
            s = (s.reshape(2, T, width) + bias[None]).reshape(2 * T, width)
        return _softmax_step(s, v_ref[start:start + width, :], carry)

    lam_p = lam_ref[...]
    lam = (jnp.exp(jnp.sum(lam_p[0:1] * lam_p[1:2], axis=-1, keepdims=True))
           - jnp.exp(jnp.sum(lam_p[2:3] * lam_p[3:4], axis=-1, keepdims=True)) + lambda_init)

    def tile(c):
        carry = _softmax_init(2 * T, DIFF_V_DIM)
        for start, width, table in _key_chunks(c):
            bias = None if table is None else nb_ref[table[0], :, table[1]:table[1] + width]
            carry = step(start, width, bias, carry)
        a = carry[2] / carry[1]
        o = a[:T] - lam * a[T:]
        o_ref[...] = (_rms(o, og_ref[...]) * (1.0 - lambda_init)).astype(BF16)

    for c in range(n_tiles):
        pl.when(qi == c)(functools.partial(tile, c))


def _diff_attention(proj, lam_p, out_gain, near_bias, lambda_init):
    B, S, _ = proj.shape
    T, dh, H = TQ, HEAD_DIM, DIFF_HEADS
    return pl.pallas_call(
        functools.partial(_diff_kernel, lambda_init=lambda_init, n_tiles=S // T),
        grid=(B, H, S // T),
        in_specs=[
            pl.BlockSpec((None, T, 2 * dh), lambda b, h, i: (b, i, C_QD // (2 * dh) + h)),
            pl.BlockSpec((None, S, 2 * dh), lambda b, h, i: (b, 0, C_KD // (2 * dh) + h)),
            pl.BlockSpec((None, S, DIFF_V_DIM), lambda b, h, i: (b, 0, C_VD // DIFF_V_DIM + h)),
            pl.BlockSpec((4, dh), lambda b, h, i: (0, 0)),
            pl.BlockSpec((1, DIFF_V_DIM), lambda b, h, i: (0, 0)),
            pl.BlockSpec((None, 2, T, 2 * T), lambda b, h, i: (h, 0, 0, 0)),
        ],
        out_specs=pl.BlockSpec((None, T, DIFF_V_DIM), lambda b, h, i: (b, i, h)),
        out_shape=jax.ShapeDtypeStruct((B, S, H * DIFF_V_DIM), BF16),
        compiler_params=_cparams(("arbitrary", "arbitrary", "arbitrary")),
        name="diff_attention",
    )(proj, proj, proj, lam_p, out_gain, near_bias)


def _merge_kernel(x_ref, on_ref, od_ref, gm_ref, wn_ref, wd_ref, wo_ref, g2_ref, rw_ref, rb_ref, o_ref):
    tm, d = x_ref.shape
    a = jnp.dot(on_ref[...], wn_ref[...], preferred_element_type=F32)
    b = jnp.dot(od_ref[...], wd_ref[...], preferred_element_type=F32)
    sigmoid = lambda z: 0.5 * jnp.tanh(0.5 * z) + 0.5
    y = sigmoid(gm_ref[:, :d].astype(F32)) * a + sigmoid(gm_ref[:, d:].astype(F32)) * b
    x2 = x_ref[...] + jnp.dot(y.astype(BF16), wo_ref[...], preferred_element_type=F32)
    o_ref[:, :d] = x2

    h2 = _rms(x2, g2_ref[...])
    h_hi = h2.astype(BF16)
    h_lo = (h2 - h_hi.astype(F32)).astype(BF16)
    part = jnp.dot(jnp.concatenate([h_hi, h_lo], axis=0), rw_ref[...], preferred_element_type=F32)
    logits = part[:tm, :LANE] + part[:tm, LANE:] + part[tm:, :LANE] + part[tm:, LANE:] + rb_ref[...]
    lane = lax.broadcasted_iota(jnp.int32, logits.shape, 1)
    big = jnp.int32(LANE)

    def first_argmax(vals, mask):
        vals = jnp.where(mask, vals, -jnp.inf)
        top = jnp.max(vals, axis=-1, keepdims=True)
        idx = jnp.min(jnp.where(vals == top, lane, big), axis=-1, keepdims=True)
        return top, idx

    is_group = lane < N_GROUPS
    g_max, g_top = first_argmax(logits, is_group)
    g_top_p = 1.0 / jnp.sum(jnp.where(is_group, jnp.exp(logits - g_max), 0.0), axis=-1, keepdims=True)
    lo = N_GROUPS + g_top * EXPERTS_PER_GROUP
    in_group = (lane >= lo) & (lane < lo + EXPERTS_PER_GROUP)
    m1, i1 = first_argmax(logits, in_group)
    m2, i2 = first_argmax(logits, in_group & (lane != i1))
    e2 = jnp.exp(m2 - m1)
    w1 = g_top_p / (1.0 + e2)
    w2 = g_top_p * e2 / (1.0 + e2)
    la = jnp.minimum(i1, i2) - lo
    lb = jnp.maximum(i1, i2) - lo
    pair = jnp.right_shift(la * (7 - la), 1) + (lb - la - 1)
    bucket = (g_top * N_PAIRS + pair).astype(F32)
    exp_lane = lane + N_GROUPS
    route = (jnp.where(exp_lane == i1, w1, 0.0) + jnp.where(exp_lane == i2, w2, 0.0)
             + jnp.where(lane == N_EXPERTS, bucket, 0.0))
    o_ref[:, d:] = route


def _merge(x2d, o_nsa, o_diff, proj2d, wn, wd, wo, g2, rw, rb):
    n, d = x2d.shape
    tm = 512
    rw_hi = rw.astype(BF16)
    rw_cat = jnp.concatenate([rw_hi, (rw - rw_hi.astype(F32)).astype(BF16)], axis=1)
    full = lambda a: pl.BlockSpec(a.shape, lambda i: (0,) * a.ndim)
    return pl.pallas_call(
        _merge_kernel,
        grid=(n // tm,),
        in_specs=[pl.BlockSpec((tm, d), lambda i: (i, 0)),
                  pl.BlockSpec((tm, o_nsa.shape[1]), lambda i: (i, 0)),
                  pl.BlockSpec((tm, o_diff.shape[1]), lambda i: (i, 0)),
                  pl.BlockSpec((tm, 2 * d), lambda i: (i, C_GM // (2 * d))),
                  full(wn), full(wd), full(wo), full(g2), full(rw_cat), full(rb)],
        out_specs=pl.BlockSpec((tm, d + LANE), lambda i: (i, 0)),
        out_shape=jax.ShapeDtypeStruct((n, d + LANE), F32),
        compiler_params=_cparams(("arbitrary",)),
        name="merge_router",
    )(x2d, o_nsa, o_diff, proj2d, wn, wd, wo, g2, rw_cat, rb)


N_SLOT = 3


def _moe_kernel(tile_ref, ea_ref, eb_ref, lo_ref, hi_ref, first_ref, perm_ref,
                x_hbm, g2_ref, wga_ref, wua_ref, wda_ref, wgb_ref, wub_ref, wdb_ref,
                o_hbm, xb0, xb1, xb2, ac0, ac1, ac2, h_scr, g_sem, s_sem, *, tm, n_tiles):
    d = o_hbm.shape[1]
    v = pl.program_id(0)
    t = tile_ref[v]
    x_bufs, acc_bufs = (xb0, xb1, xb2), (ac0, ac1, ac2)
    last = n_tiles - 1

    def gather_copy(tt, sl, r):
        return pltpu.make_async_copy(x_hbm.at[pl.ds(perm_ref[tt * tm + r], 1), :],
                                     x_bufs[sl].at[pl.ds(r, 1), :], g_sem.at[sl])

    def scatter_copy(tt, sl, r):
        return pltpu.make_async_copy(acc_bufs[sl].at[pl.ds(r, 1), :],
                                     o_hbm.at[pl.ds(perm_ref[tt * tm + r], 1), :], s_sem.at[sl])

    def for_rows(fn):
        for r in range(tm):
            fn(r)

    def experts(sl, dma_starts):
        n_stage = 6
        per_stage = -(-len(dma_starts) // n_stage)

        def issue(stage):
            for start in dma_starts[stage * per_stage:(stage + 1) * per_stage]:
                start()

        h = h_scr[...]
        cw = x_bufs[sl][:, d:]
        lane = lax.broadcasted_iota(jnp.int32, cw.shape, 1)
        row = lax.broadcasted_iota(jnp.int32, (tm, 1), 0)
        in_bucket = (row >= lo_ref[v]) & (row < hi_ref[v])
        out = None
        for i, (e, wg_ref, wu_ref, wd_ref) in enumerate(((ea_ref[v], wga_ref, wua_ref, wda_ref),
                                                         (eb_ref[v], wgb_ref, wub_ref, wdb_ref))):
            w = jnp.sum(jnp.where(lane == e, cw, 0.0), axis=-1, keepdims=True)
            w = jnp.where(in_bucket, w, 0.0)
            gate = jnp.dot(h, wg_ref[...], preferred_element_type=F32)
            issue(3 * i)
            up = jnp.dot(h, wu_ref[...], preferred_element_type=F32)
            issue(3 * i + 1)
            hid = jax.nn.silu(gate) * up * w
            y = jnp.dot(hid.astype(BF16), wd_ref[...], preferred_element_type=F32)
            issue(3 * i + 2)
            out = y if out is None else out + y
        acc_bufs[sl][...] += out

    def first_visit(sl, scatter_previous):
        ahead, behind = (sl + 2) % N_SLOT, (sl - 1) % N_SLOT
        for_rows(lambda r: gather_copy(t, sl, r).wait())

        @pl.when(t >= N_SLOT)
        def _():
            for_rows(lambda r: scatter_copy(t - N_SLOT, sl, r).wait())

        xs = x_bufs[sl][:, :d]
        h_scr[...] = _rms(xs, g2_ref[...]).astype(BF16)
        acc_bufs[sl][...] = xs
        nxt = jnp.minimum(t + 2, last)
        starts = [functools.partial(lambda r: gather_copy(nxt, ahead, r).start(), r) for r in range(tm)]
        if scatter_previous:
            starts += [functools.partial(lambda r: scatter_copy(t - 1, behind, r).start(), r)
                       for r in range(tm)]
        experts(sl, starts)

    @pl.when(v == 0)
    def _():
        for_rows(lambda r: gather_copy(0, 0, r).start())
        for_rows(lambda r: gather_copy(1, 1, r).start())
        first_visit(0, scatter_previous=False)

    for sl in range(N_SLOT):
        pl.when((v > 0) & (first_ref[v] == 1) & (t % N_SLOT == sl))(
            functools.partial(first_visit, sl, scatter_previous=True))
        pl.when((first_ref[v] == 0) & (t % N_SLOT == sl))(functools.partial(experts, sl, []))

    @pl.when(v == pl.num_programs(0) - 1)
    def _():
        for_rows(lambda r: scatter_copy(last, last % N_SLOT, r).start())
        for tt in (last - 1, last):
            for_rows(lambda r: gather_copy(last, (tt + 2) % N_SLOT, r).wait())
        for tt in (last - 2, last - 1, last):
            for_rows(lambda r: scatter_copy(tt, tt % N_SLOT, r).wait())


def _moe(sched, perm, x2r, g2, wg, wu, wd, tm):
    n = x2r.shape[0]
    d, f = wg.shape[1], wg.shape[2]
    n_visits = sched[0].shape[0]
    n_tiles = n // tm
    assert n_tiles > N_SLOT
    w_in_a = pl.BlockSpec((None, d, f), lambda v, t, ea, eb, *_: (ea[v], 0, 0))
    w_in_b = pl.BlockSpec((None, d, f), lambda v, t, ea, eb, *_: (eb[v], 0, 0))
    w_out_a = pl.BlockSpec((None, f, d), lambda v, t, ea, eb, *_: (ea[v], 0, 0))
    w_out_b = pl.BlockSpec((None, f, d), lambda v, t, ea, eb, *_: (eb[v], 0, 0))
    return pl.pallas_call(
        functools.partial(_moe_kernel, tm=tm, n_tiles=n_tiles),
        grid_spec=pltpu.PrefetchScalarGridSpec(
            num_scalar_prefetch=7,
            grid=(n_visits,),
            in_specs=[pl.BlockSpec(memory_space=pl.ANY),
                      pl.BlockSpec((1, d), lambda v, *_: (0, 0)),
                      w_in_a, w_in_a, w_out_a, w_in_b, w_in_b, w_out_b],
            out_specs=pl.BlockSpec(memory_space=pl.ANY),
            scratch_shapes=([pltpu.VMEM((tm, x2r.shape[1]), F32)] * N_SLOT + [pltpu.VMEM((tm, d), F32)] * N_SLOT
                            + [pltpu.VMEM((tm, d), BF16),
                               pltpu.SemaphoreType.DMA((N_SLOT,)), pltpu.SemaphoreType.DMA((N_SLOT,))])),
        out_shape=jax.ShapeDtypeStruct((n, d), F32),
        compiler_params=_cparams(("arbitrary",)),
        name="moe_experts",
    )(*sched, perm, x2r, g2, wg, wu, wd, wg, wu, wd)


def _moe_schedule(bucket, tm):
    n = bucket.shape[0]
    n_tiles = n // tm
    n_visits = n_tiles + N_BUCKETS - 1
    counts = jnp.sum((bucket[:, None] == jnp.arange(N_BUCKETS)[None, :]).astype(jnp.int32), axis=0)
    ends = jnp.cumsum(counts)
    starts = ends - counts
    first_tile = starts // tm
    last_tile = jnp.maximum(ends - 1, 0) // tm
    n_vis = jnp.where(counts > 0, last_tile - first_tile + 1, 0)
    vis_end = jnp.cumsum(n_vis)
    vis_start = vis_end - n_vis
    v = jnp.arange(n_visits, dtype=jnp.int32)
    b = jnp.minimum(jnp.searchsorted(vis_end, v, side="right"), N_BUCKETS - 1).astype(jnp.int32)
    live = v < vis_end[-1]
    tile = jnp.where(live, first_tile[b] + v - vis_start[b], n_tiles - 1).astype(jnp.int32)
    lo = jnp.where(live, jnp.clip(starts[b] - tile * tm, 0, tm), 0).astype(jnp.int32)
    hi = jnp.where(live, jnp.clip(ends[b] - tile * tm, 0, tm), 0).astype(jnp.int32)
    b_last = b[jnp.maximum(vis_end[-1] - 1, 0)]
    b = jnp.where(live, b, b_last)
    pair_a = np.array([0, 0, 0, 1, 1, 2], np.int32)
    pair_b = np.array([1, 2, 3, 2, 3, 3], np.int32)
    grp = b // N_PAIRS
    ea = (grp * EXPERTS_PER_GROUP + jnp.asarray(pair_a)[b % N_PAIRS]).astype(jnp.int32)
    eb = (grp * EXPERTS_PER_GROUP + jnp.asarray(pair_b)[b % N_PAIRS]).astype(jnp.int32)
    first = jnp.concatenate([jnp.ones((1,), jnp.int32), (tile[1:] != tile[:-1]).astype(jnp.int32)])
    return tile, ea, eb, lo, hi, first


def _layer(x, layer, p):
    B, S, D = x.shape
    G, dh = NSA_KV_GROUPS, HEAD_DIM
    lambda_init = 0.8 - 0.6 * math.exp(-0.3 * layer)
    x2d = x.reshape(B * S, D)

    proj2d = _in_proj(x2d, p["norm1_g"][None, :], _reorder_w_in(p["w_in"]), _head_norm_rows(p))
    proj = proj2d.reshape(B, S, C_TOTAL)

    k_cmp, v_cmp = _compress(proj, *_compress_params(p["cmp_pos"], p["cmp_w1"], p["cmp_w2"]),
                             p["nsa_k_norm"][0:1])

    near, bias_cmp = _bias_tables(p["rel_bias"], S)
    nsa_near = near[:NSA_HEADS].reshape(G, NSA_HPG, 2, TQ, 2 * TQ).transpose(0, 2, 1, 3, 4)
    i = np.arange(TQ)
    win_mask = jnp.asarray(np.where(i[None, :] > i[:, None], 0.0, NEG).astype(np.float32))
    expand = jnp.asarray(np.repeat(np.eye(S // SLC_BLOCK, dtype=np.float32), SLC_BLOCK, axis=1), BF16)
    o_nsa = _nsa_attention(proj, k_cmp, v_cmp, bias_cmp.reshape(G, NSA_HPG, S, LANE), nsa_near, win_mask,
                           jnp.asarray(_overlap_matrix(S)), expand)
    o_diff = _diff_attention(proj, p["diff_lambda"], p["diff_out_norm"][None, :], near[NSA_HEADS:],
                             lambda_init)

    rw = jnp.concatenate([p["router_group_w"], p["router_expert_w"],
                          jnp.zeros((D, LANE - N_GROUPS - N_EXPERTS), F32)], axis=1)
    rb = jnp.concatenate([p["router_group_b"], p["router_expert_b"],
                          jnp.zeros((LANE - N_GROUPS - N_EXPERTS,), F32)])[None, :]
    x2r = _merge(x2d, o_nsa.reshape(B * S, -1), o_diff.reshape(B * S, -1), proj2d,
                 p["w_branch_nsa"].astype(BF16), p["w_branch_diff"].astype(BF16),
                 p["w_out"].astype(BF16), p["norm2_g"][None, :], rw, rb)

    tm = 256
    bucket = x2r[:, D + N_EXPERTS].astype(jnp.int32)
    perm = jnp.argsort(bucket).astype(jnp.int32)
    out = _moe(_moe_schedule(bucket, tm), perm, x2r, p["norm2_g"][None, :],
               p["expert_w_gate"].astype(BF16), p["expert_w_up"].astype(BF16),
               p["expert_w_down"].astype(BF16), tm)
    return out.reshape(B, S, D)


def kernel(x, norm1_g, w_in, nsa_q_norm, nsa_k_norm, cmp_pos, cmp_w1, cmp_w2, diff_q_norm, diff_k_norm,
           diff_lambda, diff_out_norm, rel_bias, w_branch_nsa, w_branch_diff, w_out, norm2_g,
           router_group_w, router_group_b, router_expert_w, router_expert_b, expert_w_gate,
           expert_w_up, expert_w_down):
    per_layer = dict(norm1_g=norm1_g, w_in=w_in, nsa_q_norm=nsa_q_norm, nsa_k_norm=nsa_k_norm,
                     cmp_pos=cmp_pos, cmp_w1=cmp_w1, cmp_w2=cmp_w2, diff_q_norm=diff_q_norm,
                     diff_k_norm=diff_k_norm, diff_lambda=diff_lambda, diff_out_norm=diff_out_norm,
                     w_branch_nsa=w_branch_nsa, w_branch_diff=w_branch_diff, w_out=w_out,
                     norm2_g=norm2_g, router_group_w=router_group_w, router_group_b=router_group_b,
                     router_expert_w=router_expert_w, router_expert_b=router_expert_b,
                     expert_w_gate=expert_w_gate, expert_w_up=expert_w_up, expert_w_down=expert_w_down)
    for layer in range(norm1_g.shape[0]):
        p = {k: v[layer] for k, v in per_layer.items()}
        p["rel_bias"] = rel_bias
        x = _layer(x, layer, p)
    return x
```

```python
import functools
import math

import numpy as np
import jax
import jax.numpy as jnp
from jax import lax
from jax.experimental import pallas as pl
from jax.experimental.pallas import tpu as pltpu

F32 = jnp.float32
BF16 = jnp.bfloat16

HEAD_DIM = 64
SCALE = HEAD_DIM ** -0.5
NSA_HEADS = 8
NSA_KV_GROUPS = 2
NSA_HPG = NSA_HEADS // NSA_KV_GROUPS
CMP_BLOCK = 32
CMP_STRIDE = 16
CMP_HIDDEN = 256
SLC_BLOCK = 64
SLC_TOP_N = 16
WINDOW = 512
FORCE_SCORE = 1e4
DIFF_HEADS = 4
DIFF_V_DIM = 2 * HEAD_DIM
REL_BUCKETS = 32
REL_MAX_DIST = 128
N_GROUPS = 4
EXPERTS_PER_GROUP = 4
N_EXPERTS = N_GROUPS * EXPERTS_PER_GROUP
EXPERT_HIDDEN = 512
RMS_EPS = 1e-6
NEG = -1e30
LOG2E = math.log2(math.e)

LANE = 128
MXU_DIM = 256
TQ = 256
N_PAIRS = 6
N_BUCKETS = N_GROUPS * N_PAIRS

C_GM = 0
C_QN = 2048
C_KV = 2560
C_CMP = 3072
C_GATE = 3328
C_QD = 3584
C_KD = 4096
C_VD = 4608
C_TOTAL = 5120

VMEM_LIMIT = 56 * 1024 * 1024


def _cparams(sem):
    return pltpu.CompilerParams(dimension_semantics=sem, vmem_limit_bytes=VMEM_LIMIT)


def _rms(x, gain):
    return x * lax.rsqrt(jnp.mean(x * x, axis=-1, keepdims=True) + RMS_EPS) * gain


def _reorder_w_in(w_in):
    d_model = w_in.shape[0]
    w_in = w_in.astype(BF16)
    dh, G, Hg = HEAD_DIM, NSA_KV_GROUPS, NSA_HPG
    o_q = 0
    o_kv = o_q + NSA_HEADS * dh
    o_g = o_kv + 3 * 2 * G * dh
    o_qd = o_g + 3 * NSA_HEADS
    o_gm = o_qd + 3 * DIFF_HEADS * 2 * dh

    def cols(start, n):
        return w_in[:, start:start + n]

    def kv(kind, which, g):
        return cols(o_kv + ((kind * 2 + which) * G + g) * dh, dh)

    parts = [cols(o_gm, 2 * d_model), cols(o_q, NSA_HEADS * dh)]
    parts += [kv(kind, which, g) for g in range(G) for which in (0, 1) for kind in (1, 2)]
    parts += [kv(0, which, g) for g in range(G) for which in (0, 1)]
    for g in range(G):
        parts += [cols(o_g + (kind * G + g) * Hg, Hg) for kind in range(3)]
        parts.append(jnp.zeros((d_model, LANE - 3 * Hg), w_in.dtype))
    parts.append(cols(o_qd, 3 * DIFF_HEADS * 2 * dh))
    w = jnp.concatenate(parts, axis=1)
    assert w.shape[1] == C_TOTAL
    return w


def _head_norm_rows(p):
    dh, G = HEAD_DIM, NSA_KV_GROUPS
    qs = SCALE * LOG2E
    zeros = lambda n: jnp.zeros((n,), F32)
    kn = p["nsa_k_norm"]
    parts = [zeros(C_QN - C_GM), jnp.tile(p["nsa_q_norm"] * qs, NSA_HEADS)]
    parts += [jnp.concatenate([kn[1], kn[2], zeros(2 * dh)])] * G
    parts += [zeros(C_QD - C_CMP), jnp.tile((p["diff_q_norm"] * qs).reshape(-1), DIFF_HEADS),
              jnp.tile(p["diff_k_norm"].reshape(-1), DIFF_HEADS), zeros(C_TOTAL - C_VD)]
    gain = jnp.concatenate(parts)
    assert gain.shape[0] == C_TOTAL
    return gain[None, :]


def _t5_bucket(dist):
    n = jnp.maximum(dist, 0)
    max_exact = REL_BUCKETS // 2
    nf = jnp.maximum(n, 1).astype(F32)
    large = max_exact + (jnp.log(nf / max_exact) / math.log(REL_MAX_DIST / max_exact)
                         * (REL_BUCKETS - max_exact)).astype(jnp.int32)
    large = jnp.minimum(large, REL_BUCKETS - 1)
    return jnp.where(n < max_exact, n, large)


def _toeplitz(v_ext, rows, cols):
    length = v_ext.shape[-1]
    lead = v_ext.shape[:-1]
    flat = jnp.broadcast_to(v_ext[..., None, :], lead + (rows, length)).reshape(lead + (rows * length,))
    return flat[..., :rows * (length - 1)].reshape(lead + (rows, length - 1))[..., :cols]


def _bias_tables(rel_bias, S):
    heads = rel_bias.shape[1]
    far = REL_MAX_DIST
    assert TQ == 2 * far and LANE == far
    bd = rel_bias[_t5_bucket(jnp.arange(2 * far))].T.astype(F32) * LOG2E
    rel = bd - bd[:, far:far + 1]
    neg = lambda *shape: jnp.full((heads,) + shape, NEG, F32)
    zero = lambda *shape: jnp.zeros((heads,) + shape, F32)
    rev = lambda a: a[:, ::-1]
    d0 = _toeplitz(jnp.concatenate([rel[:, 0:1], neg(far), rev(rel[:, 1:far])], axis=1), far, far)
    d1 = _toeplitz(jnp.concatenate([rev(rel[:, 1:far + 1]), zero(far)], axis=1), far, far)
    z, n = zero(far, far), neg(far, far)
    blocks = lambda rows: jnp.concatenate([jnp.concatenate(r, axis=2) for r in rows], axis=1)
    near = jnp.stack([blocks([[d0, n, n, n], [d1, d0, n, n]]),
                      blocks([[z, d1, d0, n], [z, z, d1, d0]])], axis=1)
    hb = rel[:NSA_HEADS]
    zero8 = lambda *shape: jnp.zeros((NSA_HEADS,) + shape, F32)
    neg8 = lambda *shape: jnp.full((NSA_HEADS,) + shape, NEG, F32)
    width = TQ + far - 1
    tp = _toeplitz(jnp.concatenate([rev(hb[:, 0:far]), neg8(TQ + 1), zero8(TQ - 1)], axis=1), TQ, width)
    n_tiles = S // TQ
    per_tile = TQ // CMP_STRIDE
    col0 = per_tile * (n_tiles - 1)
    shift = CMP_STRIDE * col0 - (CMP_BLOCK - 1)
    first = -(-(shift - (far - 1)) // CMP_STRIDE)
    last = (shift + TQ - 1) // CMP_STRIDE
    band = tp[:, :, CMP_STRIDE * first - shift + far - 1::CMP_STRIDE][:, :, :last - first + 1]
    g2 = jnp.concatenate([zero8(TQ, first), band, neg8(TQ, 2 * TQ - last - 1)], axis=2)
    bias_cmp = jnp.stack([g2[:, :, col0 - per_tile * i:col0 - per_tile * i + LANE] for i in range(n_tiles)],
                         axis=1)
    n_cmp = (S - CMP_BLOCK) // CMP_STRIDE + 1
    bias_cmp = jnp.where(np.arange(LANE) < n_cmp, bias_cmp, NEG)
    return near, bias_cmp.reshape(NSA_HEADS, S, LANE)


def _overlap_matrix(S):
    n_cmp = (S - CMP_BLOCK) // CMP_STRIDE + 1
    n_slc = S // SLC_BLOCK
    c_start = np.arange(n_cmp) * CMP_STRIDE
    s_start = np.arange(n_slc) * SLC_BLOCK
    lo = np.maximum(s_start[:, None], c_start[None, :])
    hi = np.minimum(s_start[:, None] + SLC_BLOCK, c_start[None, :] + CMP_BLOCK)
    overlap = (np.maximum(hi - lo, 0) / CMP_BLOCK).astype(np.float32)
    out = np.zeros((n_slc, LANE), np.float32)
    out[:, :n_cmp] = overlap
    return out


IN_CHUNK = 512
HEAD_NORM_CHUNKS = {C_QN // IN_CHUNK: "all", C_KV // IN_CHUNK: "keys",
                    C_QD // IN_CHUNK: "all", C_KD // IN_CHUNK: "all"}


def _in_proj_kernel(x_ref, g_ref, w_ref, hg_ref, seg_ref, o_ref):
    dh = HEAD_DIM
    h = _rms(x_ref[...], g_ref[...]).astype(BF16)
    for c in range(C_TOTAL // IN_CHUNK):
        cols = slice(c * IN_CHUNK, (c + 1) * IN_CHUNK)
        y = jnp.dot(h, w_ref[:, cols], preferred_element_type=F32)
        mode = HEAD_NORM_CHUNKS.get(c)
        if mode is not None:
            sq = (y * y).astype(BF16)
            seg_w = seg_ref.shape[0]
            ss = jnp.concatenate([jnp.dot(sq[:, j:j + seg_w], seg_ref[...], preferred_element_type=F32)
                                  for j in range(0, IN_CHUNK, seg_w)], axis=1)
            scale = lax.rsqrt(ss * (1.0 / dh) + RMS_EPS) * hg_ref[:, cols]
            if mode == "keys":
                lane = lax.broadcasted_iota(jnp.int32, (1, IN_CHUNK), 1)
                scale = jnp.where(lane % (4 * dh) < 2 * dh, scale, 1.0)
            y = y * scale
        o_ref[:, cols] = y.astype(BF16)


def _in_proj(x2d, gain, w, head_gain):
    n, d = x2d.shape
    tm = 256
    seg = np.arange(MXU_DIM) // HEAD_DIM
    seg_ones = jnp.asarray((seg[:, None] == seg[None, :]).astype(np.float32), BF16)
    return pl.pallas_call(
        _in_proj_kernel,
        grid=(n // tm,),
        in_specs=[pl.BlockSpec((tm, d), lambda i: (i, 0)),
                  pl.BlockSpec((1, d), lambda i: (0, 0)),
                  pl.BlockSpec((d, C_TOTAL), lambda i: (0, 0)),
                  pl.BlockSpec((1, C_TOTAL), lambda i: (0, 0)),
                  pl.BlockSpec((MXU_DIM, MXU_DIM), lambda i: (0, 0))],
        out_specs=pl.BlockSpec((tm, C_TOTAL), lambda i: (i, 0)),
        out_shape=jax.ShapeDtypeStruct((n, C_TOTAL), BF16),
        compiler_params=_cparams(("arbitrary",)),
        name="in_proj",
    )(x2d, gain, w, head_gain, seg_ones)


def _compress_kernel(t_ref, pos_ref, w1_ref, w2_ref, kg_ref, k_ref, v_ref, tok_scr):
    S, dh = t_ref.shape[0], HEAD_DIM
    m = S // CMP_STRIDE
    tok_scr[0:S, :] = t_ref[...].astype(F32)
    tok_scr[S:, :] = jnp.zeros((CMP_STRIDE, LANE), F32)
    pre = jnp.zeros((m, 2 * CMP_HIDDEN), F32)
    for l in range(CMP_BLOCK):
        x = tok_scr[pl.ds(l, m, stride=CMP_STRIDE), :] + pos_ref[l:l + 1, :]
        pre = pre + jnp.dot(x.astype(BF16), w1_ref[l], preferred_element_type=F32)
    out = jnp.dot(jax.nn.gelu(pre).astype(BF16), w2_ref[...], preferred_element_type=F32)
    k_ref[...] = jnp.concatenate([_rms(out[:, :dh], kg_ref[...]), jnp.zeros((m, LANE - dh), F32)],
                                 axis=1).astype(BF16)
    v_ref[...] = out[:, dh:].astype(BF16)


def _compress(proj, pos, w1, w2, k_gain):
    B, S, _ = proj.shape
    G, m = NSA_KV_GROUPS, S // CMP_STRIDE
    outs = [jax.ShapeDtypeStruct((B, G, m, width), BF16) for width in (LANE, HEAD_DIM)]
    ospecs = [pl.BlockSpec((None, None, m, width), lambda b, g: (b, g, 0, 0)) for width in (LANE, HEAD_DIM)]
    return pl.pallas_call(
        _compress_kernel,
        grid=(B, G),
        in_specs=[pl.BlockSpec((None, S, LANE), lambda b, g: (b, 0, C_CMP // LANE + g)),
                  pl.BlockSpec(pos.shape, lambda b, g: (0, 0)),
                  pl.BlockSpec(w1.shape, lambda b, g: (0, 0, 0)),
                  pl.BlockSpec(w2.shape, lambda b, g: (0, 0)),
                  pl.BlockSpec((1, HEAD_DIM), lambda b, g: (0, 0))],
        out_specs=ospecs,
        out_shape=outs,
        scratch_shapes=[pltpu.VMEM((S + CMP_STRIDE, LANE), F32)],
        compiler_params=_cparams(("arbitrary", "arbitrary")),
        name="nsa_compress",
    )(proj, pos, w1, w2, k_gain)


def _compress_params(cmp_pos, cmp_w1, cmp_w2):
    dh, hid = HEAD_DIM, CMP_HIDDEN
    w1 = cmp_w1.reshape(2, CMP_BLOCK, dh, hid)
    z1 = jnp.zeros((CMP_BLOCK, dh, hid), cmp_w1.dtype)
    w1 = jnp.concatenate([jnp.concatenate([w1[0], z1], axis=2),
                          jnp.concatenate([z1, w1[1]], axis=2)], axis=1)
    z2 = jnp.zeros((hid, dh), cmp_w2.dtype)
    w2 = jnp.concatenate([jnp.concatenate([cmp_w2[0], z2], axis=1),
                          jnp.concatenate([z2, cmp_w2[1]], axis=1)], axis=0)
    pos = jnp.concatenate([cmp_pos[0], cmp_pos[1]], axis=1)
    return pos, w1.astype(BF16), w2.astype(BF16)


def _dot_nt(a, b):
    return lax.dot_general(a, b, (((1,), (1,)), ((), ())), preferred_element_type=F32)


def _softmax_step(s, v, carry):
    m, l, acc = carry
    m_new = jnp.maximum(m, jnp.max(s, axis=-1, keepdims=True))
    alpha = jnp.exp2(m - m_new)
    p = jnp.exp2(s - m_new)
    l = alpha * l + jnp.sum(p, axis=-1, keepdims=True)
    acc = alpha * acc + jnp.dot(p.astype(BF16), v, preferred_element_type=F32)
    return m_new, l, acc


def _softmax_init(rows, dv):
    return (jnp.full((rows, 1), NEG, F32), jnp.zeros((rows, 1), F32), jnp.zeros((rows, dv), F32))


def _softmax_step_ones(s, v_ones, carry):
    m, acc = carry
    m_new = jnp.maximum(m, jnp.max(s, axis=-1, keepdims=True))
    p = jnp.exp2((s - m_new).astype(BF16))
    acc = jnp.exp2(m - m_new) * acc + jnp.dot(p, v_ones, preferred_element_type=F32)
    return m_new, acc


def _softmax_init_ones(rows, width):
    return jnp.full((rows, 1), NEG, F32), jnp.zeros((rows, width), F32)


def _key_chunks(c):
    T = TQ
    chunks = [(0, T, (0, 0))] if c == 0 else [((c - 1) * T, 2 * T, (1, 0))]
    n_far = max(c - 1, 0)
    if n_far % 2:
        chunks.append((0, T, None))
    chunks += [(s * T, 2 * T, None) for s in range(n_far % 2, n_far, 2)]
    return chunks


def _nsa_kernel(q_ref, kv_ref, kc_ref, vc_ref, gate_ref, bc_ref, nb_ref, wm_ref, ov_ref, ex_ref, gx_ref,
                o_ref, *, seq_len):
    T, dh, Hg = TQ, HEAD_DIM, NSA_HPG
    R = Hg * T
    n_slc = seq_len // SLC_BLOCK
    n_all = SLC_TOP_N * SLC_BLOCK // T

    def values(rows, ones_upper):
        v = kv_ref[rows, 2 * dh:4 * dh]
        keep = lax.broadcasted_iota(jnp.int32, v.shape, 1) < dh
        return jnp.where(keep if ones_upper else ~keep, v, jnp.ones_like(v))

    def selection_mask(c, pc):
        p_sum = pc[0:T] + pc[T:2 * T] + pc[2 * T:3 * T] + pc[3 * T:4 * T]
        imp = lax.dot_general(ov_ref[...], p_sum, (((1,), (1,)), ((), ())), preferred_element_type=F32,
                              precision=lax.Precision.HIGHEST)
        blk = lax.broadcasted_iota(jnp.int32, (n_slc, T), 0)
        cur = (c * T + lax.broadcasted_iota(jnp.int32, (n_slc, T), 1)) // SLC_BLOCK
        forced = (blk == 0) | (blk == cur) | (blk == cur - 1)
        score = jnp.where(blk <= cur, imp + jnp.where(forced, FORCE_SCORE, 0.0), NEG)
        rank = jnp.zeros((n_slc, T), jnp.int32)
        for i in range(n_slc):
            other = score[i:i + 1, :]
            ahead = (other > score) | ((other == score) & (blk > i))
            rank = rank + ahead.astype(jnp.int32)
        sel_bias = jnp.where(rank < SLC_TOP_N, 0.0, NEG).T.astype(BF16)
        return jnp.dot(sel_bias, ex_ref[...], preferred_element_type=F32)

    def tile(c):
        lower = lax.broadcasted_iota(jnp.int32, (T, 2 * dh), 1) < dh
        q_lo, q_hi = [], []
        for pair in range(Hg // 2):
            pq = q_ref[:, pair * 2 * dh:(pair + 1) * 2 * dh]
            sw = jnp.concatenate([pq[:, dh:], pq[:, :dh]], axis=1)
            zero = jnp.zeros_like(pq)
            q_lo += [jnp.where(lower, pq, zero), jnp.where(lower, sw, zero)]
            q_hi += [jnp.where(lower, zero, sw), jnp.where(lower, zero, pq)]
        q = jnp.concatenate(q_lo, axis=0)
        q_win = jnp.concatenate(q_hi, axis=0)

        lc = _dot_nt(q, kc_ref[...]) + bc_ref[...].reshape(R, LANE)
        pc = jnp.exp2(lc - jnp.max(lc, axis=-1, keepdims=True))
        row = lax.broadcasted_iota(jnp.int32, (Hg, T, 1), 1).reshape(R, 1)
        visible = (c * T + row >= CMP_BLOCK - 1).astype(F32)
        pc = pc * (visible / jnp.sum(pc, axis=-1, keepdims=True))
        o_cmp = jnp.dot(pc.astype(BF16), vc_ref[...], preferred_element_type=F32)

        mask = selection_mask(c, pc) if c >= n_all else None
        chunks = _key_chunks(c)
        carry = _softmax_init_ones(R, 2 * dh)
        for start, width, table in chunks:
            rows = slice(start, start + width)
            s = _dot_nt(q, kv_ref[rows, 0:2 * dh]).reshape(Hg, T, width)
            if table is not None:
                s = s + nb_ref[table[0], :, :, table[1]:table[1] + width]
            if mask is not None:
                s = s + mask[:, rows][None]
            carry = _softmax_step_ones(s.reshape(R, width), values(rows, True), carry)
        acc_s = carry[1]
        o_slc = acc_s[:, :dh] / acc_s[:, dh:]

        start, width, table = chunks[0]
        win_chunks = [(start, width, nb_ref[table[0], :, :, table[1]:table[1] + width])]
        if c >= 2:
            win_chunks.append(((c - 2) * T, T, wm_ref[...][None]))
        carry = _softmax_init_ones(R, 2 * dh)
        for start, width, bias in win_chunks:
            rows = slice(start, start + width)
            s = _dot_nt(q_win, kv_ref[rows, 0:2 * dh]).reshape(Hg, T, width) + bias
            carry = _softmax_step_ones(s.reshape(R, width), values(rows, False), carry)
        acc_w = carry[1]
        o_win = acc_w[:, dh:] / acc_w[:, :dh]

        gates = jax.nn.sigmoid(gate_ref[...].astype(F32))
        g_hi = gates.astype(BF16)
        g_lo = (gates - g_hi.astype(F32)).astype(BF16)
        g_wide = (jnp.dot(g_hi, gx_ref[...], preferred_element_type=F32)
                  + jnp.dot(g_lo, gx_ref[...], preferred_element_type=F32))
        by_head = lambda o: jnp.concatenate([o[h * T:(h + 1) * T] for h in range(Hg)], axis=1)
        w = Hg * dh
        out = (g_wide[:, 0:w] * by_head(o_cmp) + g_wide[:, w:2 * w] * by_head(o_slc)
               + g_wide[:, 2 * w:3 * w] * by_head(o_win))
        o_ref[...] = out.astype(BF16)

    qi = pl.program_id(2)
    for c in range(seq_len // T):
        pl.when(qi == c)(functools.partial(tile, c))


def _nsa_attention(proj, k_cmp, v_cmp, bias_cmp, near_bias, win_mask, overlap, expand):
    B, S, _ = proj.shape
    G, Hg, dh, T = NSA_KV_GROUPS, NSA_HPG, HEAD_DIM, TQ
    assert WINDOW == 2 * T and S % (2 * T) == 0
    n_slc = S // SLC_BLOCK
    gate_expand = np.zeros((LANE, 3 * Hg * dh), np.float32)
    for j in range(3 * Hg):
        gate_expand[j, j * dh:(j + 1) * dh] = 1.0
    gate_expand = jnp.asarray(gate_expand, BF16)
    return pl.pallas_call(
        functools.partial(_nsa_kernel, seq_len=S),
        grid=(B, G, S // T),
        in_specs=[
            pl.BlockSpec((None, T, Hg * dh), lambda b, g, i: (b, i, C_QN // (Hg * dh) + g)),
            pl.BlockSpec((None, S, 4 * dh), lambda b, g, i: (b, 0, C_KV // (4 * dh) + g)),
            pl.BlockSpec((None, None, LANE, LANE), lambda b, g, i: (b, g, 0, 0)),
            pl.BlockSpec((None, None, LANE, dh), lambda b, g, i: (b, g, 0, 0)),
            pl.BlockSpec((None, T, LANE), lambda b, g, i: (b, i, C_GATE // LANE + g)),
            pl.BlockSpec((None, Hg, T, LANE), lambda b, g, i: (g, 0, i, 0)),
            pl.BlockSpec((None, 2, Hg, T, 2 * T), lambda b, g, i: (g, 0, 0, 0, 0)),
            pl.BlockSpec((T, T), lambda b, g, i: (0, 0)),
            pl.BlockSpec((n_slc, LANE), lambda b, g, i: (0, 0)),
            pl.BlockSpec((n_slc, S), lambda b, g, i: (0, 0)),
            pl.BlockSpec(gate_expand.shape, lambda b, g, i: (0, 0)),
        ],
        out_specs=pl.BlockSpec((None, T, Hg * dh), lambda b, g, i: (b, i, g)),
        out_shape=jax.ShapeDtypeStruct((B, S, NSA_HEADS * dh), BF16),
        compiler_params=_cparams(("arbitrary", "arbitrary", "arbitrary")),
        name="nsa_attention",
    )(proj, proj, k_cmp, v_cmp, proj, bias_cmp, near_bias, win_mask, overlap, expand, gate_expand)


def _diff_kernel(q_ref, k_ref, v_ref, lam_ref, og_ref, nb_ref, o_ref, *, lambda_init, n_tiles):
    T, dh = TQ, HEAD_DIM
    qi = pl.program_id(2)
    q = q_ref[...]
    lower = lax.broadcasted_iota(jnp.int32, q.shape, 1) < dh
    zero = jnp.zeros_like(q)
    q12 = jnp.concatenate([jnp.where(lower, q, zero), jnp.where(lower, zero, q)], axis=0)

    def step(start, width, bias, carry):
        s = _dot_nt(q12, k_ref[start:start + width, :])
        if bias is not None:
            s = (s.reshape(2, T, width) + bias[None]).reshape(2 * T, width)
        return _softmax_step(s, v_ref[start:start + width, :], carry)

    lam_p = lam_ref[...]
    lam = (jnp.exp(jnp.sum(lam_p[0:1] * lam_p[1:2], axis=-1, keepdims=True))
           - jnp.exp(jnp.sum(lam_p[2:3] * lam_p[3:4], axis=-1, keepdims=True)) + lambda_init)

    def tile(c):
        carry = _softmax_init(2 * T, DIFF_V_DIM)
        for start, width, table in _key_chunks(c):
            bias = None if table is None else nb_ref[table[0], :, table[1]:table[1] + width]
            carry = step(start, width, bias, carry)
        a = carry[2] / carry[1]
        o = a[:T] - lam * a[T:]
        o_ref[...] = (_rms(o, og_ref[...]) * (1.0 - lambda_init)).astype(BF16)

    for c in range(n_tiles):
        pl.when(qi == c)(functools.partial(tile, c))


def _diff_attention(proj, lam_p, out_gain, near_bias, lambda_init):
    B, S, _ = proj.shape
    T, dh, H = TQ, HEAD_DIM, DIFF_HEADS
    return pl.pallas_call(
        functools.partial(_diff_kernel, lambda_init=lambda_init, n_tiles=S // T),
        grid=(B, H, S // T),
        in_specs=[
            pl.BlockSpec((None, T, 2 * dh), lambda b, h, i: (b, i, C_QD // (2 * dh) + h)),
            pl.BlockSpec((None, S, 2 * dh), lambda b, h, i: (b, 0, C_KD // (2 * dh) + h)),
            pl.BlockSpec((None, S, DIFF_V_DIM), lambda b, h, i: (b, 0, C_VD // DIFF_V_DIM + h)),
            pl.BlockSpec((4, dh), lambda b, h, i: (0, 0)),
            pl.BlockSpec((1, DIFF_V_DIM), lambda b, h, i: (0, 0)),
            pl.BlockSpec((None, 2, T, 2 * T), lambda b, h, i: (h, 0, 0, 0)),
        ],
        out_specs=pl.BlockSpec((None, T, DIFF_V_DIM), lambda b, h, i: (b, i, h)),
        out_shape=jax.ShapeDtypeStruct((B, S, H * DIFF_V_DIM), BF16),
        compiler_params=_cparams(("arbitrary", "arbitrary", "arbitrary")),
        name="diff_attention",
    )(proj, proj, proj, lam_p, out_gain, near_bias)


def _merge_kernel(x_ref, on_ref, od_ref, gm_ref, wn_ref, wd_ref, wo_ref, g2_ref, rw_ref, rb_ref, o_ref):
    tm, d = x_ref.shape
    a = jnp.dot(on_ref[...], wn_ref[...], preferred_element_type=F32)
    b = jnp.dot(od_ref[...], wd_ref[...], preferred_element_type=F32)
    sigmoid = lambda z: 0.5 * jnp.tanh(0.5 * z) + 0.5
    y = sigmoid(gm_ref[:, :d].astype(F32)) * a + sigmoid(gm_ref[:, d:].astype(F32)) * b
    x2 = x_ref[...] + jnp.dot(y.astype(BF16), wo_ref[...], preferred_element_type=F32)
    o_ref[:, :d] = x2

    h2 = _rms(x2, g2_ref[...])
    h_hi = h2.astype(BF16)
    h_lo = (h2 - h_hi.astype(F32)).astype(BF16)
    part = jnp.dot(jnp.concatenate([h_hi, h_lo], axis=0), rw_ref[...], preferred_element_type=F32)
    logits = part[:tm, :LANE] + part[:tm, LANE:] + part[tm:, :LANE] + part[tm:, LANE:] + rb_ref[...]
    lane = lax.broadcasted_iota(jnp.int32, logits.shape, 1)
    big = jnp.int32(LANE)

    def first_argmax(vals, mask):
        vals = jnp.where(mask, vals, -jnp.inf)
        top = jnp.max(vals, axis=-1, keepdims=True)
        idx = jnp.min(jnp.where(vals == top, lane, big), axis=-1, keepdims=True)
        return top, idx

    is_group = lane < N_GROUPS
    g_max, g_top = first_argmax(logits, is_group)
    g_top_p = 1.0 / jnp.sum(jnp.where(is_group, jnp.exp(logits - g_max), 0.0), axis=-1, keepdims=True)
    lo = N_GROUPS + g_top * EXPERTS_PER_GROUP
    in_group = (lane >= lo) & (lane < lo + EXPERTS_PER_GROUP)
    m1, i1 = first_argmax(logits, in_group)
    m2, i2 = first_argmax(logits, in_group & (lane != i1))
    e2 = jnp.exp(m2 - m1)
    w1 = g_top_p / (1.0 + e2)
    w2 = g_top_p * e2 / (1.0 + e2)
    la = jnp.minimum(i1, i2) - lo
    lb = jnp.maximum(i1, i2) - lo
    pair = jnp.right_shift(la * (7 - la), 1) + (lb - la - 1)
    bucket = (g_top * N_PAIRS + pair).astype(F32)
    exp_lane = lane + N_GROUPS
    route = (jnp.where(exp_lane == i1, w1, 0.0) + jnp.where(exp_lane == i2, w2, 0.0)
             + jnp.where(lane == N_EXPERTS, bucket, 0.0))
    o_ref[:, d:] = route


def _merge(x2d, o_nsa, o_diff, proj2d, wn, wd, wo, g2, rw, rb):
    n, d = x2d.shape
    tm = 512
    rw_hi = rw.astype(BF16)
    rw_cat = jnp.concatenate([rw_hi, (rw - rw_hi.astype(F32)).astype(BF16)], axis=1)
    full = lambda a: pl.BlockSpec(a.shape, lambda i: (0,) * a.ndim)
    return pl.pallas_call(
        _merge_kernel,
        grid=(n // tm,),
        in_specs=[pl.BlockSpec((tm, d), lambda i: (i, 0)),
                  pl.BlockSpec((tm, o_nsa.shape[1]), lambda i: (i, 0)),
                  pl.BlockSpec((tm, o_diff.shape[1]), lambda i: (i, 0)),
                  pl.BlockSpec((tm, 2 * d), lambda i: (i, C_GM // (2 * d))),
                  full(wn), full(wd), full(wo), full(g2), full(rw_cat), full(rb)],
        out_specs=pl.BlockSpec((tm, d + LANE), lambda i: (i, 0)),
        out_shape=jax.ShapeDtypeStruct((n, d + LANE), F32),
        compiler_params=_cparams(("arbitrary",)),
        name="merge_router",
    )(x2d, o_nsa, o_diff, proj2d, wn, wd, wo, g2, rw_cat, rb)


N_SLOT = 3


def _moe_kernel(tile_ref, ea_ref, eb_ref, lo_ref, hi_ref, first_ref, perm_ref,
                x_hbm, g2_ref, wga_ref, wua_ref, wda_ref, wgb_ref, wub_ref, wdb_ref,
                o_hbm, xb0, xb1, xb2, ac0, ac1, ac2, h_scr, g_sem, s_sem, *, tm, n_tiles):
    d = o_hbm.shape[1]
    v = pl.program_id(0)
    t = tile_ref[v]
    x_bufs, acc_bufs = (xb0, xb1, xb2), (ac0, ac1, ac2)
    last = n_tiles - 1

    def gather_copy(tt, sl, r):
        return pltpu.make_async_copy(x_hbm.at[pl.ds(perm_ref[tt * tm + r], 1), :],
                                     x_bufs[sl].at[pl.ds(r, 1), :], g_sem.at[sl])

    def scatter_copy(tt, sl, r):
        return pltpu.make_async_copy(acc_bufs[sl].at[pl.ds(r, 1), :],
                                     o_hbm.at[pl.ds(perm_ref[tt * tm + r], 1), :], s_sem.at[sl])

    def for_rows(fn):
        for r in range(tm):
            fn(r)

    def experts(sl, dma_starts):
        n_stage = 6
        per_stage = -(-len(dma_starts) // n_stage)

        def issue(stage):
            for start in dma_starts[stage * per_stage:(stage + 1) * per_stage]:
                start()

        h = h_scr[...]
        cw = x_bufs[sl][:, d:]
        lane = lax.broadcasted_iota(jnp.int32, cw.shape, 1)
        row = lax.broadcasted_iota(jnp.int32, (tm, 1), 0)
        in_bucket = (row >= lo_ref[v]) & (row < hi_ref[v])
        out = None
        for i, (e, wg_ref, wu_ref, wd_ref) in enumerate(((ea_ref[v], wga_ref, wua_ref, wda_ref),
                                                         (eb_ref[v], wgb_ref, wub_ref, wdb_ref))):
            w = jnp.sum(jnp.where(lane == e, cw, 0.0), axis=-1, keepdims=True)
            w = jnp.where(in_bucket, w, 0.0)
            gate = jnp.dot(h, wg_ref[...], preferred_element_type=F32)
            issue(3 * i)
            up = jnp.dot(h, wu_ref[...], preferred_element_type=F32)
            issue(3 * i + 1)
            hid = jax.nn.silu(gate) * up * w
            y = jnp.dot(hid.astype(BF16), wd_ref[...], preferred_element_type=F32)
            issue(3 * i + 2)
            out = y if out is None else out + y
        acc_bufs[sl][...] += out

    def first_visit(sl, scatter_previous):
        ahead, behind = (sl + 2) % N_SLOT, (sl - 1) % N_SLOT
        for_rows(lambda r: gather_copy(t, sl, r).wait())

        @pl.when(t >= N_SLOT)
        def _():
            for_rows(lambda r: scatter_copy(t - N_SLOT, sl, r).wait())

        xs = x_bufs[sl][:, :d]
        h_scr[...] = _rms(xs, g2_ref[...]).astype(BF16)
        acc_bufs[sl][...] = xs
        nxt = jnp.minimum(t + 2, last)
        starts = [functools.partial(lambda r: gather_copy(nxt, ahead, r).start(), r) for r in range(tm)]
        if scatter_previous:
            starts += [functools.partial(lambda r: scatter_copy(t - 1, behind, r).start(), r)
                       for r in range(tm)]
        experts(sl, starts)

    @pl.when(v == 0)
    def _():
        for_rows(lambda r: gather_copy(0, 0, r).start())
        for_rows(lambda r: gather_copy(1, 1, r).start())
        first_visit(0, scatter_previous=False)

    for sl in range(N_SLOT):
        pl.when((v > 0) & (first_ref[v] == 1) & (t % N_SLOT == sl))(
            functools.partial(first_visit, sl, scatter_previous=True))
        pl.when((first_ref[v] == 0) & (t % N_SLOT == sl))(functools.partial(experts, sl, []))

    @pl.when(v == pl.num_programs(0) - 1)
    def _():
        for_rows(lambda r: scatter_copy(last, last % N_SLOT, r).start())
        for tt in (last - 1, last):
            for_rows(lambda r: gather_copy(last, (tt + 2) % N_SLOT, r).wait())
        for tt in (last - 2, last - 1, last):
            for_rows(lambda r: scatter_copy(tt, tt % N_SLOT, r).wait())


def _moe(sched, perm, x2r, g2, wg, wu, wd, tm):
    n = x2r.shape[0]
    d, f = wg.shape[1], wg.shape[2]
    n_visits = sched[0].shape[0]
    n_tiles = n // tm
    assert n_tiles > N_SLOT
    w_in_a = pl.BlockSpec((None, d, f), lambda v, t, ea, eb, *_: (ea[v], 0, 0))
    w_in_b = pl.BlockSpec((None, d, f), lambda v, t, ea, eb, *_: (eb[v], 0, 0))
    w_out_a = pl.BlockSpec((None, f, d), lambda v, t, ea, eb, *_: (ea[v], 0, 0))
    w_out_b = pl.BlockSpec((None, f, d), lambda v, t, ea, eb, *_: (eb[v], 0, 0))
    return pl.pallas_call(
        functools.partial(_moe_kernel, tm=tm, n_tiles=n_tiles),
        grid_spec=pltpu.PrefetchScalarGridSpec(
            num_scalar_prefetch=7,
            grid=(n_visits,),
            in_specs=[pl.BlockSpec(memory_space=pl.ANY),
                      pl.BlockSpec((1, d), lambda v, *_: (0, 0)),
                      w_in_a, w_in_a, w_out_a, w_in_b, w_in_b, w_out_b],
            out_specs=pl.BlockSpec(memory_space=pl.ANY),
            scratch_shapes=([pltpu.VMEM((tm, x2r.shape[1]), F32)] * N_SLOT + [pltpu.VMEM((tm, d), F32)] * N_SLOT
                            + [pltpu.VMEM((tm, d), BF16),
                               pltpu.SemaphoreType.DMA((N_SLOT,)), pltpu.SemaphoreType.DMA((N_SLOT,))])),
        out_shape=jax.ShapeDtypeStruct((n, d), F32),
        compiler_params=_cparams(("arbitrary",)),
        name="moe_experts",
    )(*sched, perm, x2r, g2, wg, wu, wd, wg, wu, wd)


def _moe_schedule(bucket, tm):
    n = bucket.shape[0]
    n_tiles = n // tm
    n_visits = n_tiles + N_BUCKETS - 1
    counts = jnp.sum((bucket[:, None] == jnp.arange(N_BUCKETS)[None, :]).astype(jnp.int32), axis=0)
    ends = jnp.cumsum(counts)
    starts = ends - counts
    first_tile = starts // tm
    last_tile = jnp.maximum(ends - 1, 0) // tm
    n_vis = jnp.where(counts > 0, last_tile - first_tile + 1, 0)
    vis_end = jnp.cumsum(n_vis)
    vis_start = vis_end - n_vis
    v = jnp.arange(n_visits, dtype=jnp.int32)
    b = jnp.minimum(jnp.searchsorted(vis_end, v, side="right"), N_BUCKETS - 1).astype(jnp.int32)
    live = v < vis_end[-1]
    tile = jnp.where(live, first_tile[b] + v - vis_start[b], n_tiles - 1).astype(jnp.int32)
    lo = jnp.where(live, jnp.clip(starts[b] - tile * tm, 0, tm), 0).astype(jnp.int32)
    hi = jnp.where(live, jnp.clip(ends[b] - tile * tm, 0, tm), 0).astype(jnp.int32)
    b_last = b[jnp.maximum(vis_end[-1] - 1, 0)]
    b = jnp.where(live, b, b_last)
    pair_a = np.array([0, 0, 0, 1, 1, 2], np.int32)
    pair_b = np.array([1, 2, 3, 2, 3, 3], np.int32)
    grp = b // N_PAIRS
    ea = (grp * EXPERTS_PER_GROUP + jnp.asarray(pair_a)[b % N_PAIRS]).astype(jnp.int32)
    eb = (grp * EXPERTS_PER_GROUP + jnp.asarray(pair_b)[b % N_PAIRS]).astype(jnp.int32)
    first = jnp.concatenate([jnp.ones((1,), jnp.int32), (tile[1:] != tile[:-1]).astype(jnp.int32)])
    return tile, ea, eb, lo, hi, first


def _layer(x, layer, p):
    B, S, D = x.shape
    G, dh = NSA_KV_GROUPS, HEAD_DIM
    lambda_init = 0.8 - 0.6 * math.exp(-0.3 * layer)
    x2d = x.reshape(B * S, D)

    proj2d = _in_proj(x2d, p["norm1_g"][None, :], _reorder_w_in(p["w_in"]), _head_norm_rows(p))
    proj = proj2d.reshape(B, S, C_TOTAL)

    k_cmp, v_cmp = _compress(proj, *_compress_params(p["cmp_pos"], p["cmp_w1"], p["cmp_w2"]),
                             p["nsa_k_norm"][0:1])

    near, bias_cmp = _bias_tables(p["rel_bias"], S)
    nsa_near = near[:NSA_HEADS].reshape(G, NSA_HPG, 2, TQ, 2 * TQ).transpose(0, 2, 1, 3, 4)
    i = np.arange(TQ)
    win_mask = jnp.asarray(np.where(i[None, :] > i[:, None], 0.0, NEG).astype(np.float32))
    expand = jnp.asarray(np.repeat(np.eye(S // SLC_BLOCK, dtype=np.float32), SLC_BLOCK, axis=1), BF16)
    o_nsa = _nsa_attention(proj, k_cmp, v_cmp, bias_cmp.reshape(G, NSA_HPG, S, LANE), nsa_near, win_mask,
                           jnp.asarray(_overlap_matrix(S)), expand)
    o_diff = _diff_attention(proj, p["diff_lambda"], p["diff_out_norm"][None, :], near[NSA_HEADS:],
                             lambda_init)

    rw = jnp.concatenate([p["router_group_w"], p["router_expert_w"],
                          jnp.zeros((D, LANE - N_GROUPS - N_EXPERTS), F32)], axis=1)
    rb = jnp.concatenate([p["router_group_b"], p["router_expert_b"],
                          jnp.zeros((LANE - N_GROUPS - N_EXPERTS,), F32)])[None, :]
    x2r = _merge(x2d, o_nsa.reshape(B * S, -1), o_diff.reshape(B * S, -1), proj2d,
                 p["w_branch_nsa"].astype(BF16), p["w_branch_diff"].astype(BF16),
                 p["w_out"].astype(BF16), p["norm2_g"][None, :], rw, rb)

    tm = 256
    bucket = x2r[:, D + N_EXPERTS].astype(jnp.int32)
    perm = jnp.argsort(bucket).astype(jnp.int32)
    out = _moe(_moe_schedule(bucket, tm), perm, x2r, p["norm2_g"][None, :],
               p["expert_w_gate"].astype(BF16), p["expert_w_up"].astype(BF16),
               p["expert_w_down"].astype(BF16), tm)
    return out.reshape(B, S, D)


def kernel(x, norm1_g, w_in, nsa_q_norm, nsa_k_norm, cmp_pos, cmp_w1, cmp_w2, diff_q_norm, diff_k_norm,
           diff_lambda, diff_out_norm, rel_bias, w_branch_nsa, w_branch_diff, w_out, norm2_g,
           router_group_w, router_group_b, router_expert_w, router_expert_b, expert_w_gate,
           expert_w_up, expert_w_down):
    per_layer = dict(norm1_g=norm1_g, w_in=w_in, nsa_q_norm=nsa_q_norm, nsa_k_norm=nsa_k_norm,
                     cmp_pos=cmp_pos, cmp_w1=cmp_w1, cmp_w2=cmp_w2, diff_q_norm=diff_q_norm,
                     diff_k_norm=diff_k_norm, diff_lambda=diff_lambda, diff_out_norm=diff_out_norm,
                     w_branch_nsa=w_branch_nsa, w_branch_diff=w_branch_diff, w_out=w_out,
                     norm2_g=norm2_g, router_group_w=router_group_w, router_group_b=router_group_b,
                     router_expert_w=router_expert_w, router_expert_b=router_expert_b,
                     expert_w_gate=expert_w_gate, expert_w_up=expert_w_up, expert_w_down=expert_w_down)
    for layer in range(norm1_g.shape[0]):
        p = {k: v[layer] for k, v in per_layer.items()}
        p["rel_bias"] = rel_bias
        x = _layer(x, layer, p)
    return x
```

```python
import functools
import math

import numpy as np
import jax
import jax.numpy as jnp
from jax import lax
from jax.experimental import pallas as pl
from jax.experimental.pallas import tpu as pltpu

F32 = jnp.float32
BF16 = jnp.bfloat16

HEAD_DIM = 64
SCALE = HEAD_DIM ** -0.5
NSA_HEADS = 8
NSA_KV_GROUPS = 2
NSA_HPG = NSA_HEADS // NSA_KV_GROUPS
CMP_BLOCK = 32
CMP_STRIDE = 16
CMP_HIDDEN = 256
SLC_BLOCK = 64
SLC_TOP_N = 16
WINDOW = 512
FORCE_SCORE = 1e4
DIFF_HEADS = 4
DIFF_V_DIM = 2 * HEAD_DIM
REL_BUCKETS = 32
REL_MAX_DIST = 128
N_GROUPS = 4
EXPERTS_PER_GROUP = 4
N_EXPERTS = N_GROUPS * EXPERTS_PER_GROUP
EXPERT_HIDDEN = 512
RMS_EPS = 1e-6
NEG = -1e30
LOG2E = math.log2(math.e)

LANE = 128
MXU_DIM = 256
TQ = 256
N_PAIRS = 6
N_BUCKETS = N_GROUPS * N_PAIRS

C_GM = 0
C_QN = 2048
C_KV = 2560
C_CMP = 3072
C_GATE = 3328
C_QD = 3584
C_KD = 4096
C_VD = 4608
C_TOTAL = 5120

V7X_VMEM_BYTES = 64 * 1024 * 1024
VMEM_LIMIT = V7X_VMEM_BYTES * 7 // 8


def _cparams(sem):
    return pltpu.CompilerParams(dimension_semantics=sem, vmem_limit_bytes=VMEM_LIMIT)


def _rms(x, gain):
    return x * lax.rsqrt(jnp.mean(x * x, axis=-1, keepdims=True) + RMS_EPS) * gain


def _reorder_w_in(w_in):
    d_model = w_in.shape[0]
    w_in = w_in.astype(BF16)
    dh, G, Hg = HEAD_DIM, NSA_KV_GROUPS, NSA_HPG
    o_q = 0
    o_kv = o_q + NSA_HEADS * dh
    o_g = o_kv + 3 * 2 * G * dh
    o_qd = o_g + 3 * NSA_HEADS
    o_gm = o_qd + 3 * DIFF_HEADS * 2 * dh

    def cols(start, n):
        return w_in[:, start:start + n]

    def kv(kind, which, g):
        return cols(o_kv + ((kind * 2 + which) * G + g) * dh, dh)

    parts = [cols(o_gm, 2 * d_model), cols(o_q, NSA_HEADS * dh)]
    parts += [kv(kind, which, g) for g in range(G) for which in (0, 1) for kind in (1, 2)]
    parts += [kv(0, which, g) for g in range(G) for which in (0, 1)]
    for g in range(G):
        parts += [cols(o_g + (kind * G + g) * Hg, Hg) for kind in range(3)]
        parts.append(jnp.zeros((d_model, LANE - 3 * Hg), w_in.dtype))
    parts.append(cols(o_qd, 3 * DIFF_HEADS * 2 * dh))
    w = jnp.concatenate(parts, axis=1)
    assert w.shape[1] == C_TOTAL
    return w


def _head_norm_rows(p):
    dh, G = HEAD_DIM, NSA_KV_GROUPS
    qs = SCALE * LOG2E
    zeros = lambda n: jnp.zeros((n,), F32)
    kn = p["nsa_k_norm"]
    parts = [zeros(C_QN - C_GM), jnp.tile(p["nsa_q_norm"] * qs, NSA_HEADS)]
    parts += [jnp.concatenate([kn[1], kn[2], zeros(2 * dh)])] * G
    parts += [zeros(C_QD - C_CMP), jnp.tile((p["diff_q_norm"] * qs).reshape(-1), DIFF_HEADS),
              jnp.tile(p["diff_k_norm"].reshape(-1), DIFF_HEADS), zeros(C_TOTAL - C_VD)]
    gain = jnp.concatenate(parts)
    assert gain.shape[0] == C_TOTAL
    return gain[None, :]


def _t5_bucket(dist):
    n = jnp.maximum(dist, 0)
    max_exact = REL_BUCKETS // 2
    nf = jnp.maximum(n, 1).astype(F32)
    large = max_exact + (jnp.log(nf / max_exact) / math.log(REL_MAX_DIST / max_exact)
                         * (REL_BUCKETS - max_exact)).astype(jnp.int32)
    large = jnp.minimum(large, REL_BUCKETS - 1)
    return jnp.where(n < max_exact, n, large)


def _toeplitz(v_ext, rows, cols):
    length = v_ext.shape[-1]
    lead = v_ext.shape[:-1]
    flat = jnp.broadcast_to(v_ext[..., None, :], lead + (rows, length)).reshape(lead + (rows * length,))
    return flat[..., :rows * (length - 1)].reshape(lead + (rows, length - 1))[..., :cols]


def _bias_tables(rel_bias, S):
    heads = rel_bias.shape[1]
    far = REL_MAX_DIST
    assert TQ == 2 * far and LANE == far
    bd = rel_bias[_t5_bucket(jnp.arange(2 * far))].T.astype(F32) * LOG2E
    rel = bd - bd[:, far:far + 1]
    neg = lambda *shape: jnp.full((heads,) + shape, NEG, F32)
    zero = lambda *shape: jnp.zeros((heads,) + shape, F32)
    rev = lambda a: a[:, ::-1]
    d0 = _toeplitz(jnp.concatenate([rel[:, 0:1], neg(far), rev(rel[:, 1:far])], axis=1), far, far)
    d1 = _toeplitz(jnp.concatenate([rev(rel[:, 1:far + 1]), zero(far)], axis=1), far, far)
    z, n = zero(far, far), neg(far, far)
    blocks = lambda rows: jnp.concatenate([jnp.concatenate(r, axis=2) for r in rows], axis=1)
    near = jnp.stack([blocks([[d0, n, n, n], [d1, d0, n, n]]),
                      blocks([[z, d1, d0, n], [z, z, d1, d0]])], axis=1)
    hb = rel[:NSA_HEADS]
    zero8 = lambda *shape: jnp.zeros((NSA_HEADS,) + shape, F32)
    neg8 = lambda *shape: jnp.full((NSA_HEADS,) + shape, NEG, F32)
    width = TQ + far - 1
    tp = _toeplitz(jnp.concatenate([rev(hb[:, 0:far]), neg8(TQ + 1), zero8(TQ - 1)], axis=1), TQ, width)
    n_tiles = S // TQ
    per_tile = TQ // CMP_STRIDE
    col0 = per_tile * (n_tiles - 1)
    shift = CMP_STRIDE * col0 - (CMP_BLOCK - 1)
    first = -(-(shift - (far - 1)) // CMP_STRIDE)
    last = (shift + TQ - 1) // CMP_STRIDE
    band = tp[:, :, CMP_STRIDE * first - shift + far - 1::CMP_STRIDE][:, :, :last - first + 1]
    g2 = jnp.concatenate([zero8(TQ, first), band, neg8(TQ, 2 * TQ - last - 1)], axis=2)
    bias_cmp = jnp.stack([g2[:, :, col0 - per_tile * i:col0 - per_tile * i + LANE] for i in range(n_tiles)],
                         axis=1)
    n_cmp = (S - CMP_BLOCK) // CMP_STRIDE + 1
    bias_cmp = jnp.where(np.arange(LANE) < n_cmp, bias_cmp, NEG)
    return near, bias_cmp.reshape(NSA_HEADS, S, LANE)


def _overlap_matrix(S):
    n_cmp = (S - CMP_BLOCK) // CMP_STRIDE + 1
    n_slc = S // SLC_BLOCK
    c_start = np.arange(n_cmp) * CMP_STRIDE
    s_start = np.arange(n_slc) * SLC_BLOCK
    lo = np.maximum(s_start[:, None], c_start[None, :])
    hi = np.minimum(s_start[:, None] + SLC_BLOCK, c_start[None, :] + CMP_BLOCK)
    overlap = (np.maximum(hi - lo, 0) / CMP_BLOCK).astype(np.float32)
    out = np.zeros((n_slc, LANE), np.float32)
    out[:, :n_cmp] = overlap
    return out


IN_CHUNK = 512
HEAD_NORM_CHUNKS = {C_QN // IN_CHUNK: "all", C_KV // IN_CHUNK: "keys",
                    C_QD // IN_CHUNK: "all", C_KD // IN_CHUNK: "all"}


def _in_proj_kernel(x_ref, g_ref, w_ref, hg_ref, seg_ref, o_ref):
    dh = HEAD_DIM
    h = _rms(x_ref[...], g_ref[...]).astype(BF16)
    for c in range(C_TOTAL // IN_CHUNK):
        cols = slice(c * IN_CHUNK, (c + 1) * IN_CHUNK)
        y = jnp.dot(h, w_ref[:, cols], preferred_element_type=F32)
        mode = HEAD_NORM_CHUNKS.get(c)
        if mode is not None:
            sq = (y * y).astype(BF16)
            seg_w = seg_ref.shape[0]
            ss = jnp.concatenate([jnp.dot(sq[:, j:j + seg_w], seg_ref[...], preferred_element_type=F32)
                                  for j in range(0, IN_CHUNK, seg_w)], axis=1)
            scale = lax.rsqrt(ss * (1.0 / dh) + RMS_EPS) * hg_ref[:, cols]
            if mode == "keys":
                lane = lax.broadcasted_iota(jnp.int32, (1, IN_CHUNK), 1)
                scale = jnp.where(lane % (4 * dh) < 2 * dh, scale, 1.0)
            y = y * scale
        o_ref[:, cols] = y.astype(BF16)


def _in_proj(x2d, gain, w, head_gain):
    n, d = x2d.shape
    tm = 512
    seg = np.arange(MXU_DIM) // HEAD_DIM
    seg_ones = jnp.asarray((seg[:, None] == seg[None, :]).astype(np.float32), BF16)
    return pl.pallas_call(
        _in_proj_kernel,
        grid=(n // tm,),
        in_specs=[pl.BlockSpec((tm, d), lambda i: (i, 0)),
                  pl.BlockSpec((1, d), lambda i: (0, 0)),
                  pl.BlockSpec((d, C_TOTAL), lambda i: (0, 0)),
                  pl.BlockSpec((1, C_TOTAL), lambda i: (0, 0)),
                  pl.BlockSpec((MXU_DIM, MXU_DIM), lambda i: (0, 0))],
        out_specs=pl.BlockSpec((tm, C_TOTAL), lambda i: (i, 0)),
        out_shape=jax.ShapeDtypeStruct((n, C_TOTAL), BF16),
        compiler_params=_cparams(("arbitrary",)),
        name="in_proj",
    )(x2d, gain, w, head_gain, seg_ones)


def _compress_kernel(t_ref, pos_ref, w1_ref, w2_ref, kg_ref, k_ref, v_ref, tok_scr):
    S, dh = t_ref.shape[0], HEAD_DIM
    m = S // CMP_STRIDE
    tok_scr[0:S, :] = t_ref[...].astype(F32)
    tok_scr[S:, :] = jnp.zeros((CMP_STRIDE, LANE), F32)
    pre = jnp.zeros((m, 2 * CMP_HIDDEN), F32)
    for l in range(CMP_BLOCK):
        x = tok_scr[pl.ds(l, m, stride=CMP_STRIDE), :] + pos_ref[l:l + 1, :]
        pre = pre + jnp.dot(x.astype(BF16), w1_ref[l], preferred_element_type=F32)
    out = jnp.dot(jax.nn.gelu(pre).astype(BF16), w2_ref[...], preferred_element_type=F32)
    k_ref[...] = jnp.concatenate([_rms(out[:, :dh], kg_ref[...]), jnp.zeros((m, LANE - dh), F32)],
                                 axis=1).astype(BF16)
    v_ref[...] = out[:, dh:].astype(BF16)


def _compress(proj, pos, w1, w2, k_gain):
    B, S, _ = proj.shape
    G, m = NSA_KV_GROUPS, S // CMP_STRIDE
    outs = [jax.ShapeDtypeStruct((B, G, m, width), BF16) for width in (LANE, HEAD_DIM)]
    ospecs = [pl.BlockSpec((None, None, m, width), lambda b, g: (b, g, 0, 0)) for width in (LANE, HEAD_DIM)]
    return pl.pallas_call(
        _compress_kernel,
        grid=(B, G),
        in_specs=[pl.BlockSpec((None, S, LANE), lambda b, g: (b, 0, C_CMP // LANE + g)),
                  pl.BlockSpec(pos.shape, lambda b, g: (0, 0)),
                  pl.BlockSpec(w1.shape, lambda b, g: (0, 0, 0)),
                  pl.BlockSpec(w2.shape, lambda b, g: (0, 0)),
                  pl.BlockSpec((1, HEAD_DIM), lambda b, g: (0, 0))],
        out_specs=ospecs,
        out_shape=outs,
        scratch_shapes=[pltpu.VMEM((S + CMP_STRIDE, LANE), F32)],
        compiler_params=_cparams(("arbitrary", "arbitrary")),
        name="nsa_compress",
    )(proj, pos, w1, w2, k_gain)


def _compress_params(cmp_pos, cmp_w1, cmp_w2):
    dh, hid = HEAD_DIM, CMP_HIDDEN
    w1 = cmp_w1.reshape(2, CMP_BLOCK, dh, hid)
    z1 = jnp.zeros((CMP_BLOCK, dh, hid), cmp_w1.dtype)
    w1 = jnp.concatenate([jnp.concatenate([w1[0], z1], axis=2),
                          jnp.concatenate([z1, w1[1]], axis=2)], axis=1)
    z2 = jnp.zeros((hid, dh), cmp_w2.dtype)
    w2 = jnp.concatenate([jnp.concatenate([cmp_w2[0], z2], axis=1),
                          jnp.concatenate([z2, cmp_w2[1]], axis=1)], axis=0)
    pos = jnp.concatenate([cmp_pos[0], cmp_pos[1]], axis=1)
    return pos, w1.astype(BF16), w2.astype(BF16)


def _dot_nt(a, b):
    return lax.dot_general(a, b, (((1,), (1,)), ((), ())), preferred_element_type=F32)


def _softmax_step(s, v, carry):
    m, l, acc = carry
    m_new = jnp.maximum(m, jnp.max(s, axis=-1, keepdims=True))
    alpha = jnp.exp2(m - m_new)
    p = jnp.exp2(s - m_new)
    l = alpha * l + jnp.sum(p, axis=-1, keepdims=True)
    acc = alpha * acc + jnp.dot(p.astype(BF16), v, preferred_element_type=F32)
    return m_new, l, acc


def _softmax_init(rows, dv):
    return (jnp.full((rows, 1), NEG, F32), jnp.zeros((rows, 1), F32), jnp.zeros((rows, dv), F32))


def _softmax_step_ones(s, v_ones, carry):
    m, acc = carry
    m_new = jnp.maximum(m, jnp.max(s, axis=-1, keepdims=True))
    p = jnp.exp2((s - m_new).astype(BF16))
    acc = jnp.exp2(m - m_new) * acc + jnp.dot(p, v_ones, preferred_element_type=F32)
    return m_new, acc


def _softmax_init_ones(rows, width):
    return jnp.full((rows, 1), NEG, F32), jnp.zeros((rows, width), F32)


def _key_chunks(c):
    T = TQ
    chunks = [(0, T, (0, 0))] if c == 0 else [((c - 1) * T, 2 * T, (1, 0))]
    n_far = max(c - 1, 0)
    if n_far % 2:
        chunks.append((0, T, None))
    chunks += [(s * T, 2 * T, None) for s in range(n_far % 2, n_far, 2)]
    return chunks


def _nsa_kernel(q_ref, kv_ref, kc_ref, vc_ref, gate_ref, bc_ref, nb_ref, wm_ref, ov_ref, ex_ref, gx_ref,
                o_ref, *, seq_len):
    T, dh, Hg = TQ, HEAD_DIM, NSA_HPG
    R = Hg * T
    n_slc = seq_len // SLC_BLOCK
    n_all = SLC_TOP_N * SLC_BLOCK // T

    def values(rows, ones_upper):
        v = kv_ref[rows, 2 * dh:4 * dh]
        keep = lax.broadcasted_iota(jnp.int32, v.shape, 1) < dh
        return jnp.where(keep if ones_upper else ~keep, v, jnp.ones_like(v))

    def selection_mask(c, pc):
        p_sum = pc[0:T] + pc[T:2 * T] + pc[2 * T:3 * T] + pc[3 * T:4 * T]
        imp = lax.dot_general(ov_ref[...], p_sum, (((1,), (1,)), ((), ())), preferred_element_type=F32,
                              precision=lax.Precision.HIGHEST)
        blk = lax.broadcasted_iota(jnp.int32, (n_slc, T), 0)
        cur = (c * T + lax.broadcasted_iota(jnp.int32, (n_slc, T), 1)) // SLC_BLOCK
        forced = (blk == 0) | (blk == cur) | (blk == cur - 1)
        score = jnp.where(blk <= cur, imp + jnp.where(forced, FORCE_SCORE, 0.0), NEG)
        rank = jnp.zeros((n_slc, T), jnp.int32)
        for i in range(n_slc):
            other = score[i:i + 1, :]
            ahead = (other > score) | ((other == score) & (blk > i))
            rank = rank + ahead.astype(jnp.int32)
        sel_bias = jnp.where(rank < SLC_TOP_N, 0.0, NEG).T.astype(BF16)
        return jnp.dot(sel_bias, ex_ref[...], preferred_element_type=F32)

    def tile(c):
        lower = lax.broadcasted_iota(jnp.int32, (T, 2 * dh), 1) < dh
        q_lo, q_hi = [], []
        for pair in range(Hg // 2):
            pq = q_ref[:, pair * 2 * dh:(pair + 1) * 2 * dh]
            sw = jnp.concatenate([pq[:, dh:], pq[:, :dh]], axis=1)
            zero = jnp.zeros_like(pq)
            q_lo += [jnp.where(lower, pq, zero), jnp.where(lower, sw, zero)]
            q_hi += [jnp.where(lower, zero, sw), jnp.where(lower, zero, pq)]
        q = jnp.concatenate(q_lo, axis=0)
        q_win = jnp.concatenate(q_hi, axis=0)

        lc = _dot_nt(q, kc_ref[...]) + bc_ref[...].reshape(R, LANE)
        pc = jnp.exp2(lc - jnp.max(lc, axis=-1, keepdims=True))
        row = lax.broadcasted_iota(jnp.int32, (Hg, T, 1), 1).reshape(R, 1)
        visible = (c * T + row >= CMP_BLOCK - 1).astype(F32)
        pc = pc * (visible / jnp.sum(pc, axis=-1, keepdims=True))
        o_cmp = jnp.dot(pc.astype(BF16), vc_ref[...], preferred_element_type=F32)

        mask = selection_mask(c, pc) if c >= n_all else None
        chunks = _key_chunks(c)
        carry = _softmax_init_ones(R, 2 * dh)
        for start, width, table in chunks:
            rows = slice(start, start + width)
            s = _dot_nt(q, kv_ref[rows, 0:2 * dh]).reshape(Hg, T, width)
            if table is not None:
                s = s + nb_ref[table[0], :, :, table[1]:table[1] + width]
            if mask is not None:
                s = s + mask[:, rows][None]
            carry = _softmax_step_ones(s.reshape(R, width), values(rows, True), carry)
        acc_s = carry[1]
        o_slc = acc_s[:, :dh] / acc_s[:, dh:]

        start, width, table = chunks[0]
        win_chunks = [(start, width, nb_ref[table[0], :, :, table[1]:table[1] + width])]
        if c >= 2:
            win_chunks.append(((c - 2) * T, T, wm_ref[...][None]))
        carry = _softmax_init_ones(R, 2 * dh)
        for start, width, bias in win_chunks:
            rows = slice(start, start + width)
            s = _dot_nt(q_win, kv_ref[rows, 0:2 * dh]).reshape(Hg, T, width) + bias
            carry = _softmax_step_ones(s.reshape(R, width), values(rows, False), carry)
        acc_w = carry[1]
        o_win = acc_w[:, dh:] / acc_w[:, :dh]

        gates = jax.nn.sigmoid(gate_ref[...].astype(F32))
        g_hi = gates.astype(BF16)
        g_lo = (gates - g_hi.astype(F32)).astype(BF16)
        g_wide = (jnp.dot(g_hi, gx_ref[...], preferred_element_type=F32)
                  + jnp.dot(g_lo, gx_ref[...], preferred_element_type=F32))
        by_head = lambda o: jnp.concatenate([o[h * T:(h + 1) * T] for h in range(Hg)], axis=1)
        w = Hg * dh
        out = (g_wide[:, 0:w] * by_head(o_cmp) + g_wide[:, w:2 * w] * by_head(o_slc)
               + g_wide[:, 2 * w:3 * w] * by_head(o_win))
        o_ref[...] = out.astype(BF16)

    qi = pl.program_id(2)
    for c in range(seq_len // T):
        pl.when(qi == c)(functools.partial(tile, c))


def _nsa_attention(proj, k_cmp, v_cmp, bias_cmp, near_bias, win_mask, overlap, expand):
    B, S, _ = proj.shape
    G, Hg, dh, T = NSA_KV_GROUPS, NSA_HPG, HEAD_DIM, TQ
    assert WINDOW == 2 * T and S % (2 * T) == 0
    n_slc = S // SLC_BLOCK
    gate_expand = np.zeros((LANE, 3 * Hg * dh), np.float32)
    for j in range(3 * Hg):
        gate_expand[j, j * dh:(j + 1) * dh] = 1.0
    gate_expand = jnp.asarray(gate_expand, BF16)
    return pl.pallas_call(
        functools.partial(_nsa_kernel, seq_len=S),
        grid=(B, G, S // T),
        in_specs=[
            pl.BlockSpec((None, T, Hg * dh), lambda b, g, i: (b, i, C_QN // (Hg * dh) + g)),
            pl.BlockSpec((None, S, 4 * dh), lambda b, g, i: (b, 0, C_KV // (4 * dh) + g)),
            pl.BlockSpec((None, None, LANE, LANE), lambda b, g, i: (b, g, 0, 0)),
            pl.BlockSpec((None, None, LANE, dh), lambda b, g, i: (b, g, 0, 0)),
            pl.BlockSpec((None, T, LANE), lambda b, g, i: (b, i, C_GATE // LANE + g)),
            pl.BlockSpec((None, Hg, T, LANE), lambda b, g, i: (g, 0, i, 0)),
            pl.BlockSpec((None, 2, Hg, T, 2 * T), lambda b, g, i: (g, 0, 0, 0, 0)),
            pl.BlockSpec((T, T), lambda b, g, i: (0, 0)),
            pl.BlockSpec((n_slc, LANE), lambda b, g, i: (0, 0)),
            pl.BlockSpec((n_slc, S), lambda b, g, i: (0, 0)),
            pl.BlockSpec(gate_expand.shape, lambda b, g, i: (0, 0)),
        ],
        out_specs=pl.BlockSpec((None, T, Hg * dh), lambda b, g, i: (b, i, g)),
        out_shape=jax.ShapeDtypeStruct((B, S, NSA_HEADS * dh), BF16),
        compiler_params=_cparams(("arbitrary", "arbitrary", "arbitrary")),
        name="nsa_attention",
    )(proj, proj, k_cmp, v_cmp, proj, bias_cmp, near_bias, win_mask, overlap, expand, gate_expand)


def _diff_kernel(q_ref, k_ref, v_ref, lam_ref, og_ref, nb_ref, o_ref, *, lambda_init, n_tiles):
    T, dh = TQ, HEAD_DIM
    qi = pl.program_id(2)
    q = q_ref[...]
    lower = lax.broadcasted_iota(jnp.int32, q.shape, 1) < dh
    zero = jnp.zeros_like(q)
    q12 = jnp.concatenate([jnp.where(lower, q, zero), jnp.where(lower, zero, q)], axis=0)

    def step(start, width, bias, carry):
        s = _dot_nt(q12, k_ref[start:start + width, :])
        if bias is not None:
            s = (s.reshape(2, T, width) + bias[None]).reshape(2 * T, width)
        return _softmax_step(s, v_ref[start:start + width, :], carry)

    lam_p = lam_ref[...]
    lam = (jnp.exp(jnp.sum(lam_p[0:1] * lam_p[1:2], axis=-1, keepdims=True))
           - jnp.exp(jnp.sum(lam_p[2:3] * lam_p[3:4], axis=-1, keepdims=True)) + lambda_init)

    def tile(c):
        carry = _softmax_init(2 * T, DIFF_V_DIM)
        for start, width, table in _key_chunks(c):
            bias = None if table is None else nb_ref[table[0], :, table[1]:table[1] + width]
            carry = step(start, width, bias, carry)
        a = carry[2] / carry[1]
        o = a[:T] - lam * a[T:]
        o_ref[...] = (_rms(o, og_ref[...]) * (1.0 - lambda_init)).astype(BF16)

    for c in range(n_tiles):
        pl.when(qi == c)(functools.partial(tile, c))


def _diff_attention(proj, lam_p, out_gain, near_bias, lambda_init):
    B, S, _ = proj.shape
    T, dh, H = TQ, HEAD_DIM, DIFF_HEADS
    return pl.pallas_call(
        functools.partial(_diff_kernel, lambda_init=lambda_init, n_tiles=S // T),
        grid=(B, H, S // T),
        in_specs=[
            pl.BlockSpec((None, T, 2 * dh), lambda b, h, i: (b, i, C_QD // (2 * dh) + h)),
            pl.BlockSpec((None, S, 2 * dh), lambda b, h, i: (b, 0, C_KD // (2 * dh) + h)),
            pl.BlockSpec((None, S, DIFF_V_DIM), lambda b, h, i: (b, 0, C_VD // DIFF_V_DIM + h)),
            pl.BlockSpec((4, dh), lambda b, h, i: (0, 0)),
            pl.BlockSpec((1, DIFF_V_DIM), lambda b, h, i: (0, 0)),
            pl.BlockSpec((None, 2, T, 2 * T), lambda b, h, i: (h, 0, 0, 0)),
        ],
        out_specs=pl.BlockSpec((None, T, DIFF_V_DIM), lambda b, h, i: (b, i, h)),
        out_shape=jax.ShapeDtypeStruct((B, S, H * DIFF_V_DIM), BF16),
        compiler_params=_cparams(("arbitrary", "arbitrary", "arbitrary")),
        name="diff_attention",
    )(proj, proj, proj, lam_p, out_gain, near_bias)


def _merge_kernel(x_ref, on_ref, od_ref, gm_ref, wn_ref, wd_ref, wo_ref, g2_ref, rw_ref, rb_ref, o_ref):
    tm, d = x_ref.shape
    a = jnp.dot(on_ref[...], wn_ref[...], preferred_element_type=F32)
    b = jnp.dot(od_ref[...], wd_ref[...], preferred_element_type=F32)
    sigmoid = lambda z: 0.5 * jnp.tanh(0.5 * z) + 0.5
    y = sigmoid(gm_ref[:, :d].astype(F32)) * a + sigmoid(gm_ref[:, d:].astype(F32)) * b
    x2 = x_ref[...] + jnp.dot(y.astype(BF16), wo_ref[...], preferred_element_type=F32)
    o_ref[:, :d] = x2

    h2 = _rms(x2, g2_ref[...])
    h_hi = h2.astype(BF16)
    h_lo = (h2 - h_hi.astype(F32)).astype(BF16)
    part = jnp.dot(jnp.concatenate([h_hi, h_lo], axis=0), rw_ref[...], preferred_element_type=F32)
    logits = part[:tm, :LANE] + part[:tm, LANE:] + part[tm:, :LANE] + part[tm:, LANE:] + rb_ref[...]
    lane = lax.broadcasted_iota(jnp.int32, logits.shape, 1)
    big = jnp.int32(LANE)

    def first_argmax(vals, mask):
        vals = jnp.where(mask, vals, -jnp.inf)
        top = jnp.max(vals, axis=-1, keepdims=True)
        idx = jnp.min(jnp.where(vals == top, lane, big), axis=-1, keepdims=True)
        return top, idx

    is_group = lane < N_GROUPS
    g_max, g_top = first_argmax(logits, is_group)
    g_top_p = 1.0 / jnp.sum(jnp.where(is_group, jnp.exp(logits - g_max), 0.0), axis=-1, keepdims=True)
    lo = N_GROUPS + g_top * EXPERTS_PER_GROUP
    in_group = (lane >= lo) & (lane < lo + EXPERTS_PER_GROUP)
    m1, i1 = first_argmax(logits, in_group)
    m2, i2 = first_argmax(logits, in_group & (lane != i1))
    e2 = jnp.exp(m2 - m1)
    w1 = g_top_p / (1.0 + e2)
    w2 = g_top_p * e2 / (1.0 + e2)
    la = jnp.minimum(i1, i2) - lo
    lb = jnp.maximum(i1, i2) - lo
    pair = jnp.right_shift(la * (7 - la), 1) + (lb - la - 1)
    bucket = (g_top * N_PAIRS + pair).astype(F32)
    exp_lane = lane + N_GROUPS
    route = (jnp.where(exp_lane == i1, w1, 0.0) + jnp.where(exp_lane == i2, w2, 0.0)
             + jnp.where(lane == N_EXPERTS, bucket, 0.0))
    o_ref[:, d:] = route


def _merge(x2d, o_nsa, o_diff, proj2d, wn, wd, wo, g2, rw, rb):
    n, d = x2d.shape
    tm = 512
    rw_hi = rw.astype(BF16)
    rw_cat = jnp.concatenate([rw_hi, (rw - rw_hi.astype(F32)).astype(BF16)], axis=1)
    full = lambda a: pl.BlockSpec(a.shape, lambda i: (0,) * a.ndim)
    return pl.pallas_call(
        _merge_kernel,
        grid=(n // tm,),
        in_specs=[pl.BlockSpec((tm, d), lambda i: (i, 0)),
                  pl.BlockSpec((tm, o_nsa.shape[1]), lambda i: (i, 0)),
                  pl.BlockSpec((tm, o_diff.shape[1]), lambda i: (i, 0)),
                  pl.BlockSpec((tm, 2 * d), lambda i: (i, C_GM // (2 * d))),
                  full(wn), full(wd), full(wo), full(g2), full(rw_cat), full(rb)],
        out_specs=pl.BlockSpec((tm, d + LANE), lambda i: (i, 0)),
        out_shape=jax.ShapeDtypeStruct((n, d + LANE), F32),
        compiler_params=_cparams(("arbitrary",)),
        name="merge_router",
    )(x2d, o_nsa, o_diff, proj2d, wn, wd, wo, g2, rw_cat, rb)


N_SLOT = 3


def _moe_kernel(tile_ref, ea_ref, eb_ref, lo_ref, hi_ref, first_ref, perm_ref,
                x_hbm, g2_ref, wga_ref, wua_ref, wda_ref, wgb_ref, wub_ref, wdb_ref,
                o_hbm, xb0, xb1, xb2, ac0, ac1, ac2, h_scr, g_sem, s_sem, *, tm, n_tiles):
    d = o_hbm.shape[1]
    v = pl.program_id(0)
    t = tile_ref[v]
    x_bufs, acc_bufs = (xb0, xb1, xb2), (ac0, ac1, ac2)
    last = n_tiles - 1

    def gather_copy(tt, sl, r):
        return pltpu.make_async_copy(x_hbm.at[pl.ds(perm_ref[tt * tm + r], 1), :],
                                     x_bufs[sl].at[pl.ds(r, 1), :], g_sem.at[sl])

    def scatter_copy(tt, sl, r):
        return pltpu.make_async_copy(acc_bufs[sl].at[pl.ds(r, 1), :],
                                     o_hbm.at[pl.ds(perm_ref[tt * tm + r], 1), :], s_sem.at[sl])

    def for_rows(fn):
        for r in range(tm):
            fn(r)

    def experts(sl, dma_starts):
        n_stage = 6
        per_stage = -(-len(dma_starts) // n_stage)

        def issue(stage):
            for start in dma_starts[stage * per_stage:(stage + 1) * per_stage]:
                start()

        h = h_scr[...]
        cw = x_bufs[sl][:, d:]
        lane = lax.broadcasted_iota(jnp.int32, cw.shape, 1)
        row = lax.broadcasted_iota(jnp.int32, (tm, 1), 0)
        in_bucket = (row >= lo_ref[v]) & (row < hi_ref[v])
        out = None
        for i, (e, wg_ref, wu_ref, wd_ref) in enumerate(((ea_ref[v], wga_ref, wua_ref, wda_ref),
                                                         (eb_ref[v], wgb_ref, wub_ref, wdb_ref))):
            w = jnp.sum(jnp.where(lane == e, cw, 0.0), axis=-1, keepdims=True)
            w = jnp.where(in_bucket, w, 0.0)
            gate = jnp.dot(h, wg_ref[...], preferred_element_type=F32)
            issue(3 * i)
            up = jnp.dot(h, wu_ref[...], preferred_element_type=F32)
            issue(3 * i + 1)
            hid = jax.nn.silu(gate) * up * w
            y = jnp.dot(hid.astype(BF16), wd_ref[...], preferred_element_type=F32)
            issue(3 * i + 2)
            out = y if out is None else out + y
        acc_bufs[sl][...] += out

    def first_visit(sl, scatter_previous):
        ahead, behind = (sl + 2) % N_SLOT, (sl - 1) % N_SLOT
        for_rows(lambda r: gather_copy(t, sl, r).wait())

        @pl.when(t >= N_SLOT)
        def _():
            for_rows(lambda r: scatter_copy(t - N_SLOT, sl, r).wait())

        xs = x_bufs[sl][:, :d]
        h_scr[...] = _rms(xs, g2_ref[...]).astype(BF16)
        acc_bufs[sl][...] = xs
        nxt = jnp.minimum(t + 2, last)
        starts = [functools.partial(lambda r: gather_copy(nxt, ahead, r).start(), r) for r in range(tm)]
        if scatter_previous:
            starts += [functools.partial(lambda r: scatter_copy(t - 1, behind, r).start(), r)
                       for r in range(tm)]
        experts(sl, starts)

    @pl.when(v == 0)
    def _():
        for_rows(lambda r: gather_copy(0, 0, r).start())
        for_rows(lambda r: gather_copy(1, 1, r).start())
        first_visit(0, scatter_previous=False)

    for sl in range(N_SLOT):
        pl.when((v > 0) & (first_ref[v] == 1) & (t % N_SLOT == sl))(
            functools.partial(first_visit, sl, scatter_previous=True))
        pl.when((first_ref[v] == 0) & (t % N_SLOT == sl))(functools.partial(experts, sl, []))

    @pl.when(v == pl.num_programs(0) - 1)
    def _():
        for_rows(lambda r: scatter_copy(last, last % N_SLOT, r).start())
        for tt in (last - 1, last):
            for_rows(lambda r: gather_copy(last, (tt + 2) % N_SLOT, r).wait())
        for tt in (last - 2, last - 1, last):
            for_rows(lambda r: scatter_copy(tt, tt % N_SLOT, r).wait())


def _moe(sched, perm, x2r, g2, wg, wu, wd, tm):
    n = x2r.shape[0]
    d, f = wg.shape[1], wg.shape[2]
    n_visits = sched[0].shape[0]
    n_tiles = n // tm
    assert n_tiles > N_SLOT
    w_in_a = pl.BlockSpec((None, d, f), lambda v, t, ea, eb, *_: (ea[v], 0, 0))
    w_in_b = pl.BlockSpec((None, d, f), lambda v, t, ea, eb, *_: (eb[v], 0, 0))
    w_out_a = pl.BlockSpec((None, f, d), lambda v, t, ea, eb, *_: (ea[v], 0, 0))
    w_out_b = pl.BlockSpec((None, f, d), lambda v, t, ea, eb, *_: (eb[v], 0, 0))
    return pl.pallas_call(
        functools.partial(_moe_kernel, tm=tm, n_tiles=n_tiles),
        grid_spec=pltpu.PrefetchScalarGridSpec(
            num_scalar_prefetch=7,
            grid=(n_visits,),
            in_specs=[pl.BlockSpec(memory_space=pl.ANY),
                      pl.BlockSpec((1, d), lambda v, *_: (0, 0)),
                      w_in_a, w_in_a, w_out_a, w_in_b, w_in_b, w_out_b],
            out_specs=pl.BlockSpec(memory_space=pl.ANY),
            scratch_shapes=([pltpu.VMEM((tm, x2r.shape[1]), F32)] * N_SLOT + [pltpu.VMEM((tm, d), F32)] * N_SLOT
                            + [pltpu.VMEM((tm, d), BF16),
                               pltpu.SemaphoreType.DMA((N_SLOT,)), pltpu.SemaphoreType.DMA((N_SLOT,))])),
        out_shape=jax.ShapeDtypeStruct((n, d), F32),
        compiler_params=_cparams(("arbitrary",)),
        name="moe_experts",
    )(*sched, perm, x2r, g2, wg, wu, wd, wg, wu, wd)


def _moe_schedule(bucket, tm):
    n = bucket.shape[0]
    n_tiles = n // tm
    n_visits = n_tiles + N_BUCKETS - 1
    counts = jnp.sum((bucket[:, None] == jnp.arange(N_BUCKETS)[None, :]).astype(jnp.int32), axis=0)
    ends = jnp.cumsum(counts)
    starts = ends - counts
    first_tile = starts // tm
    last_tile = jnp.maximum(ends - 1, 0) // tm
    n_vis = jnp.where(counts > 0, last_tile - first_tile + 1, 0)
    vis_end = jnp.cumsum(n_vis)
    vis_start = vis_end - n_vis
    v = jnp.arange(n_visits, dtype=jnp.int32)
    b = jnp.minimum(jnp.searchsorted(vis_end, v, side="right"), N_BUCKETS - 1).astype(jnp.int32)
    live = v < vis_end[-1]
    tile = jnp.where(live, first_tile[b] + v - vis_start[b], n_tiles - 1).astype(jnp.int32)
    lo = jnp.where(live, jnp.clip(starts[b] - tile * tm, 0, tm), 0).astype(jnp.int32)
    hi = jnp.where(live, jnp.clip(ends[b] - tile * tm, 0, tm), 0).astype(jnp.int32)
    b_last = b[jnp.maximum(vis_end[-1] - 1, 0)]
    b = jnp.where(live, b, b_last)
    pair_a = np.array([0, 0, 0, 1, 1, 2], np.int32)
    pair_b = np.array([1, 2, 3, 2, 3, 3], np.int32)
    grp = b // N_PAIRS
    ea = (grp * EXPERTS_PER_GROUP + jnp.asarray(pair_a)[b % N_PAIRS]).astype(jnp.int32)
    eb = (grp * EXPERTS_PER_GROUP + jnp.asarray(pair_b)[b % N_PAIRS]).astype(jnp.int32)
    first = jnp.concatenate([jnp.ones((1,), jnp.int32), (tile[1:] != tile[:-1]).astype(jnp.int32)])
    return tile, ea, eb, lo, hi, first


def _layer(x, layer, p):
    B, S, D = x.shape
    G, dh = NSA_KV_GROUPS, HEAD_DIM
    lambda_init = 0.8 - 0.6 * math.exp(-0.3 * layer)
    x2d = x.reshape(B * S, D)

    proj2d = _in_proj(x2d, p["norm1_g"][None, :], _reorder_w_in(p["w_in"]), _head_norm_rows(p))
    proj = proj2d.reshape(B, S, C_TOTAL)

    k_cmp, v_cmp = _compress(proj, *_compress_params(p["cmp_pos"], p["cmp_w1"], p["cmp_w2"]),
                             p["nsa_k_norm"][0:1])

    near, bias_cmp = _bias_tables(p["rel_bias"], S)
    nsa_near = near[:NSA_HEADS].reshape(G, NSA_HPG, 2, TQ, 2 * TQ).transpose(0, 2, 1, 3, 4)
    i = np.arange(TQ)
    win_mask = jnp.asarray(np.where(i[None, :] > i[:, None], 0.0, NEG).astype(np.float32))
    expand = jnp.asarray(np.repeat(np.eye(S // SLC_BLOCK, dtype=np.float32), SLC_BLOCK, axis=1), BF16)
    o_nsa = _nsa_attention(proj, k_cmp, v_cmp, bias_cmp.reshape(G, NSA_HPG, S, LANE), nsa_near, win_mask,
                           jnp.asarray(_overlap_matrix(S)), expand)
    o_diff = _diff_attention(proj, p["diff_lambda"], p["diff_out_norm"][None, :], near[NSA_HEADS:],
                             lambda_init)

    rw = jnp.concatenate([p["router_group_w"], p["router_expert_w"],
                          jnp.zeros((D, LANE - N_GROUPS - N_EXPERTS), F32)], axis=1)
    rb = jnp.concatenate([p["router_group_b"], p["router_expert_b"],
                          jnp.zeros((LANE - N_GROUPS - N_EXPERTS,), F32)])[None, :]
    x2r = _merge(x2d, o_nsa.reshape(B * S, -1), o_diff.reshape(B * S, -1), proj2d,
                 p["w_branch_nsa"].astype(BF16), p["w_branch_diff"].astype(BF16),
                 p["w_out"].astype(BF16), p["norm2_g"][None, :], rw, rb)

    tm = 256
    bucket = x2r[:, D + N_EXPERTS].astype(jnp.int32)
    perm = jnp.argsort(bucket).astype(jnp.int32)
    out = _moe(_moe_schedule(bucket, tm), perm, x2r, p["norm2_g"][None, :],
               p["expert_w_gate"].astype(BF16), p["expert_w_up"].astype(BF16),
               p["expert_w_down"].astype(BF16), tm)
    return out.reshape(B, S, D)


def kernel(x, norm1_g, w_in, nsa_q_norm, nsa_k_norm, cmp_pos, cmp_w1, cmp_w2, diff_q_norm, diff_k_norm,
           diff_lambda, diff_out_norm, rel_bias, w_branch_nsa, w_branch_diff, w_out, norm2_g,
           router_group_w, router_group_b, router_expert_w, router_expert_b, expert_w_gate,
           expert_w_up, expert_w_down):
    per_layer = dict(norm1_g=norm1_g, w_in=w_in, nsa_q_norm=nsa_q_norm, nsa_k_norm=nsa_k_norm,
                     cmp_pos=cmp_pos, cmp_w1=cmp_w1, cmp_w2=cmp_w2, diff_q_norm=diff_q_norm,
                     diff_k_norm=diff_k_norm, diff_lambda=diff_lambda, diff_out_norm=diff_out_norm,
                     w_branch_nsa=w_branch_nsa, w_branch_diff=w_branch_diff, w_out=w_out,
                     norm2_g=norm2_g, router_group_w=router_group_w, router_group_b=router_group_b,
                     router_expert_w=router_expert_w, router_expert_b=router_expert_b,
                     expert_w_gate=expert_w_gate, expert_w_up=expert_w_up, expert_w_down=expert_w_down)
    for layer in range(norm1_g.shape[0]):
        p = {k: v[layer] for k, v in per_layer.items()}
        p["rel_bias"] = rel_bias
        x = _layer(x, layer, p)
    return x
```

```python
import functools
import math

import numpy as np
import jax
import jax.numpy as jnp
from jax import lax
from jax.experimental import pallas as pl
from jax.experimental.pallas import tpu as pltpu

F32 = jnp.float32
BF16 = jnp.bfloat16

HEAD_DIM = 64
SCALE = HEAD_DIM ** -0.5
NSA_HEADS = 8
NSA_KV_GROUPS = 2
NSA_HPG = NSA_HEADS // NSA_KV_GROUPS
CMP_BLOCK = 32
CMP_STRIDE = 16
CMP_HIDDEN = 256
SLC_BLOCK = 64
SLC_TOP_N = 16
WINDOW = 512
FORCE_SCORE = 1e4
DIFF_HEADS = 4
DIFF_V_DIM = 2 * HEAD_DIM
REL_BUCKETS = 32
REL_MAX_DIST = 128
N_GROUPS = 4
EXPERTS_PER_GROUP = 4
N_EXPERTS = N_GROUPS * EXPERTS_PER_GROUP
EXPERT_HIDDEN = 512
RMS_EPS = 1e-6
NEG = -1e30
LOG2E = math.log2(math.e)

LANE = 128
MXU_DIM = 256
TQ = 256
N_PAIRS = 6
N_BUCKETS = N_GROUPS * N_PAIRS

C_GM = 0
C_QN = 2048
C_KV = 2560
C_CMP = 3072
C_GATE = 3328
C_QD = 3584
C_KD = 4096
C_VD = 4608
C_TOTAL = 5120

V7X_VMEM_BYTES = 64 * 1024 * 1024
VMEM_LIMIT = V7X_VMEM_BYTES * 7 // 8


def _cparams(sem):
    return pltpu.CompilerParams(dimension_semantics=sem, vmem_limit_bytes=VMEM_LIMIT)


def _rms(x, gain):
    return x * lax.rsqrt(jnp.mean(x * x, axis=-1, keepdims=True) + RMS_EPS) * gain


def _reorder_w_in(w_in):
    d_model = w_in.shape[0]
    w_in = w_in.astype(BF16)
    dh, G, Hg = HEAD_DIM, NSA_KV_GROUPS, NSA_HPG
    o_q = 0
    o_kv = o_q + NSA_HEADS * dh
    o_g = o_kv + 3 * 2 * G * dh
    o_qd = o_g + 3 * NSA_HEADS
    o_gm = o_qd + 3 * DIFF_HEADS * 2 * dh

    def cols(start, n):
        return w_in[:, start:start + n]

    def kv(kind, which, g):
        return cols(o_kv + ((kind * 2 + which) * G + g) * dh, dh)

    parts = [cols(o_gm, 2 * d_model), cols(o_q, NSA_HEADS * dh)]
    parts += [kv(kind, which, g) for g in range(G) for which in (0, 1) for kind in (1, 2)]
    parts += [kv(0, which, g) for g in range(G) for which in (0, 1)]
    for g in range(G):
        parts += [cols(o_g + (kind * G + g) * Hg, Hg) for kind in range(3)]
        parts.append(jnp.zeros((d_model, LANE - 3 * Hg), w_in.dtype))
    parts.append(cols(o_qd, 3 * DIFF_HEADS * 2 * dh))
    w = jnp.concatenate(parts, axis=1)
    assert w.shape[1] == C_TOTAL
    return w


def _head_norm_rows(p):
    dh, G = HEAD_DIM, NSA_KV_GROUPS
    qs = SCALE * LOG2E
    zeros = lambda n: jnp.zeros((n,), F32)
    kn = p["nsa_k_norm"]
    parts = [zeros(C_QN - C_GM), jnp.tile(p["nsa_q_norm"] * qs, NSA_HEADS)]
    parts += [jnp.concatenate([kn[1], kn[2], zeros(2 * dh)])] * G
    parts += [zeros(C_QD - C_CMP), jnp.tile((p["diff_q_norm"] * qs).reshape(-1), DIFF_HEADS),
              jnp.tile(p["diff_k_norm"].reshape(-1), DIFF_HEADS), zeros(C_TOTAL - C_VD)]
    gain = jnp.concatenate(parts)
    assert gain.shape[0] == C_TOTAL
    return gain[None, :]


def _t5_bucket(dist):
    n = jnp.maximum(dist, 0)
    max_exact = REL_BUCKETS // 2
    nf = jnp.maximum(n, 1).astype(F32)
    large = max_exact + (jnp.log(nf / max_exact) / math.log(REL_MAX_DIST / max_exact)
                         * (REL_BUCKETS - max_exact)).astype(jnp.int32)
    large = jnp.minimum(large, REL_BUCKETS - 1)
    return jnp.where(n < max_exact, n, large)


def _toeplitz(v_ext, rows, cols):
    length = v_ext.shape[-1]
    lead = v_ext.shape[:-1]
    flat = jnp.broadcast_to(v_ext[..., None, :], lead + (rows, length)).reshape(lead + (rows * length,))
    return flat[..., :rows * (length - 1)].reshape(lead + (rows, length - 1))[..., :cols]


def _bias_tables(rel_bias, S):
    heads = rel_bias.shape[1]
    far = REL_MAX_DIST
    assert TQ == 2 * far and LANE == far
    bd = rel_bias[_t5_bucket(jnp.arange(2 * far))].T.astype(F32) * LOG2E
    rel = bd - bd[:, far:far + 1]
    neg = lambda *shape: jnp.full((heads,) + shape, NEG, F32)
    zero = lambda *shape: jnp.zeros((heads,) + shape, F32)
    rev = lambda a: a[:, ::-1]
    d0 = _toeplitz(jnp.concatenate([rel[:, 0:1], neg(far), rev(rel[:, 1:far])], axis=1), far, far)
    d1 = _toeplitz(jnp.concatenate([rev(rel[:, 1:far + 1]), zero(far)], axis=1), far, far)
    z, n = zero(far, far), neg(far, far)
    blocks = lambda rows: jnp.concatenate([jnp.concatenate(r, axis=2) for r in rows], axis=1)
    near = jnp.stack([blocks([[d0, n, n, n], [d1, d0, n, n]]),
                      blocks([[z, d1, d0, n], [z, z, d1, d0]])], axis=1)
    hb = rel[:NSA_HEADS]
    zero8 = lambda *shape: jnp.zeros((NSA_HEADS,) + shape, F32)
    neg8 = lambda *shape: jnp.full((NSA_HEADS,) + shape, NEG, F32)
    width = TQ + far - 1
    tp = _toeplitz(jnp.concatenate([rev(hb[:, 0:far]), neg8(TQ + 1), zero8(TQ - 1)], axis=1), TQ, width)
    n_tiles = S // TQ
    per_tile = TQ // CMP_STRIDE
    col0 = per_tile * (n_tiles - 1)
    shift = CMP_STRIDE * col0 - (CMP_BLOCK - 1)
    first = -(-(shift - (far - 1)) // CMP_STRIDE)
    last = (shift + TQ - 1) // CMP_STRIDE
    band = tp[:, :, CMP_STRIDE * first - shift + far - 1::CMP_STRIDE][:, :, :last - first + 1]
    g2 = jnp.concatenate([zero8(TQ, first), band, neg8(TQ, 2 * TQ - last - 1)], axis=2)
    bias_cmp = jnp.stack([g2[:, :, col0 - per_tile * i:col0 - per_tile * i + LANE] for i in range(n_tiles)],
                         axis=1)
    n_cmp = (S - CMP_BLOCK) // CMP_STRIDE + 1
    bias_cmp = jnp.where(np.arange(LANE) < n_cmp, bias_cmp, NEG)
    return near, bias_cmp.reshape(NSA_HEADS, S, LANE)


def _overlap_matrix(S):
    n_cmp = (S - CMP_BLOCK) // CMP_STRIDE + 1
    n_slc = S // SLC_BLOCK
    c_start = np.arange(n_cmp) * CMP_STRIDE
    s_start = np.arange(n_slc) * SLC_BLOCK
    lo = np.maximum(s_start[:, None], c_start[None, :])
    hi = np.minimum(s_start[:, None] + SLC_BLOCK, c_start[None, :] + CMP_BLOCK)
    overlap = (np.maximum(hi - lo, 0) / CMP_BLOCK).astype(np.float32)
    out = np.zeros((n_slc, LANE), np.float32)
    out[:, :n_cmp] = overlap
    return out


IN_CHUNK = 512
HEAD_NORM_CHUNKS = {C_QN // IN_CHUNK: "all", C_KV // IN_CHUNK: "keys",
                    C_QD // IN_CHUNK: "all", C_KD // IN_CHUNK: "all"}


def _in_proj_kernel(x_ref, g_ref, w_ref, hg_ref, seg_ref, o_ref):
    dh = HEAD_DIM
    h = _rms(x_ref[...], g_ref[...]).astype(BF16)
    for c in range(C_TOTAL // IN_CHUNK):
        cols = slice(c * IN_CHUNK, (c + 1) * IN_CHUNK)
        y = jnp.dot(h, w_ref[:, cols], preferred_element_type=F32)
        mode = HEAD_NORM_CHUNKS.get(c)
        if mode is not None:
            sq = (y * y).astype(BF16)
            seg_w = seg_ref.shape[0]
            ss = jnp.concatenate([jnp.dot(sq[:, j:j + seg_w], seg_ref[...], preferred_element_type=F32)
                                  for j in range(0, IN_CHUNK, seg_w)], axis=1)
            scale = lax.rsqrt(ss * (1.0 / dh) + RMS_EPS) * hg_ref[:, cols]
            if mode == "keys":
                lane = lax.broadcasted_iota(jnp.int32, (1, IN_CHUNK), 1)
                scale = jnp.where(lane % (4 * dh) < 2 * dh, scale, 1.0)
            y = y * scale
        o_ref[:, cols] = y.astype(BF16)


def _in_proj(x2d, gain, w, head_gain):
    n, d = x2d.shape
    tm = 1024
    seg = np.arange(MXU_DIM) // HEAD_DIM
    seg_ones = jnp.asarray((seg[:, None] == seg[None, :]).astype(np.float32), BF16)
    return pl.pallas_call(
        _in_proj_kernel,
        grid=(n // tm,),
        in_specs=[pl.BlockSpec((tm, d), lambda i: (i, 0)),
                  pl.BlockSpec((1, d), lambda i: (0, 0)),
                  pl.BlockSpec((d, C_TOTAL), lambda i: (0, 0), pipeline_mode=pl.Buffered(1)),
                  pl.BlockSpec((1, C_TOTAL), lambda i: (0, 0)),
                  pl.BlockSpec((MXU_DIM, MXU_DIM), lambda i: (0, 0))],
        out_specs=pl.BlockSpec((tm, C_TOTAL), lambda i: (i, 0)),
        out_shape=jax.ShapeDtypeStruct((n, C_TOTAL), BF16),
        compiler_params=_cparams(("arbitrary",)),
        name="in_proj",
    )(x2d, gain, w, head_gain, seg_ones)


def _compress_kernel(t_ref, pos_ref, w1_ref, w2_ref, kg_ref, k_ref, v_ref, tok_scr):
    S, dh = t_ref.shape[0], HEAD_DIM
    m = S // CMP_STRIDE
    tok_scr[0:S, :] = t_ref[...].astype(F32)
    tok_scr[S:, :] = jnp.zeros((CMP_STRIDE, LANE), F32)
    pre = jnp.zeros((m, 2 * CMP_HIDDEN), F32)
    for l in range(CMP_BLOCK):
        x = tok_scr[pl.ds(l, m, stride=CMP_STRIDE), :] + pos_ref[l:l + 1, :]
        pre = pre + jnp.dot(x.astype(BF16), w1_ref[l], preferred_element_type=F32)
    out = jnp.dot(jax.nn.gelu(pre).astype(BF16), w2_ref[...], preferred_element_type=F32)
    k_ref[...] = jnp.concatenate([_rms(out[:, :dh], kg_ref[...]), jnp.zeros((m, LANE - dh), F32)],
                                 axis=1).astype(BF16)
    v_ref[...] = out[:, dh:].astype(BF16)


def _compress(proj, pos, w1, w2, k_gain):
    B, S, _ = proj.shape
    G, m = NSA_KV_GROUPS, S // CMP_STRIDE
    outs = [jax.ShapeDtypeStruct((B, G, m, width), BF16) for width in (LANE, HEAD_DIM)]
    ospecs = [pl.BlockSpec((None, None, m, width), lambda b, g: (b, g, 0, 0)) for width in (LANE, HEAD_DIM)]
    return pl.pallas_call(
        _compress_kernel,
        grid=(B, G),
        in_specs=[pl.BlockSpec((None, S, LANE), lambda b, g: (b, 0, C_CMP // LANE + g)),
                  pl.BlockSpec(pos.shape, lambda b, g: (0, 0)),
                  pl.BlockSpec(w1.shape, lambda b, g: (0, 0, 0)),
                  pl.BlockSpec(w2.shape, lambda b, g: (0, 0)),
                  pl.BlockSpec((1, HEAD_DIM), lambda b, g: (0, 0))],
        out_specs=ospecs,
        out_shape=outs,
        scratch_shapes=[pltpu.VMEM((S + CMP_STRIDE, LANE), F32)],
        compiler_params=_cparams(("arbitrary", "arbitrary")),
        name="nsa_compress",
    )(proj, pos, w1, w2, k_gain)


def _compress_params(cmp_pos, cmp_w1, cmp_w2):
    dh, hid = HEAD_DIM, CMP_HIDDEN
    w1 = cmp_w1.reshape(2, CMP_BLOCK, dh, hid)
    z1 = jnp.zeros((CMP_BLOCK, dh, hid), cmp_w1.dtype)
    w1 = jnp.concatenate([jnp.concatenate([w1[0], z1], axis=2),
                          jnp.concatenate([z1, w1[1]], axis=2)], axis=1)
    z2 = jnp.zeros((hid, dh), cmp_w2.dtype)
    w2 = jnp.concatenate([jnp.concatenate([cmp_w2[0], z2], axis=1),
                          jnp.concatenate([z2, cmp_w2[1]], axis=1)], axis=0)
    pos = jnp.concatenate([cmp_pos[0], cmp_pos[1]], axis=1)
    return pos, w1.astype(BF16), w2.astype(BF16)


def _dot_nt(a, b):
    return lax.dot_general(a, b, (((1,), (1,)), ((), ())), preferred_element_type=F32)


def _softmax_step(s, v, carry):
    m, l, acc = carry
    m_new = jnp.maximum(m, jnp.max(s, axis=-1, keepdims=True))
    alpha = jnp.exp2(m - m_new)
    p = jnp.exp2(s - m_new)
    l = alpha * l + jnp.sum(p, axis=-1, keepdims=True)
    acc = alpha * acc + jnp.dot(p.astype(BF16), v, preferred_element_type=F32)
    return m_new, l, acc


def _softmax_init(rows, dv):
    return (jnp.full((rows, 1), NEG, F32), jnp.zeros((rows, 1), F32), jnp.zeros((rows, dv), F32))


def _softmax_step_ones(s, v_ones, carry):
    m, acc = carry
    m_new = jnp.maximum(m, jnp.max(s, axis=-1, keepdims=True))
    p = jnp.exp2((s - m_new).astype(BF16))
    acc = jnp.exp2(m - m_new) * acc + jnp.dot(p, v_ones, preferred_element_type=F32)
    return m_new, acc


def _softmax_init_ones(rows, width):
    return jnp.full((rows, 1), NEG, F32), jnp.zeros((rows, width), F32)


def _key_chunks(c):
    T = TQ
    chunks = [(0, T, (0, 0))] if c == 0 else [((c - 1) * T, 2 * T, (1, 0))]
    n_far = max(c - 1, 0)
    pos = 0
    for size in (1, 2):
        if n_far & size:
            chunks.append((pos * T, size * T, None))
            pos += size
    chunks += [(s * T, 4 * T, None) for s in range(pos, n_far, 4)]
    return chunks


def _nsa_kernel(q_ref, kv_ref, kc_ref, vc_ref, gate_ref, bc_ref, nb_ref, wm_ref, ov_ref, ex_ref, gx_ref,
                o_ref, *, seq_len):
    T, dh, Hg = TQ, HEAD_DIM, NSA_HPG
    R = Hg * T
    n_slc = seq_len // SLC_BLOCK
    n_all = SLC_TOP_N * SLC_BLOCK // T

    def values(rows, ones_upper):
        v = kv_ref[rows, 2 * dh:4 * dh]
        keep = lax.broadcasted_iota(jnp.int32, v.shape, 1) < dh
        return jnp.where(keep if ones_upper else ~keep, v, jnp.ones_like(v))

    def selection_mask(c, pc):
        p_sum = pc[0:T] + pc[T:2 * T] + pc[2 * T:3 * T] + pc[3 * T:4 * T]
        imp = lax.dot_general(ov_ref[...], p_sum, (((1,), (1,)), ((), ())), preferred_element_type=F32,
                              precision=lax.Precision.HIGHEST)
        blk = lax.broadcasted_iota(jnp.int32, (n_slc, T), 0)
        cur = (c * T + lax.broadcasted_iota(jnp.int32, (n_slc, T), 1)) // SLC_BLOCK
        forced = (blk == 0) | (blk == cur) | (blk == cur - 1)
        score = jnp.where(blk <= cur, imp + jnp.where(forced, FORCE_SCORE, 0.0), NEG)
        rank = jnp.zeros((n_slc, T), jnp.int32)
        for i in range(n_slc):
            other = score[i:i + 1, :]
            ahead = (other > score) | ((other == score) & (blk > i))
            rank = rank + ahead.astype(jnp.int32)
        sel_bias = jnp.where(rank < SLC_TOP_N, 0.0, NEG).T.astype(BF16)
        return jnp.dot(sel_bias, ex_ref[...], preferred_element_type=F32)

    def tile(c):
        lower = lax.broadcasted_iota(jnp.int32, (T, 2 * dh), 1) < dh
        q_lo, q_hi = [], []
        for pair in range(Hg // 2):
            pq = q_ref[:, pair * 2 * dh:(pair + 1) * 2 * dh]
            sw = jnp.concatenate([pq[:, dh:], pq[:, :dh]], axis=1)
            zero = jnp.zeros_like(pq)
            q_lo += [jnp.where(lower, pq, zero), jnp.where(lower, sw, zero)]
            q_hi += [jnp.where(lower, zero, sw), jnp.where(lower, zero, pq)]
        q = jnp.concatenate(q_lo, axis=0)
        q_win = jnp.concatenate(q_hi, axis=0)

        lc = _dot_nt(q, kc_ref[...]) + bc_ref[...].reshape(R, LANE)
        pc = jnp.exp2(lc - jnp.max(lc, axis=-1, keepdims=True))
        row = lax.broadcasted_iota(jnp.int32, (Hg, T, 1), 1).reshape(R, 1)
        visible = (c * T + row >= CMP_BLOCK - 1).astype(F32)
        pc = pc * (visible / jnp.sum(pc, axis=-1, keepdims=True))
        o_cmp = jnp.dot(pc.astype(BF16), vc_ref[...], preferred_element_type=F32)

        mask = selection_mask(c, pc) if c >= n_all else None
        chunks = _key_chunks(c)
        carry = _softmax_init_ones(R, 2 * dh)
        for start, width, table in chunks:
            rows = slice(start, start + width)
            s = _dot_nt(q, kv_ref[rows, 0:2 * dh]).reshape(Hg, T, width)
            if table is not None:
                s = s + nb_ref[table[0], :, :, table[1]:table[1] + width]
            if mask is not None:
                s = s + mask[:, rows][None]
            carry = _softmax_step_ones(s.reshape(R, width), values(rows, True), carry)
        acc_s = carry[1]
        o_slc = acc_s[:, :dh] / acc_s[:, dh:]

        start, width, table = chunks[0]
        win_chunks = [(start, width, nb_ref[table[0], :, :, table[1]:table[1] + width])]
        if c >= 2:
            win_chunks.append(((c - 2) * T, T, wm_ref[...][None]))
        carry = _softmax_init_ones(R, 2 * dh)
        for start, width, bias in win_chunks:
            rows = slice(start, start + width)
            s = _dot_nt(q_win, kv_ref[rows, 0:2 * dh]).reshape(Hg, T, width) + bias
            carry = _softmax_step_ones(s.reshape(R, width), values(rows, False), carry)
        acc_w = carry[1]
        o_win = acc_w[:, dh:] / acc_w[:, :dh]

        gates = jax.nn.sigmoid(gate_ref[...].astype(F32))
        g_hi = gates.astype(BF16)
        g_lo = (gates - g_hi.astype(F32)).astype(BF16)
        g_wide = (jnp.dot(g_hi, gx_ref[...], preferred_element_type=F32)
                  + jnp.dot(g_lo, gx_ref[...], preferred_element_type=F32))
        by_head = lambda o: jnp.concatenate([o[h * T:(h + 1) * T] for h in range(Hg)], axis=1)
        w = Hg * dh
        out = (g_wide[:, 0:w] * by_head(o_cmp) + g_wide[:, w:2 * w] * by_head(o_slc)
               + g_wide[:, 2 * w:3 * w] * by_head(o_win))
        o_ref[...] = out.astype(BF16)

    qi = pl.program_id(2)
    for c in range(seq_len // T):
        pl.when(qi == c)(functools.partial(tile, c))


def _nsa_attention(proj, k_cmp, v_cmp, bias_cmp, near_bias, win_mask, overlap, expand):
    B, S, _ = proj.shape
    G, Hg, dh, T = NSA_KV_GROUPS, NSA_HPG, HEAD_DIM, TQ
    assert WINDOW == 2 * T and S % (2 * T) == 0
    n_slc = S // SLC_BLOCK
    gate_expand = np.zeros((LANE, 3 * Hg * dh), np.float32)
    for j in range(3 * Hg):
        gate_expand[j, j * dh:(j + 1) * dh] = 1.0
    gate_expand = jnp.asarray(gate_expand, BF16)
    return pl.pallas_call(
        functools.partial(_nsa_kernel, seq_len=S),
        grid=(B, G, S // T),
        in_specs=[
            pl.BlockSpec((None, T, Hg * dh), lambda b, g, i: (b, i, C_QN // (Hg * dh) + g)),
            pl.BlockSpec((None, S, 4 * dh), lambda b, g, i: (b, 0, C_KV // (4 * dh) + g)),
            pl.BlockSpec((None, None, LANE, LANE), lambda b, g, i: (b, g, 0, 0)),
            pl.BlockSpec((None, None, LANE, dh), lambda b, g, i: (b, g, 0, 0)),
            pl.BlockSpec((None, T, LANE), lambda b, g, i: (b, i, C_GATE // LANE + g)),
            pl.BlockSpec((None, Hg, T, LANE), lambda b, g, i: (g, 0, i, 0)),
            pl.BlockSpec((None, 2, Hg, T, 2 * T), lambda b, g, i: (g, 0, 0, 0, 0)),
            pl.BlockSpec((T, T), lambda b, g, i: (0, 0)),
            pl.BlockSpec((n_slc, LANE), lambda b, g, i: (0, 0)),
            pl.BlockSpec((n_slc, S), lambda b, g, i: (0, 0)),
            pl.BlockSpec(gate_expand.shape, lambda b, g, i: (0, 0)),
        ],
        out_specs=pl.BlockSpec((None, T, Hg * dh), lambda b, g, i: (b, i, g)),
        out_shape=jax.ShapeDtypeStruct((B, S, NSA_HEADS * dh), BF16),
        compiler_params=_cparams(("arbitrary", "arbitrary", "arbitrary")),
        name="nsa_attention",
    )(proj, proj, k_cmp, v_cmp, proj, bias_cmp, near_bias, win_mask, overlap, expand, gate_expand)


def _diff_kernel(q_ref, k_ref, v_ref, lam_ref, og_ref, nb_ref, o_ref, *, lambda_init, n_tiles):
    T, dh = TQ, HEAD_DIM
    qi = pl.program_id(2)
    q = q_ref[...]
    lower = lax.broadcasted_iota(jnp.int32, q.shape, 1) < dh
    zero = jnp.zeros_like(q)
    q12 = jnp.concatenate([jnp.where(lower, q, zero), jnp.where(lower, zero, q)], axis=0)

    def step(start, width, bias, carry):
        s = _dot_nt(q12, k_ref[start:start + width, :])
        if bias is not None:
            s = (s.reshape(2, T, width) + bias[None]).reshape(2 * T, width)
        return _softmax_step(s, v_ref[start:start + width, :], carry)

    lam_p = lam_ref[...]
    lam = (jnp.exp(jnp.sum(lam_p[0:1] * lam_p[1:2], axis=-1, keepdims=True))
           - jnp.exp(jnp.sum(lam_p[2:3] * lam_p[3:4], axis=-1, keepdims=True)) + lambda_init)

    def tile(c):
        carry = _softmax_init(2 * T, DIFF_V_DIM)
        for start, width, table in _key_chunks(c):
            bias = None if table is None else nb_ref[table[0], :, table[1]:table[1] + width]
            carry = step(start, width, bias, carry)
        a = carry[2] / carry[1]
        o = a[:T] - lam * a[T:]
        o_ref[...] = (_rms(o, og_ref[...]) * (1.0 - lambda_init)).astype(BF16)

    for c in range(n_tiles):
        pl.when(qi == c)(functools.partial(tile, c))


def _diff_attention(proj, lam_p, out_gain, near_bias, lambda_init):
    B, S, _ = proj.shape
    T, dh, H = TQ, HEAD_DIM, DIFF_HEADS
    return pl.pallas_call(
        functools.partial(_diff_kernel, lambda_init=lambda_init, n_tiles=S // T),
        grid=(B, H, S // T),
        in_specs=[
            pl.BlockSpec((None, T, 2 * dh), lambda b, h, i: (b, i, C_QD // (2 * dh) + h)),
            pl.BlockSpec((None, S, 2 * dh), lambda b, h, i: (b, 0, C_KD // (2 * dh) + h)),
            pl.BlockSpec((None, S, DIFF_V_DIM), lambda b, h, i: (b, 0, C_VD // DIFF_V_DIM + h)),
            pl.BlockSpec((4, dh), lambda b, h, i: (0, 0)),
            pl.BlockSpec((1, DIFF_V_DIM), lambda b, h, i: (0, 0)),
            pl.BlockSpec((None, 2, T, 2 * T), lambda b, h, i: (h, 0, 0, 0)),
        ],
        out_specs=pl.BlockSpec((None, T, DIFF_V_DIM), lambda b, h, i: (b, i, h)),
        out_shape=jax.ShapeDtypeStruct((B, S, H * DIFF_V_DIM), BF16),
        compiler_params=_cparams(("arbitrary", "arbitrary", "arbitrary")),
        name="diff_attention",
    )(proj, proj, proj, lam_p, out_gain, near_bias)


def _merge_kernel(x_ref, on_ref, od_ref, gm_ref, wn_ref, wd_ref, wo_ref, g2_ref, rw_ref, rb_ref, o_ref):
    tm, d = x_ref.shape
    a = jnp.dot(on_ref[...], wn_ref[...], preferred_element_type=F32)
    b = jnp.dot(od_ref[...], wd_ref[...], preferred_element_type=F32)
    sigmoid = lambda z: 0.5 * jnp.tanh(0.5 * z) + 0.5
    y = sigmoid(gm_ref[:, :d].astype(F32)) * a + sigmoid(gm_ref[:, d:].astype(F32)) * b
    x2 = x_ref[...] + jnp.dot(y.astype(BF16), wo_ref[...], preferred_element_type=F32)
    o_ref[:, :d] = x2

    h2 = _rms(x2, g2_ref[...])
    h_hi = h2.astype(BF16)
    h_lo = (h2 - h_hi.astype(F32)).astype(BF16)
    part = jnp.dot(jnp.concatenate([h_hi, h_lo], axis=0), rw_ref[...], preferred_element_type=F32)
    logits = part[:tm, :LANE] + part[:tm, LANE:] + part[tm:, :LANE] + part[tm:, LANE:] + rb_ref[...]
    lane = lax.broadcasted_iota(jnp.int32, logits.shape, 1)
    big = jnp.int32(LANE)

    def first_argmax(vals, mask):
        vals = jnp.where(mask, vals, -jnp.inf)
        top = jnp.max(vals, axis=-1, keepdims=True)
        idx = jnp.min(jnp.where(vals == top, lane, big), axis=-1, keepdims=True)
        return top, idx

    is_group = lane < N_GROUPS
    g_max, g_top = first_argmax(logits, is_group)
    g_top_p = 1.0 / jnp.sum(jnp.where(is_group, jnp.exp(logits - g_max), 0.0), axis=-1, keepdims=True)
    lo = N_GROUPS + g_top * EXPERTS_PER_GROUP
    in_group = (lane >= lo) & (lane < lo + EXPERTS_PER_GROUP)
    m1, i1 = first_argmax(logits, in_group)
    m2, i2 = first_argmax(logits, in_group & (lane != i1))
    e2 = jnp.exp(m2 - m1)
    w1 = g_top_p / (1.0 + e2)
    w2 = g_top_p * e2 / (1.0 + e2)
    la = jnp.minimum(i1, i2) - lo
    lb = jnp.maximum(i1, i2) - lo
    pair = jnp.right_shift(la * (7 - la), 1) + (lb - la - 1)
    bucket = (g_top * N_PAIRS + pair).astype(F32)
    exp_lane = lane + N_GROUPS
    route = (jnp.where(exp_lane == i1, w1, 0.0) + jnp.where(exp_lane == i2, w2, 0.0)
             + jnp.where(lane == N_EXPERTS, bucket, 0.0))
    o_ref[:, d:] = route


def _merge(x2d, o_nsa, o_diff, proj2d, wn, wd, wo, g2, rw, rb):
    n, d = x2d.shape
    tm = 512
    rw_hi = rw.astype(BF16)
    rw_cat = jnp.concatenate([rw_hi, (rw - rw_hi.astype(F32)).astype(BF16)], axis=1)
    full = lambda a: pl.BlockSpec(a.shape, lambda i: (0,) * a.ndim)
    return pl.pallas_call(
        _merge_kernel,
        grid=(n // tm,),
        in_specs=[pl.BlockSpec((tm, d), lambda i: (i, 0)),
                  pl.BlockSpec((tm, o_nsa.shape[1]), lambda i: (i, 0)),
                  pl.BlockSpec((tm, o_diff.shape[1]), lambda i: (i, 0)),
                  pl.BlockSpec((tm, 2 * d), lambda i: (i, C_GM // (2 * d))),
                  full(wn), full(wd), full(wo), full(g2), full(rw_cat), full(rb)],
        out_specs=pl.BlockSpec((tm, d + LANE), lambda i: (i, 0)),
        out_shape=jax.ShapeDtypeStruct((n, d + LANE), F32),
        compiler_params=_cparams(("arbitrary",)),
        name="merge_router",
    )(x2d, o_nsa, o_diff, proj2d, wn, wd, wo, g2, rw_cat, rb)


N_SLOT = 3


def _moe_kernel(tile_ref, ea_ref, eb_ref, lo_ref, hi_ref, first_ref, perm_ref,
                x_hbm, g2_ref, wga_ref, wua_ref, wda_ref, wgb_ref, wub_ref, wdb_ref,
                o_hbm, xb0, xb1, xb2, ac0, ac1, ac2, h_scr, g_sem, s_sem, *, tm, n_tiles):
    d = o_hbm.shape[1]
    v = pl.program_id(0)
    t = tile_ref[v]
    x_bufs, acc_bufs = (xb0, xb1, xb2), (ac0, ac1, ac2)
    last = n_tiles - 1

    def gather_copy(tt, sl, r):
        return pltpu.make_async_copy(x_hbm.at[pl.ds(perm_ref[tt * tm + r], 1), :],
                                     x_bufs[sl].at[pl.ds(r, 1), :], g_sem.at[sl])

    def scatter_copy(tt, sl, r):
        return pltpu.make_async_copy(acc_bufs[sl].at[pl.ds(r, 1), :],
                                     o_hbm.at[pl.ds(perm_ref[tt * tm + r], 1), :], s_sem.at[sl])

    def for_rows(fn):
        for r in range(tm):
            fn(r)

    def experts(sl, dma_starts):
        n_stage = 6
        per_stage = -(-len(dma_starts) // n_stage)

        def issue(stage):
            for start in dma_starts[stage * per_stage:(stage + 1) * per_stage]:
                start()

        h = h_scr[...]
        cw = x_bufs[sl][:, d:]
        lane = lax.broadcasted_iota(jnp.int32, cw.shape, 1)
        row = lax.broadcasted_iota(jnp.int32, (tm, 1), 0)
        in_bucket = (row >= lo_ref[v]) & (row < hi_ref[v])
        out = None
        for i, (e, wg_ref, wu_ref, wd_ref) in enumerate(((ea_ref[v], wga_ref, wua_ref, wda_ref),
                                                         (eb_ref[v], wgb_ref, wub_ref, wdb_ref))):
            w = jnp.sum(jnp.where(lane == e, cw, 0.0), axis=-1, keepdims=True)
            w = jnp.where(in_bucket, w, 0.0)
            gate = jnp.dot(h, wg_ref[...], preferred_element_type=F32)
            issue(3 * i)
            up = jnp.dot(h, wu_ref[...], preferred_element_type=F32)
            issue(3 * i + 1)
            hid = jax.nn.silu(gate) * up * w
            y = jnp.dot(hid.astype(BF16), wd_ref[...], preferred_element_type=F32)
            issue(3 * i + 2)
            out = y if out is None else out + y
        acc_bufs[sl][...] += out

    def first_visit(sl, scatter_previous):
        ahead, behind = (sl + 2) % N_SLOT, (sl - 1) % N_SLOT
        for_rows(lambda r: gather_copy(t, sl, r).wait())

        @pl.when(t >= N_SLOT)
        def _():
            for_rows(lambda r: scatter_copy(t - N_SLOT, sl, r).wait())

        xs = x_bufs[sl][:, :d]
        h_scr[...] = _rms(xs, g2_ref[...]).astype(BF16)
        acc_bufs[sl][...] = xs
        nxt = jnp.minimum(t + 2, last)
        starts = [functools.partial(lambda r: gather_copy(nxt, ahead, r).start(), r) for r in range(tm)]
        if scatter_previous:
            starts += [functools.partial(lambda r: scatter_copy(t - 1, behind, r).start(), r)
                       for r in range(tm)]
        experts(sl, starts)

    @pl.when(v == 0)
    def _():
        for_rows(lambda r: gather_copy(0, 0, r).start())
        for_rows(lambda r: gather_copy(1, 1, r).start())
        first_visit(0, scatter_previous=False)

    for sl in range(N_SLOT):
        pl.when((v > 0) & (first_ref[v] == 1) & (t % N_SLOT == sl))(
            functools.partial(first_visit, sl, scatter_previous=True))
        pl.when((first_ref[v] == 0) & (t % N_SLOT == sl))(functools.partial(experts, sl, []))

    @pl.when(v == pl.num_programs(0) - 1)
    def _():
        for_rows(lambda r: scatter_copy(last, last % N_SLOT, r).start())
        for tt in (last - 1, last):
            for_rows(lambda r: gather_copy(last, (tt + 2) % N_SLOT, r).wait())
        for tt in (last - 2, last - 1, last):
            for_rows(lambda r: scatter_copy(tt, tt % N_SLOT, r).wait())


def _moe(sched, perm, x2r, g2, wg, wu, wd, tm):
    n = x2r.shape[0]
    d, f = wg.shape[1], wg.shape[2]
    n_visits = sched[0].shape[0]
    n_tiles = n // tm
    assert n_tiles > N_SLOT
    w_in_a = pl.BlockSpec((None, d, f), lambda v, t, ea, eb, *_: (ea[v], 0, 0))
    w_in_b = pl.BlockSpec((None, d, f), lambda v, t, ea, eb, *_: (eb[v], 0, 0))
    w_out_a = pl.BlockSpec((None, f, d), lambda v, t, ea, eb, *_: (ea[v], 0, 0))
    w_out_b = pl.BlockSpec((None, f, d), lambda v, t, ea, eb, *_: (eb[v], 0, 0))
    return pl.pallas_call(
        functools.partial(_moe_kernel, tm=tm, n_tiles=n_tiles),
        grid_spec=pltpu.PrefetchScalarGridSpec(
            num_scalar_prefetch=7,
            grid=(n_visits,),
            in_specs=[pl.BlockSpec(memory_space=pl.ANY),
                      pl.BlockSpec((1, d), lambda v, *_: (0, 0)),
                      w_in_a, w_in_a, w_out_a, w_in_b, w_in_b, w_out_b],
            out_specs=pl.BlockSpec(memory_space=pl.ANY),
            scratch_shapes=([pltpu.VMEM((tm, x2r.shape[1]), F32)] * N_SLOT + [pltpu.VMEM((tm, d), F32)] * N_SLOT
                            + [pltpu.VMEM((tm, d), BF16),
                               pltpu.SemaphoreType.DMA((N_SLOT,)), pltpu.SemaphoreType.DMA((N_SLOT,))])),
        out_shape=jax.ShapeDtypeStruct((n, d), F32),
        compiler_params=_cparams(("arbitrary",)),
        name="moe_experts",
    )(*sched, perm, x2r, g2, wg, wu, wd, wg, wu, wd)


def _moe_schedule(bucket, tm):
    n = bucket.shape[0]
    n_tiles = n // tm
    n_visits = n_tiles + N_BUCKETS - 1
    counts = jnp.sum((bucket[:, None] == jnp.arange(N_BUCKETS)[None, :]).astype(jnp.int32), axis=0)
    ends = jnp.cumsum(counts)
    starts = ends - counts
    first_tile = starts // tm
    last_tile = jnp.maximum(ends - 1, 0) // tm
    n_vis = jnp.where(counts > 0, last_tile - first_tile + 1, 0)
    vis_end = jnp.cumsum(n_vis)
    vis_start = vis_end - n_vis
    v = jnp.arange(n_visits, dtype=jnp.int32)
    b = jnp.minimum(jnp.searchsorted(vis_end, v, side="right"), N_BUCKETS - 1).astype(jnp.int32)
    live = v < vis_end[-1]
    tile = jnp.where(live, first_tile[b] + v - vis_start[b], n_tiles - 1).astype(jnp.int32)
    lo = jnp.where(live, jnp.clip(starts[b] - tile * tm, 0, tm), 0).astype(jnp.int32)
    hi = jnp.where(live, jnp.clip(ends[b] - tile * tm, 0, tm), 0).astype(jnp.int32)
    b_last = b[jnp.maximum(vis_end[-1] - 1, 0)]
    b = jnp.where(live, b, b_last)
    pair_a = np.array([0, 0, 0, 1, 1, 2], np.int32)
    pair_b = np.array([1, 2, 3, 2, 3, 3], np.int32)
    grp = b // N_PAIRS
    ea = (grp * EXPERTS_PER_GROUP + jnp.asarray(pair_a)[b % N_PAIRS]).astype(jnp.int32)
    eb = (grp * EXPERTS_PER_GROUP + jnp.asarray(pair_b)[b % N_PAIRS]).astype(jnp.int32)
    first = jnp.concatenate([jnp.ones((1,), jnp.int32), (tile[1:] != tile[:-1]).astype(jnp.int32)])
    return tile, ea, eb, lo, hi, first


def _layer(x, layer, p):
    B, S, D = x.shape
    G, dh = NSA_KV_GROUPS, HEAD_DIM
    lambda_init = 0.8 - 0.6 * math.exp(-0.3 * layer)
    x2d = x.reshape(B * S, D)

    proj2d = _in_proj(x2d, p["norm1_g"][None, :], _reorder_w_in(p["w_in"]), _head_norm_rows(p))
    proj = proj2d.reshape(B, S, C_TOTAL)

    k_cmp, v_cmp = _compress(proj, *_compress_params(p["cmp_pos"], p["cmp_w1"], p["cmp_w2"]),
                             p["nsa_k_norm"][0:1])

    near, bias_cmp = _bias_tables(p["rel_bias"], S)
    nsa_near = near[:NSA_HEADS].reshape(G, NSA_HPG, 2, TQ, 2 * TQ).transpose(0, 2, 1, 3, 4)
    i = np.arange(TQ)
    win_mask = jnp.asarray(np.where(i[None, :] > i[:, None], 0.0, NEG).astype(np.float32))
    expand = jnp.asarray(np.repeat(np.eye(S // SLC_BLOCK, dtype=np.float32), SLC_BLOCK, axis=1), BF16)
    o_nsa = _nsa_attention(proj, k_cmp, v_cmp, bias_cmp.reshape(G, NSA_HPG, S, LANE), nsa_near, win_mask,
                           jnp.asarray(_overlap_matrix(S)), expand)
    o_diff = _diff_attention(proj, p["diff_lambda"], p["diff_out_norm"][None, :], near[NSA_HEADS:],
                             lambda_init)

    rw = jnp.concatenate([p["router_group_w"], p["router_expert_w"],
                          jnp.zeros((D, LANE - N_GROUPS - N_EXPERTS), F32)], axis=1)
    rb = jnp.concatenate([p["router_group_b"], p["router_expert_b"],
                          jnp.zeros((LANE - N_GROUPS - N_EXPERTS,), F32)])[None, :]
    x2r = _merge(x2d, o_nsa.reshape(B * S, -1), o_diff.reshape(B * S, -1), proj2d,
                 p["w_branch_nsa"].astype(BF16), p["w_branch_diff"].astype(BF16),
                 p["w_out"].astype(BF16), p["norm2_g"][None, :], rw, rb)

    tm = 256
    bucket = x2r[:, D + N_EXPERTS].astype(jnp.int32)
    perm = jnp.argsort(bucket).astype(jnp.int32)
    out = _moe(_moe_schedule(bucket, tm), perm, x2r, p["norm2_g"][None, :],
               p["expert_w_gate"].astype(BF16), p["expert_w_up"].astype(BF16),
               p["expert_w_down"].astype(BF16), tm)
    return out.reshape(B, S, D)


def kernel(x, norm1_g, w_in, nsa_q_norm, nsa_k_norm, cmp_pos, cmp_w1, cmp_w2, diff_q_norm, diff_k_norm,
           diff_lambda, diff_out_norm, rel_bias, w_branch_nsa, w_branch_diff, w_out, norm2_g,
           router_group_w, router_group_b, router_expert_w, router_expert_b, expert_w_gate,
           expert_w_up, expert_w_down):
    per_layer = dict(norm1_g=norm1_g, w_in=w_in, nsa_q_norm=nsa_q_norm, nsa_k_norm=nsa_k_norm,
                     cmp_pos=cmp_pos, cmp_w1=cmp_w1, cmp_w2=cmp_w2, diff_q_norm=diff_q_norm,
                     diff_k_norm=diff_k_norm, diff_lambda=diff_lambda, diff_out_norm=diff_out_norm,
                     w_branch_nsa=w_branch_nsa, w_branch_diff=w_branch_diff, w_out=w_out,
                     norm2_g=norm2_g, router_group_w=router_group_w, router_group_b=router_group_b,
                     router_expert_w=router_expert_w, router_expert_b=router_expert_b,
                     expert_w_gate=expert_w_gate, expert_w_up=expert_w_up, expert_w_down=expert_w_down)
    for layer in range(norm1_g.shape[0]):
        p = {k: v[layer] for k, v in per_layer.items()}
        p["rel_bias"] = rel_bias
        x = _layer(x, layer, p)
    return x
```

```python
import functools
import math

import numpy as np
import jax
import jax.numpy as jnp
from jax import lax
from jax.experimental import pallas as pl
from jax.experimental.pallas import tpu as pltpu

F32 = jnp.float32
BF16 = jnp.bfloat16

HEAD_DIM = 64
SCALE = HEAD_DIM ** -0.5
NSA_HEADS = 8
NSA_KV_GROUPS = 2
NSA_HPG = NSA_HEADS // NSA_KV_GROUPS
CMP_BLOCK = 32
CMP_STRIDE = 16
CMP_HIDDEN = 256
SLC_BLOCK = 64
SLC_TOP_N = 16
WINDOW = 512
FORCE_SCORE = 1e4
DIFF_HEADS = 4
DIFF_V_DIM = 2 * HEAD_DIM
REL_BUCKETS = 32
REL_MAX_DIST = 128
N_GROUPS = 4
EXPERTS_PER_GROUP = 4
N_EXPERTS = N_GROUPS * EXPERTS_PER_GROUP
EXPERT_HIDDEN = 512
RMS_EPS = 1e-6
NEG = -1e30
LOG2E = math.log2(math.e)

LANE = 128
MXU_DIM = 256
TQ = 256
N_PAIRS = 6
N_BUCKETS = N_GROUPS * N_PAIRS

C_GM = 0
C_QN = 2048
C_KV = 2560
C_CMP = 3072
C_GATE = 3328
C_QD = 3584
C_KD = 4096
C_VD = 4608
C_TOTAL = 5120

V7X_VMEM_BYTES = 64 * 1024 * 1024
VMEM_LIMIT = V7X_VMEM_BYTES * 7 // 8


def _cparams(sem):
    return pltpu.CompilerParams(dimension_semantics=sem, vmem_limit_bytes=VMEM_LIMIT)


def _rms(x, gain):
    return x * lax.rsqrt(jnp.mean(x * x, axis=-1, keepdims=True) + RMS_EPS) * gain


def _reorder_w_in(w_in):
    d_model = w_in.shape[0]
    w_in = w_in.astype(BF16)
    dh, G, Hg = HEAD_DIM, NSA_KV_GROUPS, NSA_HPG
    o_q = 0
    o_kv = o_q + NSA_HEADS * dh
    o_g = o_kv + 3 * 2 * G * dh
    o_qd = o_g + 3 * NSA_HEADS
    o_gm = o_qd + 3 * DIFF_HEADS * 2 * dh

    def cols(start, n):
        return w_in[:, start:start + n]

    def kv(kind, which, g):
        return cols(o_kv + ((kind * 2 + which) * G + g) * dh, dh)

    parts = [cols(o_gm, 2 * d_model), cols(o_q, NSA_HEADS * dh)]
    parts += [kv(kind, which, g) for g in range(G) for which in (0, 1) for kind in (1, 2)]
    parts += [kv(0, which, g) for g in range(G) for which in (0, 1)]
    for g in range(G):
        parts += [cols(o_g + (kind * G + g) * Hg, Hg) for kind in range(3)]
        parts.append(jnp.zeros((d_model, LANE - 3 * Hg), w_in.dtype))
    parts.append(cols(o_qd, 3 * DIFF_HEADS * 2 * dh))
    w = jnp.concatenate(parts, axis=1)
    assert w.shape[1] == C_TOTAL
    return w


def _head_norm_rows(p):
    dh, G = HEAD_DIM, NSA_KV_GROUPS
    qs = SCALE * LOG2E
    zeros = lambda n: jnp.zeros((n,), F32)
    kn = p["nsa_k_norm"]
    parts = [zeros(C_QN - C_GM), jnp.tile(p["nsa_q_norm"] * qs, NSA_HEADS)]
    parts += [jnp.concatenate([kn[1], kn[2], zeros(2 * dh)])] * G
    parts += [zeros(C_QD - C_CMP), jnp.tile((p["diff_q_norm"] * qs).reshape(-1), DIFF_HEADS),
              jnp.tile(p["diff_k_norm"].reshape(-1), DIFF_HEADS), zeros(C_TOTAL - C_VD)]
    gain = jnp.concatenate(parts)
    assert gain.shape[0] == C_TOTAL
    return gain[None, :]


def _t5_bucket(dist):
    n = jnp.maximum(dist, 0)
    max_exact = REL_BUCKETS // 2
    nf = jnp.maximum(n, 1).astype(F32)
    large = max_exact + (jnp.log(nf / max_exact) / math.log(REL_MAX_DIST / max_exact)
                         * (REL_BUCKETS - max_exact)).astype(jnp.int32)
    large = jnp.minimum(large, REL_BUCKETS - 1)
    return jnp.where(n < max_exact, n, large)


def _toeplitz(v_ext, rows, cols):
    length = v_ext.shape[-1]
    lead = v_ext.shape[:-1]
    flat = jnp.broadcast_to(v_ext[..., None, :], lead + (rows, length)).reshape(lead + (rows * length,))
    return flat[..., :rows * (length - 1)].reshape(lead + (rows, length - 1))[..., :cols]


def _bias_tables(rel_bias, S):
    heads = rel_bias.shape[1]
    far = REL_MAX_DIST
    assert TQ == 2 * far and LANE == far
    bd = rel_bias[_t5_bucket(jnp.arange(2 * far))].T.astype(F32) * LOG2E
    rel = bd - bd[:, far:far + 1]
    neg = lambda *shape: jnp.full((heads,) + shape, NEG, F32)
    zero = lambda *shape: jnp.zeros((heads,) + shape, F32)
    rev = lambda a: a[:, ::-1]
    d0 = _toeplitz(jnp.concatenate([rel[:, 0:1], neg(far), rev(rel[:, 1:far])], axis=1), far, far)
    d1 = _toeplitz(jnp.concatenate([rev(rel[:, 1:far + 1]), zero(far)], axis=1), far, far)
    z, n = zero(far, far), neg(far, far)
    blocks = lambda rows: jnp.concatenate([jnp.concatenate(r, axis=2) for r in rows], axis=1)
    near = jnp.stack([blocks([[d0, n, n, n], [d1, d0, n, n]]),
                      blocks([[z, d1, d0, n], [z, z, d1, d0]])], axis=1)
    hb = rel[:NSA_HEADS]
    zero8 = lambda *shape: jnp.zeros((NSA_HEADS,) + shape, F32)
    neg8 = lambda *shape: jnp.full((NSA_HEADS,) + shape, NEG, F32)
    width = TQ + far - 1
    tp = _toeplitz(jnp.concatenate([rev(hb[:, 0:far]), neg8(TQ + 1), zero8(TQ - 1)], axis=1), TQ, width)
    n_tiles = S // TQ
    per_tile = TQ // CMP_STRIDE
    col0 = per_tile * (n_tiles - 1)
    shift = CMP_STRIDE * col0 - (CMP_BLOCK - 1)
    first = -(-(shift - (far - 1)) // CMP_STRIDE)
    last = (shift + TQ - 1) // CMP_STRIDE
    band = tp[:, :, CMP_STRIDE * first - shift + far - 1::CMP_STRIDE][:, :, :last - first + 1]
    g2 = jnp.concatenate([zero8(TQ, first), band, neg8(TQ, 2 * TQ - last - 1)], axis=2)
    bias_cmp = jnp.stack([g2[:, :, col0 - per_tile * i:col0 - per_tile * i + LANE] for i in range(n_tiles)],
                         axis=1)
    n_cmp = (S - CMP_BLOCK) // CMP_STRIDE + 1
    bias_cmp = jnp.where(np.arange(LANE) < n_cmp, bias_cmp, NEG)
    return near, bias_cmp.reshape(NSA_HEADS, S, LANE)


def _overlap_matrix(S):
    n_cmp = (S - CMP_BLOCK) // CMP_STRIDE + 1
    n_slc = S // SLC_BLOCK
    c_start = np.arange(n_cmp) * CMP_STRIDE
    s_start = np.arange(n_slc) * SLC_BLOCK
    lo = np.maximum(s_start[:, None], c_start[None, :])
    hi = np.minimum(s_start[:, None] + SLC_BLOCK, c_start[None, :] + CMP_BLOCK)
    overlap = (np.maximum(hi - lo, 0) / CMP_BLOCK).astype(np.float32)
    out = np.zeros((n_slc, LANE), np.float32)
    out[:, :n_cmp] = overlap
    return out


IN_CHUNK = 512
HEAD_NORM_CHUNKS = {C_QN // IN_CHUNK: "all", C_KV // IN_CHUNK: "keys",
                    C_QD // IN_CHUNK: "all", C_KD // IN_CHUNK: "all"}


def _in_proj_kernel(x_ref, g_ref, w_ref, hg_ref, seg_ref, o_ref):
    dh = HEAD_DIM
    h = _rms(x_ref[...], g_ref[...]).astype(BF16)
    for c in range(C_TOTAL // IN_CHUNK):
        cols = slice(c * IN_CHUNK, (c + 1) * IN_CHUNK)
        y = jnp.dot(h, w_ref[:, cols], preferred_element_type=F32)
        mode = HEAD_NORM_CHUNKS.get(c)
        if mode is not None:
            sq = (y * y).astype(BF16)
            seg_w = seg_ref.shape[0]
            ss = jnp.concatenate([jnp.dot(sq[:, j:j + seg_w], seg_ref[...], preferred_element_type=F32)
                                  for j in range(0, IN_CHUNK, seg_w)], axis=1)
            scale = lax.rsqrt(ss * (1.0 / dh) + RMS_EPS) * hg_ref[:, cols]
            if mode == "keys":
                lane = lax.broadcasted_iota(jnp.int32, (1, IN_CHUNK), 1)
                scale = jnp.where(lane % (4 * dh) < 2 * dh, scale, 1.0)
            y = y * scale
        o_ref[:, cols] = y.astype(BF16)


def _in_proj(x2d, gain, w, head_gain):
    n, d = x2d.shape
    tm = 1024
    seg = np.arange(MXU_DIM) // HEAD_DIM
    seg_ones = jnp.asarray((seg[:, None] == seg[None, :]).astype(np.float32), BF16)
    return pl.pallas_call(
        _in_proj_kernel,
        grid=(n // tm,),
        in_specs=[pl.BlockSpec((tm, d), lambda i: (i, 0)),
                  pl.BlockSpec((1, d), lambda i: (0, 0)),
                  pl.BlockSpec((d, C_TOTAL), lambda i: (0, 0), pipeline_mode=pl.Buffered(1)),
                  pl.BlockSpec((1, C_TOTAL), lambda i: (0, 0)),
                  pl.BlockSpec((MXU_DIM, MXU_DIM), lambda i: (0, 0))],
        out_specs=pl.BlockSpec((tm, C_TOTAL), lambda i: (i, 0)),
        out_shape=jax.ShapeDtypeStruct((n, C_TOTAL), BF16),
        compiler_params=_cparams(("arbitrary",)),
        name="in_proj",
    )(x2d, gain, w, head_gain, seg_ones)


def _compress_kernel(t_ref, pos_ref, w1_ref, w2_ref, kg_ref, k_ref, v_ref, tok_scr):
    S, dh = t_ref.shape[0], HEAD_DIM
    m = S // CMP_STRIDE
    tok_scr[0:S, :] = t_ref[...].astype(F32)
    tok_scr[S:, :] = jnp.zeros((CMP_STRIDE, LANE), F32)
    pre = jnp.zeros((m, 2 * CMP_HIDDEN), F32)
    for l in range(CMP_BLOCK):
        x = tok_scr[pl.ds(l, m, stride=CMP_STRIDE), :] + pos_ref[l:l + 1, :]
        pre = pre + jnp.dot(x.astype(BF16), w1_ref[l], preferred_element_type=F32)
    out = jnp.dot(jax.nn.gelu(pre).astype(BF16), w2_ref[...], preferred_element_type=F32)
    k_ref[...] = jnp.concatenate([_rms(out[:, :dh], kg_ref[...]), jnp.zeros((m, LANE - dh), F32)],
                                 axis=1).astype(BF16)
    v_ref[...] = out[:, dh:].astype(BF16)


def _compress(proj, pos, w1, w2, k_gain):
    B, S, _ = proj.shape
    G, m = NSA_KV_GROUPS, S // CMP_STRIDE
    outs = [jax.ShapeDtypeStruct((B, G, m, width), BF16) for width in (LANE, HEAD_DIM)]
    ospecs = [pl.BlockSpec((None, None, m, width), lambda b, g: (b, g, 0, 0)) for width in (LANE, HEAD_DIM)]
    return pl.pallas_call(
        _compress_kernel,
        grid=(B, G),
        in_specs=[pl.BlockSpec((None, S, LANE), lambda b, g: (b, 0, C_CMP // LANE + g)),
                  pl.BlockSpec(pos.shape, lambda b, g: (0, 0)),
                  pl.BlockSpec(w1.shape, lambda b, g: (0, 0, 0)),
                  pl.BlockSpec(w2.shape, lambda b, g: (0, 0)),
                  pl.BlockSpec((1, HEAD_DIM), lambda b, g: (0, 0))],
        out_specs=ospecs,
        out_shape=outs,
        scratch_shapes=[pltpu.VMEM((S + CMP_STRIDE, LANE), F32)],
        compiler_params=_cparams(("arbitrary", "arbitrary")),
        name="nsa_compress",
    )(proj, pos, w1, w2, k_gain)


def _compress_params(cmp_pos, cmp_w1, cmp_w2):
    dh, hid = HEAD_DIM, CMP_HIDDEN
    w1 = cmp_w1.reshape(2, CMP_BLOCK, dh, hid)
    z1 = jnp.zeros((CMP_BLOCK, dh, hid), cmp_w1.dtype)
    w1 = jnp.concatenate([jnp.concatenate([w1[0], z1], axis=2),
                          jnp.concatenate([z1, w1[1]], axis=2)], axis=1)
    z2 = jnp.zeros((hid, dh), cmp_w2.dtype)
    w2 = jnp.concatenate([jnp.concatenate([cmp_w2[0], z2], axis=1),
                          jnp.concatenate([z2, cmp_w2[1]], axis=1)], axis=0)
    pos = jnp.concatenate([cmp_pos[0], cmp_pos[1]], axis=1)
    return pos, w1.astype(BF16), w2.astype(BF16)


def _dot_nt(a, b):
    return lax.dot_general(a, b, (((1,), (1,)), ((), ())), preferred_element_type=F32)


def _softmax_step(s, v, carry):
    m, l, acc = carry
    m_new = jnp.maximum(m, jnp.max(s, axis=-1, keepdims=True))
    alpha = jnp.exp2(m - m_new)
    p = jnp.exp2(s - m_new)
    l = alpha * l + jnp.sum(p, axis=-1, keepdims=True)
    acc = alpha * acc + jnp.dot(p.astype(BF16), v, preferred_element_type=F32)
    return m_new, l, acc


def _softmax_init(rows, dv):
    return (jnp.full((rows, 1), NEG, F32), jnp.zeros((rows, 1), F32), jnp.zeros((rows, dv), F32))


def _softmax_step_ones(s, v_ones, carry):
    m, acc = carry
    m_new = jnp.maximum(m, jnp.max(s, axis=-1, keepdims=True))
    p = jnp.exp2((s - m_new).astype(BF16))
    acc = jnp.exp2(m - m_new) * acc + jnp.dot(p, v_ones, preferred_element_type=F32)
    return m_new, acc


def _softmax_init_ones(rows, width):
    return jnp.full((rows, 1), NEG, F32), jnp.zeros((rows, width), F32)


def _key_chunks(c):
    T = TQ
    chunks = [(0, T, (0, 0))] if c == 0 else [((c - 1) * T, 2 * T, (1, 0))]
    n_far = max(c - 1, 0)
    pos = 0
    for size in (1, 2):
        if n_far & size:
            chunks.append((pos * T, size * T, None))
            pos += size
    chunks += [(s * T, 4 * T, None) for s in range(pos, n_far, 4)]
    return chunks


def _nsa_kernel(q_ref, kv_ref, kc_ref, vc_ref, gate_ref, bc_ref, nb_ref, wm_ref, ov_ref, ex_ref, gx_ref,
                o_ref, *, seq_len):
    T, dh, Hg = TQ, HEAD_DIM, NSA_HPG
    R = Hg * T
    n_slc = seq_len // SLC_BLOCK
    n_all = SLC_TOP_N * SLC_BLOCK // T

    def values(rows, ones_upper):
        v = kv_ref[rows, 2 * dh:4 * dh]
        keep = lax.broadcasted_iota(jnp.int32, v.shape, 1) < dh
        return jnp.where(keep if ones_upper else ~keep, v, jnp.ones_like(v))

    def selection_mask(c, pc):
        p_sum = pc[0:T] + pc[T:2 * T] + pc[2 * T:3 * T] + pc[3 * T:4 * T]
        imp = lax.dot_general(ov_ref[...], p_sum, (((1,), (1,)), ((), ())), preferred_element_type=F32,
                              precision=lax.Precision.HIGHEST)
        blk = lax.broadcasted_iota(jnp.int32, (n_slc, T), 0)
        cur = (c * T + lax.broadcasted_iota(jnp.int32, (n_slc, T), 1)) // SLC_BLOCK
        forced = (blk == 0) | (blk == cur) | (blk == cur - 1)
        score = jnp.where(blk <= cur, imp + jnp.where(forced, FORCE_SCORE, 0.0), NEG)
        rank = jnp.zeros((n_slc, T), jnp.int32)
        for i in range(n_slc):
            other = score[i:i + 1, :]
            ahead = (other > score) | ((other == score) & (blk > i))
            rank = rank + ahead.astype(jnp.int32)
        sel_bias = jnp.where(rank < SLC_TOP_N, 0.0, NEG).T.astype(BF16)
        return jnp.dot(sel_bias, ex_ref[...], preferred_element_type=F32)

    def tile(c):
        lower = lax.broadcasted_iota(jnp.int32, (T, 2 * dh), 1) < dh
        q_lo, q_hi = [], []
        for pair in range(Hg // 2):
            pq = q_ref[:, pair * 2 * dh:(pair + 1) * 2 * dh]
            sw = jnp.concatenate([pq[:, dh:], pq[:, :dh]], axis=1)
            zero = jnp.zeros_like(pq)
            q_lo += [jnp.where(lower, pq, zero), jnp.where(lower, sw, zero)]
            q_hi += [jnp.where(lower, zero, sw), jnp.where(lower, zero, pq)]
        q = jnp.concatenate(q_lo, axis=0)
        q_win = jnp.concatenate(q_hi, axis=0)

        lc = _dot_nt(q, kc_ref[...]) + bc_ref[...].reshape(R, LANE)
        pc = jnp.exp2(lc - jnp.max(lc, axis=-1, keepdims=True))
        row = lax.broadcasted_iota(jnp.int32, (Hg, T, 1), 1).reshape(R, 1)
        visible = (c * T + row >= CMP_BLOCK - 1).astype(F32)
        pc = pc * (visible / jnp.sum(pc, axis=-1, keepdims=True))
        o_cmp = jnp.dot(pc.astype(BF16), vc_ref[...], preferred_element_type=F32)

        mask = selection_mask(c, pc) if c >= n_all else None
        chunks = _key_chunks(c)
        Hh = Hg // 2
        halves = []
        for hh in range(2):
            heads = slice(hh * Hh, (hh + 1) * Hh)
            q_half = q[hh * Hh * T:(hh + 1) * Hh * T]
            carry = _softmax_init_ones(Hh * T, 2 * dh)
            for start, width, table in chunks:
                rows = slice(start, start + width)
                s = _dot_nt(q_half, kv_ref[rows, 0:2 * dh]).reshape(Hh, T, width)
                if table is not None:
                    s = s + nb_ref[table[0], heads, :, table[1]:table[1] + width]
                if mask is not None:
                    s = s + mask[:, rows][None]
                carry = _softmax_step_ones(s.reshape(Hh * T, width), values(rows, True), carry)
            halves.append(carry[1])
        acc_s = jnp.concatenate(halves, axis=0)
        o_slc = acc_s[:, :dh] / acc_s[:, dh:]

        start, width, table = chunks[0]
        win_chunks = [(start, width, nb_ref[table[0], :, :, table[1]:table[1] + width])]
        if c >= 2:
            win_chunks.append(((c - 2) * T, T, wm_ref[...][None]))
        carry = _softmax_init_ones(R, 2 * dh)
        for start, width, bias in win_chunks:
            rows = slice(start, start + width)
            s = _dot_nt(q_win, kv_ref[rows, 0:2 * dh]).reshape(Hg, T, width) + bias
            carry = _softmax_step_ones(s.reshape(R, width), values(rows, False), carry)
        acc_w = carry[1]
        o_win = acc_w[:, dh:] / acc_w[:, :dh]

        gates = jax.nn.sigmoid(gate_ref[...].astype(F32))
        g_hi = gates.astype(BF16)
        g_lo = (gates - g_hi.astype(F32)).astype(BF16)
        g_wide = (jnp.dot(g_hi, gx_ref[...], preferred_element_type=F32)
                  + jnp.dot(g_lo, gx_ref[...], preferred_element_type=F32))
        by_head = lambda o: jnp.concatenate([o[h * T:(h + 1) * T] for h in range(Hg)], axis=1)
        w = Hg * dh
        out = (g_wide[:, 0:w] * by_head(o_cmp) + g_wide[:, w:2 * w] * by_head(o_slc)
               + g_wide[:, 2 * w:3 * w] * by_head(o_win))
        o_ref[...] = out.astype(BF16)

    qi = pl.program_id(2)
    for c in range(seq_len // T):
        pl.when(qi == c)(functools.partial(tile, c))


def _nsa_attention(proj, k_cmp, v_cmp, bias_cmp, near_bias, win_mask, overlap, expand):
    B, S, _ = proj.shape
    G, Hg, dh, T = NSA_KV_GROUPS, NSA_HPG, HEAD_DIM, TQ
    assert WINDOW == 2 * T and S % (2 * T) == 0
    n_slc = S // SLC_BLOCK
    gate_expand = np.zeros((LANE, 3 * Hg * dh), np.float32)
    for j in range(3 * Hg):
        gate_expand[j, j * dh:(j + 1) * dh] = 1.0
    gate_expand = jnp.asarray(gate_expand, BF16)
    return pl.pallas_call(
        functools.partial(_nsa_kernel, seq_len=S),
        grid=(B, G, S // T),
        in_specs=[
            pl.BlockSpec((None, T, Hg * dh), lambda b, g, i: (b, i, C_QN // (Hg * dh) + g)),
            pl.BlockSpec((None, S, 4 * dh), lambda b, g, i: (b, 0, C_KV // (4 * dh) + g)),
            pl.BlockSpec((None, None, LANE, LANE), lambda b, g, i: (b, g, 0, 0)),
            pl.BlockSpec((None, None, LANE, dh), lambda b, g, i: (b, g, 0, 0)),
            pl.BlockSpec((None, T, LANE), lambda b, g, i: (b, i, C_GATE // LANE + g)),
            pl.BlockSpec((None, Hg, T, LANE), lambda b, g, i: (g, 0, i, 0)),
            pl.BlockSpec((None, 2, Hg, T, 2 * T), lambda b, g, i: (g, 0, 0, 0, 0)),
            pl.BlockSpec((T, T), lambda b, g, i: (0, 0)),
            pl.BlockSpec((n_slc, LANE), lambda b, g, i: (0, 0)),
            pl.BlockSpec((n_slc, S), lambda b, g, i: (0, 0)),
            pl.BlockSpec(gate_expand.shape, lambda b, g, i: (0, 0)),
        ],
        out_specs=pl.BlockSpec((None, T, Hg * dh), lambda b, g, i: (b, i, g)),
        out_shape=jax.ShapeDtypeStruct((B, S, NSA_HEADS * dh), BF16),
        compiler_params=_cparams(("arbitrary", "arbitrary", "arbitrary")),
        name="nsa_attention",
    )(proj, proj, k_cmp, v_cmp, proj, bias_cmp, near_bias, win_mask, overlap, expand, gate_expand)


def _diff_kernel(q_ref, k_ref, v_ref, lam_ref, og_ref, nb_ref, o_ref, *, lambda_init, n_tiles):
    T, dh = TQ, HEAD_DIM
    qi = pl.program_id(2)
    q = q_ref[...]
    lower = lax.broadcasted_iota(jnp.int32, q.shape, 1) < dh
    zero = jnp.zeros_like(q)
    q12 = jnp.concatenate([jnp.where(lower, q, zero), jnp.where(lower, zero, q)], axis=0)

    def step(start, width, bias, carry):
        s = _dot_nt(q12, k_ref[start:start + width, :])
        if bias is not None:
            s = (s.reshape(2, T, width) + bias[None]).reshape(2 * T, width)
        return _softmax_step(s, v_ref[start:start + width, :], carry)

    lam_p = lam_ref[...]
    lam = (jnp.exp(jnp.sum(lam_p[0:1] * lam_p[1:2], axis=-1, keepdims=True))
           - jnp.exp(jnp.sum(lam_p[2:3] * lam_p[3:4], axis=-1, keepdims=True)) + lambda_init)

    def tile(c):
        carry = _softmax_init(2 * T, DIFF_V_DIM)
        for start, width, table in _key_chunks(c):
            bias = None if table is None else nb_ref[table[0], :, table[1]:table[1] + width]
            carry = step(start, width, bias, carry)
        a = carry[2] / carry[1]
        o = a[:T] - lam * a[T:]
        o_ref[...] = (_rms(o, og_ref[...]) * (1.0 - lambda_init)).astype(BF16)

    for c in range(n_tiles):
        pl.when(qi == c)(functools.partial(tile, c))


def _diff_attention(proj, lam_p, out_gain, near_bias, lambda_init):
    B, S, _ = proj.shape
    T, dh, H = TQ, HEAD_DIM, DIFF_HEADS
    return pl.pallas_call(
        functools.partial(_diff_kernel, lambda_init=lambda_init, n_tiles=S // T),
        grid=(B, H, S // T),
        in_specs=[
            pl.BlockSpec((None, T, 2 * dh), lambda b, h, i: (b, i, C_QD // (2 * dh) + h)),
            pl.BlockSpec((None, S, 2 * dh), lambda b, h, i: (b, 0, C_KD // (2 * dh) + h)),
            pl.BlockSpec((None, S, DIFF_V_DIM), lambda b, h, i: (b, 0, C_VD // DIFF_V_DIM + h)),
            pl.BlockSpec((4, dh), lambda b, h, i: (0, 0)),
            pl.BlockSpec((1, DIFF_V_DIM), lambda b, h, i: (0, 0)),
            pl.BlockSpec((None, 2, T, 2 * T), lambda b, h, i: (h, 0, 0, 0)),
        ],
        out_specs=pl.BlockSpec((None, T, DIFF_V_DIM), lambda b, h, i: (b, i, h)),
        out_shape=jax.ShapeDtypeStruct((B, S, H * DIFF_V_DIM), BF16),
        compiler_params=_cparams(("arbitrary", "arbitrary", "arbitrary")),
        name="diff_attention",
    )(proj, proj, proj, lam_p, out_gain, near_bias)


def _merge_kernel(x_ref, on_ref, od_ref, gm_ref, wn_ref, wd_ref, wo_ref, g2_ref, rw_ref, rb_ref, o_ref):
    tm, d = x_ref.shape
    a = jnp.dot(on_ref[...], wn_ref[...], preferred_element_type=F32)
    b = jnp.dot(od_ref[...], wd_ref[...], preferred_element_type=F32)
    sigmoid = lambda z: 0.5 * jnp.tanh(0.5 * z) + 0.5
    y = sigmoid(gm_ref[:, :d].astype(F32)) * a + sigmoid(gm_ref[:, d:].astype(F32)) * b
    x2 = x_ref[...] + jnp.dot(y.astype(BF16), wo_ref[...], preferred_element_type=F32)
    o_ref[:, :d] = x2

    h2 = _rms(x2, g2_ref[...])
    h_hi = h2.astype(BF16)
    h_lo = (h2 - h_hi.astype(F32)).astype(BF16)
    part = jnp.dot(jnp.concatenate([h_hi, h_lo], axis=0), rw_ref[...], preferred_element_type=F32)
    logits = part[:tm, :LANE] + part[:tm, LANE:] + part[tm:, :LANE] + part[tm:, LANE:] + rb_ref[...]
    lane = lax.broadcasted_iota(jnp.int32, logits.shape, 1)
    big = jnp.int32(LANE)

    def first_argmax(vals, mask):
        vals = jnp.where(mask, vals, -jnp.inf)
        top = jnp.max(vals, axis=-1, keepdims=True)
        idx = jnp.min(jnp.where(vals == top, lane, big), axis=-1, keepdims=True)
        return top, idx

    is_group = lane < N_GROUPS
    g_max, g_top = first_argmax(logits, is_group)
    g_top_p = 1.0 / jnp.sum(jnp.where(is_group, jnp.exp(logits - g_max), 0.0), axis=-1, keepdims=True)
    lo = N_GROUPS + g_top * EXPERTS_PER_GROUP
    in_group = (lane >= lo) & (lane < lo + EXPERTS_PER_GROUP)
    m1, i1 = first_argmax(logits, in_group)
    m2, i2 = first_argmax(logits, in_group & (lane != i1))
    e2 = jnp.exp(m2 - m1)
    w1 = g_top_p / (1.0 + e2)
    w2 = g_top_p * e2 / (1.0 + e2)
    la = jnp.minimum(i1, i2) - lo
    lb = jnp.maximum(i1, i2) - lo
    pair = jnp.right_shift(la * (7 - la), 1) + (lb - la - 1)
    bucket = (g_top * N_PAIRS + pair).astype(F32)
    exp_lane = lane + N_GROUPS
    route = (jnp.where(exp_lane == i1, w1, 0.0) + jnp.where(exp_lane == i2, w2, 0.0)
             + jnp.where(lane == N_EXPERTS, bucket, 0.0))
    o_ref[:, d:] = route


def _merge(x2d, o_nsa, o_diff, proj2d, wn, wd, wo, g2, rw, rb):
    n, d = x2d.shape
    tm = 512
    rw_hi = rw.astype(BF16)
    rw_cat = jnp.concatenate([rw_hi, (rw - rw_hi.astype(F32)).astype(BF16)], axis=1)
    full = lambda a: pl.BlockSpec(a.shape, lambda i: (0,) * a.ndim)
    return pl.pallas_call(
        _merge_kernel,
        grid=(n // tm,),
        in_specs=[pl.BlockSpec((tm, d), lambda i: (i, 0)),
                  pl.BlockSpec((tm, o_nsa.shape[1]), lambda i: (i, 0)),
                  pl.BlockSpec((tm, o_diff.shape[1]), lambda i: (i, 0)),
                  pl.BlockSpec((tm, 2 * d), lambda i: (i, C_GM // (2 * d))),
                  full(wn), full(wd), full(wo), full(g2), full(rw_cat), full(rb)],
        out_specs=pl.BlockSpec((tm, d + LANE), lambda i: (i, 0)),
        out_shape=jax.ShapeDtypeStruct((n, d + LANE), F32),
        compiler_params=_cparams(("arbitrary",)),
        name="merge_router",
    )(x2d, o_nsa, o_diff, proj2d, wn, wd, wo, g2, rw_cat, rb)


N_SLOT = 3


def _moe_kernel(tile_ref, ea_ref, eb_ref, lo_ref, hi_ref, first_ref, perm_ref,
                x_hbm, g2_ref, wga_ref, wua_ref, wda_ref, wgb_ref, wub_ref, wdb_ref,
                o_hbm, xb0, xb1, xb2, ac0, ac1, ac2, h_scr, g_sem, s_sem, *, tm, n_tiles):
    d = o_hbm.shape[1]
    v = pl.program_id(0)
    t = tile_ref[v]
    x_bufs, acc_bufs = (xb0, xb1, xb2), (ac0, ac1, ac2)
    last = n_tiles - 1

    def gather_copy(tt, sl, r):
        return pltpu.make_async_copy(x_hbm.at[pl.ds(perm_ref[tt * tm + r], 1), :],
                                     x_bufs[sl].at[pl.ds(r, 1), :], g_sem.at[sl])

    def scatter_copy(tt, sl, r):
        return pltpu.make_async_copy(acc_bufs[sl].at[pl.ds(r, 1), :],
                                     o_hbm.at[pl.ds(perm_ref[tt * tm + r], 1), :], s_sem.at[sl])

    def for_rows(fn):
        for r in range(tm):
            fn(r)

    def experts(sl, dma_starts):
        n_stage = 6
        per_stage = -(-len(dma_starts) // n_stage)

        def issue(stage):
            for start in dma_starts[stage * per_stage:(stage + 1) * per_stage]:
                start()

        h = h_scr[...]
        cw = x_bufs[sl][:, d:]
        lane = lax.broadcasted_iota(jnp.int32, cw.shape, 1)
        row = lax.broadcasted_iota(jnp.int32, (tm, 1), 0)
        in_bucket = (row >= lo_ref[v]) & (row < hi_ref[v])
        out = None
        for i, (e, wg_ref, wu_ref, wd_ref) in enumerate(((ea_ref[v], wga_ref, wua_ref, wda_ref),
                                                         (eb_ref[v], wgb_ref, wub_ref, wdb_ref))):
            w = jnp.sum(jnp.where(lane == e, cw, 0.0), axis=-1, keepdims=True)
            w = jnp.where(in_bucket, w, 0.0)
            gate = jnp.dot(h, wg_ref[...], preferred_element_type=F32)
            issue(3 * i)
            up = jnp.dot(h, wu_ref[...], preferred_element_type=F32)
            issue(3 * i + 1)
            hid = jax.nn.silu(gate) * up * w
            y = jnp.dot(hid.astype(BF16), wd_ref[...], preferred_element_type=F32)
            issue(3 * i + 2)
            out = y if out is None else out + y
        acc_bufs[sl][...] += out

    def first_visit(sl, scatter_previous):
        ahead, behind = (sl + 2) % N_SLOT, (sl - 1) % N_SLOT
        for_rows(lambda r: gather_copy(t, sl, r).wait())

        @pl.when(t >= N_SLOT)
        def _():
            for_rows(lambda r: scatter_copy(t - N_SLOT, sl, r).wait())

        xs = x_bufs[sl][:, :d]
        h_scr[...] = _rms(xs, g2_ref[...]).astype(BF16)
        acc_bufs[sl][...] = xs
        nxt = jnp.minimum(t + 2, last)
        starts = [functools.partial(lambda r: gather_copy(nxt, ahead, r).start(), r) for r in range(tm)]
        if scatter_previous:
            starts += [functools.partial(lambda r: scatter_copy(t - 1, behind, r).start(), r)
                       for r in range(tm)]
        experts(sl, starts)

    @pl.when(v == 0)
    def _():
        for_rows(lambda r: gather_copy(0, 0, r).start())
        for_rows(lambda r: gather_copy(1, 1, r).start())
        first_visit(0, scatter_previous=False)

    for sl in range(N_SLOT):
        pl.when((v > 0) & (first_ref[v] == 1) & (t % N_SLOT == sl))(
            functools.partial(first_visit, sl, scatter_previous=True))
        pl.when((first_ref[v] == 0) & (t % N_SLOT == sl))(functools.partial(experts, sl, []))

    @pl.when(v == pl.num_programs(0) - 1)
    def _():
        for_rows(lambda r: scatter_copy(last, last % N_SLOT, r).start())
        for tt in (last - 1, last):
            for_rows(lambda r: gather_copy(last, (tt + 2) % N_SLOT, r).wait())
        for tt in (last - 2, last - 1, last):
            for_rows(lambda r: scatter_copy(tt, tt % N_SLOT, r).wait())


def _moe(sched, perm, x2r, g2, wg, wu, wd, tm):
    n = x2r.shape[0]
    d, f = wg.shape[1], wg.shape[2]
    n_visits = sched[0].shape[0]
    n_tiles = n // tm
    assert n_tiles > N_SLOT
    w_in_a = pl.BlockSpec((None, d, f), lambda v, t, ea, eb, *_: (ea[v], 0, 0))
    w_in_b = pl.BlockSpec((None, d, f), lambda v, t, ea, eb, *_: (eb[v], 0, 0))
    w_out_a = pl.BlockSpec((None, f, d), lambda v, t, ea, eb, *_: (ea[v], 0, 0))
    w_out_b = pl.BlockSpec((None, f, d), lambda v, t, ea, eb, *_: (eb[v], 0, 0))
    return pl.pallas_call(
        functools.partial(_moe_kernel, tm=tm, n_tiles=n_tiles),
        grid_spec=pltpu.PrefetchScalarGridSpec(
            num_scalar_prefetch=7,
            grid=(n_visits,),
            in_specs=[pl.BlockSpec(memory_space=pl.ANY),
                      pl.BlockSpec((1, d), lambda v, *_: (0, 0)),
                      w_in_a, w_in_a, w_out_a, w_in_b, w_in_b, w_out_b],
            out_specs=pl.BlockSpec(memory_space=pl.ANY),
            scratch_shapes=([pltpu.VMEM((tm, x2r.shape[1]), F32)] * N_SLOT + [pltpu.VMEM((tm, d), F32)] * N_SLOT
                            + [pltpu.VMEM((tm, d), BF16),
                               pltpu.SemaphoreType.DMA((N_SLOT,)), pltpu.SemaphoreType.DMA((N_SLOT,))])),
        out_shape=jax.ShapeDtypeStruct((n, d), F32),
        compiler_params=_cparams(("arbitrary",)),
        name="moe_experts",
    )(*sched, perm, x2r, g2, wg, wu, wd, wg, wu, wd)


def _moe_schedule(bucket, tm):
    n = bucket.shape[0]
    n_tiles = n // tm
    n_visits = n_tiles + N_BUCKETS - 1
    counts = jnp.sum((bucket[:, None] == jnp.arange(N_BUCKETS)[None, :]).astype(jnp.int32), axis=0)
    ends = jnp.cumsum(counts)
    starts = ends - counts
    first_tile = starts // tm
    last_tile = jnp.maximum(ends - 1, 0) // tm
    n_vis = jnp.where(counts > 0, last_tile - first_tile + 1, 0)
    vis_end = jnp.cumsum(n_vis)
    vis_start = vis_end - n_vis
    v = jnp.arange(n_visits, dtype=jnp.int32)
    b = jnp.minimum(jnp.searchsorted(vis_end, v, side="right"), N_BUCKETS - 1).astype(jnp.int32)
    live = v < vis_end[-1]
    tile = jnp.where(live, first_tile[b] + v - vis_start[b], n_tiles - 1).astype(jnp.int32)
    lo = jnp.where(live, jnp.clip(starts[b] - tile * tm, 0, tm), 0).astype(jnp.int32)
    hi = jnp.where(live, jnp.clip(ends[b] - tile * tm, 0, tm), 0).astype(jnp.int32)
    b_last = b[jnp.maximum(vis_end[-1] - 1, 0)]
    b = jnp.where(live, b, b_last)
    pair_a = np.array([0, 0, 0, 1, 1, 2], np.int32)
    pair_b = np.array([1, 2, 3, 2, 3, 3], np.int32)
    grp = b // N_PAIRS
    ea = (grp * EXPERTS_PER_GROUP + jnp.asarray(pair_a)[b % N_PAIRS]).astype(jnp.int32)
    eb = (grp * EXPERTS_PER_GROUP + jnp.asarray(pair_b)[b % N_PAIRS]).astype(jnp.int32)
    first = jnp.concatenate([jnp.ones((1,), jnp.int32), (tile[1:] != tile[:-1]).astype(jnp.int32)])
    return tile, ea, eb, lo, hi, first


def _layer(x, layer, p):
    B, S, D = x.shape
    G, dh = NSA_KV_GROUPS, HEAD_DIM
    lambda_init = 0.8 - 0.6 * math.exp(-0.3 * layer)
    x2d = x.reshape(B * S, D)

    proj2d = _in_proj(x2d, p["norm1_g"][None, :], _reorder_w_in(p["w_in"]), _head_norm_rows(p))
    proj = proj2d.reshape(B, S, C_TOTAL)

    k_cmp, v_cmp = _compress(proj, *_compress_params(p["cmp_pos"], p["cmp_w1"], p["cmp_w2"]),
                             p["nsa_k_norm"][0:1])

    near, bias_cmp = _bias_tables(p["rel_bias"], S)
    nsa_near = near[:NSA_HEADS].reshape(G, NSA_HPG, 2, TQ, 2 * TQ).transpose(0, 2, 1, 3, 4)
    i = np.arange(TQ)
    win_mask = jnp.asarray(np.where(i[None, :] > i[:, None], 0.0, NEG).astype(np.float32))
    expand = jnp.asarray(np.repeat(np.eye(S // SLC_BLOCK, dtype=np.float32), SLC_BLOCK, axis=1), BF16)
    o_nsa = _nsa_attention(proj, k_cmp, v_cmp, bias_cmp.reshape(G, NSA_HPG, S, LANE), nsa_near, win_mask,
                           jnp.asarray(_overlap_matrix(S)), expand)
    o_diff = _diff_attention(proj, p["diff_lambda"], p["diff_out_norm"][None, :], near[NSA_HEADS:],
                             lambda_init)

    rw = jnp.concatenate([p["router_group_w"], p["router_expert_w"],
                          jnp.zeros((D, LANE - N_GROUPS - N_EXPERTS), F32)], axis=1)
    rb = jnp.concatenate([p["router_group_b"], p["router_expert_b"],
                          jnp.zeros((LANE - N_GROUPS - N_EXPERTS,), F32)])[None, :]
    x2r = _merge(x2d, o_nsa.reshape(B * S, -1), o_diff.reshape(B * S, -1), proj2d,
                 p["w_branch_nsa"].astype(BF16), p["w_branch_diff"].astype(BF16),
                 p["w_out"].astype(BF16), p["norm2_g"][None, :], rw, rb)

    tm = 256
    bucket = x2r[:, D + N_EXPERTS].astype(jnp.int32)
    perm = jnp.argsort(bucket).astype(jnp.int32)
    out = _moe(_moe_schedule(bucket, tm), perm, x2r, p["norm2_g"][None, :],
               p["expert_w_gate"].astype(BF16), p["expert_w_up"].astype(BF16),
               p["expert_w_down"].astype(BF16), tm)
    return out.reshape(B, S, D)


def kernel(x, norm1_g, w_in, nsa_q_norm, nsa_k_norm, cmp_pos, cmp_w1, cmp_w2, diff_q_norm, diff_k_norm,
           diff_lambda, diff_out_norm, rel_bias, w_branch_nsa, w_branch_diff, w_out, norm2_g,
           router_group_w, router_group_b, router_expert_w, router_expert_b, expert_w_gate,
           expert_w_up, expert_w_down):
    per_layer = dict(norm1_g=norm1_g, w_in=w_in, nsa_q_norm=nsa_q_norm, nsa_k_norm=nsa_k_norm,
                     cmp_pos=cmp_pos, cmp_w1=cmp_w1, cmp_w2=cmp_w2, diff_q_norm=diff_q_norm,
                     diff_k_norm=diff_k_norm, diff_lambda=diff_lambda, diff_out_norm=diff_out_norm,
                     w_branch_nsa=w_branch_nsa, w_branch_diff=w_branch_diff, w_out=w_out,
                     norm2_g=norm2_g, router_group_w=router_group_w, router_group_b=router_group_b,
                     router_expert_w=router_expert_w, router_expert_b=router_expert_b,
                     expert_w_gate=expert_w_gate, expert_w_up=expert_w_up, expert_w_down=expert_w_down)
    for layer in range(norm1_g.shape[0]):
        p = {k: v[layer] for k, v in per_layer.items()}
        p["rel_bias"] = rel_bias
        x = _layer(x, layer, p)
    return x
```
